```python
import math
import jax
import jax.numpy as jnp
from jax import lax
import numpy as np

D_MODEL = 1024
BATCH = 8
SEQ = 4096
DEPTH = 2

GRID_W = 64
CTX_LEN = 256
N_MOD = 6
EPS = 1e-6
NEG_INF = -1e30
ROPE_BASE = 10000.0
BLOCK = 128
DA_HEADS = 4
DA_QK_DIM = 32
DA_V_DIM = 64
HEAD_DIM = 64
SW_HEADS = 6
SW_KV_HEADS = 2
SW_GROUP = SW_HEADS // SW_KV_HEADS
WINDOW = 128
LRU_WIDTH = 384
LRU_BLOCKS = 6
LRU_BLOCK_DIM = LRU_WIDTH // LRU_BLOCKS
LRU_CONV = 4
LRU_C = 8.0
D_MIX = DA_HEADS * DA_V_DIM + SW_HEADS * HEAD_DIM + LRU_WIDTH
IN_SIZES = (DA_HEADS * 2 * DA_QK_DIM, DA_HEADS * 2 * DA_QK_DIM, DA_HEADS * DA_V_DIM,
            SW_HEADS * HEAD_DIM, SW_KV_HEADS * HEAD_DIM, SW_KV_HEADS * HEAD_DIM,
            LRU_WIDTH, LRU_WIDTH)
D_IN = sum(IN_SIZES)
D_FF = 2816
FFN_CONV = 3

kernel_name = 'hybrid_diffusion_parallel_heads'

F32 = jnp.float32


def rms_norm(x, gain):
    xf = x.astype(F32)
    y = xf * lax.rsqrt(jnp.mean(xf * xf, axis=-1, keepdims=True) + EPS)
    return (y * gain.astype(F32)).astype(x.dtype)


def modulate(h, shift, scale):
    return h * (1 + scale) + shift


def axial_rope_tables(rows, head_dim):
    row = jnp.repeat(jnp.arange(rows, dtype=F32), GRID_W)
    col = jnp.tile(jnp.arange(GRID_W, dtype=F32), rows)
    quarter = head_dim // 4
    inv_freq = ROPE_BASE ** (-jnp.arange(quarter, dtype=F32) / quarter)
    ang = jnp.concatenate([row[:, None] * inv_freq, col[:, None] * inv_freq], axis=-1)
    return jnp.cos(ang), jnp.sin(ang)


def apply_rope(x, cos, sin):
    xf = x.astype(F32)
    half = x.shape[-1] // 2
    x1, x2 = xf[..., :half], xf[..., half:]
    shape = (1, cos.shape[0]) + (1,) * (x.ndim - 3) + (half,)
    cs, sn = cos.reshape(shape), sin.reshape(shape)
    return jnp.concatenate([x1 * cs - x2 * sn, x1 * sn + x2 * cs], axis=-1).astype(x.dtype)


def dwconv(x, w, b, left):
    k_width = w.shape[0]
    t_len = x.shape[1]
    xp = jnp.pad(x, ((0, 0), (left, k_width - 1 - left), (0, 0)))
    out = b
    for k in range(k_width):
        out = out + xp[:, k:k + t_len] * w[k]
    return out


def linear_scan(a, b, h0):
    def combine(l, r):
        return (l[0] * r[0], r[0] * l[1] + r[1])
    a_cum, b_cum = lax.associative_scan(combine, (a, b), axis=1)
    return a_cum * h0[:, None] + b_cum


def da_qkv(aq, ak, av, q_gain, k_gain, rope):
    bsz, t_len = aq.shape[:2]
    q = rms_norm(aq.reshape(bsz, t_len, DA_HEADS, 2, DA_QK_DIM), q_gain)
    k = rms_norm(ak.reshape(bsz, t_len, DA_HEADS, 2, DA_QK_DIM), k_gain)
    v = av.reshape(bsz, t_len, DA_HEADS, DA_V_DIM)
    if rope is not None:
        q = apply_rope(q, rope[0], rope[1])
        k = apply_rope(k, rope[0], rope[1])
    return q, k, v


def diff_attention(q, k, v, qc, kc, vc, lam, lambda_init, sub_gain, ctx_out):
    bsz, n_tok = q.shape[:2]
    scale = DA_QK_DIM ** -0.5

    def attend(qb, keys, vals):
        s = jnp.einsum('bqhmd,bkhmd->bhmqk', qb, keys).astype(F32) * scale
        p = jax.nn.softmax(s, axis=-1)
        w = p[:, :, 0] - lam * p[:, :, 1]
        return jnp.einsum('bhqk,bkhd->bqhd', w.astype(vals.dtype), vals)

    def finish(o):
        return (rms_norm(o, sub_gain) * (1.0 - lambda_init)).reshape(o.shape[0], o.shape[1], -1)

    k_all = jnp.concatenate([kc, k], axis=1)
    v_all = jnp.concatenate([vc, v], axis=1)
    n_blk = n_tok // BLOCK
    qb = q.reshape((bsz, n_blk, BLOCK) + q.shape[2:]).swapaxes(0, 1)
    o = lax.map(lambda blk: attend(blk, k_all, v_all), qb)
    y = finish(o.swapaxes(0, 1).reshape(bsz, n_tok, DA_HEADS, DA_V_DIM))
    yc = finish(attend(qc, kc, vc)) if ctx_out else None
    return y, yc


def sw_qkv(bq, bk, bv, q_gain, k_gain, rope):
    bsz, t_len = bq.shape[:2]
    q = rms_norm(bq.reshape(bsz, t_len, SW_HEADS, HEAD_DIM), q_gain)
    k = rms_norm(bk.reshape(bsz, t_len, SW_KV_HEADS, HEAD_DIM), k_gain)
    v = bv.reshape(bsz, t_len, SW_KV_HEADS, HEAD_DIM)
    if rope is not None:
        q = apply_rope(q, rope[0], rope[1])
        k = apply_rope(k, rope[0], rope[1])
    return q, k, v


def window_attention(q, k, v, qc, kc, vc, sink, ctx_out):
    bsz, n_tok = q.shape[:2]
    n_ctx = kc.shape[1]
    n_blk = n_tok // BLOCK
    scale = HEAD_DIM ** -0.5
    sink_l = sink.astype(F32).reshape(SW_KV_HEADS, SW_GROUP, 1, 1)

    def band(t):
        tp = jnp.pad(t, ((0, 0), (BLOCK, BLOCK), (0, 0), (0, 0)))
        tp = tp.reshape(bsz, n_blk + 2, BLOCK, SW_KV_HEADS, HEAD_DIM)
        bd = jnp.concatenate([tp[:, :-2], tp[:, 1:-1], tp[:, 2:]], axis=2)
        return bd.swapaxes(0, 1)

    kb, vb = band(k), band(v)
    qb = q.reshape(bsz, n_blk, BLOCK, SW_KV_HEADS, SW_GROUP, HEAD_DIM).swapaxes(0, 1)
    qi = jnp.arange(BLOCK)
    kj = jnp.arange(3 * BLOCK) - BLOCK

    def one_block(args):
        n, qblk, kblk, vblk = args
        qpos = n * BLOCK + qi
        kpos = n * BLOCK + kj
        valid = ((jnp.abs(qpos[:, None] - kpos[None, :]) <= WINDOW)
                 & (kpos >= 0)[None, :] & (kpos < n_tok)[None, :])
        s_lat = jnp.einsum('bqkgd,bjkd->bkgqj', qblk, kblk).astype(F32) * scale
        s_lat = jnp.where(valid, s_lat, NEG_INF)
        s_ctx = jnp.einsum('bqkgd,bckd->bkgqc', qblk, kc).astype(F32) * scale
        sink_col = jnp.broadcast_to(sink_l, s_ctx.shape[:-1] + (1,))
        p = jax.nn.softmax(jnp.concatenate([s_ctx, s_lat, sink_col], axis=-1), axis=-1)
        p_ctx = p[..., :n_ctx]
        p_lat = p[..., n_ctx:n_ctx + 3 * BLOCK]
        return (jnp.einsum('bkgqc,bckd->bqkgd', p_ctx.astype(vc.dtype), vc)
                + jnp.einsum('bkgqj,bjkd->bqkgd', p_lat.astype(vblk.dtype), vblk))

    o = lax.map(one_block, (jnp.arange(n_blk), qb, kb, vb))
    y = o.swapaxes(0, 1).reshape(bsz, n_tok, SW_HEADS * HEAD_DIM)
    yc = None
    if ctx_out:
        qcg = qc.reshape(bsz, n_ctx, SW_KV_HEADS, SW_GROUP, HEAD_DIM)
        s = jnp.einsum('bqkgd,bckd->bkgqc', qcg, kc).astype(F32) * scale
        sink_col = jnp.broadcast_to(sink_l, s.shape[:-1] + (1,))
        p = jax.nn.softmax(jnp.concatenate([s, sink_col], axis=-1), axis=-1)[..., :n_ctx]
        yc = jnp.einsum('bkgqc,bckd->bqkgd', p.astype(vc.dtype), vc).reshape(bsz, n_ctx, -1)
    return y, yc


def rglru_gates(xs, conv_w, conv_b, wa, ba, wx, bx, lam_param):
    xc = dwconv(xs, conv_w, conv_b, LRU_CONV - 1)
    xblk = xc.reshape(xc.shape[0], xc.shape[1], LRU_BLOCKS, LRU_BLOCK_DIM)
    r = jax.nn.sigmoid(jnp.einsum('bthi,hij->bthj', xblk, wa).reshape(xc.shape) + ba)
    i = jax.nn.sigmoid(jnp.einsum('bthi,hij->bthj', xblk, wx).reshape(xc.shape) + bx)
    log_a = -LRU_C * r.astype(F32) * jax.nn.softplus(-lam_param.astype(F32))
    a = jnp.exp(log_a)
    b = jnp.sqrt(-jnp.expm1(2.0 * log_a)) * (i * xc).astype(F32)
    return a, b


def rglru_mixer(cx, cg, cxc, cgc, lp, ctx_out):
    bsz = cx.shape[0]
    lat_sum = 0.0
    ctx_sum = 0.0
    for d in range(2):
        flip = (lambda t: t[:, ::-1]) if d == 1 else (lambda t: t)
        params = (lp['lru_conv_w'][d], lp['lru_conv_b'][d], lp['lru_wa'][d], lp['lru_ba'][d],
                  lp['lru_wx'][d], lp['lru_bx'][d], lp['lru_lambda'][d])
        a_c, b_c = rglru_gates(flip(cxc), *params)
        h_c = linear_scan(a_c, b_c, jnp.zeros((bsz, LRU_WIDTH), F32))
        a_l, b_l = rglru_gates(flip(cx), *params)
        h_l = linear_scan(a_l, b_l, h_c[:, -1])
        lat_sum = lat_sum + flip(h_l)
        ctx_sum = ctx_sum + flip(h_c)
    y = (lat_sum * jax.nn.gelu(cg.astype(F32), approximate=True)).astype(cx.dtype)
    yc = (ctx_sum * jax.nn.gelu(cgc.astype(F32), approximate=True)).astype(cxc.dtype) if ctx_out else None
    return y, yc


def conv_ffn(h, lp):
    up = h @ lp['w_up']
    gate, val = jnp.split(up, 2, axis=-1)
    gate = dwconv(gate, lp['ffn_conv_w'], lp['ffn_conv_b'], (FFN_CONV - 1) // 2)
    return (jax.nn.silu(gate) * val) @ lp['w_down']


def hybrid_layer(x, xc, c, c_ctx, lp, lambda_init, rope_a, rope_b, ctx_out):
    mod = jax.nn.silu(c) @ lp['w_mod'] + lp['b_mod']
    mod_c = jax.nn.silu(c_ctx) @ lp['w_mod'] + lp['b_mod']
    sh1, sc1, g1, sh2, sc2, g2 = [m[:, None] for m in jnp.split(mod, N_MOD, axis=-1)]
    sh1c, sc1c, g1c, sh2c, sc2c, g2c = jnp.split(mod_c, N_MOD, axis=-1)

    h = modulate(rms_norm(x, lp['norm1_gain']), sh1, sc1)
    hc = modulate(rms_norm(xc, lp['norm1_gain']), sh1c, sc1c)
    split_pts = np.cumsum(IN_SIZES)[:-1].tolist()
    aq, ak, av, bq, bk, bv, cx, cg = jnp.split(h @ lp['w_in'], split_pts, axis=-1)
    aqc, akc, avc, bqc, bkc, bvc, cxc, cgc = jnp.split(hc @ lp['w_in'], split_pts, axis=-1)

    lam = (jnp.exp(jnp.sum(lp['da_lam_q1'].astype(F32) * lp['da_lam_k1'].astype(F32)))
           - jnp.exp(jnp.sum(lp['da_lam_q2'].astype(F32) * lp['da_lam_k2'].astype(F32)))
           + lambda_init)
    qa, ka, va = da_qkv(aq, ak, av, lp['da_q_gain'], lp['da_k_gain'], rope_a)
    qac, kac, vac = da_qkv(aqc, akc, avc, lp['da_q_gain'], lp['da_k_gain'], None)
    ya, yac = diff_attention(qa, ka, va, qac, kac, vac, lam, lambda_init, lp['da_sub_gain'], ctx_out)

    qb, kb, vb = sw_qkv(bq, bk, bv, lp['sw_q_gain'], lp['sw_k_gain'], rope_b)
    qbc, kbc, vbc = sw_qkv(bqc, bkc, bvc, lp['sw_q_gain'], lp['sw_k_gain'], None)
    yb, ybc = window_attention(qb, kb, vb, qbc, kbc, vbc, lp['sw_sink'], ctx_out)

    yc_, ycc = rglru_mixer(cx, cg, cxc, cgc, lp, ctx_out)

    y = jnp.concatenate([ya, yb, yc_], axis=-1) @ lp['w_out']
    x = x + g1 * y
    x = x + g2 * conv_ffn(modulate(rms_norm(x, lp['norm2_gain']), sh2, sc2), lp)
    if ctx_out:
        yctx = jnp.concatenate([yac, ybc, ycc], axis=-1) @ lp['w_out']
        xc = xc + g1c * yctx
        xc = xc + g2c * conv_ffn(modulate(rms_norm(xc, lp['norm2_gain']), sh2c, sc2c), lp)
    return x, xc


def setup_inputs(seed: int = 0) -> dict:
    key = jax.random.key(seed)
    ks = iter(jax.random.split(key, 40))
    D = D_MODEL

    def nrm(shape, scale):
        return jax.random.normal(next(ks), shape, F32) * scale

    def gain(shape):
        return 1.0 + nrm(shape, 0.02)

    x = nrm((BATCH, SEQ, D), 1.0)
    c = nrm((BATCH, D), 1.0)
    ctx = nrm((BATCH, CTX_LEN, D), 1.0)
    c_ctx = nrm((D,), 1.0)
    w_mod = nrm((DEPTH, D, N_MOD * D), 0.3 * D ** -0.5)
    b_mod = nrm((DEPTH, N_MOD * D), 0.02)
    norm1_gain = gain((DEPTH, D))
    norm2_gain = gain((DEPTH, D))
    w_in = nrm((DEPTH, D, D_IN), D ** -0.5)
    da_q_gain = gain((DEPTH, DA_QK_DIM))
    da_k_gain = gain((DEPTH, DA_QK_DIM))
    da_lam_q1 = nrm((DEPTH, DA_QK_DIM), 0.1)
    da_lam_k1 = nrm((DEPTH, DA_QK_DIM), 0.1)
    da_lam_q2 = nrm((DEPTH, DA_QK_DIM), 0.1)
    da_lam_k2 = nrm((DEPTH, DA_QK_DIM), 0.1)
    da_sub_gain = gain((DEPTH, DA_V_DIM))
    sw_q_gain = gain((DEPTH, HEAD_DIM))
    sw_k_gain = gain((DEPTH, HEAD_DIM))
    sw_sink = nrm((DEPTH, SW_HEADS), 1.0)
    lru_conv_w = nrm((DEPTH, 2, LRU_CONV, LRU_WIDTH), LRU_CONV ** -0.5)
    lru_conv_b = nrm((DEPTH, 2, LRU_WIDTH), 0.02)
    lru_wa = nrm((DEPTH, 2, LRU_BLOCKS, LRU_BLOCK_DIM, LRU_BLOCK_DIM), LRU_BLOCK_DIM ** -0.5)
    lru_ba = nrm((DEPTH, 2, LRU_WIDTH), 0.02)
    lru_wx = nrm((DEPTH, 2, LRU_BLOCKS, LRU_BLOCK_DIM, LRU_BLOCK_DIM), LRU_BLOCK_DIM ** -0.5)
    lru_bx = nrm((DEPTH, 2, LRU_WIDTH), 0.02)
    a_min, a_max = 0.9 ** (1.0 / LRU_C), 0.999 ** (1.0 / LRU_C)
    a = jax.random.uniform(next(ks), (DEPTH, 2, LRU_WIDTH), F32, a_min, a_max)
    lru_lambda = jnp.log(a) - jnp.log1p(-a)
    w_out = nrm((DEPTH, D_MIX, D), D_MIX ** -0.5)
    w_up = nrm((DEPTH, D, 2 * D_FF), D ** -0.5)
    ffn_conv_w = nrm((DEPTH, FFN_CONV, D_FF), FFN_CONV ** -0.5)
    ffn_conv_b = nrm((DEPTH, D_FF), 0.02)
    w_down = nrm((DEPTH, D_FF, D), D_FF ** -0.5)
    return {'x': x, 'c': c, 'ctx': ctx, 'c_ctx': c_ctx, 'w_mod': w_mod, 'b_mod': b_mod,
            'norm1_gain': norm1_gain, 'norm2_gain': norm2_gain, 'w_in': w_in,
            'da_q_gain': da_q_gain, 'da_k_gain': da_k_gain, 'da_lam_q1': da_lam_q1,
            'da_lam_k1': da_lam_k1, 'da_lam_q2': da_lam_q2, 'da_lam_k2': da_lam_k2,
            'da_sub_gain': da_sub_gain, 'sw_q_gain': sw_q_gain, 'sw_k_gain': sw_k_gain,
            'sw_sink': sw_sink, 'lru_conv_w': lru_conv_w, 'lru_conv_b': lru_conv_b,
            'lru_wa': lru_wa, 'lru_ba': lru_ba, 'lru_wx': lru_wx, 'lru_bx': lru_bx,
            'lru_lambda': lru_lambda, 'w_out': w_out, 'w_up': w_up, 'ffn_conv_w': ffn_conv_w,
            'ffn_conv_b': ffn_conv_b, 'w_down': w_down}


def reference(x, c, ctx, c_ctx, w_mod, b_mod, norm1_gain, norm2_gain, w_in, da_q_gain, da_k_gain,
              da_lam_q1, da_lam_k1, da_lam_q2, da_lam_k2, da_sub_gain, sw_q_gain, sw_k_gain,
              sw_sink, lru_conv_w, lru_conv_b, lru_wa, lru_ba, lru_wx, lru_bx, lru_lambda,
              w_out, w_up, ffn_conv_w, ffn_conv_b, w_down):
    n_tok = x.shape[1]
    rows = n_tok // GRID_W
    rope_a = axial_rope_tables(rows, DA_QK_DIM)
    rope_b = axial_rope_tables(rows, HEAD_DIM)
    xc = ctx
    for i in range(DEPTH):
        lp = dict(w_mod=w_mod[i], b_mod=b_mod[i], norm1_gain=norm1_gain[i], norm2_gain=norm2_gain[i],
                  w_in=w_in[i], da_q_gain=da_q_gain[i], da_k_gain=da_k_gain[i],
                  da_lam_q1=da_lam_q1[i], da_lam_k1=da_lam_k1[i], da_lam_q2=da_lam_q2[i],
                  da_lam_k2=da_lam_k2[i], da_sub_gain=da_sub_gain[i], sw_q_gain=sw_q_gain[i],
                  sw_k_gain=sw_k_gain[i], sw_sink=sw_sink[i], lru_conv_w=lru_conv_w[i],
                  lru_conv_b=lru_conv_b[i], lru_wa=lru_wa[i], lru_ba=lru_ba[i], lru_wx=lru_wx[i],
                  lru_bx=lru_bx[i], lru_lambda=lru_lambda[i], w_out=w_out[i], w_up=w_up[i],
                  ffn_conv_w=ffn_conv_w[i], ffn_conv_b=ffn_conv_b[i], w_down=w_down[i])
        lambda_init = 0.8 - 0.6 * math.exp(-0.3 * i)
        x, xc = hybrid_layer(x, xc, c, c_ctx, lp, lambda_init, rope_a, rope_b, i < DEPTH - 1)
    return x
```

```python
import functools
import math

import numpy as np
import jax
import jax.numpy as jnp
from jax import lax
from jax.experimental import pallas as pl
from jax.experimental.pallas import tpu as pltpu

F32 = jnp.float32
BF16 = jnp.bfloat16

D_MODEL = 1024
GRID_W = 64
N_MOD = 6
EPS = 1e-6
NEG_INF = -1e30
ROPE_BASE = 10000.0
DA_HEADS = 4
DA_QK_DIM = 32
DA_V_DIM = 64
HEAD_DIM = 64
SW_HEADS = 6
SW_KV_HEADS = 2
SW_GROUP = SW_HEADS // SW_KV_HEADS
WINDOW = 128
LRU_WIDTH = 384
LRU_BLOCKS = 6
LRU_BLOCK_DIM = LRU_WIDTH // LRU_BLOCKS
LRU_CONV = 4
LRU_C = 8.0
D_FF = 2816
FFN_CONV = 3

DA_Q = DA_HEADS * 2 * DA_QK_DIM
DA_V = DA_HEADS * DA_V_DIM
SW_Q = SW_HEADS * HEAD_DIM
SW_KV = SW_KV_HEADS * HEAD_DIM

LANES = 128
SUBLANES = 8
BF16_ROWS = 16
VMEM_LIMIT = 56 * 1024 * 1024

DA_VEXT = DA_HEADS * LANES
C_AQ = 0
C_AK = C_AQ + DA_Q
C_AV = C_AK + DA_Q
C_BQ = C_AV + DA_VEXT
C_BK = C_BQ + SW_Q
C_BV = C_BK + SW_KV
C_CX = C_BV + SW_KV
C_CG = C_CX + LRU_WIDTH
D_INX = C_CG + LRU_WIDTH

TM = 512
FFN_HALO = BF16_ROWS
FFN_CHUNKS = ((0, 1024), (1024, 2048), (2048, D_FF))
DA_TQ = 256
DA_TK = 512
SW_TQ = 256
LRU_TC = 256


def _cparams(n_axes):
    return pltpu.CompilerParams(dimension_semantics=("arbitrary",) * n_axes,
                                vmem_limit_bytes=VMEM_LIMIT)


def _const_spec(shape):
    nd = len(shape)
    return pl.BlockSpec(shape, lambda *_: (0,) * nd, pipeline_mode=pl.Buffered(1))


def _nt_dot(a, b):
    return lax.dot_general(a, b, (((1,), (1,)), ((), ())), preferred_element_type=F32)


def _dot(a, b):
    return jnp.dot(a, b, preferred_element_type=F32)


def _mod_kernel(c_ref, w_ref, b_ref, o_ref):
    c = c_ref[...]
    s = c * jax.nn.sigmoid(c)
    o_ref[0] = _dot(s.astype(BF16), w_ref[0].astype(BF16)) + b_ref[0]


def _modulation(cc, w_mod, b_mod):
    depth, d, n = w_mod.shape
    tn = 1536
    return pl.pallas_call(
        _mod_kernel,
        out_shape=jax.ShapeDtypeStruct((depth, cc.shape[0], n), F32),
        grid=(depth, n // tn),
        in_specs=[pl.BlockSpec(cc.shape, lambda i, j: (0, 0)),
                  pl.BlockSpec((1, d, tn), lambda i, j: (i, 0, j)),
                  pl.BlockSpec((1, 1, tn), lambda i, j: (i, 0, j))],
        out_specs=pl.BlockSpec((1, cc.shape[0], tn), lambda i, j: (i, 0, j)),
        compiler_params=_cparams(2),
        name="modulation",
    )(cc, w_mod, b_mod.reshape(depth, 1, n))


def _norm_modulate(x, gain, shift, scale):
    ms = jnp.mean(x * x, axis=-1, keepdims=True)
    h = x * lax.rsqrt(ms + EPS) * gain
    return h * (1.0 + scale) + shift


def _group_rms(a, g_ref):
    sq = a * a
    hi = sq.astype(BF16)
    lo = (sq - hi.astype(F32)).astype(BF16)
    g = g_ref[...]
    ms = _dot(hi, g) + _dot(lo, g)
    return a * lax.rsqrt(ms + EPS)


def _rope(y, cos, sin_signed, half):
    n = y.shape[1]
    lane = lax.broadcasted_iota(jnp.int32, y.shape, 1)
    first = (lane % (2 * half)) < half
    partner = jnp.where(first, pltpu.roll(y, n - half, 1), pltpu.roll(y, half, 1))
    return y * cos + partner * sin_signed


def _inproj_kernel(x_ref, sh_ref, sc_ref, ng_ref, w_ref, vone_ref,
                   cosa_ref, sina_ref, cosb_ref, sinb_ref,
                   gqa_ref, gka_ref, gqb_ref, gkb_ref, g32_ref, g64_ref,
                   qa_ref, ka_ref, va_ref, qb_ref, kb_ref, vb_ref, cx_ref, cg_ref):
    h = _norm_modulate(x_ref[0], ng_ref[...], sh_ref[0], sc_ref[0]).astype(BF16)

    def proj(lo, hi):
        return _dot(h, w_ref[:, lo:hi])

    cosa, sina = cosa_ref[...], sina_ref[...]
    aq = _group_rms(proj(C_AQ, C_AK), g32_ref) * gqa_ref[...]
    qa_ref[0] = _rope(aq, cosa, sina, DA_QK_DIM // 2).astype(BF16)
    ak = _group_rms(proj(C_AK, C_AV), g32_ref) * gka_ref[...]
    ka_ref[0] = _rope(ak, cosa, sina, DA_QK_DIM // 2).astype(BF16)
    va_ref[0] = (proj(C_AV, C_BQ) + vone_ref[...]).astype(BF16)

    cosb, sinb = cosb_ref[...], sinb_ref[...]
    bq = _group_rms(proj(C_BQ, C_BK), g64_ref) * gqb_ref[...]
    qb_ref[0] = _rope(bq, cosb, sinb, HEAD_DIM // 2).astype(BF16)
    bkv = proj(C_BK, C_CX)
    bk = _group_rms(bkv[:, :SW_KV], g64_ref.at[:SW_KV, :SW_KV]) * gkb_ref[...]
    kb_ref[0] = _rope(bk, cosb[:, :SW_KV], sinb[:, :SW_KV], HEAD_DIM // 2).astype(BF16)
    vb_ref[0] = bkv[:, SW_KV:].astype(BF16)

    cxg = proj(C_CX, D_INX)
    cx_ref[0] = cxg[:, :LRU_WIDTH]
    cg_ref[0] = cxg[:, LRU_WIDTH:]


def _input_projection(x, shift, scale, lw, rope):
    bsz, t_len, d = x.shape
    tm = min(TM, t_len)
    per_b = shift.shape[0] > 1
    mod_map = (lambda b, j: (b, 0, 0)) if per_b else (lambda b, j: (0, 0, 0))
    tok = lambda n: pl.BlockSpec((1, tm, n), lambda b, j: (b, j, 0))
    tab = lambda n: pl.BlockSpec((tm, n), lambda b, j: (j, 0))
    widths = (DA_Q, DA_Q, DA_VEXT, SW_Q, SW_KV, SW_KV, LRU_WIDTH, LRU_WIDTH)
    dtypes = (BF16,) * 6 + (F32, F32)
    return pl.pallas_call(
        _inproj_kernel,
        out_shape=tuple(jax.ShapeDtypeStruct((bsz, t_len, n), dt) for n, dt in zip(widths, dtypes)),
        grid=(bsz, t_len // tm),
        in_specs=[tok(d),
                  pl.BlockSpec((1, 1, d), mod_map), pl.BlockSpec((1, 1, d), mod_map),
                  _const_spec((1, d)), _const_spec((d, D_INX)), _const_spec((1, DA_VEXT)),
                  tab(DA_Q), tab(DA_Q), tab(SW_Q), tab(SW_Q),
                  _const_spec((1, DA_Q)), _const_spec((1, DA_Q)),
                  _const_spec((1, SW_Q)), _const_spec((1, SW_KV)),
                  _const_spec((DA_Q, DA_Q)), _const_spec((SW_Q, SW_Q))],
        out_specs=tuple(tok(n) for n in widths),
        compiler_params=_cparams(2),
        name="input_projection",
    )(x, shift, scale, lw["norm1_gain"], lw["w_in"], lw["v_one"], *rope,
      lw["gq_a"], lw["gk_a"], lw["gq_b"], lw["gk_b"], lw["g32"], lw["g64"])


def _da_kernel(*refs, n_seg, seg_len, tq, lambda_init):
    q_ref = refs[0]
    kv_refs = refs[1:1 + 2 * n_seg]
    hmask_ref, lamv_ref, sg_ref, o_ref = refs[1 + 2 * n_seg:]

    lv = lamv_ref[...]
    lam = (jnp.exp(jnp.sum(lv[0:1] * lv[1:2], axis=1, keepdims=True))
           - jnp.exp(jnp.sum(lv[2:3] * lv[3:4], axis=1, keepdims=True)) + lambda_init)
    lane = lax.broadcasted_iota(jnp.int32, (1, LANES), 1)
    vmask = (lane < DA_V_DIM).astype(F32)
    sg = sg_ref[...]

    def head_out(h):
        maps = []
        for m in range(2):
            j = 2 * h + m
            half = j // 4
            qm = q_ref[0, :, half * LANES:(half + 1) * LANES] * hmask_ref[j:j + 1, :]

            def step(k_ref, v_ref, start, size, carry):
                m_i, acc = carry
                kblk = k_ref[0, pl.ds(start, size), half * LANES:(half + 1) * LANES]
                s = _nt_dot(qm, kblk)
                m_new = jnp.maximum(m_i, jnp.max(s, axis=1, keepdims=True))
                alpha = jnp.exp(m_i - m_new)
                p = jnp.exp(s - m_new).astype(BF16)
                vblk = v_ref[0, pl.ds(start, size), h * LANES:(h + 1) * LANES]
                return m_new, alpha * acc + _dot(p, vblk)

            carry = (jnp.full((tq, 1), NEG_INF, F32), jnp.zeros((tq, LANES), F32))
            for si in range(n_seg):
                k_ref, v_ref = kv_refs[2 * si], kv_refs[2 * si + 1]
                tk = min(DA_TK, seg_len[si])
                n_t = seg_len[si] // tk
                if n_t == 1:
                    carry = step(k_ref, v_ref, 0, tk, carry)
                else:
                    carry = lax.fori_loop(
                        0, n_t,
                        lambda i, c, k_ref=k_ref, v_ref=v_ref, tk=tk: step(
                            k_ref, v_ref, pl.multiple_of(i * tk, tk), tk, c),
                        carry)
            acc = carry[1]
            maps.append(acc / acc[:, DA_V_DIM:DA_V_DIM + 1])
        o = (maps[0] - lam * maps[1]) * vmask
        ms = jnp.sum(o * o, axis=1, keepdims=True) * (1.0 / DA_V_DIM)
        return o * lax.rsqrt(ms + EPS) * sg

    ys = [head_out(h) for h in range(DA_HEADS)]
    left = lane < DA_V_DIM
    for c in range(DA_HEADS // 2):
        pair = jnp.where(left, ys[2 * c], pltpu.roll(ys[2 * c + 1], DA_V_DIM, 1))
        o_ref[0, :, c * LANES:(c + 1) * LANES] = pair.astype(o_ref.dtype)


def _diff_attention(q, kvs, lw, lambda_init):
    bsz, t_q, _ = q.shape
    tq = min(DA_TQ, t_q)
    seg_len = tuple(k.shape[1] for k, _ in kvs)
    kv_specs, kv_args = [], []
    for k, v in kvs:
        kv_specs += [pl.BlockSpec((1, k.shape[1], DA_Q), lambda b, j: (b, 0, 0)),
                     pl.BlockSpec((1, v.shape[1], DA_VEXT), lambda b, j: (b, 0, 0))]
        kv_args += [k, v]
    sg = lw["da_sg"] * (1.0 - lambda_init)
    return pl.pallas_call(
        functools.partial(_da_kernel, n_seg=len(kvs), seg_len=seg_len, tq=tq,
                          lambda_init=lambda_init),
        out_shape=jax.ShapeDtypeStruct((bsz, t_q, DA_V), BF16),
        grid=(bsz, t_q // tq),
        in_specs=[pl.BlockSpec((1, tq, DA_Q), lambda b, j: (b, j, 0))] + kv_specs
                 + [_const_spec((2 * DA_HEADS, LANES)), _const_spec((4, DA_QK_DIM)),
                    _const_spec((1, LANES))],
        out_specs=pl.BlockSpec((1, tq, DA_V), lambda b, j: (b, j, 0)),
        compiler_params=_cparams(2),
        name="diff_attention",
    )(q, *kv_args, lw["da_hmask"], lw["da_lamv"], sg)


def _sw_kernel(*refs, has_lat, tq, t_len):
    if has_lat:
        sink_ref, q_ref, kc_ref, vc_ref, k_ref, v_ref, hm_ref, o_ref = refs
    else:
        sink_ref, q_ref, kc_ref, vc_ref, hm_ref, o_ref = refs
    kc, vc = kc_ref[0], vc_ref[0]
    if has_lat:
        band = tq + 2 * WINDOW
        q0 = pl.program_id(1) * tq
        start = pl.multiple_of(jnp.clip(q0 - WINDOW, 0, t_len - band), WINDOW)
        kb = k_ref[0, pl.ds(start, band), :]
        vb = v_ref[0, pl.ds(start, band), :]
        qpos = q0 + lax.broadcasted_iota(jnp.int32, (tq, 1), 0)
        kpos = start + lax.broadcasted_iota(jnp.int32, (1, band), 1)
        valid = jnp.abs(qpos - kpos) <= WINDOW
    left = lax.broadcasted_iota(jnp.int32, (1, LANES), 1) < HEAD_DIM
    for g in range(SW_GROUP):
        qg = q_ref[0, :, g * LANES:(g + 1) * LANES]
        outs = []
        for kv in range(SW_KV_HEADS):
            sink = sink_ref[kv * SW_GROUP + g]
            qm = qg * hm_ref[kv:kv + 1, :]
            s_ctx = _nt_dot(qm, kc)
            m = jnp.maximum(jnp.max(s_ctx, axis=1, keepdims=True), sink)
            if has_lat:
                s_lat = jnp.where(valid, _nt_dot(qm, kb), NEG_INF)
                m = jnp.maximum(m, jnp.max(s_lat, axis=1, keepdims=True))
            p_ctx = jnp.exp(s_ctx - m)
            l = jnp.sum(p_ctx, axis=1, keepdims=True) + jnp.exp(sink - m)
            o = _dot(p_ctx.astype(BF16), vc)
            if has_lat:
                p_lat = jnp.exp(s_lat - m)
                l = l + jnp.sum(p_lat, axis=1, keepdims=True)
                o = o + _dot(p_lat.astype(BF16), vb)
            outs.append(o / l)
        o_ref[0, :, g * LANES:(g + 1) * LANES] = jnp.where(left, outs[0], outs[1]).astype(o_ref.dtype)


def _window_attention(q, kc, vc, k, v, lw):
    bsz, t_q, _ = q.shape
    has_lat = k is not None
    tq = min(SW_TQ, t_q)
    n_ctx = kc.shape[1]
    full = lambda t: pl.BlockSpec((1, t, SW_KV), lambda b, j: (b, 0, 0))
    specs = [pl.BlockSpec(memory_space=pltpu.SMEM),
             pl.BlockSpec((1, tq, SW_Q), lambda b, j: (b, j, 0)), full(n_ctx), full(n_ctx)]
    args = [lw["sw_sink"], q, kc, vc]
    if has_lat:
        specs += [full(t_q), full(t_q)]
        args += [k, v]
    specs.append(_const_spec((SW_KV_HEADS, LANES)))
    args.append(lw["sw_hmask"])
    return pl.pallas_call(
        functools.partial(_sw_kernel, has_lat=has_lat, tq=tq, t_len=t_q),
        out_shape=jax.ShapeDtypeStruct((bsz, t_q, SW_Q), BF16),
        grid=(bsz, t_q // tq),
        in_specs=specs,
        out_specs=pl.BlockSpec((1, tq, SW_Q), lambda b, j: (b, j, 0)),
        compiler_params=_cparams(2),
        name="window_attention",
    )(*args)


def _lru_kernel(xf_ref, xb_ref, xc_ref, cw_ref, cb_ref, wa_ref, ba_ref, wx_ref, bx_ref, lam_ref,
                hf_ref, hb_ref, hfc_ref, hbc_ref, halo_ref, carry_ref, *, tc):
    j = pl.program_id(1)
    is_ctx = j == 0
    row = lax.broadcasted_iota(jnp.int32, (tc, 1), 0)
    row8 = lax.broadcasted_iota(jnp.int32, (SUBLANES, 1), 0)

    for d, x_lat_ref, out_ref, outc_ref in ((0, xf_ref, hf_ref, hfc_ref), (1, xb_ref, hb_ref, hbc_ref)):
        x = jnp.where(is_ctx, xc_ref[0], x_lat_ref[0])
        halo = jnp.where(j <= 1, 0.0, halo_ref[d])
        cw = cw_ref[d]
        xc = cb_ref[d] + x * cw[LRU_CONV - 1:LRU_CONV]
        for k in range(LRU_CONV - 1):
            s = LRU_CONV - 1 - k
            if d == 0:
                xs = pltpu.roll(x, s, 0)
                edge = jnp.where(row8 < s, pltpu.roll(halo, s, 0), xs[:SUBLANES])
                xs = jnp.concatenate([edge, xs[SUBLANES:]], axis=0)
            else:
                xs = pltpu.roll(x, tc - s, 0)
                edge = jnp.where(row8 >= SUBLANES - s, pltpu.roll(halo, SUBLANES - s, 0),
                                 xs[tc - SUBLANES:])
                xs = jnp.concatenate([xs[:tc - SUBLANES], edge], axis=0)
            xc = xc + xs * cw[k:k + 1]
        halo_ref[d] = x[tc - SUBLANES:] if d == 0 else x[:SUBLANES]

        xcb = xc.astype(BF16)
        r = jax.nn.sigmoid(_dot(xcb, wa_ref[d]) + ba_ref[d])
        gi = jax.nn.sigmoid(_dot(xcb, wx_ref[d]) + bx_ref[d])
        nl = -lam_ref[d]
        softplus = jnp.maximum(nl, 0.0) + jnp.log1p(jnp.exp(-jnp.abs(nl)))
        log_a = -LRU_C * r * softplus
        a = jnp.exp(log_a)
        b = jnp.sqrt(-jnp.tanh(log_a) * (a * a + 1.0)) * (gi * xc)

        step = 1
        while step < tc:
            if d == 0:
                keep = row < step
                a_s = jnp.where(keep, 1.0, pltpu.roll(a, step, 0))
                b_s = jnp.where(keep, 0.0, pltpu.roll(b, step, 0))
            else:
                keep = row >= tc - step
                a_s = jnp.where(keep, 1.0, pltpu.roll(a, tc - step, 0))
                b_s = jnp.where(keep, 0.0, pltpu.roll(b, tc - step, 0))
            b = a * b_s + b
            a = a * a_s
            step *= 2

        edge_row = SUBLANES - 1 if d == 0 else 0
        h0 = jnp.where(is_ctx, 0.0, carry_ref[d, edge_row:edge_row + 1, :])
        h = a * h0 + b
        carry_ref[d] = h[tc - SUBLANES:] if d == 0 else h[:SUBLANES]

        @pl.when(is_ctx)
        def _():
            outc_ref[0] = h

        @pl.when(jnp.logical_not(is_ctx))
        def _():
            out_ref[0] = h


def _rglru_scans(cx, cxc, lw):
    bsz, t_len, n = cx.shape
    tc = LRU_TC
    assert cxc.shape[1] == tc and t_len % tc == 0
    n_lat = t_len // tc
    fwd = lambda b, j: (b, jnp.maximum(j - 1, 0), 0)
    bwd = lambda b, j: (b, n_lat - jnp.maximum(j, 1), 0)
    ctx = lambda b, j: (b, 0, 0)
    blk = lambda m: pl.BlockSpec((1, tc, n), m)
    return pl.pallas_call(
        functools.partial(_lru_kernel, tc=tc),
        out_shape=(jax.ShapeDtypeStruct(cx.shape, F32), jax.ShapeDtypeStruct(cx.shape, F32),
                   jax.ShapeDtypeStruct(cxc.shape, F32), jax.ShapeDtypeStruct(cxc.shape, F32)),
        grid=(bsz, n_lat + 1),
        in_specs=[blk(fwd), blk(bwd), blk(ctx),
                  _const_spec((2, LRU_CONV, n)), _const_spec((2, 1, n)),
                  _const_spec((2, n, n)), _const_spec((2, 1, n)),
                  _const_spec((2, n, n)), _const_spec((2, 1, n)), _const_spec((2, 1, n))],
        out_specs=(blk(fwd), blk(bwd), blk(ctx), blk(ctx)),
        scratch_shapes=[pltpu.VMEM((2, SUBLANES, n), F32), pltpu.VMEM((2, SUBLANES, n), F32)],
        compiler_params=_cparams(2),
        name="rglru_scans",
    )(cx, cx, cxc, lw["lru_conv_w"], lw["lru_conv_b"], lw["lru_wa"], lw["lru_ba"],
      lw["lru_wx"], lw["lru_bx"], lw["lru_lambda"])


def _gelu_tanh(x):
    return 0.5 * x * (1.0 + jnp.tanh(math.sqrt(2.0 / math.pi) * (x + 0.044715 * (x * x * x))))


def _outproj_kernel(x_ref, ya_ref, yb_ref, hf_ref, hb_ref, cg_ref, g1_ref, w_ref, o_ref, cat_ref):
    cat_ref[:, 0:DA_V] = ya_ref[0]
    cat_ref[:, DA_V:DA_V + SW_Q] = yb_ref[0]
    yc = (hf_ref[0] + hb_ref[0]) * _gelu_tanh(cg_ref[0])
    cat_ref[:, DA_V + SW_Q:] = yc.astype(BF16)
    o_ref[0] = x_ref[0] + g1_ref[0] * _dot(cat_ref[...], w_ref[...])


def _output_projection(x, ya, yb, hf, hb, cg, gate, lw):
    bsz, t_len, d = x.shape
    tm = min(TM, t_len)
    per_b = gate.shape[0] > 1
    mod_map = (lambda b, j: (b, 0, 0)) if per_b else (lambda b, j: (0, 0, 0))
    tok = lambda n: pl.BlockSpec((1, tm, n), lambda b, j: (b, j, 0))
    return pl.pallas_call(
        _outproj_kernel,
        out_shape=jax.ShapeDtypeStruct(x.shape, F32),
        grid=(bsz, t_len // tm),
        in_specs=[tok(d), tok(DA_V), tok(SW_Q), tok(LRU_WIDTH), tok(LRU_WIDTH), tok(LRU_WIDTH),
                  pl.BlockSpec((1, 1, d), mod_map), _const_spec((d, d))],
        out_specs=tok(d),
        scratch_shapes=[pltpu.VMEM((tm, d), BF16)],
        compiler_params=_cparams(2),
        name="output_projection",
    )(x, ya, yb, hf, hb, cg, gate, lw["w_out"])


def _ffn_kernel(xm_ref, xp_ref, xn_ref, sh_ref, sc_ref, g2_ref, ng_ref, wup_ref, cw_ref, cb_ref,
                wdn_ref, o_ref, h_ref, *, tm):
    j = pl.program_id(1)
    last = pl.num_programs(1) - 1
    gain, shift, scale = ng_ref[...], sh_ref[0], sc_ref[0]
    xm = xm_ref[0]
    hp = jnp.where(j > 0, _norm_modulate(xp_ref[0], gain, shift, scale), 0.0)
    hn = jnp.where(j < last, _norm_modulate(xn_ref[0], gain, shift, scale), 0.0)
    h_ref[0:FFN_HALO] = hp.astype(BF16)
    h_ref[FFN_HALO:FFN_HALO + tm] = _norm_modulate(xm, gain, shift, scale).astype(BF16)
    h_ref[FFN_HALO + tm:] = hn.astype(BF16)

    rows = tm + 2 * FFN_HALO
    acc = jnp.zeros((tm, D_MODEL), F32)
    for lo, hi in FFN_CHUNKS:
        gp = _dot(h_ref[...], wup_ref[:, lo:hi])
        g_prev = pltpu.roll(gp, 1, 0)[FFN_HALO:FFN_HALO + tm]
        g_next = pltpu.roll(gp, rows - 1, 0)[FFN_HALO:FFN_HALO + tm]
        cw = cw_ref[:, lo:hi]
        gate = (cb_ref[:, lo:hi] + g_prev * cw[0:1] + gp[FFN_HALO:FFN_HALO + tm] * cw[1:2]
                + g_next * cw[2:3])
        val = _dot(h_ref[FFN_HALO:FFN_HALO + tm], wup_ref[:, D_FF + lo:D_FF + hi])
        act = (gate * jax.nn.sigmoid(gate) * val).astype(BF16)
        acc = acc + _dot(act, wdn_ref[lo:hi, :])
    o_ref[0] = xm + g2_ref[0] * acc


def _conv_ffn(x, shift, scale, gate, lw):
    bsz, t_len, d = x.shape
    tm = min(TM, t_len)
    per_b = gate.shape[0] > 1
    mod_map = (lambda b, j: (b, 0, 0)) if per_b else (lambda b, j: (0, 0, 0))
    hb = tm // FFN_HALO
    n_hb = t_len // FFN_HALO
    prev_map = lambda b, j: (b, jnp.maximum(j * hb - 1, 0), 0)
    next_map = lambda b, j: (b, jnp.minimum((j + 1) * hb, n_hb - 1), 0)
    mod_spec = pl.BlockSpec((1, 1, d), mod_map)
    return pl.pallas_call(
        functools.partial(_ffn_kernel, tm=tm),
        out_shape=jax.ShapeDtypeStruct(x.shape, F32),
        grid=(bsz, t_len // tm),
        in_specs=[pl.BlockSpec((1, tm, d), lambda b, j: (b, j, 0)),
                  pl.BlockSpec((1, FFN_HALO, d), prev_map),
                  pl.BlockSpec((1, FFN_HALO, d), next_map),
                  mod_spec, mod_spec, mod_spec,
                  _const_spec((1, d)), _const_spec((d, 2 * D_FF)),
                  _const_spec((FFN_CONV, D_FF)), _const_spec((1, D_FF)), _const_spec((D_FF, d))],
        out_specs=pl.BlockSpec((1, tm, d), lambda b, j: (b, j, 0)),
        scratch_shapes=[pltpu.VMEM((tm + 2 * FFN_HALO, d), BF16)],
        compiler_params=_cparams(2),
        name="conv_ffn",
    )(x, x, x, shift, scale, gate, lw["norm2_gain"], lw["w_up"], lw["ffn_conv_w"],
      lw["ffn_conv_b"], lw["w_down"])


def _sw_head_order():
    return [kv * SW_GROUP + g for g in range(SW_GROUP) for kv in range(SW_KV_HEADS)]


def _rope_tables(rows, head_dim, n_heads):
    row = jnp.repeat(jnp.arange(rows, dtype=F32), GRID_W)
    col = jnp.tile(jnp.arange(GRID_W, dtype=F32), rows)
    quarter = head_dim // 4
    inv_freq = ROPE_BASE ** (-jnp.arange(quarter, dtype=F32) / quarter)
    ang = jnp.concatenate([row[:, None] * inv_freq, col[:, None] * inv_freq], axis=-1)
    cos, sin = jnp.cos(ang), jnp.sin(ang)
    return (jnp.tile(jnp.concatenate([cos, cos], axis=-1), (1, n_heads)),
            jnp.tile(jnp.concatenate([-sin, sin], axis=-1), (1, n_heads)))


def _block_diag_mean(n, group):
    idx = np.arange(n) // group
    return jnp.asarray((idx[:, None] == idx[None, :]).astype(np.float32) / group, dtype=BF16)


def _block_diag(w):
    two, nb, bi, bj = w.shape
    eye = jnp.eye(nb, dtype=w.dtype)
    return jnp.einsum("dhij,hg->dhigj", w, eye).reshape(two, nb * bi, nb * bj)


def _prepare_layer(p, i):
    d = D_MODEL
    split = np.cumsum((DA_Q, DA_Q, DA_V, SW_Q, SW_KV, SW_KV, LRU_WIDTH))
    w_in = p["w_in"][i]
    w_aq, w_ak, w_av, w_bq, w_bk, w_bv, w_cx, w_cg = jnp.split(w_in, split.tolist(), axis=1)
    w_av = jnp.pad(w_av.reshape(d, DA_HEADS, DA_V_DIM), ((0, 0), (0, 0), (0, LANES - DA_V_DIM)))
    order = _sw_head_order()
    w_bq = w_bq.reshape(d, SW_HEADS, HEAD_DIM)[:, order].reshape(d, SW_Q)
    w_inx = jnp.concatenate([w_aq, w_ak, w_av.reshape(d, DA_VEXT), w_bq, w_bk, w_bv, w_cx, w_cg],
                            axis=1).astype(BF16)
    v_one = np.zeros((1, DA_VEXT), np.float32)
    v_one[0, DA_V_DIM::LANES] = 1.0

    w_out = p["w_out"][i]
    w_ob = w_out[DA_V:DA_V + SW_Q].reshape(SW_HEADS, HEAD_DIM, d)[jnp.asarray(order)]
    w_out = jnp.concatenate([w_out[:DA_V], w_ob.reshape(SW_Q, d), w_out[DA_V + SW_Q:]], axis=0)

    da_hmask = np.zeros((2 * DA_HEADS, LANES), np.float32)
    for j in range(2 * DA_HEADS):
        off = (j % 4) * DA_QK_DIM
        da_hmask[j, off:off + DA_QK_DIM] = 1.0
    sw_hmask = np.zeros((SW_KV_HEADS, LANES), np.float32)
    for kv in range(SW_KV_HEADS):
        sw_hmask[kv, kv * HEAD_DIM:(kv + 1) * HEAD_DIM] = 1.0

    return dict(
        norm1_gain=p["norm1_gain"][i].reshape(1, d), norm2_gain=p["norm2_gain"][i].reshape(1, d),
        w_in=w_inx, v_one=jnp.asarray(v_one),
        gq_a=jnp.tile(p["da_q_gain"][i], 2 * DA_HEADS).reshape(1, DA_Q) * (DA_QK_DIM ** -0.5),
        gk_a=jnp.tile(p["da_k_gain"][i], 2 * DA_HEADS).reshape(1, DA_Q),
        gq_b=jnp.tile(p["sw_q_gain"][i], SW_HEADS).reshape(1, SW_Q) * (HEAD_DIM ** -0.5),
        gk_b=jnp.tile(p["sw_k_gain"][i], SW_KV_HEADS).reshape(1, SW_KV),
        g32=_block_diag_mean(DA_Q, DA_QK_DIM), g64=_block_diag_mean(SW_Q, HEAD_DIM),
        da_hmask=jnp.asarray(da_hmask, dtype=BF16),
        da_lamv=jnp.stack([p["da_lam_q1"][i], p["da_lam_k1"][i], p["da_lam_q2"][i], p["da_lam_k2"][i]]),
        da_sg=jnp.pad(p["da_sub_gain"][i], (0, LANES - DA_V_DIM)).reshape(1, LANES),
        sw_sink=p["sw_sink"][i], sw_hmask=jnp.asarray(sw_hmask, dtype=BF16),
        lru_conv_w=p["lru_conv_w"][i], lru_conv_b=p["lru_conv_b"][i].reshape(2, 1, LRU_WIDTH),
        lru_wa=_block_diag(p["lru_wa"][i]).astype(BF16), lru_ba=p["lru_ba"][i].reshape(2, 1, LRU_WIDTH),
        lru_wx=_block_diag(p["lru_wx"][i]).astype(BF16), lru_bx=p["lru_bx"][i].reshape(2, 1, LRU_WIDTH),
        lru_lambda=p["lru_lambda"][i].reshape(2, 1, LRU_WIDTH),
        w_out=w_out.astype(BF16), w_up=p["w_up"][i].astype(BF16),
        ffn_conv_w=p["ffn_conv_w"][i], ffn_conv_b=p["ffn_conv_b"][i].reshape(1, D_FF),
        w_down=p["w_down"][i].astype(BF16),
    )


def kernel(x, c, ctx, c_ctx, w_mod, b_mod, norm1_gain, norm2_gain, w_in, da_q_gain, da_k_gain, da_lam_q1, da_lam_k1, da_lam_q2, da_lam_k2, da_sub_gain, sw_q_gain, sw_k_gain, sw_sink, lru_conv_w, lru_conv_b, lru_wa, lru_ba, lru_wx, lru_bx, lru_lambda, w_out, w_up, ffn_conv_w, ffn_conv_b, w_down):
    p = dict(w_in=w_in, norm1_gain=norm1_gain, norm2_gain=norm2_gain, da_q_gain=da_q_gain,
             da_k_gain=da_k_gain, da_lam_q1=da_lam_q1, da_lam_k1=da_lam_k1, da_lam_q2=da_lam_q2,
             da_lam_k2=da_lam_k2, da_sub_gain=da_sub_gain, sw_q_gain=sw_q_gain, sw_k_gain=sw_k_gain,
             sw_sink=sw_sink, lru_conv_w=lru_conv_w, lru_conv_b=lru_conv_b, lru_wa=lru_wa,
             lru_ba=lru_ba, lru_wx=lru_wx, lru_bx=lru_bx, lru_lambda=lru_lambda, w_out=w_out,
             w_up=w_up, ffn_conv_w=ffn_conv_w, ffn_conv_b=ffn_conv_b, w_down=w_down)
    bsz, n_tok, d = x.shape
    n_ctx = ctx.shape[1]
    depth = w_mod.shape[0]

    cc = jnp.zeros((2 * SUBLANES, d), F32).at[:bsz].set(c).at[bsz].set(c_ctx)
    mod_all = _modulation(cc, w_mod, b_mod)

    rope_lat = (_rope_tables(n_tok // GRID_W, DA_QK_DIM, 2 * DA_HEADS)
                + _rope_tables(n_tok // GRID_W, HEAD_DIM, SW_HEADS))
    rope_ctx = (jnp.ones((n_ctx, DA_Q), F32), jnp.zeros((n_ctx, DA_Q), F32),
                jnp.ones((n_ctx, SW_Q), F32), jnp.zeros((n_ctx, SW_Q), F32))

    xc = ctx
    for i in range(depth):
        lw = _prepare_layer(p, i)
        lambda_init = 0.8 - 0.6 * math.exp(-0.3 * i)
        ctx_out = i < depth - 1
        mod = mod_all[i, :bsz].reshape(bsz, 1, N_MOD, d)
        sh1, sc1, g1, sh2, sc2, g2 = [mod[:, :, m] for m in range(N_MOD)]
        mod_c = mod_all[i, bsz].reshape(1, 1, N_MOD, d)
        sh1c, sc1c, g1c, sh2c, sc2c, g2c = [mod_c[:, :, m] for m in range(N_MOD)]

        qa, ka, va, qb, kb, vb, cx, cg = _input_projection(x, sh1, sc1, lw, rope_lat)
        qac, kac, vac, qbc, kbc, vbc, cxc, cgc = _input_projection(xc, sh1c, sc1c, lw, rope_ctx)

        ya = _diff_attention(qa, [(ka, va), (kac, vac)], lw, lambda_init)
        yb = _window_attention(qb, kbc, vbc, kb, vb, lw)
        hf, hb, hfc, hbc = _rglru_scans(cx, cxc, lw)

        x = _output_projection(x, ya, yb, hf, hb, cg, g1, lw)
        x = _conv_ffn(x, sh2, sc2, g2, lw)
        if ctx_out:
            yac = _diff_attention(qac, [(kac, vac)], lw, lambda_init)
            ybc = _window_attention(qbc, kbc, vbc, None, None, lw)
            xc = _output_projection(xc, yac, ybc, hfc, hbc, cgc, g1c, lw)
            xc = _conv_ffn(xc, sh2c, sc2c, g2c, lw)
    return x
```

```python
import functools
import math

import numpy as np
import jax
import jax.numpy as jnp
from jax import lax
from jax.experimental import pallas as pl
from jax.experimental.pallas import tpu as pltpu

F32 = jnp.float32
BF16 = jnp.bfloat16

D_MODEL = 1024
GRID_W = 64
N_MOD = 6
EPS = 1e-6
NEG_INF = -1e30
ROPE_BASE = 10000.0
DA_HEADS = 4
DA_QK_DIM = 32
DA_V_DIM = 64
HEAD_DIM = 64
SW_HEADS = 6
SW_KV_HEADS = 2
SW_GROUP = SW_HEADS // SW_KV_HEADS
WINDOW = 128
LRU_WIDTH = 384
LRU_BLOCKS = 6
LRU_BLOCK_DIM = LRU_WIDTH // LRU_BLOCKS
LRU_CONV = 4
LRU_C = 8.0
D_FF = 2816
FFN_CONV = 3

DA_Q = DA_HEADS * 2 * DA_QK_DIM
DA_V = DA_HEADS * DA_V_DIM
SW_Q = SW_HEADS * HEAD_DIM
SW_KV = SW_KV_HEADS * HEAD_DIM

LANES = 128
SUBLANES = 8
BF16_ROWS = 16
VMEM_LIMIT = 56 * 1024 * 1024

DA_VEXT = DA_HEADS * LANES
C_AQ = 0
C_AK = C_AQ + DA_Q
C_AV = C_AK + DA_Q
C_BQ = C_AV + DA_VEXT
C_BK = C_BQ + SW_Q
C_BV = C_BK + SW_KV
C_CX = C_BV + SW_KV
C_CG = C_CX + LRU_WIDTH
D_INX = C_CG + LRU_WIDTH

TM = 512
FFN_HALO = BF16_ROWS
FFN_CHUNKS = ((0, 1024), (1024, 2048), (2048, D_FF))
DA_TQ = 256
SW_TQ = 256
LRU_TC = 256


def _cparams(n_axes):
    return pltpu.CompilerParams(dimension_semantics=("arbitrary",) * n_axes,
                                vmem_limit_bytes=VMEM_LIMIT)


def _const_spec(shape):
    nd = len(shape)
    return pl.BlockSpec(shape, lambda *_: (0,) * nd, pipeline_mode=pl.Buffered(1))


def _nt_dot(a, b):
    return lax.dot_general(a, b, (((1,), (1,)), ((), ())), preferred_element_type=F32)


def _dot(a, b):
    return jnp.dot(a, b, preferred_element_type=F32)


def _mod_kernel(c_ref, w_ref, b_ref, o_ref):
    c = c_ref[...]
    s = c * jax.nn.sigmoid(c)
    o_ref[0] = _dot(s.astype(BF16), w_ref[0].astype(BF16)) + b_ref[0]


def _modulation(cc, w_mod, b_mod):
    depth, d, n = w_mod.shape
    tn = 1536
    return pl.pallas_call(
        _mod_kernel,
        out_shape=jax.ShapeDtypeStruct((depth, cc.shape[0], n), F32),
        grid=(depth, n // tn),
        in_specs=[pl.BlockSpec(cc.shape, lambda i, j: (0, 0)),
                  pl.BlockSpec((1, d, tn), lambda i, j: (i, 0, j)),
                  pl.BlockSpec((1, 1, tn), lambda i, j: (i, 0, j))],
        out_specs=pl.BlockSpec((1, cc.shape[0], tn), lambda i, j: (i, 0, j)),
        compiler_params=_cparams(2),
        name="modulation",
    )(cc, w_mod, b_mod.reshape(depth, 1, n))


def _norm_modulate(x, gain, shift, scale):
    ms = jnp.mean(x * x, axis=-1, keepdims=True)
    h = x * lax.rsqrt(ms + EPS) * gain
    return h * (1.0 + scale) + shift


def _group_rms(a, g_ref):
    sq = a * a
    hi = sq.astype(BF16)
    lo = (sq - hi.astype(F32)).astype(BF16)
    g = g_ref[...]
    ms = _dot(hi, g) + _dot(lo, g)
    return a * lax.rsqrt(ms + EPS)


def _rope(y, cos, sin_signed, half):
    n = y.shape[1]
    lane = lax.broadcasted_iota(jnp.int32, y.shape, 1)
    first = (lane % (2 * half)) < half
    partner = jnp.where(first, pltpu.roll(y, n - half, 1), pltpu.roll(y, half, 1))
    return y * cos + partner * sin_signed


def _inproj_kernel(x_ref, sh_ref, sc_ref, ng_ref, w_ref, vone_ref,
                   cosa_ref, sina_ref, cosb_ref, sinb_ref,
                   gqa_ref, gka_ref, gqb_ref, gkb_ref, g32_ref, g64_ref,
                   qa_ref, ka_ref, va_ref, qb_ref, kb_ref, vb_ref, cx_ref, cg_ref):
    h = _norm_modulate(x_ref[0], ng_ref[...], sh_ref[0], sc_ref[0]).astype(BF16)

    def proj(lo, hi):
        return _dot(h, w_ref[:, lo:hi])

    cosa, sina = cosa_ref[...], sina_ref[...]
    aq = _group_rms(proj(C_AQ, C_AK), g32_ref) * gqa_ref[...]
    qa_ref[0] = _rope(aq, cosa, sina, DA_QK_DIM // 2).astype(BF16)
    ak = _group_rms(proj(C_AK, C_AV), g32_ref) * gka_ref[...]
    ka_ref[0] = _rope(ak, cosa, sina, DA_QK_DIM // 2).astype(BF16)
    va_ref[0, 0] = (proj(C_AV, C_BQ) + vone_ref[...]).T.astype(BF16)

    cosb, sinb = cosb_ref[...], sinb_ref[...]
    bq = _group_rms(proj(C_BQ, C_BK), g64_ref) * gqb_ref[...]
    qb_ref[0] = _rope(bq, cosb, sinb, HEAD_DIM // 2).astype(BF16)
    bkv = proj(C_BK, C_CX)
    bk = _group_rms(bkv[:, :SW_KV], g64_ref.at[:SW_KV, :SW_KV]) * gkb_ref[...]
    kb_ref[0] = _rope(bk, cosb[:, :SW_KV], sinb[:, :SW_KV], HEAD_DIM // 2).astype(BF16)
    vb_ref[0] = bkv[:, SW_KV:].astype(BF16)

    cxg = proj(C_CX, D_INX)
    cx_ref[0] = cxg[:, :LRU_WIDTH]
    cg_ref[0] = cxg[:, LRU_WIDTH:]


def _input_projection(x, shift, scale, lw, rope):
    bsz, t_len, d = x.shape
    tm = min(TM, t_len)
    per_b = shift.shape[0] > 1
    mod_map = (lambda b, j: (b, 0, 0)) if per_b else (lambda b, j: (0, 0, 0))
    tok = lambda n: pl.BlockSpec((1, tm, n), lambda b, j: (b, j, 0))
    tab = lambda n: pl.BlockSpec((tm, n), lambda b, j: (j, 0))
    widths = (DA_Q, DA_Q, DA_VEXT, SW_Q, SW_KV, SW_KV, LRU_WIDTH, LRU_WIDTH)
    dtypes = (BF16,) * 6 + (F32, F32)
    out_shape = [jax.ShapeDtypeStruct((bsz, t_len, n), dt) for n, dt in zip(widths, dtypes)]
    out_specs = [tok(n) for n in widths]
    out_shape[2] = jax.ShapeDtypeStruct((bsz, t_len // tm, DA_VEXT, tm), BF16)
    out_specs[2] = pl.BlockSpec((1, 1, DA_VEXT, tm), lambda b, j: (b, j, 0, 0))
    return pl.pallas_call(
        _inproj_kernel,
        out_shape=tuple(out_shape),
        grid=(bsz, t_len // tm),
        in_specs=[tok(d),
                  pl.BlockSpec((1, 1, d), mod_map), pl.BlockSpec((1, 1, d), mod_map),
                  _const_spec((1, d)), _const_spec((d, D_INX)), _const_spec((1, DA_VEXT)),
                  tab(DA_Q), tab(DA_Q), tab(SW_Q), tab(SW_Q),
                  _const_spec((1, DA_Q)), _const_spec((1, DA_Q)),
                  _const_spec((1, SW_Q)), _const_spec((1, SW_KV)),
                  _const_spec((DA_Q, DA_Q)), _const_spec((SW_Q, SW_Q))],
        out_specs=tuple(out_specs),
        compiler_params=_cparams(2),
        name="input_projection",
    )(x, shift, scale, lw["norm1_gain"], lw["w_in"], lw["v_one"], *rope,
      lw["gq_a"], lw["gk_a"], lw["gq_b"], lw["gk_b"], lw["g32"], lw["g64"])


def _da_kernel(*refs, n_seg, lambda_init):
    q_ref = refs[0]
    kv_refs = refs[1:1 + 2 * n_seg]
    hmask_ref, lamv_ref, sg_ref, o_ref, m_ref, acc_ref = refs[1 + 2 * n_seg:]
    n_map = 2 * DA_HEADS

    m_ref[...] = jnp.full(m_ref.shape, NEG_INF, F32)
    acc_ref[...] = jnp.zeros(acc_ref.shape, F32)

    def tile(k_ref, v_ref, i):
        tk = v_ref.shape[3]
        start = i * tk if isinstance(i, int) else pl.multiple_of(i * tk, tk)

        def scores(j):
            half = j // 4
            qm = q_ref[0, :, half * LANES:(half + 1) * LANES] * hmask_ref[j:j + 1, :]
            kblk = k_ref[0, pl.ds(start, tk), half * LANES:(half + 1) * LANES]
            return _nt_dot(kblk, qm)

        s_next = scores(0)
        for j in range(n_map):
            s_t = s_next
            if j + 1 < n_map:
                s_next = scores(j + 1)
            m_old = m_ref[j, 0:1, :]
            m_new = jnp.maximum(m_old, jnp.max(s_t, axis=0, keepdims=True))
            p_t = jnp.exp2(s_t - m_new).astype(BF16)
            pv = _dot(v_ref[0, i, (j // 2) * LANES:(j // 2 + 1) * LANES, :], p_t)
            acc_ref[j] = jnp.exp2(m_old - m_new) * acc_ref[j] + pv
            m_ref[j] = jnp.broadcast_to(m_new, m_ref.shape[1:])

    for si in range(n_seg):
        k_ref, v_ref = kv_refs[2 * si], kv_refs[2 * si + 1]
        n_t = v_ref.shape[1]
        if n_t == 1:
            tile(k_ref, v_ref, 0)
        else:
            def body(i, carry, k_ref=k_ref, v_ref=v_ref):
                tile(k_ref, v_ref, i)
                return carry
            lax.fori_loop(0, n_t, body, 0)

    lv = lamv_ref[...]
    lam = (jnp.exp(jnp.sum(lv[0:1] * lv[1:2], axis=1, keepdims=True))
           - jnp.exp(jnp.sum(lv[2:3] * lv[3:4], axis=1, keepdims=True)) + lambda_init)

    def head_out(h):
        a1, a2 = acc_ref[2 * h], acc_ref[2 * h + 1]
        o = (a1 / a1[DA_V_DIM:DA_V_DIM + 1] - lam * (a2 / a2[DA_V_DIM:DA_V_DIM + 1]))[:DA_V_DIM]
        ms = jnp.sum(o * o, axis=0, keepdims=True) * (1.0 / DA_V_DIM)
        return o * lax.rsqrt(ms + EPS)

    for c in range(DA_HEADS // 2):
        pair_t = jnp.concatenate([head_out(2 * c), head_out(2 * c + 1)], axis=0)
        o_ref[0, :, c * LANES:(c + 1) * LANES] = (pair_t.T * sg_ref[...]).astype(o_ref.dtype)


def _diff_attention(q, kvs, lw, lambda_init):
    bsz, t_q, _ = q.shape
    tq = min(DA_TQ, t_q)
    kv_specs, kv_args = [], []
    for k, v in kvs:
        kv_specs += [pl.BlockSpec((1,) + k.shape[1:], lambda b, j: (b, 0, 0)),
                     pl.BlockSpec((1,) + v.shape[1:], lambda b, j: (b, 0, 0, 0))]
        kv_args += [k, v]
    sg = lw["da_sg"] * (1.0 - lambda_init)
    return pl.pallas_call(
        functools.partial(_da_kernel, n_seg=len(kvs), lambda_init=lambda_init),
        out_shape=jax.ShapeDtypeStruct((bsz, t_q, DA_V), BF16),
        grid=(bsz, t_q // tq),
        in_specs=[pl.BlockSpec((1, tq, DA_Q), lambda b, j: (b, j, 0))] + kv_specs
                 + [_const_spec((2 * DA_HEADS, LANES)), _const_spec((4, DA_QK_DIM)),
                    _const_spec((1, LANES))],
        out_specs=pl.BlockSpec((1, tq, DA_V), lambda b, j: (b, j, 0)),
        scratch_shapes=[pltpu.VMEM((2 * DA_HEADS, SUBLANES, tq), F32),
                        pltpu.VMEM((2 * DA_HEADS, LANES, tq), F32)],
        compiler_params=_cparams(2),
        name="diff_attention",
    )(q, *kv_args, lw["da_hmask"], lw["da_lamv"], sg)


def _sw_kernel(*refs, has_lat, tq, t_len):
    if has_lat:
        sink_ref, q_ref, kc_ref, vc_ref, k_ref, v_ref, hm_ref, o_ref = refs
    else:
        sink_ref, q_ref, kc_ref, vc_ref, hm_ref, o_ref = refs
    kc, vc = kc_ref[0], vc_ref[0]
    if has_lat:
        band = tq + 2 * WINDOW
        q0 = pl.program_id(1) * tq
        start = pl.multiple_of(jnp.clip(q0 - WINDOW, 0, t_len - band), WINDOW)
        kb = k_ref[0, pl.ds(start, band), :]
        vb = v_ref[0, pl.ds(start, band), :]
        qpos = q0 + lax.broadcasted_iota(jnp.int32, (tq, 1), 0)
        kpos = start + lax.broadcasted_iota(jnp.int32, (1, band), 1)
        valid = jnp.abs(qpos - kpos) <= WINDOW
    left = lax.broadcasted_iota(jnp.int32, (1, LANES), 1) < HEAD_DIM
    for g in range(SW_GROUP):
        qg = q_ref[0, :, g * LANES:(g + 1) * LANES]
        outs = []
        for kv in range(SW_KV_HEADS):
            sink = sink_ref[kv * SW_GROUP + g]
            qm = qg * hm_ref[kv:kv + 1, :]
            s_ctx = _nt_dot(qm, kc)
            m = jnp.maximum(jnp.max(s_ctx, axis=1, keepdims=True), sink)
            if has_lat:
                s_lat = jnp.where(valid, _nt_dot(qm, kb), NEG_INF)
                m = jnp.maximum(m, jnp.max(s_lat, axis=1, keepdims=True))
            p_ctx = jnp.exp(s_ctx - m)
            l = jnp.sum(p_ctx, axis=1, keepdims=True) + jnp.exp(sink - m)
            o = _dot(p_ctx.astype(BF16), vc)
            if has_lat:
                p_lat = jnp.exp(s_lat - m)
                l = l + jnp.sum(p_lat, axis=1, keepdims=True)
                o = o + _dot(p_lat.astype(BF16), vb)
            outs.append(o / l)
        o_ref[0, :, g * LANES:(g + 1) * LANES] = jnp.where(left, outs[0], outs[1]).astype(o_ref.dtype)


def _window_attention(q, kc, vc, k, v, lw):
    bsz, t_q, _ = q.shape
    has_lat = k is not None
    tq = min(SW_TQ, t_q)
    n_ctx = kc.shape[1]
    full = lambda t: pl.BlockSpec((1, t, SW_KV), lambda b, j: (b, 0, 0))
    specs = [pl.BlockSpec(memory_space=pltpu.SMEM),
             pl.BlockSpec((1, tq, SW_Q), lambda b, j: (b, j, 0)), full(n_ctx), full(n_ctx)]
    args = [lw["sw_sink"], q, kc, vc]
    if has_lat:
        specs += [full(t_q), full(t_q)]
        args += [k, v]
    specs.append(_const_spec((SW_KV_HEADS, LANES)))
    args.append(lw["sw_hmask"])
    return pl.pallas_call(
        functools.partial(_sw_kernel, has_lat=has_lat, tq=tq, t_len=t_q),
        out_shape=jax.ShapeDtypeStruct((bsz, t_q, SW_Q), BF16),
        grid=(bsz, t_q // tq),
        in_specs=specs,
        out_specs=pl.BlockSpec((1, tq, SW_Q), lambda b, j: (b, j, 0)),
        compiler_params=_cparams(2),
        name="window_attention",
    )(*args)


def _lru_kernel(xf_ref, xb_ref, xc_ref, cw_ref, cb_ref, wa_ref, ba_ref, wx_ref, bx_ref, lam_ref,
                hf_ref, hb_ref, hfc_ref, hbc_ref, halo_ref, carry_ref, *, tc):
    j = pl.program_id(1)
    is_ctx = j == 0
    row = lax.broadcasted_iota(jnp.int32, (tc, 1), 0)
    row8 = lax.broadcasted_iota(jnp.int32, (SUBLANES, 1), 0)

    for d, x_lat_ref, out_ref, outc_ref in ((0, xf_ref, hf_ref, hfc_ref), (1, xb_ref, hb_ref, hbc_ref)):
        x = jnp.where(is_ctx, xc_ref[0], x_lat_ref[0])
        halo = jnp.where(j <= 1, 0.0, halo_ref[d])
        cw = cw_ref[d]
        xc = cb_ref[d] + x * cw[LRU_CONV - 1:LRU_CONV]
        for k in range(LRU_CONV - 1):
            s = LRU_CONV - 1 - k
            if d == 0:
                xs = pltpu.roll(x, s, 0)
                edge = jnp.where(row8 < s, pltpu.roll(halo, s, 0), xs[:SUBLANES])
                xs = jnp.concatenate([edge, xs[SUBLANES:]], axis=0)
            else:
                xs = pltpu.roll(x, tc - s, 0)
                edge = jnp.where(row8 >= SUBLANES - s, pltpu.roll(halo, SUBLANES - s, 0),
                                 xs[tc - SUBLANES:])
                xs = jnp.concatenate([xs[:tc - SUBLANES], edge], axis=0)
            xc = xc + xs * cw[k:k + 1]
        halo_ref[d] = x[tc - SUBLANES:] if d == 0 else x[:SUBLANES]

        xcb = xc.astype(BF16)
        r = jax.nn.sigmoid(_dot(xcb, wa_ref[d]) + ba_ref[d])
        gi = jax.nn.sigmoid(_dot(xcb, wx_ref[d]) + bx_ref[d])
        nl = -lam_ref[d]
        softplus = jnp.maximum(nl, 0.0) + jnp.log1p(jnp.exp(-jnp.abs(nl)))
        log_a = -LRU_C * r * softplus
        a = jnp.exp(log_a)
        b = jnp.sqrt(-jnp.tanh(log_a) * (a * a + 1.0)) * (gi * xc)

        step = 1
        while step < tc:
            if d == 0:
                keep = row < step
                a_s = jnp.where(keep, 1.0, pltpu.roll(a, step, 0))
                b_s = jnp.where(keep, 0.0, pltpu.roll(b, step, 0))
            else:
                keep = row >= tc - step
                a_s = jnp.where(keep, 1.0, pltpu.roll(a, tc - step, 0))
                b_s = jnp.where(keep, 0.0, pltpu.roll(b, tc - step, 0))
            b = a * b_s + b
            a = a * a_s
            step *= 2

        edge_row = SUBLANES - 1 if d == 0 else 0
        h0 = jnp.where(is_ctx, 0.0, carry_ref[d, edge_row:edge_row + 1, :])
        h = a * h0 + b
        carry_ref[d] = h[tc - SUBLANES:] if d == 0 else h[:SUBLANES]

        @pl.when(is_ctx)
        def _():
            outc_ref[0] = h

        @pl.when(jnp.logical_not(is_ctx))
        def _():
            out_ref[0] = h


def _rglru_scans(cx, cxc, lw):
    bsz, t_len, n = cx.shape
    tc = LRU_TC
    assert cxc.shape[1] == tc and t_len % tc == 0
    n_lat = t_len // tc
    fwd = lambda b, j: (b, jnp.maximum(j - 1, 0), 0)
    bwd = lambda b, j: (b, n_lat - jnp.maximum(j, 1), 0)
    ctx = lambda b, j: (b, 0, 0)
    blk = lambda m: pl.BlockSpec((1, tc, n), m)
    return pl.pallas_call(
        functools.partial(_lru_kernel, tc=tc),
        out_shape=(jax.ShapeDtypeStruct(cx.shape, F32), jax.ShapeDtypeStruct(cx.shape, F32),
                   jax.ShapeDtypeStruct(cxc.shape, F32), jax.ShapeDtypeStruct(cxc.shape, F32)),
        grid=(bsz, n_lat + 1),
        in_specs=[blk(fwd), blk(bwd), blk(ctx),
                  _const_spec((2, LRU_CONV, n)), _const_spec((2, 1, n)),
                  _const_spec((2, n, n)), _const_spec((2, 1, n)),
                  _const_spec((2, n, n)), _const_spec((2, 1, n)), _const_spec((2, 1, n))],
        out_specs=(blk(fwd), blk(bwd), blk(ctx), blk(ctx)),
        scratch_shapes=[pltpu.VMEM((2, SUBLANES, n), F32), pltpu.VMEM((2, SUBLANES, n), F32)],
        compiler_params=_cparams(2),
        name="rglru_scans",
    )(cx, cx, cxc, lw["lru_conv_w"], lw["lru_conv_b"], lw["lru_wa"], lw["lru_ba"],
      lw["lru_wx"], lw["lru_bx"], lw["lru_lambda"])


def _gelu_tanh(x):
    return 0.5 * x * (1.0 + jnp.tanh(math.sqrt(2.0 / math.pi) * (x + 0.044715 * (x * x * x))))


def _outproj_kernel(x_ref, ya_ref, yb_ref, hf_ref, hb_ref, cg_ref, g1_ref, w_ref, o_ref, cat_ref):
    cat_ref[:, 0:DA_V] = ya_ref[0]
    cat_ref[:, DA_V:DA_V + SW_Q] = yb_ref[0]
    yc = (hf_ref[0] + hb_ref[0]) * _gelu_tanh(cg_ref[0])
    cat_ref[:, DA_V + SW_Q:] = yc.astype(BF16)
    o_ref[0] = x_ref[0] + g1_ref[0] * _dot(cat_ref[...], w_ref[...])


def _output_projection(x, ya, yb, hf, hb, cg, gate, lw):
    bsz, t_len, d = x.shape
    tm = min(TM, t_len)
    per_b = gate.shape[0] > 1
    mod_map = (lambda b, j: (b, 0, 0)) if per_b else (lambda b, j: (0, 0, 0))
    tok = lambda n: pl.BlockSpec((1, tm, n), lambda b, j: (b, j, 0))
    return pl.pallas_call(
        _outproj_kernel,
        out_shape=jax.ShapeDtypeStruct(x.shape, F32),
        grid=(bsz, t_len // tm),
        in_specs=[tok(d), tok(DA_V), tok(SW_Q), tok(LRU_WIDTH), tok(LRU_WIDTH), tok(LRU_WIDTH),
                  pl.BlockSpec((1, 1, d), mod_map), _const_spec((d, d))],
        out_specs=tok(d),
        scratch_shapes=[pltpu.VMEM((tm, d), BF16)],
        compiler_params=_cparams(2),
        name="output_projection",
    )(x, ya, yb, hf, hb, cg, gate, lw["w_out"])


def _ffn_kernel(xm_ref, xp_ref, xn_ref, sh_ref, sc_ref, g2_ref, ng_ref, wup_ref, cw_ref, cb_ref,
                wdn_ref, o_ref, h_ref, *, tm):
    j = pl.program_id(1)
    last = pl.num_programs(1) - 1
    gain, shift, scale = ng_ref[...], sh_ref[0], sc_ref[0]
    xm = xm_ref[0]
    hp = jnp.where(j > 0, _norm_modulate(xp_ref[0], gain, shift, scale), 0.0)
    hn = jnp.where(j < last, _norm_modulate(xn_ref[0], gain, shift, scale), 0.0)
    h_ref[0:FFN_HALO] = hp.astype(BF16)
    h_ref[FFN_HALO:FFN_HALO + tm] = _norm_modulate(xm, gain, shift, scale).astype(BF16)
    h_ref[FFN_HALO + tm:] = hn.astype(BF16)

    rows = tm + 2 * FFN_HALO
    acc = jnp.zeros((tm, D_MODEL), F32)
    for lo, hi in FFN_CHUNKS:
        gp = _dot(h_ref[...], wup_ref[:, lo:hi])
        g_prev = pltpu.roll(gp, 1, 0)[FFN_HALO:FFN_HALO + tm]
        g_next = pltpu.roll(gp, rows - 1, 0)[FFN_HALO:FFN_HALO + tm]
        cw = cw_ref[:, lo:hi]
        gate = (cb_ref[:, lo:hi] + g_prev * cw[0:1] + gp[FFN_HALO:FFN_HALO + tm] * cw[1:2]
                + g_next * cw[2:3])
        val = _dot(h_ref[FFN_HALO:FFN_HALO + tm], wup_ref[:, D_FF + lo:D_FF + hi])
        act = (gate * jax.nn.sigmoid(gate) * val).astype(BF16)
        acc = acc + _dot(act, wdn_ref[lo:hi, :])
    o_ref[0] = xm + g2_ref[0] * acc


def _conv_ffn(x, shift, scale, gate, lw):
    bsz, t_len, d = x.shape
    tm = min(TM, t_len)
    per_b = gate.shape[0] > 1
    mod_map = (lambda b, j: (b, 0, 0)) if per_b else (lambda b, j: (0, 0, 0))
    hb = tm // FFN_HALO
    n_hb = t_len // FFN_HALO
    prev_map = lambda b, j: (b, jnp.maximum(j * hb - 1, 0), 0)
    next_map = lambda b, j: (b, jnp.minimum((j + 1) * hb, n_hb - 1), 0)
    mod_spec = pl.BlockSpec((1, 1, d), mod_map)
    return pl.pallas_call(
        functools.partial(_ffn_kernel, tm=tm),
        out_shape=jax.ShapeDtypeStruct(x.shape, F32),
        grid=(bsz, t_len // tm),
        in_specs=[pl.BlockSpec((1, tm, d), lambda b, j: (b, j, 0)),
                  pl.BlockSpec((1, FFN_HALO, d), prev_map),
                  pl.BlockSpec((1, FFN_HALO, d), next_map),
                  mod_spec, mod_spec, mod_spec,
                  _const_spec((1, d)), _const_spec((d, 2 * D_FF)),
                  _const_spec((FFN_CONV, D_FF)), _const_spec((1, D_FF)), _const_spec((D_FF, d))],
        out_specs=pl.BlockSpec((1, tm, d), lambda b, j: (b, j, 0)),
        scratch_shapes=[pltpu.VMEM((tm + 2 * FFN_HALO, d), BF16)],
        compiler_params=_cparams(2),
        name="conv_ffn",
    )(x, x, x, shift, scale, gate, lw["norm2_gain"], lw["w_up"], lw["ffn_conv_w"],
      lw["ffn_conv_b"], lw["w_down"])


def _sw_head_order():
    return [kv * SW_GROUP + g for g in range(SW_GROUP) for kv in range(SW_KV_HEADS)]


def _rope_tables(rows, head_dim, n_heads):
    row = jnp.repeat(jnp.arange(rows, dtype=F32), GRID_W)
    col = jnp.tile(jnp.arange(GRID_W, dtype=F32), rows)
    quarter = head_dim // 4
    inv_freq = ROPE_BASE ** (-jnp.arange(quarter, dtype=F32) / quarter)
    ang = jnp.concatenate([row[:, None] * inv_freq, col[:, None] * inv_freq], axis=-1)
    cos, sin = jnp.cos(ang), jnp.sin(ang)
    return (jnp.tile(jnp.concatenate([cos, cos], axis=-1), (1, n_heads)),
            jnp.tile(jnp.concatenate([-sin, sin], axis=-1), (1, n_heads)))


def _block_diag_mean(n, group):
    idx = np.arange(n) // group
    return jnp.asarray((idx[:, None] == idx[None, :]).astype(np.float32) / group, dtype=BF16)


def _block_diag(w):
    two, nb, bi, bj = w.shape
    eye = jnp.eye(nb, dtype=w.dtype)
    return jnp.einsum("dhij,hg->dhigj", w, eye).reshape(two, nb * bi, nb * bj)


def _prepare_layer(p, i):
    d = D_MODEL
    split = np.cumsum((DA_Q, DA_Q, DA_V, SW_Q, SW_KV, SW_KV, LRU_WIDTH))
    w_in = p["w_in"][i]
    w_aq, w_ak, w_av, w_bq, w_bk, w_bv, w_cx, w_cg = jnp.split(w_in, split.tolist(), axis=1)
    w_av = jnp.pad(w_av.reshape(d, DA_HEADS, DA_V_DIM), ((0, 0), (0, 0), (0, LANES - DA_V_DIM)))
    order = _sw_head_order()
    w_bq = w_bq.reshape(d, SW_HEADS, HEAD_DIM)[:, order].reshape(d, SW_Q)
    w_inx = jnp.concatenate([w_aq, w_ak, w_av.reshape(d, DA_VEXT), w_bq, w_bk, w_bv, w_cx, w_cg],
                            axis=1).astype(BF16)
    v_one = np.zeros((1, DA_VEXT), np.float32)
    v_one[0, DA_V_DIM::LANES] = 1.0

    w_out = p["w_out"][i]
    w_ob = w_out[DA_V:DA_V + SW_Q].reshape(SW_HEADS, HEAD_DIM, d)[jnp.asarray(order)]
    w_out = jnp.concatenate([w_out[:DA_V], w_ob.reshape(SW_Q, d), w_out[DA_V + SW_Q:]], axis=0)

    da_hmask = np.zeros((2 * DA_HEADS, LANES), np.float32)
    for j in range(2 * DA_HEADS):
        off = (j % 4) * DA_QK_DIM
        da_hmask[j, off:off + DA_QK_DIM] = 1.0
    sw_hmask = np.zeros((SW_KV_HEADS, LANES), np.float32)
    for kv in range(SW_KV_HEADS):
        sw_hmask[kv, kv * HEAD_DIM:(kv + 1) * HEAD_DIM] = 1.0

    return dict(
        norm1_gain=p["norm1_gain"][i].reshape(1, d), norm2_gain=p["norm2_gain"][i].reshape(1, d),
        w_in=w_inx, v_one=jnp.asarray(v_one),
        gq_a=(jnp.tile(p["da_q_gain"][i], 2 * DA_HEADS).reshape(1, DA_Q)
              * (DA_QK_DIM ** -0.5 * math.log2(math.e))),
        gk_a=jnp.tile(p["da_k_gain"][i], 2 * DA_HEADS).reshape(1, DA_Q),
        gq_b=jnp.tile(p["sw_q_gain"][i], SW_HEADS).reshape(1, SW_Q) * (HEAD_DIM ** -0.5),
        gk_b=jnp.tile(p["sw_k_gain"][i], SW_KV_HEADS).reshape(1, SW_KV),
        g32=_block_diag_mean(DA_Q, DA_QK_DIM), g64=_block_diag_mean(SW_Q, HEAD_DIM),
        da_hmask=jnp.asarray(da_hmask, dtype=BF16),
        da_lamv=jnp.stack([p["da_lam_q1"][i], p["da_lam_k1"][i], p["da_lam_q2"][i], p["da_lam_k2"][i]]),
        da_sg=jnp.tile(p["da_sub_gain"][i], LANES // DA_V_DIM).reshape(1, LANES),
        sw_sink=p["sw_sink"][i], sw_hmask=jnp.asarray(sw_hmask, dtype=BF16),
        lru_conv_w=p["lru_conv_w"][i], lru_conv_b=p["lru_conv_b"][i].reshape(2, 1, LRU_WIDTH),
        lru_wa=_block_diag(p["lru_wa"][i]).astype(BF16), lru_ba=p["lru_ba"][i].reshape(2, 1, LRU_WIDTH),
        lru_wx=_block_diag(p["lru_wx"][i]).astype(BF16), lru_bx=p["lru_bx"][i].reshape(2, 1, LRU_WIDTH),
        lru_lambda=p["lru_lambda"][i].reshape(2, 1, LRU_WIDTH),
        w_out=w_out.astype(BF16), w_up=p["w_up"][i].astype(BF16),
        ffn_conv_w=p["ffn_conv_w"][i], ffn_conv_b=p["ffn_conv_b"][i].reshape(1, D_FF),
        w_down=p["w_down"][i].astype(BF16),
    )


def kernel(x, c, ctx, c_ctx, w_mod, b_mod, norm1_gain, norm2_gain, w_in, da_q_gain, da_k_gain, da_lam_q1, da_lam_k1, da_lam_q2, da_lam_k2, da_sub_gain, sw_q_gain, sw_k_gain, sw_sink, lru_conv_w, lru_conv_b, lru_wa, lru_ba, lru_wx, lru_bx, lru_lambda, w_out, w_up, ffn_conv_w, ffn_conv_b, w_down):
    p = dict(w_in=w_in, norm1_gain=norm1_gain, norm2_gain=norm2_gain, da_q_gain=da_q_gain,
             da_k_gain=da_k_gain, da_lam_q1=da_lam_q1, da_lam_k1=da_lam_k1, da_lam_q2=da_lam_q2,
             da_lam_k2=da_lam_k2, da_sub_gain=da_sub_gain, sw_q_gain=sw_q_gain, sw_k_gain=sw_k_gain,
             sw_sink=sw_sink, lru_conv_w=lru_conv_w, lru_conv_b=lru_conv_b, lru_wa=lru_wa,
             lru_ba=lru_ba, lru_wx=lru_wx, lru_bx=lru_bx, lru_lambda=lru_lambda, w_out=w_out,
             w_up=w_up, ffn_conv_w=ffn_conv_w, ffn_conv_b=ffn_conv_b, w_down=w_down)
    bsz, n_tok, d = x.shape
    n_ctx = ctx.shape[1]
    depth = w_mod.shape[0]

    cc = jnp.zeros((2 * SUBLANES, d), F32).at[:bsz].set(c).at[bsz].set(c_ctx)
    mod_all = _modulation(cc, w_mod, b_mod)

    rope_lat = (_rope_tables(n_tok // GRID_W, DA_QK_DIM, 2 * DA_HEADS)
                + _rope_tables(n_tok // GRID_W, HEAD_DIM, SW_HEADS))
    rope_ctx = (jnp.ones((n_ctx, DA_Q), F32), jnp.zeros((n_ctx, DA_Q), F32),
                jnp.ones((n_ctx, SW_Q), F32), jnp.zeros((n_ctx, SW_Q), F32))

    xc = ctx
    for i in range(depth):
        lw = _prepare_layer(p, i)
        lambda_init = 0.8 - 0.6 * math.exp(-0.3 * i)
        ctx_out = i < depth - 1
        mod = mod_all[i, :bsz].reshape(bsz, 1, N_MOD, d)
        sh1, sc1, g1, sh2, sc2, g2 = [mod[:, :, m] for m in range(N_MOD)]
        mod_c = mod_all[i, bsz].reshape(1, 1, N_MOD, d)
        sh1c, sc1c, g1c, sh2c, sc2c, g2c = [mod_c[:, :, m] for m in range(N_MOD)]

        qa, ka, va, qb, kb, vb, cx, cg = _input_projection(x, sh1, sc1, lw, rope_lat)
        qac, kac, vac, qbc, kbc, vbc, cxc, cgc = _input_projection(xc, sh1c, sc1c, lw, rope_ctx)

        ya = _diff_attention(qa, [(ka, va), (kac, vac)], lw, lambda_init)
        yb = _window_attention(qb, kbc, vbc, kb, vb, lw)
        hf, hb, hfc, hbc = _rglru_scans(cx, cxc, lw)

        x = _output_projection(x, ya, yb, hf, hb, cg, g1, lw)
        x = _conv_ffn(x, sh2, sc2, g2, lw)
        if ctx_out:
            yac = _diff_attention(qac, [(kac, vac)], lw, lambda_init)
            ybc = _window_attention(qbc, kbc, vbc, None, None, lw)
            xc = _output_projection(xc, yac, ybc, hfc, hbc, cgc, g1c, lw)
            xc = _conv_ffn(xc, sh2c, sc2c, g2c, lw)
    return x
```

```python
import functools
import math

import numpy as np
import jax
import jax.numpy as jnp
from jax import lax
from jax.experimental import pallas as pl
from jax.experimental.pallas import tpu as pltpu

F32 = jnp.float32
BF16 = jnp.bfloat16

D_MODEL = 1024
GRID_W = 64
N_MOD = 6
EPS = 1e-6
NEG_INF = -1e30
ROPE_BASE = 10000.0
DA_HEADS = 4
DA_QK_DIM = 32
DA_V_DIM = 64
HEAD_DIM = 64
SW_HEADS = 6
SW_KV_HEADS = 2
SW_GROUP = SW_HEADS // SW_KV_HEADS
WINDOW = 128
LRU_WIDTH = 384
LRU_BLOCKS = 6
LRU_BLOCK_DIM = LRU_WIDTH // LRU_BLOCKS
LRU_CONV = 4
LRU_C = 8.0
D_FF = 2816
FFN_CONV = 3

DA_Q = DA_HEADS * 2 * DA_QK_DIM
DA_V = DA_HEADS * DA_V_DIM
SW_Q = SW_HEADS * HEAD_DIM
SW_KV = SW_KV_HEADS * HEAD_DIM

LANES = 128
SUBLANES = 8
BF16_ROWS = 16
VMEM_LIMIT = 56 * 1024 * 1024

DA_VEXT = DA_HEADS * LANES
DA_VROWS = DA_V_DIM + BF16_ROWS
C_AQ = 0
C_AK = C_AQ + DA_Q
C_AV = C_AK + DA_Q
C_BQ = C_AV + DA_VEXT
C_BK = C_BQ + SW_Q
C_BV = C_BK + SW_KV
C_CX = C_BV + SW_KV
C_CG = C_CX + LRU_WIDTH
D_INX = C_CG + LRU_WIDTH

TM = 512
FFN_HALO = BF16_ROWS
FFN_CHUNKS = ((0, 1024), (1024, 2048), (2048, D_FF))
DA_TQ = 256
DA_AHEAD = 3
DA_UNROLL = 2
SW_TQ = 256
LRU_TC = 256


def _cparams(n_axes):
    return pltpu.CompilerParams(dimension_semantics=("arbitrary",) * n_axes,
                                vmem_limit_bytes=VMEM_LIMIT)


def _const_spec(shape):
    nd = len(shape)
    return pl.BlockSpec(shape, lambda *_: (0,) * nd, pipeline_mode=pl.Buffered(1))


def _nt_dot(a, b):
    return lax.dot_general(a, b, (((1,), (1,)), ((), ())), preferred_element_type=F32)


def _dot(a, b):
    return jnp.dot(a, b, preferred_element_type=F32)


def _mod_kernel(c_ref, w_ref, b_ref, o_ref):
    c = c_ref[...]
    s = c * jax.nn.sigmoid(c)
    o_ref[0] = _dot(s.astype(BF16), w_ref[0].astype(BF16)) + b_ref[0]


def _modulation(cc, w_mod, b_mod):
    depth, d, n = w_mod.shape
    tn = 1536
    return pl.pallas_call(
        _mod_kernel,
        out_shape=jax.ShapeDtypeStruct((depth, cc.shape[0], n), F32),
        grid=(depth, n // tn),
        in_specs=[pl.BlockSpec(cc.shape, lambda i, j: (0, 0)),
                  pl.BlockSpec((1, d, tn), lambda i, j: (i, 0, j)),
                  pl.BlockSpec((1, 1, tn), lambda i, j: (i, 0, j))],
        out_specs=pl.BlockSpec((1, cc.shape[0], tn), lambda i, j: (i, 0, j)),
        compiler_params=_cparams(2),
        name="modulation",
    )(cc, w_mod, b_mod.reshape(depth, 1, n))


def _norm_modulate(x, gain, shift, scale):
    ms = jnp.mean(x * x, axis=-1, keepdims=True)
    h = x * lax.rsqrt(ms + EPS) * gain
    return h * (1.0 + scale) + shift


def _group_rms(a, g_ref):
    sq = a * a
    hi = sq.astype(BF16)
    lo = (sq - hi.astype(F32)).astype(BF16)
    g = g_ref[...]
    ms = _dot(hi, g) + _dot(lo, g)
    return a * lax.rsqrt(ms + EPS)


def _rope(y, cos, sin_signed, half):
    n = y.shape[1]
    lane = lax.broadcasted_iota(jnp.int32, y.shape, 1)
    first = (lane % (2 * half)) < half
    partner = jnp.where(first, pltpu.roll(y, n - half, 1), pltpu.roll(y, half, 1))
    return y * cos + partner * sin_signed


def _inproj_kernel(x_ref, sh_ref, sc_ref, ng_ref, w_ref, vone_ref,
                   cosa_ref, sina_ref, cosb_ref, sinb_ref,
                   gqa_ref, gka_ref, gqb_ref, gkb_ref, g32_ref, g64_ref,
                   qa_ref, ka_ref, va_ref, qb_ref, kb_ref, vb_ref, cx_ref, cg_ref):
    h = _norm_modulate(x_ref[0], ng_ref[...], sh_ref[0], sc_ref[0]).astype(BF16)

    def proj(lo, hi):
        return _dot(h, w_ref[:, lo:hi])

    cosa, sina = cosa_ref[...], sina_ref[...]
    aq = _group_rms(proj(C_AQ, C_AK), g32_ref) * gqa_ref[...]
    qa_ref[0] = _rope(aq, cosa, sina, DA_QK_DIM // 2).astype(BF16)
    ak = _group_rms(proj(C_AK, C_AV), g32_ref) * gka_ref[...]
    ka_ref[0] = _rope(ak, cosa, sina, DA_QK_DIM // 2).astype(BF16)
    va_t = (proj(C_AV, C_BQ) + vone_ref[...]).T.astype(BF16)
    for hd in range(DA_HEADS):
        va_ref[0, 0, hd * DA_VROWS:(hd + 1) * DA_VROWS, :] = va_t[hd * LANES:hd * LANES + DA_VROWS]

    cosb, sinb = cosb_ref[...], sinb_ref[...]
    bq = _group_rms(proj(C_BQ, C_BK), g64_ref) * gqb_ref[...]
    qb_ref[0] = _rope(bq, cosb, sinb, HEAD_DIM // 2).astype(BF16)
    bkv = proj(C_BK, C_CX)
    bk = _group_rms(bkv[:, :SW_KV], g64_ref.at[:SW_KV, :SW_KV]) * gkb_ref[...]
    kb_ref[0] = _rope(bk, cosb[:, :SW_KV], sinb[:, :SW_KV], HEAD_DIM // 2).astype(BF16)
    vb_ref[0] = bkv[:, SW_KV:].astype(BF16)

    cxg = proj(C_CX, D_INX)
    cx_ref[0] = cxg[:, :LRU_WIDTH]
    cg_ref[0] = cxg[:, LRU_WIDTH:]


def _input_projection(x, shift, scale, lw, rope):
    bsz, t_len, d = x.shape
    tm = min(TM, t_len)
    per_b = shift.shape[0] > 1
    mod_map = (lambda b, j: (b, 0, 0)) if per_b else (lambda b, j: (0, 0, 0))
    tok = lambda n: pl.BlockSpec((1, tm, n), lambda b, j: (b, j, 0))
    tab = lambda n: pl.BlockSpec((tm, n), lambda b, j: (j, 0))
    widths = (DA_Q, DA_Q, DA_VEXT, SW_Q, SW_KV, SW_KV, LRU_WIDTH, LRU_WIDTH)
    dtypes = (BF16,) * 6 + (F32, F32)
    out_shape = [jax.ShapeDtypeStruct((bsz, t_len, n), dt) for n, dt in zip(widths, dtypes)]
    out_specs = [tok(n) for n in widths]
    out_shape[2] = jax.ShapeDtypeStruct((bsz, t_len // tm, DA_HEADS * DA_VROWS, tm), BF16)
    out_specs[2] = pl.BlockSpec((1, 1, DA_HEADS * DA_VROWS, tm), lambda b, j: (b, j, 0, 0))
    return pl.pallas_call(
        _inproj_kernel,
        out_shape=tuple(out_shape),
        grid=(bsz, t_len // tm),
        in_specs=[tok(d),
                  pl.BlockSpec((1, 1, d), mod_map), pl.BlockSpec((1, 1, d), mod_map),
                  _const_spec((1, d)), _const_spec((d, D_INX)), _const_spec((1, DA_VEXT)),
                  tab(DA_Q), tab(DA_Q), tab(SW_Q), tab(SW_Q),
                  _const_spec((1, DA_Q)), _const_spec((1, DA_Q)),
                  _const_spec((1, SW_Q)), _const_spec((1, SW_KV)),
                  _const_spec((DA_Q, DA_Q)), _const_spec((SW_Q, SW_Q))],
        out_specs=tuple(out_specs),
        compiler_params=_cparams(2),
        name="input_projection",
    )(x, shift, scale, lw["norm1_gain"], lw["w_in"], lw["v_one"], *rope,
      lw["gq_a"], lw["gk_a"], lw["gq_b"], lw["gk_b"], lw["g32"], lw["g64"])


def _da_kernel(*refs, n_seg, lambda_init):
    q_ref = refs[0]
    kv_refs = refs[1:1 + 2 * n_seg]
    hmask_ref, lamv_ref, sg_ref, o_ref, m_ref, acc_ref, s_ref = refs[1 + 2 * n_seg:]
    n_map = 2 * DA_HEADS

    m_ref[...] = jnp.full(m_ref.shape, NEG_INF, F32)
    acc_ref[...] = jnp.zeros(acc_ref.shape, F32)

    def scores(k_ref, tk, i, j):
        half = j // 4
        start = i * tk if isinstance(i, int) else pl.multiple_of(i * tk, tk)
        qm = q_ref[0, :, half * LANES:(half + 1) * LANES] * hmask_ref[j:j + 1, :]
        kblk = k_ref[0, pl.ds(start, tk), half * LANES:(half + 1) * LANES]
        return _nt_dot(kblk, qm)

    def tile(k_ref, v_ref, i, from_scratch, nxt):
        tk = v_ref.shape[3]
        pending = {}
        if not from_scratch:
            for j in range(DA_AHEAD):
                pending[j] = scores(k_ref, tk, i, j)
        for j in range(n_map):
            s_t = s_ref[j, :tk] if from_scratch and j < DA_AHEAD else pending.pop(j)
            ja = j + DA_AHEAD
            if ja < n_map:
                pending[ja] = scores(k_ref, tk, i, ja)
            elif nxt is not None:
                s_ref[ja - n_map, :nxt[1]] = scores(nxt[0], nxt[1], nxt[2], ja - n_map)
            m_old = m_ref[j, 0:1, :]
            m_new = jnp.maximum(m_old, jnp.max(s_t, axis=0, keepdims=True))
            p_t = jnp.exp2(s_t - m_new).astype(BF16)
            pv = _dot(v_ref[0, i, (j // 2) * DA_VROWS:(j // 2 + 1) * DA_VROWS, :], p_t)
            acc_ref[j] = jnp.exp2(m_old - m_new) * acc_ref[j] + pv
            m_ref[j] = jnp.broadcast_to(m_new, m_ref.shape[1:])

    segs = [(kv_refs[2 * si], kv_refs[2 * si + 1]) for si in range(n_seg)]
    for si, (k_ref, v_ref) in enumerate(segs):
        n_t, tk = v_ref.shape[1], v_ref.shape[3]
        after = (segs[si + 1][0], segs[si + 1][1].shape[3], 0) if si + 1 < n_seg else None
        lo, hi = 0, n_t
        if si == 0:
            tile(k_ref, v_ref, 0, False, (k_ref, tk, 1) if n_t > 1 else after)
            lo = 1
        if after is not None and hi > lo:
            hi -= 1
        if hi > lo:
            unroll = DA_UNROLL if (hi - lo) % DA_UNROLL == 0 else 1

            def body(g, carry, k_ref=k_ref, v_ref=v_ref, n_t=n_t, tk=tk, lo=lo, unroll=unroll):
                for u in range(unroll):
                    i = lo + g * unroll + u
                    tile(k_ref, v_ref, i, True, (k_ref, tk, jnp.minimum(i + 1, n_t - 1)))
                return carry
            lax.fori_loop(0, (hi - lo) // unroll, body, 0)
        if after is not None and n_t > lo:
            tile(k_ref, v_ref, n_t - 1, True, after)

    lv = lamv_ref[...]
    lam = (jnp.exp(jnp.sum(lv[0:1] * lv[1:2], axis=1, keepdims=True))
           - jnp.exp(jnp.sum(lv[2:3] * lv[3:4], axis=1, keepdims=True)) + lambda_init)

    def head_out(h):
        a1, a2 = acc_ref[2 * h], acc_ref[2 * h + 1]
        o = (a1 / a1[DA_V_DIM:DA_V_DIM + 1] - lam * (a2 / a2[DA_V_DIM:DA_V_DIM + 1]))[:DA_V_DIM]
        ms = jnp.sum(o * o, axis=0, keepdims=True) * (1.0 / DA_V_DIM)
        return o * lax.rsqrt(ms + EPS)

    for c in range(DA_HEADS // 2):
        pair_t = jnp.concatenate([head_out(2 * c), head_out(2 * c + 1)], axis=0)
        o_ref[0, :, c * LANES:(c + 1) * LANES] = (pair_t.T * sg_ref[...]).astype(o_ref.dtype)


def _diff_attention(q, kvs, lw, lambda_init):
    bsz, t_q, _ = q.shape
    tq = min(DA_TQ, t_q)
    kv_specs, kv_args = [], []
    for k, v in kvs:
        kv_specs += [pl.BlockSpec((1,) + k.shape[1:], lambda b, j: (b, 0, 0)),
                     pl.BlockSpec((1,) + v.shape[1:], lambda b, j: (b, 0, 0, 0))]
        kv_args += [k, v]
    sg = lw["da_sg"] * (1.0 - lambda_init)
    return pl.pallas_call(
        functools.partial(_da_kernel, n_seg=len(kvs), lambda_init=lambda_init),
        out_shape=jax.ShapeDtypeStruct((bsz, t_q, DA_V), BF16),
        grid=(bsz, t_q // tq),
        in_specs=[pl.BlockSpec((1, tq, DA_Q), lambda b, j: (b, j, 0))] + kv_specs
                 + [_const_spec((2 * DA_HEADS, LANES)), _const_spec((4, DA_QK_DIM)),
                    _const_spec((1, LANES))],
        out_specs=pl.BlockSpec((1, tq, DA_V), lambda b, j: (b, j, 0)),
        scratch_shapes=[pltpu.VMEM((2 * DA_HEADS, SUBLANES, tq), F32),
                        pltpu.VMEM((2 * DA_HEADS, DA_VROWS, tq), F32),
                        pltpu.VMEM((DA_AHEAD, max(v.shape[3] for _, v in kvs), tq), F32)],
        compiler_params=_cparams(2),
        name="diff_attention",
    )(q, *kv_args, lw["da_hmask"], lw["da_lamv"], sg)


def _sw_kernel(*refs, has_lat, tq, t_len):
    if has_lat:
        sink_ref, q_ref, kc_ref, vc_ref, k_ref, v_ref, hm_ref, o_ref = refs
    else:
        sink_ref, q_ref, kc_ref, vc_ref, hm_ref, o_ref = refs
    kc, vc = kc_ref[0], vc_ref[0]
    if has_lat:
        band = tq + 2 * WINDOW
        q0 = pl.program_id(1) * tq
        start = pl.multiple_of(jnp.clip(q0 - WINDOW, 0, t_len - band), WINDOW)
        kb = k_ref[0, pl.ds(start, band), :]
        vb = v_ref[0, pl.ds(start, band), :]
        qpos = q0 + lax.broadcasted_iota(jnp.int32, (tq, 1), 0)
        kpos = start + lax.broadcasted_iota(jnp.int32, (1, band), 1)
        valid = jnp.abs(qpos - kpos) <= WINDOW
    left = lax.broadcasted_iota(jnp.int32, (1, LANES), 1) < HEAD_DIM
    for g in range(SW_GROUP):
        qg = q_ref[0, :, g * LANES:(g + 1) * LANES]
        outs = []
        for kv in range(SW_KV_HEADS):
            sink = sink_ref[kv * SW_GROUP + g]
            qm = qg * hm_ref[kv:kv + 1, :]
            s_ctx = _nt_dot(qm, kc)
            m = jnp.maximum(jnp.max(s_ctx, axis=1, keepdims=True), sink)
            if has_lat:
                s_lat = jnp.where(valid, _nt_dot(qm, kb), NEG_INF)
                m = jnp.maximum(m, jnp.max(s_lat, axis=1, keepdims=True))
            p_ctx = jnp.exp(s_ctx - m)
            l = jnp.sum(p_ctx, axis=1, keepdims=True) + jnp.exp(sink - m)
            o = _dot(p_ctx.astype(BF16), vc)
            if has_lat:
                p_lat = jnp.exp(s_lat - m)
                l = l + jnp.sum(p_lat, axis=1, keepdims=True)
                o = o + _dot(p_lat.astype(BF16), vb)
            outs.append(o / l)
        o_ref[0, :, g * LANES:(g + 1) * LANES] = jnp.where(left, outs[0], outs[1]).astype(o_ref.dtype)


def _window_attention(q, kc, vc, k, v, lw):
    bsz, t_q, _ = q.shape
    has_lat = k is not None
    tq = min(SW_TQ, t_q)
    n_ctx = kc.shape[1]
    full = lambda t: pl.BlockSpec((1, t, SW_KV), lambda b, j: (b, 0, 0))
    specs = [pl.BlockSpec(memory_space=pltpu.SMEM),
             pl.BlockSpec((1, tq, SW_Q), lambda b, j: (b, j, 0)), full(n_ctx), full(n_ctx)]
    args = [lw["sw_sink"], q, kc, vc]
    if has_lat:
        specs += [full(t_q), full(t_q)]
        args += [k, v]
    specs.append(_const_spec((SW_KV_HEADS, LANES)))
    args.append(lw["sw_hmask"])
    return pl.pallas_call(
        functools.partial(_sw_kernel, has_lat=has_lat, tq=tq, t_len=t_q),
        out_shape=jax.ShapeDtypeStruct((bsz, t_q, SW_Q), BF16),
        grid=(bsz, t_q // tq),
        in_specs=specs,
        out_specs=pl.BlockSpec((1, tq, SW_Q), lambda b, j: (b, j, 0)),
        compiler_params=_cparams(2),
        name="window_attention",
    )(*args)


def _lru_kernel(xf_ref, xb_ref, xc_ref, cw_ref, cb_ref, wa_ref, ba_ref, wx_ref, bx_ref, lam_ref,
                hf_ref, hb_ref, hfc_ref, hbc_ref, halo_ref, carry_ref, *, tc):
    j = pl.program_id(1)
    is_ctx = j == 0
    row = lax.broadcasted_iota(jnp.int32, (tc, 1), 0)
    row8 = lax.broadcasted_iota(jnp.int32, (SUBLANES, 1), 0)

    for d, x_lat_ref, out_ref, outc_ref in ((0, xf_ref, hf_ref, hfc_ref), (1, xb_ref, hb_ref, hbc_ref)):
        x = jnp.where(is_ctx, xc_ref[0], x_lat_ref[0])
        halo = jnp.where(j <= 1, 0.0, halo_ref[d])
        cw = cw_ref[d]
        xc = cb_ref[d] + x * cw[LRU_CONV - 1:LRU_CONV]
        for k in range(LRU_CONV - 1):
            s = LRU_CONV - 1 - k
            if d == 0:
                xs = pltpu.roll(x, s, 0)
                edge = jnp.where(row8 < s, pltpu.roll(halo, s, 0), xs[:SUBLANES])
                xs = jnp.concatenate([edge, xs[SUBLANES:]], axis=0)
            else:
                xs = pltpu.roll(x, tc - s, 0)
                edge = jnp.where(row8 >= SUBLANES - s, pltpu.roll(halo, SUBLANES - s, 0),
                                 xs[tc - SUBLANES:])
                xs = jnp.concatenate([xs[:tc - SUBLANES], edge], axis=0)
            xc = xc + xs * cw[k:k + 1]
        halo_ref[d] = x[tc - SUBLANES:] if d == 0 else x[:SUBLANES]

        xcb = xc.astype(BF16)
        r = jax.nn.sigmoid(_dot(xcb, wa_ref[d]) + ba_ref[d])
        gi = jax.nn.sigmoid(_dot(xcb, wx_ref[d]) + bx_ref[d])
        nl = -lam_ref[d]
        softplus = jnp.maximum(nl, 0.0) + jnp.log1p(jnp.exp(-jnp.abs(nl)))
        log_a = -LRU_C * r * softplus
        a = jnp.exp(log_a)
        b = jnp.sqrt(-jnp.tanh(log_a) * (a * a + 1.0)) * (gi * xc)

        step = 1
        while step < tc:
            if d == 0:
                keep = row < step
                a_s = jnp.where(keep, 1.0, pltpu.roll(a, step, 0))
                b_s = jnp.where(keep, 0.0, pltpu.roll(b, step, 0))
            else:
                keep = row >= tc - step
                a_s = jnp.where(keep, 1.0, pltpu.roll(a, tc - step, 0))
                b_s = jnp.where(keep, 0.0, pltpu.roll(b, tc - step, 0))
            b = a * b_s + b
            a = a * a_s
            step *= 2

        edge_row = SUBLANES - 1 if d == 0 else 0
        h0 = jnp.where(is_ctx, 0.0, carry_ref[d, edge_row:edge_row + 1, :])
        h = a * h0 + b
        carry_ref[d] = h[tc - SUBLANES:] if d == 0 else h[:SUBLANES]

        @pl.when(is_ctx)
        def _():
            outc_ref[0] = h

        @pl.when(jnp.logical_not(is_ctx))
        def _():
            out_ref[0] = h


def _rglru_scans(cx, cxc, lw):
    bsz, t_len, n = cx.shape
    tc = LRU_TC
    assert cxc.shape[1] == tc and t_len % tc == 0
    n_lat = t_len // tc
    fwd = lambda b, j: (b, jnp.maximum(j - 1, 0), 0)
    bwd = lambda b, j: (b, n_lat - jnp.maximum(j, 1), 0)
    ctx = lambda b, j: (b, 0, 0)
    blk = lambda m: pl.BlockSpec((1, tc, n), m)
    return pl.pallas_call(
        functools.partial(_lru_kernel, tc=tc),
        out_shape=(jax.ShapeDtypeStruct(cx.shape, F32), jax.ShapeDtypeStruct(cx.shape, F32),
                   jax.ShapeDtypeStruct(cxc.shape, F32), jax.ShapeDtypeStruct(cxc.shape, F32)),
        grid=(bsz, n_lat + 1),
        in_specs=[blk(fwd), blk(bwd), blk(ctx),
                  _const_spec((2, LRU_CONV, n)), _const_spec((2, 1, n)),
                  _const_spec((2, n, n)), _const_spec((2, 1, n)),
                  _const_spec((2, n, n)), _const_spec((2, 1, n)), _const_spec((2, 1, n))],
        out_specs=(blk(fwd), blk(bwd), blk(ctx), blk(ctx)),
        scratch_shapes=[pltpu.VMEM((2, SUBLANES, n), F32), pltpu.VMEM((2, SUBLANES, n), F32)],
        compiler_params=_cparams(2),
        name="rglru_scans",
    )(cx, cx, cxc, lw["lru_conv_w"], lw["lru_conv_b"], lw["lru_wa"], lw["lru_ba"],
      lw["lru_wx"], lw["lru_bx"], lw["lru_lambda"])


def _gelu_tanh(x):
    return 0.5 * x * (1.0 + jnp.tanh(math.sqrt(2.0 / math.pi) * (x + 0.044715 * (x * x * x))))


def _outproj_kernel(x_ref, ya_ref, yb_ref, hf_ref, hb_ref, cg_ref, g1_ref, w_ref, o_ref, cat_ref):
    cat_ref[:, 0:DA_V] = ya_ref[0]
    cat_ref[:, DA_V:DA_V + SW_Q] = yb_ref[0]
    yc = (hf_ref[0] + hb_ref[0]) * _gelu_tanh(cg_ref[0])
    cat_ref[:, DA_V + SW_Q:] = yc.astype(BF16)
    o_ref[0] = x_ref[0] + g1_ref[0] * _dot(cat_ref[...], w_ref[...])


def _output_projection(x, ya, yb, hf, hb, cg, gate, lw):
    bsz, t_len, d = x.shape
    tm = min(TM, t_len)
    per_b = gate.shape[0] > 1
    mod_map = (lambda b, j: (b, 0, 0)) if per_b else (lambda b, j: (0, 0, 0))
    tok = lambda n: pl.BlockSpec((1, tm, n), lambda b, j: (b, j, 0))
    return pl.pallas_call(
        _outproj_kernel,
        out_shape=jax.ShapeDtypeStruct(x.shape, F32),
        grid=(bsz, t_len // tm),
        in_specs=[tok(d), tok(DA_V), tok(SW_Q), tok(LRU_WIDTH), tok(LRU_WIDTH), tok(LRU_WIDTH),
                  pl.BlockSpec((1, 1, d), mod_map), _const_spec((d, d))],
        out_specs=tok(d),
        scratch_shapes=[pltpu.VMEM((tm, d), BF16)],
        compiler_params=_cparams(2),
        name="output_projection",
    )(x, ya, yb, hf, hb, cg, gate, lw["w_out"])


def _ffn_kernel(xm_ref, xp_ref, xn_ref, sh_ref, sc_ref, g2_ref, ng_ref, wup_ref, cw_ref, cb_ref,
                wdn_ref, o_ref, h_ref, *, tm):
    j = pl.program_id(1)
    last = pl.num_programs(1) - 1
    gain, shift, scale = ng_ref[...], sh_ref[0], sc_ref[0]
    xm = xm_ref[0]
    hp = jnp.where(j > 0, _norm_modulate(xp_ref[0], gain, shift, scale), 0.0)
    hn = jnp.where(j < last, _norm_modulate(xn_ref[0], gain, shift, scale), 0.0)
    h_ref[0:FFN_HALO] = hp.astype(BF16)
    h_ref[FFN_HALO:FFN_HALO + tm] = _norm_modulate(xm, gain, shift, scale).astype(BF16)
    h_ref[FFN_HALO + tm:] = hn.astype(BF16)

    rows = tm + 2 * FFN_HALO
    acc = jnp.zeros((tm, D_MODEL), F32)
    for lo, hi in FFN_CHUNKS:
        gp = _dot(h_ref[...], wup_ref[:, lo:hi])
        g_prev = pltpu.roll(gp, 1, 0)[FFN_HALO:FFN_HALO + tm]
        g_next = pltpu.roll(gp, rows - 1, 0)[FFN_HALO:FFN_HALO + tm]
        cw = cw_ref[:, lo:hi]
        gate = (cb_ref[:, lo:hi] + g_prev * cw[0:1] + gp[FFN_HALO:FFN_HALO + tm] * cw[1:2]
                + g_next * cw[2:3])
        val = _dot(h_ref[FFN_HALO:FFN_HALO + tm], wup_ref[:, D_FF + lo:D_FF + hi])
        act = (gate * jax.nn.sigmoid(gate) * val).astype(BF16)
        acc = acc + _dot(act, wdn_ref[lo:hi, :])
    o_ref[0] = xm + g2_ref[0] * acc


def _conv_ffn(x, shift, scale, gate, lw):
    bsz, t_len, d = x.shape
    tm = min(TM, t_len)
    per_b = gate.shape[0] > 1
    mod_map = (lambda b, j: (b, 0, 0)) if per_b else (lambda b, j: (0, 0, 0))
    hb = tm // FFN_HALO
    n_hb = t_len // FFN_HALO
    prev_map = lambda b, j: (b, jnp.maximum(j * hb - 1, 0), 0)
    next_map = lambda b, j: (b, jnp.minimum((j + 1) * hb, n_hb - 1), 0)
    mod_spec = pl.BlockSpec((1, 1, d), mod_map)
    return pl.pallas_call(
        functools.partial(_ffn_kernel, tm=tm),
        out_shape=jax.ShapeDtypeStruct(x.shape, F32),
        grid=(bsz, t_len // tm),
        in_specs=[pl.BlockSpec((1, tm, d), lambda b, j: (b, j, 0)),
                  pl.BlockSpec((1, FFN_HALO, d), prev_map),
                  pl.BlockSpec((1, FFN_HALO, d), next_map),
                  mod_spec, mod_spec, mod_spec,
                  _const_spec((1, d)), _const_spec((d, 2 * D_FF)),
                  _const_spec((FFN_CONV, D_FF)), _const_spec((1, D_FF)), _const_spec((D_FF, d))],
        out_specs=pl.BlockSpec((1, tm, d), lambda b, j: (b, j, 0)),
        scratch_shapes=[pltpu.VMEM((tm + 2 * FFN_HALO, d), BF16)],
        compiler_params=_cparams(2),
        name="conv_ffn",
    )(x, x, x, shift, scale, gate, lw["norm2_gain"], lw["w_up"], lw["ffn_conv_w"],
      lw["ffn_conv_b"], lw["w_down"])


def _sw_head_order():
    return [kv * SW_GROUP + g for g in range(SW_GROUP) for kv in range(SW_KV_HEADS)]


def _rope_tables(rows, head_dim, n_heads):
    row = jnp.repeat(jnp.arange(rows, dtype=F32), GRID_W)
    col = jnp.tile(jnp.arange(GRID_W, dtype=F32), rows)
    quarter = head_dim // 4
    inv_freq = ROPE_BASE ** (-jnp.arange(quarter, dtype=F32) / quarter)
    ang = jnp.concatenate([row[:, None] * inv_freq, col[:, None] * inv_freq], axis=-1)
    cos, sin = jnp.cos(ang), jnp.sin(ang)
    return (jnp.tile(jnp.concatenate([cos, cos], axis=-1), (1, n_heads)),
            jnp.tile(jnp.concatenate([-sin, sin], axis=-1), (1, n_heads)))


def _block_diag_mean(n, group):
    idx = np.arange(n) // group
    return jnp.asarray((idx[:, None] == idx[None, :]).astype(np.float32) / group, dtype=BF16)


def _block_diag(w):
    two, nb, bi, bj = w.shape
    eye = jnp.eye(nb, dtype=w.dtype)
    return jnp.einsum("dhij,hg->dhigj", w, eye).reshape(two, nb * bi, nb * bj)


def _prepare_layer(p, i):
    d = D_MODEL
    split = np.cumsum((DA_Q, DA_Q, DA_V, SW_Q, SW_KV, SW_KV, LRU_WIDTH))
    w_in = p["w_in"][i]
    w_aq, w_ak, w_av, w_bq, w_bk, w_bv, w_cx, w_cg = jnp.split(w_in, split.tolist(), axis=1)
    w_av = jnp.pad(w_av.reshape(d, DA_HEADS, DA_V_DIM), ((0, 0), (0, 0), (0, LANES - DA_V_DIM)))
    order = _sw_head_order()
    w_bq = w_bq.reshape(d, SW_HEADS, HEAD_DIM)[:, order].reshape(d, SW_Q)
    w_inx = jnp.concatenate([w_aq, w_ak, w_av.reshape(d, DA_VEXT), w_bq, w_bk, w_bv, w_cx, w_cg],
                            axis=1).astype(BF16)
    v_one = np.zeros((1, DA_VEXT), np.float32)
    v_one[0, DA_V_DIM::LANES] = 1.0

    w_out = p["w_out"][i]
    w_ob = w_out[DA_V:DA_V + SW_Q].reshape(SW_HEADS, HEAD_DIM, d)[jnp.asarray(order)]
    w_out = jnp.concatenate([w_out[:DA_V], w_ob.reshape(SW_Q, d), w_out[DA_V + SW_Q:]], axis=0)

    da_hmask = np.zeros((2 * DA_HEADS, LANES), np.float32)
    for j in range(2 * DA_HEADS):
        off = (j % 4) * DA_QK_DIM
        da_hmask[j, off:off + DA_QK_DIM] = 1.0
    sw_hmask = np.zeros((SW_KV_HEADS, LANES), np.float32)
    for kv in range(SW_KV_HEADS):
        sw_hmask[kv, kv * HEAD_DIM:(kv + 1) * HEAD_DIM] = 1.0

    return dict(
        norm1_gain=p["norm1_gain"][i].reshape(1, d), norm2_gain=p["norm2_gain"][i].reshape(1, d),
        w_in=w_inx, v_one=jnp.asarray(v_one),
        gq_a=(jnp.tile(p["da_q_gain"][i], 2 * DA_HEADS).reshape(1, DA_Q)
              * (DA_QK_DIM ** -0.5 * math.log2(math.e))),
        gk_a=jnp.tile(p["da_k_gain"][i], 2 * DA_HEADS).reshape(1, DA_Q),
        gq_b=jnp.tile(p["sw_q_gain"][i], SW_HEADS).reshape(1, SW_Q) * (HEAD_DIM ** -0.5),
        gk_b=jnp.tile(p["sw_k_gain"][i], SW_KV_HEADS).reshape(1, SW_KV),
        g32=_block_diag_mean(DA_Q, DA_QK_DIM), g64=_block_diag_mean(SW_Q, HEAD_DIM),
        da_hmask=jnp.asarray(da_hmask, dtype=BF16),
        da_lamv=jnp.stack([p["da_lam_q1"][i], p["da_lam_k1"][i], p["da_lam_q2"][i], p["da_lam_k2"][i]]),
        da_sg=jnp.tile(p["da_sub_gain"][i], LANES // DA_V_DIM).reshape(1, LANES),
        sw_sink=p["sw_sink"][i], sw_hmask=jnp.asarray(sw_hmask, dtype=BF16),
        lru_conv_w=p["lru_conv_w"][i], lru_conv_b=p["lru_conv_b"][i].reshape(2, 1, LRU_WIDTH),
        lru_wa=_block_diag(p["lru_wa"][i]).astype(BF16), lru_ba=p["lru_ba"][i].reshape(2, 1, LRU_WIDTH),
        lru_wx=_block_diag(p["lru_wx"][i]).astype(BF16), lru_bx=p["lru_bx"][i].reshape(2, 1, LRU_WIDTH),
        lru_lambda=p["lru_lambda"][i].reshape(2, 1, LRU_WIDTH),
        w_out=w_out.astype(BF16), w_up=p["w_up"][i].astype(BF16),
        ffn_conv_w=p["ffn_conv_w"][i], ffn_conv_b=p["ffn_conv_b"][i].reshape(1, D_FF),
        w_down=p["w_down"][i].astype(BF16),
    )


def kernel(x, c, ctx, c_ctx, w_mod, b_mod, norm1_gain, norm2_gain, w_in, da_q_gain, da_k_gain, da_lam_q1, da_lam_k1, da_lam_q2, da_lam_k2, da_sub_gain, sw_q_gain, sw_k_gain, sw_sink, lru_conv_w, lru_conv_b, lru_wa, lru_ba, lru_wx, lru_bx, lru_lambda, w_out, w_up, ffn_conv_w, ffn_conv_b, w_down):
    p = dict(w_in=w_in, norm1_gain=norm1_gain, norm2_gain=norm2_gain, da_q_gain=da_q_gain,
             da_k_gain=da_k_gain, da_lam_q1=da_lam_q1, da_lam_k1=da_lam_k1, da_lam_q2=da_lam_q2,
             da_lam_k2=da_lam_k2, da_sub_gain=da_sub_gain, sw_q_gain=sw_q_gain, sw_k_gain=sw_k_gain,
             sw_sink=sw_sink, lru_conv_w=lru_conv_w, lru_conv_b=lru_conv_b, lru_wa=lru_wa,
             lru_ba=lru_ba, lru_wx=lru_wx, lru_bx=lru_bx, lru_lambda=lru_lambda, w_out=w_out,
             w_up=w_up, ffn_conv_w=ffn_conv_w, ffn_conv_b=ffn_conv_b, w_down=w_down)
    bsz, n_tok, d = x.shape
    n_ctx = ctx.shape[1]
    depth = w_mod.shape[0]

    cc = jnp.zeros((2 * SUBLANES, d), F32).at[:bsz].set(c).at[bsz].set(c_ctx)
    mod_all = _modulation(cc, w_mod, b_mod)

    rope_lat = (_rope_tables(n_tok // GRID_W, DA_QK_DIM, 2 * DA_HEADS)
                + _rope_tables(n_tok // GRID_W, HEAD_DIM, SW_HEADS))
    rope_ctx = (jnp.ones((n_ctx, DA_Q), F32), jnp.zeros((n_ctx, DA_Q), F32),
                jnp.ones((n_ctx, SW_Q), F32), jnp.zeros((n_ctx, SW_Q), F32))

    xc = ctx
    for i in range(depth):
        lw = _prepare_layer(p, i)
        lambda_init = 0.8 - 0.6 * math.exp(-0.3 * i)
        ctx_out = i < depth - 1
        mod = mod_all[i, :bsz].reshape(bsz, 1, N_MOD, d)
        sh1, sc1, g1, sh2, sc2, g2 = [mod[:, :, m] for m in range(N_MOD)]
        mod_c = mod_all[i, bsz].reshape(1, 1, N_MOD, d)
        sh1c, sc1c, g1c, sh2c, sc2c, g2c = [mod_c[:, :, m] for m in range(N_MOD)]

        qa, ka, va, qb, kb, vb, cx, cg = _input_projection(x, sh1, sc1, lw, rope_lat)
        qac, kac, vac, qbc, kbc, vbc, cxc, cgc = _input_projection(xc, sh1c, sc1c, lw, rope_ctx)

        ya = _diff_attention(qa, [(kac, vac), (ka, va)], lw, lambda_init)
        yb = _window_attention(qb, kbc, vbc, kb, vb, lw)
        hf, hb, hfc, hbc = _rglru_scans(cx, cxc, lw)

        x = _output_projection(x, ya, yb, hf, hb, cg, g1, lw)
        x = _conv_ffn(x, sh2, sc2, g2, lw)
        if ctx_out:
            yac = _diff_attention(qac, [(kac, vac)], lw, lambda_init)
            ybc = _window_attention(qbc, kbc, vbc, None, None, lw)
            xc = _output_projection(xc, yac, ybc, hfc, hbc, cgc, g1c, lw)
            xc = _conv_ffn(xc, sh2c, sc2c, g2c, lw)
    return x
```

```python
import functools
import math

import numpy as np
import jax
import jax.numpy as jnp
from jax import lax
from jax.experimental import pallas as pl
from jax.experimental.pallas import tpu as pltpu

F32 = jnp.float32
BF16 = jnp.bfloat16

D_MODEL = 1024
GRID_W = 64
N_MOD = 6
EPS = 1e-6
NEG_INF = -1e30
ROPE_BASE = 10000.0
DA_HEADS = 4
DA_QK_DIM = 32
DA_V_DIM = 64
HEAD_DIM = 64
SW_HEADS = 6
SW_KV_HEADS = 2
SW_GROUP = SW_HEADS // SW_KV_HEADS
WINDOW = 128
LRU_WIDTH = 384
LRU_BLOCKS = 6
LRU_BLOCK_DIM = LRU_WIDTH // LRU_BLOCKS
LRU_CONV = 4
LRU_C = 8.0
D_FF = 2816
FFN_CONV = 3

DA_Q = DA_HEADS * 2 * DA_QK_DIM
DA_V = DA_HEADS * DA_V_DIM
SW_Q = SW_HEADS * HEAD_DIM
SW_KV = SW_KV_HEADS * HEAD_DIM

LANES = 128
SUBLANES = 8
BF16_ROWS = 16
VMEM_LIMIT = 56 * 1024 * 1024

DA_VEXT = DA_HEADS * LANES
DA_VROWS = DA_V_DIM + BF16_ROWS
C_AQ = 0
C_AK = C_AQ + DA_Q
C_AV = C_AK + DA_Q
C_BQ = C_AV + DA_VEXT
C_BK = C_BQ + SW_Q
C_BV = C_BK + SW_KV
C_CX = C_BV + SW_KV
C_CG = C_CX + LRU_WIDTH
D_INX = C_CG + LRU_WIDTH

TM = 512
FFN_HALO = BF16_ROWS
FFN_CHUNKS = ((0, 1024), (1024, 2048), (2048, D_FF))
DA_TQ = 256
DA_AHEAD = 3
DA_UNROLL = 4
SW_AHEAD = 2
SW_TQ = 256
LRU_TC = 256


def _cparams(n_axes):
    return pltpu.CompilerParams(dimension_semantics=("arbitrary",) * n_axes,
                                vmem_limit_bytes=VMEM_LIMIT)


def _const_spec(shape):
    nd = len(shape)
    return pl.BlockSpec(shape, lambda *_: (0,) * nd, pipeline_mode=pl.Buffered(1))


def _nt_dot(a, b):
    return lax.dot_general(a, b, (((1,), (1,)), ((), ())), preferred_element_type=F32)


def _dot(a, b):
    return jnp.dot(a, b, preferred_element_type=F32)


def _mod_kernel(c_ref, w_ref, b_ref, o_ref):
    c = c_ref[...]
    s = c * jax.nn.sigmoid(c)
    o_ref[0] = _dot(s.astype(BF16), w_ref[0].astype(BF16)) + b_ref[0]


def _modulation(cc, w_mod, b_mod):
    depth, d, n = w_mod.shape
    tn = 1536
    return pl.pallas_call(
        _mod_kernel,
        out_shape=jax.ShapeDtypeStruct((depth, cc.shape[0], n), F32),
        grid=(depth, n // tn),
        in_specs=[pl.BlockSpec(cc.shape, lambda i, j: (0, 0)),
                  pl.BlockSpec((1, d, tn), lambda i, j: (i, 0, j)),
                  pl.BlockSpec((1, 1, tn), lambda i, j: (i, 0, j))],
        out_specs=pl.BlockSpec((1, cc.shape[0], tn), lambda i, j: (i, 0, j)),
        compiler_params=_cparams(2),
        name="modulation",
    )(cc, w_mod, b_mod.reshape(depth, 1, n))


def _norm_modulate(x, gain, shift, scale):
    ms = jnp.mean(x * x, axis=-1, keepdims=True)
    h = x * lax.rsqrt(ms + EPS) * gain
    return h * (1.0 + scale) + shift


def _group_rms(a, g_ref):
    sq = a * a
    hi = sq.astype(BF16)
    lo = (sq - hi.astype(F32)).astype(BF16)
    g = g_ref[...]
    ms = _dot(hi, g) + _dot(lo, g)
    return a * lax.rsqrt(ms + EPS)


def _rope(y, cos, sin_signed, half):
    n = y.shape[1]
    lane = lax.broadcasted_iota(jnp.int32, y.shape, 1)
    first = (lane % (2 * half)) < half
    partner = jnp.where(first, pltpu.roll(y, n - half, 1), pltpu.roll(y, half, 1))
    return y * cos + partner * sin_signed


def _inproj_kernel(x_ref, sh_ref, sc_ref, ng_ref, w_ref, vone_ref,
                   cosa_ref, sina_ref, cosb_ref, sinb_ref,
                   gqa_ref, gka_ref, gqb_ref, gkb_ref, g32_ref, g64_ref,
                   qa_ref, ka_ref, va_ref, qb_ref, kb_ref, vb_ref, cx_ref, cg_ref):
    h = _norm_modulate(x_ref[0], ng_ref[...], sh_ref[0], sc_ref[0]).astype(BF16)

    def proj(lo, hi):
        return _dot(h, w_ref[:, lo:hi])

    cosa, sina = cosa_ref[...], sina_ref[...]
    aq = _group_rms(proj(C_AQ, C_AK), g32_ref) * gqa_ref[...]
    qa_ref[0] = _rope(aq, cosa, sina, DA_QK_DIM // 2).astype(BF16)
    ak = _group_rms(proj(C_AK, C_AV), g32_ref) * gka_ref[...]
    ka_ref[0] = _rope(ak, cosa, sina, DA_QK_DIM // 2).astype(BF16)
    va_t = (proj(C_AV, C_BQ) + vone_ref[...]).T.astype(BF16)
    for hd in range(DA_HEADS):
        va_ref[0, 0, hd * DA_VROWS:(hd + 1) * DA_VROWS, :] = va_t[hd * LANES:hd * LANES + DA_VROWS]

    cosb, sinb = cosb_ref[...], sinb_ref[...]
    bq = _group_rms(proj(C_BQ, C_BK), g64_ref) * gqb_ref[...]
    qb_ref[0] = _rope(bq, cosb, sinb, HEAD_DIM // 2).astype(BF16)
    bkv = proj(C_BK, C_CX)
    bk = _group_rms(bkv[:, :SW_KV], g64_ref.at[:SW_KV, :SW_KV]) * gkb_ref[...]
    kb_ref[0] = _rope(bk, cosb[:, :SW_KV], sinb[:, :SW_KV], HEAD_DIM // 2).astype(BF16)
    vb_ref[0] = bkv[:, SW_KV:].astype(BF16)

    cxg = proj(C_CX, D_INX)
    cx_ref[0] = cxg[:, :LRU_WIDTH]
    cg_ref[0] = cxg[:, LRU_WIDTH:]


def _input_projection(x, shift, scale, lw, rope):
    bsz, t_len, d = x.shape
    tm = min(TM, t_len)
    per_b = shift.shape[0] > 1
    mod_map = (lambda b, j: (b, 0, 0)) if per_b else (lambda b, j: (0, 0, 0))
    tok = lambda n: pl.BlockSpec((1, tm, n), lambda b, j: (b, j, 0))
    tab = lambda n: pl.BlockSpec((tm, n), lambda b, j: (j, 0))
    widths = (DA_Q, DA_Q, DA_VEXT, SW_Q, SW_KV, SW_KV, LRU_WIDTH, LRU_WIDTH)
    dtypes = (BF16,) * 6 + (F32, F32)
    out_shape = [jax.ShapeDtypeStruct((bsz, t_len, n), dt) for n, dt in zip(widths, dtypes)]
    out_specs = [tok(n) for n in widths]
    out_shape[2] = jax.ShapeDtypeStruct((bsz, t_len // tm, DA_HEADS * DA_VROWS, tm), BF16)
    out_specs[2] = pl.BlockSpec((1, 1, DA_HEADS * DA_VROWS, tm), lambda b, j: (b, j, 0, 0))
    return pl.pallas_call(
        _inproj_kernel,
        out_shape=tuple(out_shape),
        grid=(bsz, t_len // tm),
        in_specs=[tok(d),
                  pl.BlockSpec((1, 1, d), mod_map), pl.BlockSpec((1, 1, d), mod_map),
                  _const_spec((1, d)), _const_spec((d, D_INX)), _const_spec((1, DA_VEXT)),
                  tab(DA_Q), tab(DA_Q), tab(SW_Q), tab(SW_Q),
                  _const_spec((1, DA_Q)), _const_spec((1, DA_Q)),
                  _const_spec((1, SW_Q)), _const_spec((1, SW_KV)),
                  _const_spec((DA_Q, DA_Q)), _const_spec((SW_Q, SW_Q))],
        out_specs=tuple(out_specs),
        compiler_params=_cparams(2),
        name="input_projection",
    )(x, shift, scale, lw["norm1_gain"], lw["w_in"], lw["v_one"], *rope,
      lw["gq_a"], lw["gk_a"], lw["gq_b"], lw["gk_b"], lw["g32"], lw["g64"])


def _da_kernel(*refs, n_seg, lambda_init):
    q_ref = refs[0]
    kv_refs = refs[1:1 + 2 * n_seg]
    hmask_ref, lamv_ref, sg_ref, o_ref, m_ref, acc_ref, s_ref = refs[1 + 2 * n_seg:]
    n_map = 2 * DA_HEADS

    m_ref[...] = jnp.full(m_ref.shape, NEG_INF, F32)
    acc_ref[...] = jnp.zeros(acc_ref.shape, F32)

    def scores(k_ref, tk, i, j):
        half = j // 4
        start = i * tk if isinstance(i, int) else pl.multiple_of(i * tk, tk)
        qm = q_ref[0, :, half * LANES:(half + 1) * LANES] * hmask_ref[j:j + 1, :]
        kblk = k_ref[0, pl.ds(start, tk), half * LANES:(half + 1) * LANES]
        return _nt_dot(kblk, qm)

    def tile(k_ref, v_ref, i, from_scratch, nxt):
        tk = v_ref.shape[3]
        pending = {}
        if not from_scratch:
            for j in range(DA_AHEAD):
                pending[j] = scores(k_ref, tk, i, j)
        for j in range(n_map):
            s_t = s_ref[j, :tk] if from_scratch and j < DA_AHEAD else pending.pop(j)
            ja = j + DA_AHEAD
            if ja < n_map:
                pending[ja] = scores(k_ref, tk, i, ja)
            elif nxt is not None:
                s_ref[ja - n_map, :nxt[1]] = scores(nxt[0], nxt[1], nxt[2], ja - n_map)
            m_old = m_ref[j, 0:1, :]
            m_new = jnp.maximum(m_old, jnp.max(s_t, axis=0, keepdims=True))
            p_t = jnp.exp2(s_t - m_new).astype(BF16)
            pv = _dot(v_ref[0, i, (j // 2) * DA_VROWS:(j // 2 + 1) * DA_VROWS, :], p_t)
            acc_ref[j] = jnp.exp2(m_old - m_new) * acc_ref[j] + pv
            m_ref[j] = jnp.broadcast_to(m_new, m_ref.shape[1:])

    segs = [(kv_refs[2 * si], kv_refs[2 * si + 1]) for si in range(n_seg)]
    for si, (k_ref, v_ref) in enumerate(segs):
        n_t, tk = v_ref.shape[1], v_ref.shape[3]
        after = (segs[si + 1][0], segs[si + 1][1].shape[3], 0) if si + 1 < n_seg else None
        lo, hi = 0, n_t
        if si == 0:
            tile(k_ref, v_ref, 0, False, (k_ref, tk, 1) if n_t > 1 else after)
            lo = 1
        if after is not None and hi > lo:
            hi -= 1
        if hi > lo:
            unroll = DA_UNROLL if (hi - lo) % DA_UNROLL == 0 else 1

            def body(g, carry, k_ref=k_ref, v_ref=v_ref, n_t=n_t, tk=tk, lo=lo, unroll=unroll):
                for u in range(unroll):
                    i = lo + g * unroll + u
                    tile(k_ref, v_ref, i, True, (k_ref, tk, jnp.minimum(i + 1, n_t - 1)))
                return carry
            lax.fori_loop(0, (hi - lo) // unroll, body, 0)
        if after is not None and n_t > lo:
            tile(k_ref, v_ref, n_t - 1, True, after)

    lv = lamv_ref[...]
    lam = (jnp.exp(jnp.sum(lv[0:1] * lv[1:2], axis=1, keepdims=True))
           - jnp.exp(jnp.sum(lv[2:3] * lv[3:4], axis=1, keepdims=True)) + lambda_init)

    def head_out(h):
        a1, a2 = acc_ref[2 * h], acc_ref[2 * h + 1]
        o = (a1 / a1[DA_V_DIM:DA_V_DIM + 1] - lam * (a2 / a2[DA_V_DIM:DA_V_DIM + 1]))[:DA_V_DIM]
        ms = jnp.sum(o * o, axis=0, keepdims=True) * (1.0 / DA_V_DIM)
        return o * lax.rsqrt(ms + EPS)

    for c in range(DA_HEADS // 2):
        pair_t = jnp.concatenate([head_out(2 * c), head_out(2 * c + 1)], axis=0)
        o_ref[0, :, c * LANES:(c + 1) * LANES] = (pair_t.T * sg_ref[...]).astype(o_ref.dtype)


def _diff_attention(q, kvs, lw, lambda_init):
    bsz, t_q, _ = q.shape
    tq = min(DA_TQ, t_q)
    kv_specs, kv_args = [], []
    for k, v in kvs:
        kv_specs += [pl.BlockSpec((1,) + k.shape[1:], lambda b, j: (b, 0, 0)),
                     pl.BlockSpec((1,) + v.shape[1:], lambda b, j: (b, 0, 0, 0))]
        kv_args += [k, v]
    sg = lw["da_sg"] * (1.0 - lambda_init)
    return pl.pallas_call(
        functools.partial(_da_kernel, n_seg=len(kvs), lambda_init=lambda_init),
        out_shape=jax.ShapeDtypeStruct((bsz, t_q, DA_V), BF16),
        grid=(bsz, t_q // tq),
        in_specs=[pl.BlockSpec((1, tq, DA_Q), lambda b, j: (b, j, 0))] + kv_specs
                 + [_const_spec((2 * DA_HEADS, LANES)), _const_spec((4, DA_QK_DIM)),
                    _const_spec((1, LANES))],
        out_specs=pl.BlockSpec((1, tq, DA_V), lambda b, j: (b, j, 0)),
        scratch_shapes=[pltpu.VMEM((2 * DA_HEADS, SUBLANES, tq), F32),
                        pltpu.VMEM((2 * DA_HEADS, DA_VROWS, tq), F32),
                        pltpu.VMEM((DA_AHEAD, max(v.shape[3] for _, v in kvs), tq), F32)],
        compiler_params=_cparams(2),
        name="diff_attention",
    )(q, *kv_args, lw["da_hmask"], lw["da_lamv"], sg)


def _sw_kernel(*refs, has_lat, tq, t_len):
    if has_lat:
        sink_ref, q_ref, kc_ref, vc_ref, k_ref, v_ref, hm_ref, o_ref = refs
    else:
        sink_ref, q_ref, kc_ref, vc_ref, hm_ref, o_ref = refs
    def values_t(v):
        n_k = v.shape[0]
        v_t = v.astype(F32).T
        tail = (lax.broadcasted_iota(jnp.int32, (BF16_ROWS, n_k), 0) == 0).astype(F32)
        return [jnp.concatenate([v_t[kv * HEAD_DIM:(kv + 1) * HEAD_DIM], tail], axis=0).astype(BF16)
                for kv in range(SW_KV_HEADS)]

    kc = kc_ref[0]
    vc_t = values_t(vc_ref[0])
    if has_lat:
        band = tq + 2 * WINDOW
        q0 = pl.program_id(1) * tq
        start = pl.multiple_of(jnp.clip(q0 - WINDOW, 0, t_len - band), WINDOW)
        kb = k_ref[0, pl.ds(start, band), :]
        vb_t = values_t(v_ref[0, pl.ds(start, band), :])
        qpos = q0 + lax.broadcasted_iota(jnp.int32, (1, tq), 1)
        kpos = start + lax.broadcasted_iota(jnp.int32, (band, 1), 0)
        valid = jnp.abs(qpos - kpos) <= WINDOW

    combos = [(g, kv) for g in range(SW_GROUP) for kv in range(SW_KV_HEADS)]

    def scores(c):
        g, kv = combos[c]
        qm = q_ref[0, :, g * LANES:(g + 1) * LANES] * hm_ref[kv:kv + 1, :]
        s_lat = _nt_dot(kb, qm) if has_lat else None
        return _nt_dot(kc, qm), s_lat

    pending = {c: scores(c) for c in range(SW_AHEAD)}
    outs = {}
    for c, (g, kv) in enumerate(combos):
        s_ctx, s_lat = pending.pop(c)
        if c + SW_AHEAD < len(combos):
            pending[c + SW_AHEAD] = scores(c + SW_AHEAD)
        sink = sink_ref[kv * SW_GROUP + g]
        m = jnp.maximum(jnp.max(s_ctx, axis=0, keepdims=True), sink)
        if has_lat:
            s_lat = jnp.where(valid, s_lat, NEG_INF)
            m = jnp.maximum(m, jnp.max(s_lat, axis=0, keepdims=True))
        acc = _dot(vc_t[kv], jnp.exp2(s_ctx - m).astype(BF16))
        if has_lat:
            acc = acc + _dot(vb_t[kv], jnp.exp2(s_lat - m).astype(BF16))
        l = acc[HEAD_DIM:HEAD_DIM + 1] + jnp.exp2(sink - m)
        outs[(g, kv)] = acc[:HEAD_DIM] / l
        if kv == SW_KV_HEADS - 1:
            pair_t = jnp.concatenate([outs.pop((g, k2)) for k2 in range(SW_KV_HEADS)], axis=0)
            o_ref[0, :, g * LANES:(g + 1) * LANES] = pair_t.T.astype(o_ref.dtype)


def _window_attention(q, kc, vc, k, v, lw):
    bsz, t_q, _ = q.shape
    has_lat = k is not None
    tq = min(SW_TQ, t_q)
    n_ctx = kc.shape[1]
    full = lambda t: pl.BlockSpec((1, t, SW_KV), lambda b, j: (b, 0, 0))
    specs = [pl.BlockSpec(memory_space=pltpu.SMEM),
             pl.BlockSpec((1, tq, SW_Q), lambda b, j: (b, j, 0)), full(n_ctx), full(n_ctx)]
    args = [lw["sw_sink"], q, kc, vc]
    if has_lat:
        specs += [full(t_q), full(t_q)]
        args += [k, v]
    specs.append(_const_spec((SW_KV_HEADS, LANES)))
    args.append(lw["sw_hmask"])
    return pl.pallas_call(
        functools.partial(_sw_kernel, has_lat=has_lat, tq=tq, t_len=t_q),
        out_shape=jax.ShapeDtypeStruct((bsz, t_q, SW_Q), BF16),
        grid=(bsz, t_q // tq),
        in_specs=specs,
        out_specs=pl.BlockSpec((1, tq, SW_Q), lambda b, j: (b, j, 0)),
        compiler_params=_cparams(2),
        name="window_attention",
    )(*args)


def _lru_kernel(xf_ref, xb_ref, xc_ref, cw_ref, cb_ref, wa_ref, ba_ref, wx_ref, bx_ref, lam_ref,
                hf_ref, hb_ref, hfc_ref, hbc_ref, halo_ref, carry_ref, *, tc):
    j = pl.program_id(1)
    is_ctx = j == 0
    sub = lax.broadcasted_iota(jnp.int32, (1, SUBLANES, 1), 1)
    row8 = lax.broadcasted_iota(jnp.int32, (SUBLANES, 1), 0)

    for d, x_lat_ref, out_ref, outc_ref in ((0, xf_ref, hf_ref, hfc_ref), (1, xb_ref, hb_ref, hbc_ref)):
        x = jnp.where(is_ctx, xc_ref[0], x_lat_ref[0])
        halo = jnp.where(j <= 1, 0.0, halo_ref[d])
        cw = cw_ref[d]
        xc = cb_ref[d] + x * cw[LRU_CONV - 1:LRU_CONV]
        for k in range(LRU_CONV - 1):
            s = LRU_CONV - 1 - k
            if d == 0:
                xs = pltpu.roll(x, s, 0)
                edge = jnp.where(row8 < s, pltpu.roll(halo, s, 0), xs[:SUBLANES])
                xs = jnp.concatenate([edge, xs[SUBLANES:]], axis=0)
            else:
                xs = pltpu.roll(x, tc - s, 0)
                edge = jnp.where(row8 >= SUBLANES - s, pltpu.roll(halo, SUBLANES - s, 0),
                                 xs[tc - SUBLANES:])
                xs = jnp.concatenate([xs[:tc - SUBLANES], edge], axis=0)
            xc = xc + xs * cw[k:k + 1]
        halo_ref[d] = x[tc - SUBLANES:] if d == 0 else x[:SUBLANES]

        xcb = xc.astype(BF16)
        r = jax.nn.sigmoid(_dot(xcb, wa_ref[d]) + ba_ref[d])
        gi = jax.nn.sigmoid(_dot(xcb, wx_ref[d]) + bx_ref[d])
        nl = -lam_ref[d]
        softplus = jnp.maximum(nl, 0.0) + jnp.log1p(jnp.exp(-jnp.abs(nl)))
        log_a = -LRU_C * r * softplus
        a = jnp.exp(log_a)
        b = jnp.sqrt(-jnp.tanh(log_a) * (a * a + 1.0)) * (gi * xc)

        n_grp = tc // SUBLANES
        a3 = a.reshape(n_grp, SUBLANES, a.shape[1])
        b3 = b.reshape(n_grp, SUBLANES, b.shape[1])
        step = 1
        while step < SUBLANES:
            keep = (sub < step) if d == 0 else (sub >= SUBLANES - step)
            shift = step if d == 0 else SUBLANES - step
            a_s = jnp.where(keep, 1.0, pltpu.roll(a3, shift, 1))
            b_s = jnp.where(keep, 0.0, pltpu.roll(b3, shift, 1))
            b3 = a3 * b_s + b3
            a3 = a3 * a_s
            step *= 2

        edge_row = SUBLANES - 1 if d == 0 else 0
        h_in = jnp.where(is_ctx, 0.0, carry_ref[d, edge_row:edge_row + 1, :])
        groups = [None] * n_grp
        for gi in (range(n_grp) if d == 0 else range(n_grp - 1, -1, -1)):
            groups[gi] = a3[gi] * h_in + b3[gi]
            h_in = groups[gi][edge_row:edge_row + 1]
        h = jnp.concatenate(groups, axis=0)
        carry_ref[d] = groups[n_grp - 1] if d == 0 else groups[0]

        @pl.when(is_ctx)
        def _():
            outc_ref[0] = h

        @pl.when(jnp.logical_not(is_ctx))
        def _():
            out_ref[0] = h


def _rglru_scans(cx, cxc, lw):
    bsz, t_len, n = cx.shape
    tc = LRU_TC
    assert cxc.shape[1] == tc and t_len % tc == 0
    n_lat = t_len // tc
    fwd = lambda b, j: (b, jnp.maximum(j - 1, 0), 0)
    bwd = lambda b, j: (b, n_lat - jnp.maximum(j, 1), 0)
    ctx = lambda b, j: (b, 0, 0)
    blk = lambda m: pl.BlockSpec((1, tc, n), m)
    return pl.pallas_call(
        functools.partial(_lru_kernel, tc=tc),
        out_shape=(jax.ShapeDtypeStruct(cx.shape, F32), jax.ShapeDtypeStruct(cx.shape, F32),
                   jax.ShapeDtypeStruct(cxc.shape, F32), jax.ShapeDtypeStruct(cxc.shape, F32)),
        grid=(bsz, n_lat + 1),
        in_specs=[blk(fwd), blk(bwd), blk(ctx),
                  _const_spec((2, LRU_CONV, n)), _const_spec((2, 1, n)),
                  _const_spec((2, n, n)), _const_spec((2, 1, n)),
                  _const_spec((2, n, n)), _const_spec((2, 1, n)), _const_spec((2, 1, n))],
        out_specs=(blk(fwd), blk(bwd), blk(ctx), blk(ctx)),
        scratch_shapes=[pltpu.VMEM((2, SUBLANES, n), F32), pltpu.VMEM((2, SUBLANES, n), F32)],
        compiler_params=_cparams(2),
        name="rglru_scans",
    )(cx, cx, cxc, lw["lru_conv_w"], lw["lru_conv_b"], lw["lru_wa"], lw["lru_ba"],
      lw["lru_wx"], lw["lru_bx"], lw["lru_lambda"])


def _gelu_tanh(x):
    return 0.5 * x * (1.0 + jnp.tanh(math.sqrt(2.0 / math.pi) * (x + 0.044715 * (x * x * x))))


def _outproj_kernel(x_ref, ya_ref, yb_ref, hf_ref, hb_ref, cg_ref, g1_ref, w_ref, o_ref, cat_ref):
    cat_ref[:, 0:DA_V] = ya_ref[0]
    cat_ref[:, DA_V:DA_V + SW_Q] = yb_ref[0]
    yc = (hf_ref[0] + hb_ref[0]) * _gelu_tanh(cg_ref[0])
    cat_ref[:, DA_V + SW_Q:] = yc.astype(BF16)
    o_ref[0] = x_ref[0] + g1_ref[0] * _dot(cat_ref[...], w_ref[...])


def _output_projection(x, ya, yb, hf, hb, cg, gate, lw):
    bsz, t_len, d = x.shape
    tm = min(TM, t_len)
    per_b = gate.shape[0] > 1
    mod_map = (lambda b, j: (b, 0, 0)) if per_b else (lambda b, j: (0, 0, 0))
    tok = lambda n: pl.BlockSpec((1, tm, n), lambda b, j: (b, j, 0))
    return pl.pallas_call(
        _outproj_kernel,
        out_shape=jax.ShapeDtypeStruct(x.shape, F32),
        grid=(bsz, t_len // tm),
        in_specs=[tok(d), tok(DA_V), tok(SW_Q), tok(LRU_WIDTH), tok(LRU_WIDTH), tok(LRU_WIDTH),
                  pl.BlockSpec((1, 1, d), mod_map), _const_spec((d, d))],
        out_specs=tok(d),
        scratch_shapes=[pltpu.VMEM((tm, d), BF16)],
        compiler_params=_cparams(2),
        name="output_projection",
    )(x, ya, yb, hf, hb, cg, gate, lw["w_out"])


def _ffn_kernel(xm_ref, xp_ref, xn_ref, sh_ref, sc_ref, g2_ref, ng_ref, wup_ref, cw_ref, cb_ref,
                wdn_ref, o_ref, h_ref, *, tm):
    j = pl.program_id(1)
    last = pl.num_programs(1) - 1
    gain, shift, scale = ng_ref[...], sh_ref[0], sc_ref[0]
    xm = xm_ref[0]
    hp = jnp.where(j > 0, _norm_modulate(xp_ref[0], gain, shift, scale), 0.0)
    hn = jnp.where(j < last, _norm_modulate(xn_ref[0], gain, shift, scale), 0.0)
    h_ref[0:FFN_HALO] = hp.astype(BF16)
    h_ref[FFN_HALO:FFN_HALO + tm] = _norm_modulate(xm, gain, shift, scale).astype(BF16)
    h_ref[FFN_HALO + tm:] = hn.astype(BF16)

    rows = tm + 2 * FFN_HALO
    acc = jnp.zeros((tm, D_MODEL), F32)
    for lo, hi in FFN_CHUNKS:
        gp = _dot(h_ref[...], wup_ref[:, lo:hi])
        g_prev = pltpu.roll(gp, 1, 0)[FFN_HALO:FFN_HALO + tm]
        g_next = pltpu.roll(gp, rows - 1, 0)[FFN_HALO:FFN_HALO + tm]
        cw = cw_ref[:, lo:hi]
        gate = (cb_ref[:, lo:hi] + g_prev * cw[0:1] + gp[FFN_HALO:FFN_HALO + tm] * cw[1:2]
                + g_next * cw[2:3])
        val = _dot(h_ref[FFN_HALO:FFN_HALO + tm], wup_ref[:, D_FF + lo:D_FF + hi])
        act = (gate * jax.nn.sigmoid(gate) * val).astype(BF16)
        acc = acc + _dot(act, wdn_ref[lo:hi, :])
    o_ref[0] = xm + g2_ref[0] * acc


def _conv_ffn(x, shift, scale, gate, lw):
    bsz, t_len, d = x.shape
    tm = min(TM, t_len)
    per_b = gate.shape[0] > 1
    mod_map = (lambda b, j: (b, 0, 0)) if per_b else (lambda b, j: (0, 0, 0))
    hb = tm // FFN_HALO
    n_hb = t_len // FFN_HALO
    prev_map = lambda b, j: (b, jnp.maximum(j * hb - 1, 0), 0)
    next_map = lambda b, j: (b, jnp.minimum((j + 1) * hb, n_hb - 1), 0)
    mod_spec = pl.BlockSpec((1, 1, d), mod_map)
    return pl.pallas_call(
        functools.partial(_ffn_kernel, tm=tm),
        out_shape=jax.ShapeDtypeStruct(x.shape, F32),
        grid=(bsz, t_len // tm),
        in_specs=[pl.BlockSpec((1, tm, d), lambda b, j: (b, j, 0)),
                  pl.BlockSpec((1, FFN_HALO, d), prev_map),
                  pl.BlockSpec((1, FFN_HALO, d), next_map),
                  mod_spec, mod_spec, mod_spec,
                  _const_spec((1, d)), _const_spec((d, 2 * D_FF)),
                  _const_spec((FFN_CONV, D_FF)), _const_spec((1, D_FF)), _const_spec((D_FF, d))],
        out_specs=pl.BlockSpec((1, tm, d), lambda b, j: (b, j, 0)),
        scratch_shapes=[pltpu.VMEM((tm + 2 * FFN_HALO, d), BF16)],
        compiler_params=_cparams(2),
        name="conv_ffn",
    )(x, x, x, shift, scale, gate, lw["norm2_gain"], lw["w_up"], lw["ffn_conv_w"],
      lw["ffn_conv_b"], lw["w_down"])


def _sw_head_order():
    return [kv * SW_GROUP + g for g in range(SW_GROUP) for kv in range(SW_KV_HEADS)]


def _rope_tables(rows, head_dim, n_heads):
    row = jnp.repeat(jnp.arange(rows, dtype=F32), GRID_W)
    col = jnp.tile(jnp.arange(GRID_W, dtype=F32), rows)
    quarter = head_dim // 4
    inv_freq = ROPE_BASE ** (-jnp.arange(quarter, dtype=F32) / quarter)
    ang = jnp.concatenate([row[:, None] * inv_freq, col[:, None] * inv_freq], axis=-1)
    cos, sin = jnp.cos(ang), jnp.sin(ang)
    return (jnp.tile(jnp.concatenate([cos, cos], axis=-1), (1, n_heads)),
            jnp.tile(jnp.concatenate([-sin, sin], axis=-1), (1, n_heads)))


def _block_diag_mean(n, group):
    idx = np.arange(n) // group
    return jnp.asarray((idx[:, None] == idx[None, :]).astype(np.float32) / group, dtype=BF16)


def _block_diag(w):
    two, nb, bi, bj = w.shape
    eye = jnp.eye(nb, dtype=w.dtype)
    return jnp.einsum("dhij,hg->dhigj", w, eye).reshape(two, nb * bi, nb * bj)


def _prepare_layer(p, i):
    d = D_MODEL
    split = np.cumsum((DA_Q, DA_Q, DA_V, SW_Q, SW_KV, SW_KV, LRU_WIDTH))
    w_in = p["w_in"][i]
    w_aq, w_ak, w_av, w_bq, w_bk, w_bv, w_cx, w_cg = jnp.split(w_in, split.tolist(), axis=1)
    w_av = jnp.pad(w_av.reshape(d, DA_HEADS, DA_V_DIM), ((0, 0), (0, 0), (0, LANES - DA_V_DIM)))
    order = _sw_head_order()
    w_bq = w_bq.reshape(d, SW_HEADS, HEAD_DIM)[:, order].reshape(d, SW_Q)
    w_inx = jnp.concatenate([w_aq, w_ak, w_av.reshape(d, DA_VEXT), w_bq, w_bk, w_bv, w_cx, w_cg],
                            axis=1).astype(BF16)
    v_one = np.zeros((1, DA_VEXT), np.float32)
    v_one[0, DA_V_DIM::LANES] = 1.0

    w_out = p["w_out"][i]
    w_ob = w_out[DA_V:DA_V + SW_Q].reshape(SW_HEADS, HEAD_DIM, d)[jnp.asarray(order)]
    w_out = jnp.concatenate([w_out[:DA_V], w_ob.reshape(SW_Q, d), w_out[DA_V + SW_Q:]], axis=0)

    da_hmask = np.zeros((2 * DA_HEADS, LANES), np.float32)
    for j in range(2 * DA_HEADS):
        off = (j % 4) * DA_QK_DIM
        da_hmask[j, off:off + DA_QK_DIM] = 1.0
    sw_hmask = np.zeros((SW_KV_HEADS, LANES), np.float32)
    for kv in range(SW_KV_HEADS):
        sw_hmask[kv, kv * HEAD_DIM:(kv + 1) * HEAD_DIM] = 1.0

    return dict(
        norm1_gain=p["norm1_gain"][i].reshape(1, d), norm2_gain=p["norm2_gain"][i].reshape(1, d),
        w_in=w_inx, v_one=jnp.asarray(v_one),
        gq_a=(jnp.tile(p["da_q_gain"][i], 2 * DA_HEADS).reshape(1, DA_Q)
              * (DA_QK_DIM ** -0.5 * math.log2(math.e))),
        gk_a=jnp.tile(p["da_k_gain"][i], 2 * DA_HEADS).reshape(1, DA_Q),
        gq_b=(jnp.tile(p["sw_q_gain"][i], SW_HEADS).reshape(1, SW_Q)
              * (HEAD_DIM ** -0.5 * math.log2(math.e))),
        gk_b=jnp.tile(p["sw_k_gain"][i], SW_KV_HEADS).reshape(1, SW_KV),
        g32=_block_diag_mean(DA_Q, DA_QK_DIM), g64=_block_diag_mean(SW_Q, HEAD_DIM),
        da_hmask=jnp.asarray(da_hmask, dtype=BF16),
        da_lamv=jnp.stack([p["da_lam_q1"][i], p["da_lam_k1"][i], p["da_lam_q2"][i], p["da_lam_k2"][i]]),
        da_sg=jnp.tile(p["da_sub_gain"][i], LANES // DA_V_DIM).reshape(1, LANES),
        sw_sink=p["sw_sink"][i] * math.log2(math.e), sw_hmask=jnp.asarray(sw_hmask, dtype=BF16),
        lru_conv_w=p["lru_conv_w"][i], lru_conv_b=p["lru_conv_b"][i].reshape(2, 1, LRU_WIDTH),
        lru_wa=_block_diag(p["lru_wa"][i]).astype(BF16), lru_ba=p["lru_ba"][i].reshape(2, 1, LRU_WIDTH),
        lru_wx=_block_diag(p["lru_wx"][i]).astype(BF16), lru_bx=p["lru_bx"][i].reshape(2, 1, LRU_WIDTH),
        lru_lambda=p["lru_lambda"][i].reshape(2, 1, LRU_WIDTH),
        w_out=w_out.astype(BF16), w_up=p["w_up"][i].astype(BF16),
        ffn_conv_w=p["ffn_conv_w"][i], ffn_conv_b=p["ffn_conv_b"][i].reshape(1, D_FF),
        w_down=p["w_down"][i].astype(BF16),
    )


def kernel(x, c, ctx, c_ctx, w_mod, b_mod, norm1_gain, norm2_gain, w_in, da_q_gain, da_k_gain, da_lam_q1, da_lam_k1, da_lam_q2, da_lam_k2, da_sub_gain, sw_q_gain, sw_k_gain, sw_sink, lru_conv_w, lru_conv_b, lru_wa, lru_ba, lru_wx, lru_bx, lru_lambda, w_out, w_up, ffn_conv_w, ffn_conv_b, w_down):
    p = dict(w_in=w_in, norm1_gain=norm1_gain, norm2_gain=norm2_gain, da_q_gain=da_q_gain,
             da_k_gain=da_k_gain, da_lam_q1=da_lam_q1, da_lam_k1=da_lam_k1, da_lam_q2=da_lam_q2,
             da_lam_k2=da_lam_k2, da_sub_gain=da_sub_gain, sw_q_gain=sw_q_gain, sw_k_gain=sw_k_gain,
             sw_sink=sw_sink, lru_conv_w=lru_conv_w, lru_conv_b=lru_conv_b, lru_wa=lru_wa,
             lru_ba=lru_ba, lru_wx=lru_wx, lru_bx=lru_bx, lru_lambda=lru_lambda, w_out=w_out,
             w_up=w_up, ffn_conv_w=ffn_conv_w, ffn_conv_b=ffn_conv_b, w_down=w_down)
    bsz, n_tok, d = x.shape
    n_ctx = ctx.shape[1]
    depth = w_mod.shape[0]

    cc = jnp.zeros((2 * SUBLANES, d), F32).at[:bsz].set(c).at[bsz].set(c_ctx)
    mod_all = _modulation(cc, w_mod, b_mod)

    rope_lat = (_rope_tables(n_tok // GRID_W, DA_QK_DIM, 2 * DA_HEADS)
                + _rope_tables(n_tok // GRID_W, HEAD_DIM, SW_HEADS))
    rope_ctx = (jnp.ones((n_ctx, DA_Q), F32), jnp.zeros((n_ctx, DA_Q), F32),
                jnp.ones((n_ctx, SW_Q), F32), jnp.zeros((n_ctx, SW_Q), F32))

    xc = ctx
    for i in range(depth):
        lw = _prepare_layer(p, i)
        lambda_init = 0.8 - 0.6 * math.exp(-0.3 * i)
        ctx_out = i < depth - 1
        mod = mod_all[i, :bsz].reshape(bsz, 1, N_MOD, d)
        sh1, sc1, g1, sh2, sc2, g2 = [mod[:, :, m] for m in range(N_MOD)]
        mod_c = mod_all[i, bsz].reshape(1, 1, N_MOD, d)
        sh1c, sc1c, g1c, sh2c, sc2c, g2c = [mod_c[:, :, m] for m in range(N_MOD)]

        qa, ka, va, qb, kb, vb, cx, cg = _input_projection(x, sh1, sc1, lw, rope_lat)
        qac, kac, vac, qbc, kbc, vbc, cxc, cgc = _input_projection(xc, sh1c, sc1c, lw, rope_ctx)

        ya = _diff_attention(qa, [(kac, vac), (ka, va)], lw, lambda_init)
        yb = _window_attention(qb, kbc, vbc, kb, vb, lw)
        hf, hb, hfc, hbc = _rglru_scans(cx, cxc, lw)

        x = _output_projection(x, ya, yb, hf, hb, cg, g1, lw)
        x = _conv_ffn(x, sh2, sc2, g2, lw)
        if ctx_out:
            yac = _diff_attention(qac, [(kac, vac)], lw, lambda_init)
            ybc = _window_attention(qbc, kbc, vbc, None, None, lw)
            xc = _output_projection(xc, yac, ybc, hfc, hbc, cgc, g1c, lw)
            xc = _conv_ffn(xc, sh2c, sc2c, g2c, lw)
    return x
```

```python
import functools
import math

import numpy as np
import jax
import jax.numpy as jnp
from jax import lax
from jax.experimental import pallas as pl
from jax.experimental.pallas import tpu as pltpu

F32 = jnp.float32
BF16 = jnp.bfloat16

D_MODEL = 1024
GRID_W = 64
N_MOD = 6
EPS = 1e-6
NEG_INF = -1e30
ROPE_BASE = 10000.0
DA_HEADS = 4
DA_QK_DIM = 32
DA_V_DIM = 64
HEAD_DIM = 64
SW_HEADS = 6
SW_KV_HEADS = 2
SW_GROUP = SW_HEADS // SW_KV_HEADS
WINDOW = 128
LRU_WIDTH = 384
LRU_BLOCKS = 6
LRU_BLOCK_DIM = LRU_WIDTH // LRU_BLOCKS
LRU_CONV = 4
LRU_C = 8.0
D_FF = 2816
FFN_CONV = 3

DA_Q = DA_HEADS * 2 * DA_QK_DIM
DA_V = DA_HEADS * DA_V_DIM
SW_Q = SW_HEADS * HEAD_DIM
SW_KV = SW_KV_HEADS * HEAD_DIM

LANES = 128
SUBLANES = 8
BF16_ROWS = 16
VMEM_LIMIT = 56 * 1024 * 1024

DA_VEXT = DA_HEADS * LANES
DA_VROWS = DA_V_DIM + BF16_ROWS
C_AQ = 0
C_AK = C_AQ + DA_Q
C_AV = C_AK + DA_Q
C_BQ = C_AV + DA_VEXT
C_BK = C_BQ + SW_Q
C_BV = C_BK + SW_KV
C_CX = C_BV + SW_KV
C_CG = C_CX + LRU_WIDTH
D_INX = C_CG + LRU_WIDTH

TM = 512
FFN_HALO = BF16_ROWS
FFN_CHUNKS = ((0, 1024), (1024, 2048), (2048, D_FF))
DA_TQ = 256
DA_AHEAD = 3
DA_UNROLL = 4
SW_AHEAD = 2
SW_TQ = 256
LRU_TC = 256


def _cparams(n_axes):
    return pltpu.CompilerParams(dimension_semantics=("arbitrary",) * n_axes,
                                vmem_limit_bytes=VMEM_LIMIT)


def _const_spec(shape):
    nd = len(shape)
    return pl.BlockSpec(shape, lambda *_: (0,) * nd, pipeline_mode=pl.Buffered(1))


def _nt_dot(a, b):
    return lax.dot_general(a, b, (((1,), (1,)), ((), ())), preferred_element_type=F32)


def _dot(a, b):
    return jnp.dot(a, b, preferred_element_type=F32)


def _mod_kernel(c_ref, w_ref, b_ref, o_ref):
    c = c_ref[...]
    s = c * jax.nn.sigmoid(c)
    o_ref[0] = _dot(s.astype(BF16), w_ref[0].astype(BF16)) + b_ref[0]


def _modulation(cc, w_mod, b_mod):
    depth, d, n = w_mod.shape
    tn = 1536
    return pl.pallas_call(
        _mod_kernel,
        out_shape=jax.ShapeDtypeStruct((depth, cc.shape[0], n), F32),
        grid=(depth, n // tn),
        in_specs=[pl.BlockSpec(cc.shape, lambda i, j: (0, 0)),
                  pl.BlockSpec((1, d, tn), lambda i, j: (i, 0, j)),
                  pl.BlockSpec((1, 1, tn), lambda i, j: (i, 0, j))],
        out_specs=pl.BlockSpec((1, cc.shape[0], tn), lambda i, j: (i, 0, j)),
        compiler_params=_cparams(2),
        name="modulation",
    )(cc, w_mod, b_mod.reshape(depth, 1, n))


def _norm_modulate(x, gain, shift, scale):
    ms = jnp.mean(x * x, axis=-1, keepdims=True)
    h = x * lax.rsqrt(ms + EPS) * gain
    return h * (1.0 + scale) + shift


def _group_rms(a, g_ref):
    sq = a * a
    hi = sq.astype(BF16)
    lo = (sq - hi.astype(F32)).astype(BF16)
    g = g_ref[...]
    ms = _dot(hi, g) + _dot(lo, g)
    return a * lax.rsqrt(ms + EPS)


def _rope(y, cos, sin_signed, half):
    n = y.shape[1]
    lane = lax.broadcasted_iota(jnp.int32, y.shape, 1)
    first = (lane % (2 * half)) < half
    partner = jnp.where(first, pltpu.roll(y, n - half, 1), pltpu.roll(y, half, 1))
    return y * cos + partner * sin_signed


def _inproj_kernel(x_ref, sh_ref, sc_ref, ng_ref, w_ref, vone_ref,
                   cosa_ref, sina_ref, cosb_ref, sinb_ref,
                   gqa_ref, gka_ref, gqb_ref, gkb_ref, g32_ref, g64_ref,
                   qa_ref, ka_ref, va_ref, qb_ref, kb_ref, vb_ref, cx_ref, cg_ref):
    h = _norm_modulate(x_ref[0], ng_ref[...], sh_ref[0], sc_ref[0]).astype(BF16)
    full = _dot(h, w_ref[...])

    def proj(lo, hi):
        return full[:, lo:hi]

    cosa, sina = cosa_ref[...], sina_ref[...]
    aq = _group_rms(proj(C_AQ, C_AK), g32_ref) * gqa_ref[...]
    qa_ref[0] = _rope(aq, cosa, sina, DA_QK_DIM // 2).astype(BF16)
    ak = _group_rms(proj(C_AK, C_AV), g32_ref) * gka_ref[...]
    ka_ref[0] = _rope(ak, cosa, sina, DA_QK_DIM // 2).astype(BF16)
    va_t = (proj(C_AV, C_BQ) + vone_ref[...]).T.astype(BF16)
    for hd in range(DA_HEADS):
        va_ref[0, 0, hd * DA_VROWS:(hd + 1) * DA_VROWS, :] = va_t[hd * LANES:hd * LANES + DA_VROWS]

    cosb, sinb = cosb_ref[...], sinb_ref[...]
    bq = _group_rms(proj(C_BQ, C_BK), g64_ref) * gqb_ref[...]
    qb_ref[0] = _rope(bq, cosb, sinb, HEAD_DIM // 2).astype(BF16)
    bkv = proj(C_BK, C_CX)
    bk = _group_rms(bkv[:, :SW_KV], g64_ref.at[:SW_KV, :SW_KV]) * gkb_ref[...]
    kb_ref[0] = _rope(bk, cosb[:, :SW_KV], sinb[:, :SW_KV], HEAD_DIM // 2).astype(BF16)
    vb_ref[0] = bkv[:, SW_KV:].astype(BF16)

    cxg = proj(C_CX, D_INX)
    cx_ref[0] = cxg[:, :LRU_WIDTH]
    cg_ref[0] = cxg[:, LRU_WIDTH:].astype(cg_ref.dtype)


def _input_projection(x, shift, scale, lw, rope):
    bsz, t_len, d = x.shape
    tm = min(TM, t_len)
    per_b = shift.shape[0] > 1
    mod_map = (lambda b, j: (b, 0, 0)) if per_b else (lambda b, j: (0, 0, 0))
    tok = lambda n: pl.BlockSpec((1, tm, n), lambda b, j: (b, j, 0))
    tab = lambda n: pl.BlockSpec((tm, n), lambda b, j: (j, 0))
    widths = (DA_Q, DA_Q, DA_VEXT, SW_Q, SW_KV, SW_KV, LRU_WIDTH, LRU_WIDTH)
    dtypes = (BF16,) * 6 + (F32, BF16)
    out_shape = [jax.ShapeDtypeStruct((bsz, t_len, n), dt) for n, dt in zip(widths, dtypes)]
    out_specs = [tok(n) for n in widths]
    out_shape[2] = jax.ShapeDtypeStruct((bsz, t_len // tm, DA_HEADS * DA_VROWS, tm), BF16)
    out_specs[2] = pl.BlockSpec((1, 1, DA_HEADS * DA_VROWS, tm), lambda b, j: (b, j, 0, 0))
    return pl.pallas_call(
        _inproj_kernel,
        out_shape=tuple(out_shape),
        grid=(bsz, t_len // tm),
        in_specs=[tok(d),
                  pl.BlockSpec((1, 1, d), mod_map), pl.BlockSpec((1, 1, d), mod_map),
                  _const_spec((1, d)), _const_spec((d, D_INX)), _const_spec((1, DA_VEXT)),
                  tab(DA_Q), tab(DA_Q), tab(SW_Q), tab(SW_Q),
                  _const_spec((1, DA_Q)), _const_spec((1, DA_Q)),
                  _const_spec((1, SW_Q)), _const_spec((1, SW_KV)),
                  _const_spec((DA_Q, DA_Q)), _const_spec((SW_Q, SW_Q))],
        out_specs=tuple(out_specs),
        compiler_params=_cparams(2),
        name="input_projection",
    )(x, shift, scale, lw["norm1_gain"], lw["w_in"], lw["v_one"], *rope,
      lw["gq_a"], lw["gk_a"], lw["gq_b"], lw["gk_b"], lw["g32"], lw["g64"])


def _da_kernel(*refs, n_seg, lambda_init):
    q_ref = refs[0]
    kv_refs = refs[1:1 + 2 * n_seg]
    hmask_ref, lamv_ref, sg_ref, o_ref, m_ref, acc_ref, s_ref = refs[1 + 2 * n_seg:]
    n_map = 2 * DA_HEADS

    m_ref[...] = jnp.full(m_ref.shape, NEG_INF, F32)
    acc_ref[...] = jnp.zeros(acc_ref.shape, F32)

    def scores(k_ref, tk, i, j):
        half = j // 4
        start = i * tk if isinstance(i, int) else pl.multiple_of(i * tk, tk)
        qm = q_ref[0, :, half * LANES:(half + 1) * LANES] * hmask_ref[j:j + 1, :]
        kblk = k_ref[0, pl.ds(start, tk), half * LANES:(half + 1) * LANES]
        return _nt_dot(kblk, qm)

    def tile(k_ref, v_ref, i, from_scratch, nxt):
        tk = v_ref.shape[3]
        pending = {}
        if not from_scratch:
            for j in range(DA_AHEAD):
                pending[j] = scores(k_ref, tk, i, j)
        for j in range(n_map):
            s_t = s_ref[j, :tk] if from_scratch and j < DA_AHEAD else pending.pop(j)
            ja = j + DA_AHEAD
            if ja < n_map:
                pending[ja] = scores(k_ref, tk, i, ja)
            elif nxt is not None:
                s_ref[ja - n_map, :nxt[1]] = scores(nxt[0], nxt[1], nxt[2], ja - n_map)
            m_old = m_ref[j, 0:1, :]
            s_b = s_t.astype(BF16)
            m_new = jnp.maximum(m_old, jnp.max(s_b, axis=0, keepdims=True).astype(F32))
            p_t = jnp.exp2(s_b - m_new.astype(BF16))
            pv = _dot(v_ref[0, i, (j // 2) * DA_VROWS:(j // 2 + 1) * DA_VROWS, :], p_t)
            acc_ref[j] = jnp.exp2(m_old - m_new) * acc_ref[j] + pv
            m_ref[j] = jnp.broadcast_to(m_new, m_ref.shape[1:])

    segs = [(kv_refs[2 * si], kv_refs[2 * si + 1]) for si in range(n_seg)]
    for si, (k_ref, v_ref) in enumerate(segs):
        n_t, tk = v_ref.shape[1], v_ref.shape[3]
        after = (segs[si + 1][0], segs[si + 1][1].shape[3], 0) if si + 1 < n_seg else None
        lo, hi = 0, n_t
        if si == 0:
            tile(k_ref, v_ref, 0, False, (k_ref, tk, 1) if n_t > 1 else after)
            lo = 1
        if after is not None and hi > lo:
            hi -= 1
        if hi > lo:
            unroll = DA_UNROLL if (hi - lo) % DA_UNROLL == 0 else 1

            def body(g, carry, k_ref=k_ref, v_ref=v_ref, n_t=n_t, tk=tk, lo=lo, unroll=unroll):
                for u in range(unroll):
                    i = lo + g * unroll + u
                    tile(k_ref, v_ref, i, True, (k_ref, tk, jnp.minimum(i + 1, n_t - 1)))
                return carry
            lax.fori_loop(0, (hi - lo) // unroll, body, 0)
        if after is not None and n_t > lo:
            tile(k_ref, v_ref, n_t - 1, True, after)

    lv = lamv_ref[...]
    lam = (jnp.exp(jnp.sum(lv[0:1] * lv[1:2], axis=1, keepdims=True))
           - jnp.exp(jnp.sum(lv[2:3] * lv[3:4], axis=1, keepdims=True)) + lambda_init)

    def head_out(h):
        a1, a2 = acc_ref[2 * h], acc_ref[2 * h + 1]
        o = (a1 / a1[DA_V_DIM:DA_V_DIM + 1] - lam * (a2 / a2[DA_V_DIM:DA_V_DIM + 1]))[:DA_V_DIM]
        ms = jnp.sum(o * o, axis=0, keepdims=True) * (1.0 / DA_V_DIM)
        return o * lax.rsqrt(ms + EPS)

    for c in range(DA_HEADS // 2):
        pair_t = jnp.concatenate([head_out(2 * c), head_out(2 * c + 1)], axis=0)
        o_ref[0, :, c * LANES:(c + 1) * LANES] = (pair_t.T * sg_ref[...]).astype(o_ref.dtype)


def _diff_attention(q, kvs, lw, lambda_init):
    bsz, t_q, _ = q.shape
    tq = min(DA_TQ, t_q)
    kv_specs, kv_args = [], []
    for k, v in kvs:
        kv_specs += [pl.BlockSpec((1,) + k.shape[1:], lambda b, j: (b, 0, 0)),
                     pl.BlockSpec((1,) + v.shape[1:], lambda b, j: (b, 0, 0, 0))]
        kv_args += [k, v]
    sg = lw["da_sg"] * (1.0 - lambda_init)
    return pl.pallas_call(
        functools.partial(_da_kernel, n_seg=len(kvs), lambda_init=lambda_init),
        out_shape=jax.ShapeDtypeStruct((bsz, t_q, DA_V), BF16),
        grid=(bsz, t_q // tq),
        in_specs=[pl.BlockSpec((1, tq, DA_Q), lambda b, j: (b, j, 0))] + kv_specs
                 + [_const_spec((2 * DA_HEADS, LANES)), _const_spec((4, DA_QK_DIM)),
                    _const_spec((1, LANES))],
        out_specs=pl.BlockSpec((1, tq, DA_V), lambda b, j: (b, j, 0)),
        scratch_shapes=[pltpu.VMEM((2 * DA_HEADS, SUBLANES, tq), F32),
                        pltpu.VMEM((2 * DA_HEADS, DA_VROWS, tq), F32),
                        pltpu.VMEM((DA_AHEAD, max(v.shape[3] for _, v in kvs), tq), F32)],
        compiler_params=_cparams(2),
        name="diff_attention",
    )(q, *kv_args, lw["da_hmask"], lw["da_lamv"], sg)


def _sw_kernel(*refs, has_lat, tq, t_len):
    if has_lat:
        sink_ref, q_ref, kc_ref, vc_ref, k_ref, v_ref, hm_ref, o_ref = refs
    else:
        sink_ref, q_ref, kc_ref, vc_ref, hm_ref, o_ref = refs
    def values_t(v):
        n_k = v.shape[0]
        v_t = v.astype(F32).T
        tail = (lax.broadcasted_iota(jnp.int32, (BF16_ROWS, n_k), 0) == 0).astype(F32)
        return [jnp.concatenate([v_t[kv * HEAD_DIM:(kv + 1) * HEAD_DIM], tail], axis=0).astype(BF16)
                for kv in range(SW_KV_HEADS)]

    kc = kc_ref[0]
    vc_t = values_t(vc_ref[0])
    if has_lat:
        band = tq + 2 * WINDOW
        q0 = pl.program_id(1) * tq
        start = pl.multiple_of(jnp.clip(q0 - WINDOW, 0, t_len - band), WINDOW)
        kb = k_ref[0, pl.ds(start, band), :]
        vb_t = values_t(v_ref[0, pl.ds(start, band), :])
        qpos = q0 + lax.broadcasted_iota(jnp.int32, (1, tq), 1)
        kpos = start + lax.broadcasted_iota(jnp.int32, (band, 1), 0)
        valid = jnp.abs(qpos - kpos) <= WINDOW

    combos = [(g, kv) for g in range(SW_GROUP) for kv in range(SW_KV_HEADS)]

    def scores(c):
        g, kv = combos[c]
        qm = q_ref[0, :, g * LANES:(g + 1) * LANES] * hm_ref[kv:kv + 1, :]
        s_lat = _nt_dot(kb, qm) if has_lat else None
        return _nt_dot(kc, qm), s_lat

    pending = {c: scores(c) for c in range(SW_AHEAD)}
    outs = {}
    for c, (g, kv) in enumerate(combos):
        s_ctx, s_lat = pending.pop(c)
        if c + SW_AHEAD < len(combos):
            pending[c + SW_AHEAD] = scores(c + SW_AHEAD)
        sink = sink_ref[kv * SW_GROUP + g]
        m = jnp.maximum(jnp.max(s_ctx, axis=0, keepdims=True), sink)
        if has_lat:
            s_lat = jnp.where(valid, s_lat, NEG_INF)
            m = jnp.maximum(m, jnp.max(s_lat, axis=0, keepdims=True))
        acc = _dot(vc_t[kv], jnp.exp2(s_ctx - m).astype(BF16))
        if has_lat:
            acc = acc + _dot(vb_t[kv], jnp.exp2(s_lat - m).astype(BF16))
        l = acc[HEAD_DIM:HEAD_DIM + 1] + jnp.exp2(sink - m)
        outs[(g, kv)] = acc[:HEAD_DIM] / l
        if kv == SW_KV_HEADS - 1:
            pair_t = jnp.concatenate([outs.pop((g, k2)) for k2 in range(SW_KV_HEADS)], axis=0)
            o_ref[0, :, g * LANES:(g + 1) * LANES] = pair_t.T.astype(o_ref.dtype)


def _window_attention(q, kc, vc, k, v, lw):
    bsz, t_q, _ = q.shape
    has_lat = k is not None
    tq = min(SW_TQ, t_q)
    n_ctx = kc.shape[1]
    full = lambda t: pl.BlockSpec((1, t, SW_KV), lambda b, j: (b, 0, 0))
    specs = [pl.BlockSpec(memory_space=pltpu.SMEM),
             pl.BlockSpec((1, tq, SW_Q), lambda b, j: (b, j, 0)), full(n_ctx), full(n_ctx)]
    args = [lw["sw_sink"], q, kc, vc]
    if has_lat:
        specs += [full(t_q), full(t_q)]
        args += [k, v]
    specs.append(_const_spec((SW_KV_HEADS, LANES)))
    args.append(lw["sw_hmask"])
    return pl.pallas_call(
        functools.partial(_sw_kernel, has_lat=has_lat, tq=tq, t_len=t_q),
        out_shape=jax.ShapeDtypeStruct((bsz, t_q, SW_Q), BF16),
        grid=(bsz, t_q // tq),
        in_specs=specs,
        out_specs=pl.BlockSpec((1, tq, SW_Q), lambda b, j: (b, j, 0)),
        compiler_params=_cparams(2),
        name="window_attention",
    )(*args)


def _lru_kernel(xf_ref, xb_ref, xc_ref, cw_ref, cb_ref, wa_ref, ba_ref, wx_ref, bx_ref, lam_ref,
                hf_ref, hb_ref, hfc_ref, hbc_ref, halo_ref, carry_ref, *, tc):
    j = pl.program_id(1)
    is_ctx = j == 0
    sub = lax.broadcasted_iota(jnp.int32, (1, SUBLANES, 1), 1)

    for d, x_lat_ref, out_ref, outc_ref in ((0, xf_ref, hf_ref, hfc_ref), (1, xb_ref, hb_ref, hbc_ref)):
        x = jnp.where(is_ctx, xc_ref[0], x_lat_ref[0])
        halo = jnp.where(j <= 1, 0.0, halo_ref[d])
        cw = cw_ref[d]
        n_grp = tc // SUBLANES
        x3 = x.reshape(n_grp, SUBLANES, x.shape[1])
        xc3 = cb_ref[d] + x3 * cw[LRU_CONV - 1:LRU_CONV]
        for k in range(LRU_CONV - 1):
            s = LRU_CONV - 1 - k
            if d == 0:
                rolled = pltpu.roll(x3, s, 1)
                other = jnp.concatenate([pltpu.roll(halo, s, 0)[None], rolled[:-1]], axis=0)
                xs = jnp.where(sub < s, other, rolled)
            else:
                rolled = pltpu.roll(x3, SUBLANES - s, 1)
                other = jnp.concatenate([rolled[1:], pltpu.roll(halo, SUBLANES - s, 0)[None]], axis=0)
                xs = jnp.where(sub >= SUBLANES - s, other, rolled)
            xc3 = xc3 + xs * cw[k:k + 1]
        xc = xc3.reshape(tc, x.shape[1])
        halo_ref[d] = x[tc - SUBLANES:] if d == 0 else x[:SUBLANES]

        xcb = xc.astype(BF16)
        r = jax.nn.sigmoid(_dot(xcb, wa_ref[d]) + ba_ref[d])
        gi = jax.nn.sigmoid(_dot(xcb, wx_ref[d]) + bx_ref[d])
        nl = -lam_ref[d]
        softplus = jnp.maximum(nl, 0.0) + jnp.log1p(jnp.exp(-jnp.abs(nl)))
        log_a = -LRU_C * r * softplus
        a = jnp.exp(log_a)
        b = jnp.sqrt(-jnp.tanh(log_a) * (a * a + 1.0)) * (gi * xc)

        a3 = a.reshape(n_grp, SUBLANES, a.shape[1])
        b3 = b.reshape(n_grp, SUBLANES, b.shape[1])
        step = 1
        while step < SUBLANES:
            keep = (sub < step) if d == 0 else (sub >= SUBLANES - step)
            shift = step if d == 0 else SUBLANES - step
            a_s = jnp.where(keep, 1.0, pltpu.roll(a3, shift, 1))
            b_s = jnp.where(keep, 0.0, pltpu.roll(b3, shift, 1))
            b3 = a3 * b_s + b3
            a3 = a3 * a_s
            step *= 2

        edge_row = SUBLANES - 1 if d == 0 else 0
        h_in = jnp.where(is_ctx, 0.0, carry_ref[d, edge_row:edge_row + 1, :])
        groups = [None] * n_grp
        for gi in (range(n_grp) if d == 0 else range(n_grp - 1, -1, -1)):
            groups[gi] = a3[gi] * h_in + b3[gi]
            h_in = groups[gi][edge_row:edge_row + 1]
        h = jnp.concatenate(groups, axis=0)
        carry_ref[d] = groups[n_grp - 1] if d == 0 else groups[0]

        h_out = h.astype(out_ref.dtype)

        @pl.when(is_ctx)
        def _():
            outc_ref[0] = h_out

        @pl.when(jnp.logical_not(is_ctx))
        def _():
            out_ref[0] = h_out


def _rglru_scans(cx, cxc, lw):
    bsz, t_len, n = cx.shape
    tc = LRU_TC
    assert cxc.shape[1] == tc and t_len % tc == 0
    n_lat = t_len // tc
    fwd = lambda b, j: (b, jnp.maximum(j - 1, 0), 0)
    bwd = lambda b, j: (b, n_lat - jnp.maximum(j, 1), 0)
    ctx = lambda b, j: (b, 0, 0)
    blk = lambda m: pl.BlockSpec((1, tc, n), m)
    return pl.pallas_call(
        functools.partial(_lru_kernel, tc=tc),
        out_shape=(jax.ShapeDtypeStruct(cx.shape, BF16), jax.ShapeDtypeStruct(cx.shape, BF16),
                   jax.ShapeDtypeStruct(cxc.shape, BF16), jax.ShapeDtypeStruct(cxc.shape, BF16)),
        grid=(bsz, n_lat + 1),
        in_specs=[blk(fwd), blk(bwd), blk(ctx),
                  _const_spec((2, LRU_CONV, n)), _const_spec((2, 1, n)),
                  _const_spec((2, n, n)), _const_spec((2, 1, n)),
                  _const_spec((2, n, n)), _const_spec((2, 1, n)), _const_spec((2, 1, n))],
        out_specs=(blk(fwd), blk(bwd), blk(ctx), blk(ctx)),
        scratch_shapes=[pltpu.VMEM((2, SUBLANES, n), F32), pltpu.VMEM((2, SUBLANES, n), F32)],
        compiler_params=_cparams(2),
        name="rglru_scans",
    )(cx, cx, cxc, lw["lru_conv_w"], lw["lru_conv_b"], lw["lru_wa"], lw["lru_ba"],
      lw["lru_wx"], lw["lru_bx"], lw["lru_lambda"])


def _gelu_tanh(x):
    return 0.5 * x * (1.0 + jnp.tanh(math.sqrt(2.0 / math.pi) * (x + 0.044715 * (x * x * x))))


def _outproj_kernel(x_ref, ya_ref, yb_ref, hf_ref, hb_ref, cg_ref, g1_ref, w_ref, o_ref, cat_ref):
    cat_ref[:, 0:DA_V] = ya_ref[0]
    cat_ref[:, DA_V:DA_V + SW_Q] = yb_ref[0]
    yc = (hf_ref[0].astype(F32) + hb_ref[0].astype(F32)) * _gelu_tanh(cg_ref[0].astype(F32))
    cat_ref[:, DA_V + SW_Q:] = yc.astype(BF16)
    o_ref[0] = x_ref[0] + g1_ref[0] * _dot(cat_ref[...], w_ref[...])


def _output_projection(x, ya, yb, hf, hb, cg, gate, lw):
    bsz, t_len, d = x.shape
    tm = min(TM, t_len)
    per_b = gate.shape[0] > 1
    mod_map = (lambda b, j: (b, 0, 0)) if per_b else (lambda b, j: (0, 0, 0))
    tok = lambda n: pl.BlockSpec((1, tm, n), lambda b, j: (b, j, 0))
    return pl.pallas_call(
        _outproj_kernel,
        out_shape=jax.ShapeDtypeStruct(x.shape, F32),
        grid=(bsz, t_len // tm),
        in_specs=[tok(d), tok(DA_V), tok(SW_Q), tok(LRU_WIDTH), tok(LRU_WIDTH), tok(LRU_WIDTH),
                  pl.BlockSpec((1, 1, d), mod_map), _const_spec((d, d))],
        out_specs=tok(d),
        scratch_shapes=[pltpu.VMEM((tm, d), BF16)],
        compiler_params=_cparams(2),
        name="output_projection",
    )(x, ya, yb, hf, hb, cg, gate, lw["w_out"])


def _ffn_kernel(xm_ref, xp_ref, xn_ref, sh_ref, sc_ref, g2_ref, ng_ref, wup_ref, cw_ref, cb_ref,
                wdn_ref, o_ref, h_ref, *, tm):
    j = pl.program_id(1)
    last = pl.num_programs(1) - 1
    gain, shift, scale = ng_ref[...], sh_ref[0], sc_ref[0]
    xm = xm_ref[0]
    hp = jnp.where(j > 0, _norm_modulate(xp_ref[0], gain, shift, scale), 0.0)
    hn = jnp.where(j < last, _norm_modulate(xn_ref[0], gain, shift, scale), 0.0)
    h_ref[0:FFN_HALO] = hp.astype(BF16)
    h_ref[FFN_HALO:FFN_HALO + tm] = _norm_modulate(xm, gain, shift, scale).astype(BF16)
    h_ref[FFN_HALO + tm:] = hn.astype(BF16)

    rows = tm + 2 * FFN_HALO
    acc = jnp.zeros((tm, D_MODEL), F32)
    for lo, hi in FFN_CHUNKS:
        gp = _dot(h_ref[...], wup_ref[:, lo:hi])
        g_prev = pltpu.roll(gp, 1, 0)[FFN_HALO:FFN_HALO + tm]
        g_next = pltpu.roll(gp, rows - 1, 0)[FFN_HALO:FFN_HALO + tm]
        cw = cw_ref[:, lo:hi]
        gate = (cb_ref[:, lo:hi] + g_prev * cw[0:1] + gp[FFN_HALO:FFN_HALO + tm] * cw[1:2]
                + g_next * cw[2:3])
        val = _dot(h_ref[FFN_HALO:FFN_HALO + tm], wup_ref[:, D_FF + lo:D_FF + hi])
        act = (gate * jax.nn.sigmoid(gate) * val).astype(BF16)
        acc = acc + _dot(act, wdn_ref[lo:hi, :])
    o_ref[0] = xm + g2_ref[0] * acc


def _conv_ffn(x, shift, scale, gate, lw):
    bsz, t_len, d = x.shape
    tm = min(TM, t_len)
    per_b = gate.shape[0] > 1
    mod_map = (lambda b, j: (b, 0, 0)) if per_b else (lambda b, j: (0, 0, 0))
    hb = tm // FFN_HALO
    n_hb = t_len // FFN_HALO
    prev_map = lambda b, j: (b, jnp.maximum(j * hb - 1, 0), 0)
    next_map = lambda b, j: (b, jnp.minimum((j + 1) * hb, n_hb - 1), 0)
    mod_spec = pl.BlockSpec((1, 1, d), mod_map)
    return pl.pallas_call(
        functools.partial(_ffn_kernel, tm=tm),
        out_shape=jax.ShapeDtypeStruct(x.shape, F32),
        grid=(bsz, t_len // tm),
        in_specs=[pl.BlockSpec((1, tm, d), lambda b, j: (b, j, 0)),
                  pl.BlockSpec((1, FFN_HALO, d), prev_map),
                  pl.BlockSpec((1, FFN_HALO, d), next_map),
                  mod_spec, mod_spec, mod_spec,
                  _const_spec((1, d)), _const_spec((d, 2 * D_FF)),
                  _const_spec((FFN_CONV, D_FF)), _const_spec((1, D_FF)), _const_spec((D_FF, d))],
        out_specs=pl.BlockSpec((1, tm, d), lambda b, j: (b, j, 0)),
        scratch_shapes=[pltpu.VMEM((tm + 2 * FFN_HALO, d), BF16)],
        compiler_params=_cparams(2),
        name="conv_ffn",
    )(x, x, x, shift, scale, gate, lw["norm2_gain"], lw["w_up"], lw["ffn_conv_w"],
      lw["ffn_conv_b"], lw["w_down"])


def _sw_head_order():
    return [kv * SW_GROUP + g for g in range(SW_GROUP) for kv in range(SW_KV_HEADS)]


def _rope_tables(rows, head_dim, n_heads):
    row = jnp.repeat(jnp.arange(rows, dtype=F32), GRID_W)
    col = jnp.tile(jnp.arange(GRID_W, dtype=F32), rows)
    quarter = head_dim // 4
    inv_freq = ROPE_BASE ** (-jnp.arange(quarter, dtype=F32) / quarter)
    ang = jnp.concatenate([row[:, None] * inv_freq, col[:, None] * inv_freq], axis=-1)
    cos, sin = jnp.cos(ang), jnp.sin(ang)
    return (jnp.tile(jnp.concatenate([cos, cos], axis=-1), (1, n_heads)),
            jnp.tile(jnp.concatenate([-sin, sin], axis=-1), (1, n_heads)))


def _block_diag_mean(n, group):
    idx = np.arange(n) // group
    return jnp.asarray((idx[:, None] == idx[None, :]).astype(np.float32) / group, dtype=BF16)


def _block_diag(w):
    two, nb, bi, bj = w.shape
    eye = jnp.eye(nb, dtype=w.dtype)
    return jnp.einsum("dhij,hg->dhigj", w, eye).reshape(two, nb * bi, nb * bj)


def _prepare_layer(p, i):
    d = D_MODEL
    split = np.cumsum((DA_Q, DA_Q, DA_V, SW_Q, SW_KV, SW_KV, LRU_WIDTH))
    w_in = p["w_in"][i]
    w_aq, w_ak, w_av, w_bq, w_bk, w_bv, w_cx, w_cg = jnp.split(w_in, split.tolist(), axis=1)
    w_av = jnp.pad(w_av.reshape(d, DA_HEADS, DA_V_DIM), ((0, 0), (0, 0), (0, LANES - DA_V_DIM)))
    order = _sw_head_order()
    w_bq = w_bq.reshape(d, SW_HEADS, HEAD_DIM)[:, order].reshape(d, SW_Q)
    w_inx = jnp.concatenate([w_aq, w_ak, w_av.reshape(d, DA_VEXT), w_bq, w_bk, w_bv, w_cx, w_cg],
                            axis=1).astype(BF16)
    v_one = np.zeros((1, DA_VEXT), np.float32)
    v_one[0, DA_V_DIM::LANES] = 1.0

    w_out = p["w_out"][i]
    w_ob = w_out[DA_V:DA_V + SW_Q].reshape(SW_HEADS, HEAD_DIM, d)[jnp.asarray(order)]
    w_out = jnp.concatenate([w_out[:DA_V], w_ob.reshape(SW_Q, d), w_out[DA_V + SW_Q:]], axis=0)

    da_hmask = np.zeros((2 * DA_HEADS, LANES), np.float32)
    for j in range(2 * DA_HEADS):
        off = (j % 4) * DA_QK_DIM
        da_hmask[j, off:off + DA_QK_DIM] = 1.0
    sw_hmask = np.zeros((SW_KV_HEADS, LANES), np.float32)
    for kv in range(SW_KV_HEADS):
        sw_hmask[kv, kv * HEAD_DIM:(kv + 1) * HEAD_DIM] = 1.0

    return dict(
        norm1_gain=p["norm1_gain"][i].reshape(1, d), norm2_gain=p["norm2_gain"][i].reshape(1, d),
        w_in=w_inx, v_one=jnp.asarray(v_one),
        gq_a=(jnp.tile(p["da_q_gain"][i], 2 * DA_HEADS).reshape(1, DA_Q)
              * (DA_QK_DIM ** -0.5 * math.log2(math.e))),
        gk_a=jnp.tile(p["da_k_gain"][i], 2 * DA_HEADS).reshape(1, DA_Q),
        gq_b=(jnp.tile(p["sw_q_gain"][i], SW_HEADS).reshape(1, SW_Q)
              * (HEAD_DIM ** -0.5 * math.log2(math.e))),
        gk_b=jnp.tile(p["sw_k_gain"][i], SW_KV_HEADS).reshape(1, SW_KV),
        g32=_block_diag_mean(DA_Q, DA_QK_DIM), g64=_block_diag_mean(SW_Q, HEAD_DIM),
        da_hmask=jnp.asarray(da_hmask, dtype=BF16),
        da_lamv=jnp.stack([p["da_lam_q1"][i], p["da_lam_k1"][i], p["da_lam_q2"][i], p["da_lam_k2"][i]]),
        da_sg=jnp.tile(p["da_sub_gain"][i], LANES // DA_V_DIM).reshape(1, LANES),
        sw_sink=p["sw_sink"][i] * math.log2(math.e), sw_hmask=jnp.asarray(sw_hmask, dtype=BF16),
        lru_conv_w=p["lru_conv_w"][i], lru_conv_b=p["lru_conv_b"][i].reshape(2, 1, LRU_WIDTH),
        lru_wa=_block_diag(p["lru_wa"][i]).astype(BF16), lru_ba=p["lru_ba"][i].reshape(2, 1, LRU_WIDTH),
        lru_wx=_block_diag(p["lru_wx"][i]).astype(BF16), lru_bx=p["lru_bx"][i].reshape(2, 1, LRU_WIDTH),
        lru_lambda=p["lru_lambda"][i].reshape(2, 1, LRU_WIDTH),
        w_out=w_out.astype(BF16), w_up=p["w_up"][i].astype(BF16),
        ffn_conv_w=p["ffn_conv_w"][i], ffn_conv_b=p["ffn_conv_b"][i].reshape(1, D_FF),
        w_down=p["w_down"][i].astype(BF16),
    )


def kernel(x, c, ctx, c_ctx, w_mod, b_mod, norm1_gain, norm2_gain, w_in, da_q_gain, da_k_gain, da_lam_q1, da_lam_k1, da_lam_q2, da_lam_k2, da_sub_gain, sw_q_gain, sw_k_gain, sw_sink, lru_conv_w, lru_conv_b, lru_wa, lru_ba, lru_wx, lru_bx, lru_lambda, w_out, w_up, ffn_conv_w, ffn_conv_b, w_down):
    p = dict(w_in=w_in, norm1_gain=norm1_gain, norm2_gain=norm2_gain, da_q_gain=da_q_gain,
             da_k_gain=da_k_gain, da_lam_q1=da_lam_q1, da_lam_k1=da_lam_k1, da_lam_q2=da_lam_q2,
             da_lam_k2=da_lam_k2, da_sub_gain=da_sub_gain, sw_q_gain=sw_q_gain, sw_k_gain=sw_k_gain,
             sw_sink=sw_sink, lru_conv_w=lru_conv_w, lru_conv_b=lru_conv_b, lru_wa=lru_wa,
             lru_ba=lru_ba, lru_wx=lru_wx, lru_bx=lru_bx, lru_lambda=lru_lambda, w_out=w_out,
             w_up=w_up, ffn_conv_w=ffn_conv_w, ffn_conv_b=ffn_conv_b, w_down=w_down)
    bsz, n_tok, d = x.shape
    n_ctx = ctx.shape[1]
    depth = w_mod.shape[0]

    cc = jnp.zeros((2 * SUBLANES, d), F32).at[:bsz].set(c).at[bsz].set(c_ctx)
    mod_all = _modulation(cc, w_mod, b_mod)

    rope_lat = (_rope_tables(n_tok // GRID_W, DA_QK_DIM, 2 * DA_HEADS)
                + _rope_tables(n_tok // GRID_W, HEAD_DIM, SW_HEADS))
    rope_ctx = (jnp.ones((n_ctx, DA_Q), F32), jnp.zeros((n_ctx, DA_Q), F32),
                jnp.ones((n_ctx, SW_Q), F32), jnp.zeros((n_ctx, SW_Q), F32))

    xc = ctx
    for i in range(depth):
        lw = _prepare_layer(p, i)
        lambda_init = 0.8 - 0.6 * math.exp(-0.3 * i)
        ctx_out = i < depth - 1
        mod = mod_all[i, :bsz].reshape(bsz, 1, N_MOD, d)
        sh1, sc1, g1, sh2, sc2, g2 = [mod[:, :, m] for m in range(N_MOD)]
        mod_c = mod_all[i, bsz].reshape(1, 1, N_MOD, d)
        sh1c, sc1c, g1c, sh2c, sc2c, g2c = [mod_c[:, :, m] for m in range(N_MOD)]

        qa, ka, va, qb, kb, vb, cx, cg = _input_projection(x, sh1, sc1, lw, rope_lat)
        qac, kac, vac, qbc, kbc, vbc, cxc, cgc = _input_projection(xc, sh1c, sc1c, lw, rope_ctx)

        ya = _diff_attention(qa, [(kac, vac), (ka, va)], lw, lambda_init)
        yb = _window_attention(qb, kbc, vbc, kb, vb, lw)
        hf, hb, hfc, hbc = _rglru_scans(cx, cxc, lw)

        x = _output_projection(x, ya, yb, hf, hb, cg, g1, lw)
        x = _conv_ffn(x, sh2, sc2, g2, lw)
        if ctx_out:
            yac = _diff_attention(qac, [(kac, vac)], lw, lambda_init)
            ybc = _window_attention(qbc, kbc, vbc, None, None, lw)
            xc = _output_projection(xc, yac, ybc, hfc, hbc, cgc, g1c, lw)
            xc = _conv_ffn(xc, sh2c, sc2c, g2c, lw)
    return x
```

```python
import functools
import math

import numpy as np
import jax
import jax.numpy as jnp
from jax import lax
from jax.experimental import pallas as pl
from jax.experimental.pallas import tpu as pltpu

F32 = jnp.float32
BF16 = jnp.bfloat16

D_MODEL = 1024
GRID_W = 64
N_MOD = 6
EPS = 1e-6
NEG_INF = -1e30
ROPE_BASE = 10000.0
DA_HEADS = 4
DA_QK_DIM = 32
DA_V_DIM = 64
HEAD_DIM = 64
SW_HEADS = 6
SW_KV_HEADS = 2
SW_GROUP = SW_HEADS // SW_KV_HEADS
WINDOW = 128
LRU_WIDTH = 384
LRU_BLOCKS = 6
LRU_BLOCK_DIM = LRU_WIDTH // LRU_BLOCKS
LRU_CONV = 4
LRU_C = 8.0
D_FF = 2816
FFN_CONV = 3

DA_Q = DA_HEADS * 2 * DA_QK_DIM
DA_V = DA_HEADS * DA_V_DIM
SW_Q = SW_HEADS * HEAD_DIM
SW_KV = SW_KV_HEADS * HEAD_DIM

LANES = 128
SUBLANES = 8
BF16_ROWS = 16
MXU_DIM = 256
VMEM_LIMIT = 56 * 1024 * 1024

DA_VEXT = DA_HEADS * LANES
DA_VROWS = DA_V_DIM + BF16_ROWS
C_AQ = 0
C_AK = C_AQ + DA_Q
C_AV = C_AK + DA_Q
C_BQ = C_AV + DA_VEXT
C_BK = C_BQ + SW_Q
C_BV = C_BK + SW_KV
C_CX = C_BV + SW_KV
C_CG = C_CX + LRU_WIDTH
D_INX = C_CG + LRU_WIDTH

TM = 512
FFN_HALO = BF16_ROWS
FFN_CHUNKS = ((0, 1024), (1024, 2048), (2048, D_FF))
DA_TQ = 256
DA_AHEAD = 3
DA_UNROLL = 4
SW_AHEAD = 2
SW_TQ = 256
LRU_TC = 256


def _cparams(n_axes):
    return pltpu.CompilerParams(dimension_semantics=("arbitrary",) * n_axes,
                                vmem_limit_bytes=VMEM_LIMIT)


def _const_spec(shape):
    nd = len(shape)
    return pl.BlockSpec(shape, lambda *_: (0,) * nd, pipeline_mode=pl.Buffered(1))


def _nt_dot(a, b):
    return lax.dot_general(a, b, (((1,), (1,)), ((), ())), preferred_element_type=F32)


def _dot(a, b):
    return jnp.dot(a, b, preferred_element_type=F32)


def _mod_kernel(c_ref, w_ref, b_ref, o_ref):
    c = c_ref[...]
    s = c * jax.nn.sigmoid(c)
    o_ref[0] = _dot(s.astype(BF16), w_ref[0].astype(BF16)) + b_ref[0]


def _modulation(cc, w_mod, b_mod):
    depth, d, n = w_mod.shape
    tn = 1536
    return pl.pallas_call(
        _mod_kernel,
        out_shape=jax.ShapeDtypeStruct((depth, cc.shape[0], n), F32),
        grid=(depth, n // tn),
        in_specs=[pl.BlockSpec(cc.shape, lambda i, j: (0, 0)),
                  pl.BlockSpec((1, d, tn), lambda i, j: (i, 0, j)),
                  pl.BlockSpec((1, 1, tn), lambda i, j: (i, 0, j))],
        out_specs=pl.BlockSpec((1, cc.shape[0], tn), lambda i, j: (i, 0, j)),
        compiler_params=_cparams(2),
        name="modulation",
    )(cc, w_mod, b_mod.reshape(depth, 1, n))


def _norm_modulate(x, gain, shift, scale):
    ms = jnp.mean(x * x, axis=-1, keepdims=True)
    h = x * lax.rsqrt(ms + EPS) * gain
    return h * (1.0 + scale) + shift


def _group_rms(a, g_ref):
    sq = (a * a).astype(BF16)
    n = a.shape[1]
    ms = [_dot(sq[:, lo:min(lo + MXU_DIM, n)], g_ref[:min(MXU_DIM, n - lo), :min(MXU_DIM, n - lo)])
          for lo in range(0, n, MXU_DIM)]
    ms = ms[0] if len(ms) == 1 else jnp.concatenate(ms, axis=1)
    return a * lax.rsqrt(ms + EPS)


def _rope(y, cos, sin_signed, half):
    n = y.shape[1]
    lane = lax.broadcasted_iota(jnp.int32, y.shape, 1)
    first = (lane % (2 * half)) < half
    partner = jnp.where(first, pltpu.roll(y, n - half, 1), pltpu.roll(y, half, 1))
    return y * cos + partner * sin_signed


def _inproj_kernel(x_ref, sh_ref, sc_ref, ng_ref, w_ref, vone_ref,
                   cosa_ref, sina_ref, cosb_ref, sinb_ref,
                   gqa_ref, gka_ref, gqb_ref, gkb_ref, g32_ref, g64_ref,
                   qa_ref, ka_ref, va_ref, qb_ref, kb_ref, vb_ref, cx_ref, cg_ref):
    h = _norm_modulate(x_ref[0], ng_ref[...], sh_ref[0], sc_ref[0]).astype(BF16)
    full = _dot(h, w_ref[...])

    def proj(lo, hi):
        return full[:, lo:hi]

    cosa, sina = cosa_ref[...], sina_ref[...]
    aq = _group_rms(proj(C_AQ, C_AK), g32_ref) * gqa_ref[...]
    qa_ref[0] = _rope(aq, cosa, sina, DA_QK_DIM // 2).astype(BF16)
    ak = _group_rms(proj(C_AK, C_AV), g32_ref) * gka_ref[...]
    ka_ref[0] = _rope(ak, cosa, sina, DA_QK_DIM // 2).astype(BF16)
    va_t = (proj(C_AV, C_BQ) + vone_ref[...]).T.astype(BF16)
    for hd in range(DA_HEADS):
        va_ref[0, 0, hd * DA_VROWS:(hd + 1) * DA_VROWS, :] = va_t[hd * LANES:hd * LANES + DA_VROWS]

    cosb, sinb = cosb_ref[...], sinb_ref[...]
    bq = _group_rms(proj(C_BQ, C_BK), g64_ref) * gqb_ref[...]
    qb_ref[0] = _rope(bq, cosb, sinb, HEAD_DIM // 2).astype(BF16)
    bkv = proj(C_BK, C_CX)
    bk = _group_rms(bkv[:, :SW_KV], g64_ref) * gkb_ref[...]
    kb_ref[0] = _rope(bk, cosb[:, :SW_KV], sinb[:, :SW_KV], HEAD_DIM // 2).astype(BF16)
    vb_ref[0] = bkv[:, SW_KV:].astype(BF16)

    cxg = proj(C_CX, D_INX)
    cx_ref[0] = cxg[:, :LRU_WIDTH]
    cg_ref[0] = cxg[:, LRU_WIDTH:].astype(cg_ref.dtype)


def _input_projection(x, shift, scale, lw, rope):
    bsz, t_len, d = x.shape
    tm = min(TM, t_len)
    per_b = shift.shape[0] > 1
    mod_map = (lambda b, j: (b, 0, 0)) if per_b else (lambda b, j: (0, 0, 0))
    tok = lambda n: pl.BlockSpec((1, tm, n), lambda b, j: (b, j, 0))
    tab = lambda n: pl.BlockSpec((tm, n), lambda b, j: (j, 0))
    widths = (DA_Q, DA_Q, DA_VEXT, SW_Q, SW_KV, SW_KV, LRU_WIDTH, LRU_WIDTH)
    dtypes = (BF16,) * 6 + (F32, BF16)
    out_shape = [jax.ShapeDtypeStruct((bsz, t_len, n), dt) for n, dt in zip(widths, dtypes)]
    out_specs = [tok(n) for n in widths]
    out_shape[2] = jax.ShapeDtypeStruct((bsz, t_len // tm, DA_HEADS * DA_VROWS, tm), BF16)
    out_specs[2] = pl.BlockSpec((1, 1, DA_HEADS * DA_VROWS, tm), lambda b, j: (b, j, 0, 0))
    return pl.pallas_call(
        _inproj_kernel,
        out_shape=tuple(out_shape),
        grid=(bsz, t_len // tm),
        in_specs=[tok(d),
                  pl.BlockSpec((1, 1, d), mod_map), pl.BlockSpec((1, 1, d), mod_map),
                  _const_spec((1, d)), _const_spec((d, D_INX)), _const_spec((1, DA_VEXT)),
                  tab(DA_Q), tab(DA_Q), tab(SW_Q), tab(SW_Q),
                  _const_spec((1, DA_Q)), _const_spec((1, DA_Q)),
                  _const_spec((1, SW_Q)), _const_spec((1, SW_KV)),
                  _const_spec((MXU_DIM, MXU_DIM)), _const_spec((MXU_DIM, MXU_DIM))],
        out_specs=tuple(out_specs),
        compiler_params=_cparams(2),
        name="input_projection",
    )(x, shift, scale, lw["norm1_gain"], lw["w_in"], lw["v_one"], *rope,
      lw["gq_a"], lw["gk_a"], lw["gq_b"], lw["gk_b"], lw["g32"], lw["g64"])


def _da_kernel(*refs, n_seg, lambda_init):
    q_ref = refs[0]
    kv_refs = refs[1:1 + 2 * n_seg]
    hmask_ref, lamv_ref, sg_ref, o_ref, m_ref, acc_ref, s_ref = refs[1 + 2 * n_seg:]
    n_map = 2 * DA_HEADS

    m_ref[...] = jnp.full(m_ref.shape, NEG_INF, F32)
    acc_ref[...] = jnp.zeros(acc_ref.shape, F32)

    def scores(k_ref, tk, i, j):
        half = j // 4
        start = i * tk if isinstance(i, int) else pl.multiple_of(i * tk, tk)
        qm = q_ref[0, :, half * LANES:(half + 1) * LANES] * hmask_ref[j:j + 1, :]
        kblk = k_ref[0, pl.ds(start, tk), half * LANES:(half + 1) * LANES]
        return _nt_dot(kblk, qm)

    def tile(k_ref, v_ref, i, from_scratch, nxt):
        tk = v_ref.shape[3]
        pending = {}
        if not from_scratch:
            for j in range(DA_AHEAD):
                pending[j] = scores(k_ref, tk, i, j)
        for j in range(n_map):
            s_t = s_ref[j, :tk] if from_scratch and j < DA_AHEAD else pending.pop(j)
            ja = j + DA_AHEAD
            if ja < n_map:
                pending[ja] = scores(k_ref, tk, i, ja)
            elif nxt is not None:
                s_ref[ja - n_map, :nxt[1]] = scores(nxt[0], nxt[1], nxt[2], ja - n_map)
            m_old = m_ref[j, 0:1, :]
            s_b = s_t.astype(BF16)
            m_new = jnp.maximum(m_old, jnp.max(s_b, axis=0, keepdims=True).astype(F32))
            p_t = jnp.exp2(s_b - m_new.astype(BF16))
            pv = _dot(v_ref[0, i, (j // 2) * DA_VROWS:(j // 2 + 1) * DA_VROWS, :], p_t)
            acc_ref[j] = jnp.exp2(m_old - m_new) * acc_ref[j] + pv
            m_ref[j] = jnp.broadcast_to(m_new, m_ref.shape[1:])

    segs = [(kv_refs[2 * si], kv_refs[2 * si + 1]) for si in range(n_seg)]
    for si, (k_ref, v_ref) in enumerate(segs):
        n_t, tk = v_ref.shape[1], v_ref.shape[3]
        after = (segs[si + 1][0], segs[si + 1][1].shape[3], 0) if si + 1 < n_seg else None
        lo, hi = 0, n_t
        if si == 0:
            tile(k_ref, v_ref, 0, False, (k_ref, tk, 1) if n_t > 1 else after)
            lo = 1
        if after is not None and hi > lo:
            hi -= 1
        if hi > lo:
            unroll = max(u for u in range(1, DA_UNROLL + 1) if (hi - lo) % u == 0)

            def body(g, carry, k_ref=k_ref, v_ref=v_ref, n_t=n_t, tk=tk, lo=lo, unroll=unroll):
                for u in range(unroll):
                    i = lo + g * unroll + u
                    tile(k_ref, v_ref, i, True, (k_ref, tk, jnp.minimum(i + 1, n_t - 1)))
                return carry
            lax.fori_loop(0, (hi - lo) // unroll, body, 0)
        if after is not None and n_t > lo:
            tile(k_ref, v_ref, n_t - 1, True, after)

    lv = lamv_ref[...]
    lam = (jnp.exp(jnp.sum(lv[0:1] * lv[1:2], axis=1, keepdims=True))
           - jnp.exp(jnp.sum(lv[2:3] * lv[3:4], axis=1, keepdims=True)) + lambda_init)

    def head_out(h):
        a1, a2 = acc_ref[2 * h], acc_ref[2 * h + 1]
        o = (a1 / a1[DA_V_DIM:DA_V_DIM + 1] - lam * (a2 / a2[DA_V_DIM:DA_V_DIM + 1]))[:DA_V_DIM]
        ms = jnp.sum(o * o, axis=0, keepdims=True) * (1.0 / DA_V_DIM)
        return o * lax.rsqrt(ms + EPS)

    for c in range(DA_HEADS // 2):
        pair_t = jnp.concatenate([head_out(2 * c), head_out(2 * c + 1)], axis=0)
        o_ref[0, :, c * LANES:(c + 1) * LANES] = (pair_t.T * sg_ref[...]).astype(o_ref.dtype)


def _diff_attention(q, kvs, lw, lambda_init):
    bsz, t_q, _ = q.shape
    tq = min(DA_TQ, t_q)
    kv_specs, kv_args = [], []
    for k, v in kvs:
        kv_specs += [pl.BlockSpec((1,) + k.shape[1:], lambda b, j: (b, 0, 0)),
                     pl.BlockSpec((1,) + v.shape[1:], lambda b, j: (b, 0, 0, 0))]
        kv_args += [k, v]
    sg = lw["da_sg"] * (1.0 - lambda_init)
    return pl.pallas_call(
        functools.partial(_da_kernel, n_seg=len(kvs), lambda_init=lambda_init),
        out_shape=jax.ShapeDtypeStruct((bsz, t_q, DA_V), BF16),
        grid=(bsz, t_q // tq),
        in_specs=[pl.BlockSpec((1, tq, DA_Q), lambda b, j: (b, j, 0))] + kv_specs
                 + [_const_spec((2 * DA_HEADS, LANES)), _const_spec((4, DA_QK_DIM)),
                    _const_spec((1, LANES))],
        out_specs=pl.BlockSpec((1, tq, DA_V), lambda b, j: (b, j, 0)),
        scratch_shapes=[pltpu.VMEM((2 * DA_HEADS, SUBLANES, tq), F32),
                        pltpu.VMEM((2 * DA_HEADS, DA_VROWS, tq), F32),
                        pltpu.VMEM((DA_AHEAD, max(v.shape[3] for _, v in kvs), tq), F32)],
        compiler_params=_cparams(2),
        name="diff_attention",
    )(q, *kv_args, lw["da_hmask"], lw["da_lamv"], sg)


def _sw_kernel(*refs, has_lat, tq, t_len):
    if has_lat:
        sink_ref, q_ref, kc_ref, vc_ref, k_ref, v_ref, hm_ref, o_ref = refs
    else:
        sink_ref, q_ref, kc_ref, vc_ref, hm_ref, o_ref = refs
    def values_t(v):
        n_k = v.shape[0]
        v_t = v.astype(F32).T
        tail = (lax.broadcasted_iota(jnp.int32, (BF16_ROWS, n_k), 0) == 0).astype(F32)
        return [jnp.concatenate([v_t[kv * HEAD_DIM:(kv + 1) * HEAD_DIM], tail], axis=0).astype(BF16)
                for kv in range(SW_KV_HEADS)]

    kc = kc_ref[0]
    vc_t = values_t(vc_ref[0])
    if has_lat:
        band = tq + 2 * WINDOW
        q0 = pl.program_id(1) * tq
        start = pl.multiple_of(jnp.clip(q0 - WINDOW, 0, t_len - band), WINDOW)
        kb = k_ref[0, pl.ds(start, band), :]
        vb_t = values_t(v_ref[0, pl.ds(start, band), :])
        qpos = q0 + lax.broadcasted_iota(jnp.int32, (1, tq), 1)
        kpos = start + lax.broadcasted_iota(jnp.int32, (band, 1), 0)
        valid = jnp.abs(qpos - kpos) <= WINDOW

    combos = [(g, kv) for g in range(SW_GROUP) for kv in range(SW_KV_HEADS)]

    def scores(c):
        g, kv = combos[c]
        qm = q_ref[0, :, g * LANES:(g + 1) * LANES] * hm_ref[kv:kv + 1, :]
        s_lat = _nt_dot(kb, qm) if has_lat else None
        return _nt_dot(kc, qm), s_lat

    pending = {c: scores(c) for c in range(SW_AHEAD)}
    outs = {}
    for c, (g, kv) in enumerate(combos):
        s_ctx, s_lat = pending.pop(c)
        if c + SW_AHEAD < len(combos):
            pending[c + SW_AHEAD] = scores(c + SW_AHEAD)
        sink = sink_ref[kv * SW_GROUP + g]
        s_ctx = s_ctx.astype(BF16)
        m = jnp.maximum(jnp.max(s_ctx, axis=0, keepdims=True).astype(F32), sink)
        if has_lat:
            s_lat = jnp.where(valid, s_lat.astype(BF16), NEG_INF)
            m = jnp.maximum(m, jnp.max(s_lat, axis=0, keepdims=True).astype(F32))
        m_b = m.astype(BF16)
        m = m_b.astype(F32)
        acc = _dot(vc_t[kv], jnp.exp2(s_ctx - m_b))
        if has_lat:
            acc = acc + _dot(vb_t[kv], jnp.exp2(s_lat - m_b))
        l = acc[HEAD_DIM:HEAD_DIM + 1] + jnp.exp2(sink - m)
        outs[(g, kv)] = acc[:HEAD_DIM] / l
        if kv == SW_KV_HEADS - 1:
            pair_t = jnp.concatenate([outs.pop((g, k2)) for k2 in range(SW_KV_HEADS)], axis=0)
            o_ref[0, :, g * LANES:(g + 1) * LANES] = pair_t.T.astype(o_ref.dtype)


def _window_attention(q, kc, vc, k, v, lw):
    bsz, t_q, _ = q.shape
    has_lat = k is not None
    tq = min(SW_TQ, t_q)
    n_ctx = kc.shape[1]
    full = lambda t: pl.BlockSpec((1, t, SW_KV), lambda b, j: (b, 0, 0))
    specs = [pl.BlockSpec(memory_space=pltpu.SMEM),
             pl.BlockSpec((1, tq, SW_Q), lambda b, j: (b, j, 0)), full(n_ctx), full(n_ctx)]
    args = [lw["sw_sink"], q, kc, vc]
    if has_lat:
        specs += [full(t_q), full(t_q)]
        args += [k, v]
    specs.append(_const_spec((SW_KV_HEADS, LANES)))
    args.append(lw["sw_hmask"])
    return pl.pallas_call(
        functools.partial(_sw_kernel, has_lat=has_lat, tq=tq, t_len=t_q),
        out_shape=jax.ShapeDtypeStruct((bsz, t_q, SW_Q), BF16),
        grid=(bsz, t_q // tq),
        in_specs=specs,
        out_specs=pl.BlockSpec((1, tq, SW_Q), lambda b, j: (b, j, 0)),
        compiler_params=_cparams(2),
        name="window_attention",
    )(*args)


def _lru_kernel(xf_ref, xb_ref, xc_ref, cw_ref, cb_ref, wa_ref, ba_ref, wx_ref, bx_ref, lam_ref,
                hf_ref, hb_ref, hfc_ref, hbc_ref, halo_ref, carry_ref, *, tc):
    j = pl.program_id(1)
    is_ctx = j == 0
    sub = lax.broadcasted_iota(jnp.int32, (1, SUBLANES, 1), 1)

    for d, x_lat_ref, out_ref, outc_ref in ((0, xf_ref, hf_ref, hfc_ref), (1, xb_ref, hb_ref, hbc_ref)):
        x = jnp.where(is_ctx, xc_ref[0], x_lat_ref[0])
        halo = jnp.where(j <= 1, 0.0, halo_ref[d])
        cw = cw_ref[d]
        n_grp = tc // SUBLANES
        x3 = x.reshape(n_grp, SUBLANES, x.shape[1])
        xc3 = cb_ref[d] + x3 * cw[LRU_CONV - 1:LRU_CONV]
        for k in range(LRU_CONV - 1):
            s = LRU_CONV - 1 - k
            if d == 0:
                rolled = pltpu.roll(x3, s, 1)
                other = jnp.concatenate([pltpu.roll(halo, s, 0)[None], rolled[:-1]], axis=0)
                xs = jnp.where(sub < s, other, rolled)
            else:
                rolled = pltpu.roll(x3, SUBLANES - s, 1)
                other = jnp.concatenate([rolled[1:], pltpu.roll(halo, SUBLANES - s, 0)[None]], axis=0)
                xs = jnp.where(sub >= SUBLANES - s, other, rolled)
            xc3 = xc3 + xs * cw[k:k + 1]
        xc = xc3.reshape(tc, x.shape[1])
        halo_ref[d] = x[tc - SUBLANES:] if d == 0 else x[:SUBLANES]

        xcb = xc.astype(BF16)
        r = jax.nn.sigmoid(_dot(xcb, wa_ref[d]) + ba_ref[d])
        gi = jax.nn.sigmoid(_dot(xcb, wx_ref[d]) + bx_ref[d])
        nl = -lam_ref[d]
        softplus = jnp.maximum(nl, 0.0) + jnp.log1p(jnp.exp(-jnp.abs(nl)))
        log_a = -LRU_C * r * softplus
        a = jnp.exp(log_a)
        b = jnp.sqrt(-jnp.tanh(log_a) * (a * a + 1.0)) * (gi * xc)

        a3 = a.reshape(n_grp, SUBLANES, a.shape[1])
        b3 = b.reshape(n_grp, SUBLANES, b.shape[1])
        step = 1
        while step < SUBLANES:
            keep = (sub < step) if d == 0 else (sub >= SUBLANES - step)
            shift = step if d == 0 else SUBLANES - step
            a_s = jnp.where(keep, 1.0, pltpu.roll(a3, shift, 1))
            b_s = jnp.where(keep, 0.0, pltpu.roll(b3, shift, 1))
            b3 = a3 * b_s + b3
            a3 = a3 * a_s
            step *= 2

        edge_row = SUBLANES - 1 if d == 0 else 0
        h_in = jnp.where(is_ctx, 0.0, carry_ref[d, edge_row:edge_row + 1, :])
        groups = [None] * n_grp
        for gi in (range(n_grp) if d == 0 else range(n_grp - 1, -1, -1)):
            groups[gi] = a3[gi] * h_in + b3[gi]
            h_in = groups[gi][edge_row:edge_row + 1]
        h = jnp.concatenate(groups, axis=0)
        carry_ref[d] = groups[n_grp - 1] if d == 0 else groups[0]

        h_out = h.astype(out_ref.dtype)

        @pl.when(is_ctx)
        def _():
            outc_ref[0] = h_out

        @pl.when(jnp.logical_not(is_ctx))
        def _():
            out_ref[0] = h_out


def _rglru_scans(cx, cxc, lw):
    bsz, t_len, n = cx.shape
    tc = LRU_TC
    assert cxc.shape[1] == tc and t_len % tc == 0
    n_lat = t_len // tc
    fwd = lambda b, j: (b, jnp.maximum(j - 1, 0), 0)
    bwd = lambda b, j: (b, n_lat - jnp.maximum(j, 1), 0)
    ctx = lambda b, j: (b, 0, 0)
    blk = lambda m: pl.BlockSpec((1, tc, n), m)
    return pl.pallas_call(
        functools.partial(_lru_kernel, tc=tc),
        out_shape=(jax.ShapeDtypeStruct(cx.shape, BF16), jax.ShapeDtypeStruct(cx.shape, BF16),
                   jax.ShapeDtypeStruct(cxc.shape, BF16), jax.ShapeDtypeStruct(cxc.shape, BF16)),
        grid=(bsz, n_lat + 1),
        in_specs=[blk(fwd), blk(bwd), blk(ctx),
                  _const_spec((2, LRU_CONV, n)), _const_spec((2, 1, n)),
                  _const_spec((2, n, n)), _const_spec((2, 1, n)),
                  _const_spec((2, n, n)), _const_spec((2, 1, n)), _const_spec((2, 1, n))],
        out_specs=(blk(fwd), blk(bwd), blk(ctx), blk(ctx)),
        scratch_shapes=[pltpu.VMEM((2, SUBLANES, n), F32), pltpu.VMEM((2, SUBLANES, n), F32)],
        compiler_params=_cparams(2),
        name="rglru_scans",
    )(cx, cx, cxc, lw["lru_conv_w"], lw["lru_conv_b"], lw["lru_wa"], lw["lru_ba"],
      lw["lru_wx"], lw["lru_bx"], lw["lru_lambda"])


def _gelu_tanh(x):
    return 0.5 * x * (1.0 + jnp.tanh(math.sqrt(2.0 / math.pi) * (x + 0.044715 * (x * x * x))))


def _outproj_kernel(x_ref, ya_ref, yb_ref, hf_ref, hb_ref, cg_ref, g1_ref, w_ref, o_ref, cat_ref):
    cat_ref[:, 0:DA_V] = ya_ref[0]
    cat_ref[:, DA_V:DA_V + SW_Q] = yb_ref[0]
    yc = (hf_ref[0].astype(F32) + hb_ref[0].astype(F32)) * _gelu_tanh(cg_ref[0].astype(F32))
    cat_ref[:, DA_V + SW_Q:] = yc.astype(BF16)
    o_ref[0] = x_ref[0] + g1_ref[0] * _dot(cat_ref[...], w_ref[...])


def _output_projection(x, ya, yb, hf, hb, cg, gate, lw):
    bsz, t_len, d = x.shape
    tm = min(TM, t_len)
    per_b = gate.shape[0] > 1
    mod_map = (lambda b, j: (b, 0, 0)) if per_b else (lambda b, j: (0, 0, 0))
    tok = lambda n: pl.BlockSpec((1, tm, n), lambda b, j: (b, j, 0))
    return pl.pallas_call(
        _outproj_kernel,
        out_shape=jax.ShapeDtypeStruct(x.shape, F32),
        grid=(bsz, t_len // tm),
        in_specs=[tok(d), tok(DA_V), tok(SW_Q), tok(LRU_WIDTH), tok(LRU_WIDTH), tok(LRU_WIDTH),
                  pl.BlockSpec((1, 1, d), mod_map), _const_spec((d, d))],
        out_specs=tok(d),
        scratch_shapes=[pltpu.VMEM((tm, d), BF16)],
        compiler_params=_cparams(2),
        name="output_projection",
    )(x, ya, yb, hf, hb, cg, gate, lw["w_out"])


def _ffn_kernel(xm_ref, xp_ref, xn_ref, sh_ref, sc_ref, g2_ref, ng_ref, wup_ref, cw_ref, cb_ref,
                wdn_ref, o_ref, h_ref, *, tm):
    j = pl.program_id(1)
    last = pl.num_programs(1) - 1
    gain, shift, scale = ng_ref[...], sh_ref[0], sc_ref[0]
    xm = xm_ref[0]
    hp = jnp.where(j > 0, _norm_modulate(xp_ref[0], gain, shift, scale), 0.0)
    hn = jnp.where(j < last, _norm_modulate(xn_ref[0], gain, shift, scale), 0.0)
    h_ref[0:FFN_HALO] = hp.astype(BF16)
    h_ref[FFN_HALO:FFN_HALO + tm] = _norm_modulate(xm, gain, shift, scale).astype(BF16)
    h_ref[FFN_HALO + tm:] = hn.astype(BF16)

    rows = tm + 2 * FFN_HALO
    acc = jnp.zeros((tm, D_MODEL), F32)
    for lo, hi in FFN_CHUNKS:
        gp = _dot(h_ref[...], wup_ref[:, lo:hi])
        g_prev = pltpu.roll(gp, 1, 0)[FFN_HALO:FFN_HALO + tm]
        g_next = pltpu.roll(gp, rows - 1, 0)[FFN_HALO:FFN_HALO + tm]
        cw = cw_ref[:, lo:hi]
        gate = (cb_ref[:, lo:hi] + g_prev * cw[0:1] + gp[FFN_HALO:FFN_HALO + tm] * cw[1:2]
                + g_next * cw[2:3])
        val = _dot(h_ref[FFN_HALO:FFN_HALO + tm], wup_ref[:, D_FF + lo:D_FF + hi])
        act = (gate * jax.nn.sigmoid(gate) * val).astype(BF16)
        acc = acc + _dot(act, wdn_ref[lo:hi, :])
    o_ref[0] = xm + g2_ref[0] * acc


def _conv_ffn(x, shift, scale, gate, lw):
    bsz, t_len, d = x.shape
    tm = min(TM, t_len)
    per_b = gate.shape[0] > 1
    mod_map = (lambda b, j: (b, 0, 0)) if per_b else (lambda b, j: (0, 0, 0))
    hb = tm // FFN_HALO
    n_hb = t_len // FFN_HALO
    prev_map = lambda b, j: (b, jnp.maximum(j * hb - 1, 0), 0)
    next_map = lambda b, j: (b, jnp.minimum((j + 1) * hb, n_hb - 1), 0)
    mod_spec = pl.BlockSpec((1, 1, d), mod_map)
    return pl.pallas_call(
        functools.partial(_ffn_kernel, tm=tm),
        out_shape=jax.ShapeDtypeStruct(x.shape, F32),
        grid=(bsz, t_len // tm),
        in_specs=[pl.BlockSpec((1, tm, d), lambda b, j: (b, j, 0)),
                  pl.BlockSpec((1, FFN_HALO, d), prev_map),
                  pl.BlockSpec((1, FFN_HALO, d), next_map),
                  mod_spec, mod_spec, mod_spec,
                  _const_spec((1, d)), _const_spec((d, 2 * D_FF)),
                  _const_spec((FFN_CONV, D_FF)), _const_spec((1, D_FF)), _const_spec((D_FF, d))],
        out_specs=pl.BlockSpec((1, tm, d), lambda b, j: (b, j, 0)),
        scratch_shapes=[pltpu.VMEM((tm + 2 * FFN_HALO, d), BF16)],
        compiler_params=_cparams(2),
        name="conv_ffn",
    )(x, x, x, shift, scale, gate, lw["norm2_gain"], lw["w_up"], lw["ffn_conv_w"],
      lw["ffn_conv_b"], lw["w_down"])


def _sw_head_order():
    return [kv * SW_GROUP + g for g in range(SW_GROUP) for kv in range(SW_KV_HEADS)]


def _rope_tables(rows, head_dim, n_heads):
    row = jnp.repeat(jnp.arange(rows, dtype=F32), GRID_W)
    col = jnp.tile(jnp.arange(GRID_W, dtype=F32), rows)
    quarter = head_dim // 4
    inv_freq = ROPE_BASE ** (-jnp.arange(quarter, dtype=F32) / quarter)
    ang = jnp.concatenate([row[:, None] * inv_freq, col[:, None] * inv_freq], axis=-1)
    cos, sin = jnp.cos(ang), jnp.sin(ang)
    return (jnp.tile(jnp.concatenate([cos, cos], axis=-1), (1, n_heads)),
            jnp.tile(jnp.concatenate([-sin, sin], axis=-1), (1, n_heads)))


def _block_diag_mean(n, group):
    idx = np.arange(n) // group
    return jnp.asarray((idx[:, None] == idx[None, :]).astype(np.float32) / group, dtype=BF16)


def _block_diag(w):
    two, nb, bi, bj = w.shape
    eye = jnp.eye(nb, dtype=w.dtype)
    return jnp.einsum("dhij,hg->dhigj", w, eye).reshape(two, nb * bi, nb * bj)


def _prepare_layer(p, i):
    d = D_MODEL
    split = np.cumsum((DA_Q, DA_Q, DA_V, SW_Q, SW_KV, SW_KV, LRU_WIDTH))
    w_in = p["w_in"][i]
    w_aq, w_ak, w_av, w_bq, w_bk, w_bv, w_cx, w_cg = jnp.split(w_in, split.tolist(), axis=1)
    w_av = jnp.pad(w_av.reshape(d, DA_HEADS, DA_V_DIM), ((0, 0), (0, 0), (0, LANES - DA_V_DIM)))
    order = _sw_head_order()
    w_bq = w_bq.reshape(d, SW_HEADS, HEAD_DIM)[:, order].reshape(d, SW_Q)
    w_inx = jnp.concatenate([w_aq, w_ak, w_av.reshape(d, DA_VEXT), w_bq, w_bk, w_bv, w_cx, w_cg],
                            axis=1).astype(BF16)
    v_one = np.zeros((1, DA_VEXT), np.float32)
    v_one[0, DA_V_DIM::LANES] = 1.0

    w_out = p["w_out"][i]
    w_ob = w_out[DA_V:DA_V + SW_Q].reshape(SW_HEADS, HEAD_DIM, d)[jnp.asarray(order)]
    w_out = jnp.concatenate([w_out[:DA_V], w_ob.reshape(SW_Q, d), w_out[DA_V + SW_Q:]], axis=0)

    da_hmask = np.zeros((2 * DA_HEADS, LANES), np.float32)
    for j in range(2 * DA_HEADS):
        off = (j % 4) * DA_QK_DIM
        da_hmask[j, off:off + DA_QK_DIM] = 1.0
    sw_hmask = np.zeros((SW_KV_HEADS, LANES), np.float32)
    for kv in range(SW_KV_HEADS):
        sw_hmask[kv, kv * HEAD_DIM:(kv + 1) * HEAD_DIM] = 1.0

    return dict(
        norm1_gain=p["norm1_gain"][i].reshape(1, d), norm2_gain=p["norm2_gain"][i].reshape(1, d),
        w_in=w_inx, v_one=jnp.asarray(v_one),
        gq_a=(jnp.tile(p["da_q_gain"][i], 2 * DA_HEADS).reshape(1, DA_Q)
              * (DA_QK_DIM ** -0.5 * math.log2(math.e))),
        gk_a=jnp.tile(p["da_k_gain"][i], 2 * DA_HEADS).reshape(1, DA_Q),
        gq_b=(jnp.tile(p["sw_q_gain"][i], SW_HEADS).reshape(1, SW_Q)
              * (HEAD_DIM ** -0.5 * math.log2(math.e))),
        gk_b=jnp.tile(p["sw_k_gain"][i], SW_KV_HEADS).reshape(1, SW_KV),
        g32=_block_diag_mean(MXU_DIM, DA_QK_DIM), g64=_block_diag_mean(MXU_DIM, HEAD_DIM),
        da_hmask=jnp.asarray(da_hmask, dtype=BF16),
        da_lamv=jnp.stack([p["da_lam_q1"][i], p["da_lam_k1"][i], p["da_lam_q2"][i], p["da_lam_k2"][i]]),
        da_sg=jnp.tile(p["da_sub_gain"][i], LANES // DA_V_DIM).reshape(1, LANES),
        sw_sink=p["sw_sink"][i] * math.log2(math.e), sw_hmask=jnp.asarray(sw_hmask, dtype=BF16),
        lru_conv_w=p["lru_conv_w"][i], lru_conv_b=p["lru_conv_b"][i].reshape(2, 1, LRU_WIDTH),
        lru_wa=_block_diag(p["lru_wa"][i]).astype(BF16), lru_ba=p["lru_ba"][i].reshape(2, 1, LRU_WIDTH),
        lru_wx=_block_diag(p["lru_wx"][i]).astype(BF16), lru_bx=p["lru_bx"][i].reshape(2, 1, LRU_WIDTH),
        lru_lambda=p["lru_lambda"][i].reshape(2, 1, LRU_WIDTH),
        w_out=w_out.astype(BF16), w_up=p["w_up"][i].astype(BF16),
        ffn_conv_w=p["ffn_conv_w"][i], ffn_conv_b=p["ffn_conv_b"][i].reshape(1, D_FF),
        w_down=p["w_down"][i].astype(BF16),
    )


def kernel(x, c, ctx, c_ctx, w_mod, b_mod, norm1_gain, norm2_gain, w_in, da_q_gain, da_k_gain, da_lam_q1, da_lam_k1, da_lam_q2, da_lam_k2, da_sub_gain, sw_q_gain, sw_k_gain, sw_sink, lru_conv_w, lru_conv_b, lru_wa, lru_ba, lru_wx, lru_bx, lru_lambda, w_out, w_up, ffn_conv_w, ffn_conv_b, w_down):
    p = dict(w_in=w_in, norm1_gain=norm1_gain, norm2_gain=norm2_gain, da_q_gain=da_q_gain,
             da_k_gain=da_k_gain, da_lam_q1=da_lam_q1, da_lam_k1=da_lam_k1, da_lam_q2=da_lam_q2,
             da_lam_k2=da_lam_k2, da_sub_gain=da_sub_gain, sw_q_gain=sw_q_gain, sw_k_gain=sw_k_gain,
             sw_sink=sw_sink, lru_conv_w=lru_conv_w, lru_conv_b=lru_conv_b, lru_wa=lru_wa,
             lru_ba=lru_ba, lru_wx=lru_wx, lru_bx=lru_bx, lru_lambda=lru_lambda, w_out=w_out,
             w_up=w_up, ffn_conv_w=ffn_conv_w, ffn_conv_b=ffn_conv_b, w_down=w_down)
    bsz, n_tok, d = x.shape
    n_ctx = ctx.shape[1]
    depth = w_mod.shape[0]

    cc = jnp.zeros((2 * SUBLANES, d), F32).at[:bsz].set(c).at[bsz].set(c_ctx)
    mod_all = _modulation(cc, w_mod, b_mod)

    rope_lat = (_rope_tables(n_tok // GRID_W, DA_QK_DIM, 2 * DA_HEADS)
                + _rope_tables(n_tok // GRID_W, HEAD_DIM, SW_HEADS))
    rope_ctx = (jnp.ones((n_ctx, DA_Q), F32), jnp.zeros((n_ctx, DA_Q), F32),
                jnp.ones((n_ctx, SW_Q), F32), jnp.zeros((n_ctx, SW_Q), F32))

    xc = ctx
    for i in range(depth):
        lw = _prepare_layer(p, i)
        lambda_init = 0.8 - 0.6 * math.exp(-0.3 * i)
        ctx_out = i < depth - 1
        mod = mod_all[i, :bsz].reshape(bsz, 1, N_MOD, d)
        sh1, sc1, g1, sh2, sc2, g2 = [mod[:, :, m] for m in range(N_MOD)]
        mod_c = mod_all[i, bsz].reshape(1, 1, N_MOD, d)
        sh1c, sc1c, g1c, sh2c, sc2c, g2c = [mod_c[:, :, m] for m in range(N_MOD)]

        qa, ka, va, qb, kb, vb, cx, cg = _input_projection(x, sh1, sc1, lw, rope_lat)
        qac, kac, vac, qbc, kbc, vbc, cxc, cgc = _input_projection(xc, sh1c, sc1c, lw, rope_ctx)

        ya = _diff_attention(qa, [(kac, vac), (ka, va)], lw, lambda_init)
        yb = _window_attention(qb, kbc, vbc, kb, vb, lw)
        hf, hb, hfc, hbc = _rglru_scans(cx, cxc, lw)

        x = _output_projection(x, ya, yb, hf, hb, cg, g1, lw)
        x = _conv_ffn(x, sh2, sc2, g2, lw)
        if ctx_out:
            yac = _diff_attention(qac, [(kac, vac)], lw, lambda_init)
            ybc = _window_attention(qbc, kbc, vbc, None, None, lw)
            xc = _output_projection(xc, yac, ybc, hfc, hbc, cgc, g1c, lw)
            xc = _conv_ffn(xc, sh2c, sc2c, g2c, lw)
    return x
```

```python
import functools
import math

import numpy as np
import jax
import jax.numpy as jnp
from jax import lax
from jax.experimental import pallas as pl
from jax.experimental.pallas import tpu as pltpu

F32 = jnp.float32
BF16 = jnp.bfloat16

D_MODEL = 1024
GRID_W = 64
N_MOD = 6
EPS = 1e-6
NEG_INF = -1e30
ROPE_BASE = 10000.0
DA_HEADS = 4
DA_QK_DIM = 32
DA_V_DIM = 64
HEAD_DIM = 64
SW_HEADS = 6
SW_KV_HEADS = 2
SW_GROUP = SW_HEADS // SW_KV_HEADS
WINDOW = 128
LRU_WIDTH = 384
LRU_BLOCKS = 6
LRU_BLOCK_DIM = LRU_WIDTH // LRU_BLOCKS
LRU_CONV = 4
LRU_C = 8.0
D_FF = 2816
FFN_CONV = 3

DA_Q = DA_HEADS * 2 * DA_QK_DIM
DA_V = DA_HEADS * DA_V_DIM
SW_Q = SW_HEADS * HEAD_DIM
SW_KV = SW_KV_HEADS * HEAD_DIM

LANES = 128
SUBLANES = 8
BF16_ROWS = 16
MXU_DIM = 256
VMEM_LIMIT = 56 * 1024 * 1024

DA_VEXT = DA_HEADS * LANES
DA_VROWS = DA_V_DIM + BF16_ROWS
C_AQ = 0
C_AK = C_AQ + DA_Q
C_AV = C_AK + DA_Q
C_BQ = C_AV + DA_VEXT
C_BK = C_BQ + SW_Q
C_BV = C_BK + SW_KV
C_CX = C_BV + SW_KV
C_CG = C_CX + LRU_WIDTH
D_INX = C_CG + LRU_WIDTH

TM = 512
FFN_HALO = BF16_ROWS
FFN_CHUNKS = ((0, 1024), (1024, 2048), (2048, D_FF))
DA_TQ = 256
DA_AHEAD = 3
DA_UNROLL = 8
SW_AHEAD = 2
SW_TQ = 256
LRU_TC = 256


def _cparams(n_axes):
    return pltpu.CompilerParams(dimension_semantics=("arbitrary",) * n_axes,
                                vmem_limit_bytes=VMEM_LIMIT)


def _const_spec(shape):
    nd = len(shape)
    return pl.BlockSpec(shape, lambda *_: (0,) * nd, pipeline_mode=pl.Buffered(1))


def _nt_dot(a, b):
    return lax.dot_general(a, b, (((1,), (1,)), ((), ())), preferred_element_type=F32)


def _dot(a, b):
    return jnp.dot(a, b, preferred_element_type=F32)


def _mod_kernel(c_ref, w_ref, b_ref, o_ref):
    c = c_ref[...]
    s = c * jax.nn.sigmoid(c)
    o_ref[0] = _dot(s.astype(BF16), w_ref[0].astype(BF16)) + b_ref[0]


def _modulation(cc, w_mod, b_mod):
    depth, d, n = w_mod.shape
    tn = 1536
    return pl.pallas_call(
        _mod_kernel,
        out_shape=jax.ShapeDtypeStruct((depth, cc.shape[0], n), F32),
        grid=(depth, n // tn),
        in_specs=[pl.BlockSpec(cc.shape, lambda i, j: (0, 0)),
                  pl.BlockSpec((1, d, tn), lambda i, j: (i, 0, j)),
                  pl.BlockSpec((1, 1, tn), lambda i, j: (i, 0, j))],
        out_specs=pl.BlockSpec((1, cc.shape[0], tn), lambda i, j: (i, 0, j)),
        compiler_params=_cparams(2),
        name="modulation",
    )(cc, w_mod, b_mod.reshape(depth, 1, n))


def _norm_modulate(x, gain, shift, scale):
    ms = jnp.mean(x * x, axis=-1, keepdims=True)
    h = x * lax.rsqrt(ms + EPS) * gain
    return h * (1.0 + scale) + shift


def _group_rms(a, g_ref):
    sq = (a * a).astype(BF16)
    n = a.shape[1]
    ms = [_dot(sq[:, lo:min(lo + MXU_DIM, n)], g_ref[:min(MXU_DIM, n - lo), :min(MXU_DIM, n - lo)])
          for lo in range(0, n, MXU_DIM)]
    ms = ms[0] if len(ms) == 1 else jnp.concatenate(ms, axis=1)
    return a * lax.rsqrt(ms + EPS)


def _rope(y, cos, sin_signed, half):
    n = y.shape[1]
    lane = lax.broadcasted_iota(jnp.int32, y.shape, 1)
    first = (lane % (2 * half)) < half
    partner = jnp.where(first, pltpu.roll(y, n - half, 1), pltpu.roll(y, half, 1))
    return y * cos + partner * sin_signed


def _inproj_kernel(x_ref, sh_ref, sc_ref, ng_ref, w_ref, vone_ref,
                   cosa_ref, sina_ref, cosb_ref, sinb_ref,
                   gqa_ref, gka_ref, gqb_ref, gkb_ref, g32_ref, g64_ref,
                   qa_ref, ka_ref, va_ref, qb_ref, kb_ref, vb_ref, cx_ref, cg_ref):
    h = _norm_modulate(x_ref[0], ng_ref[...], sh_ref[0], sc_ref[0]).astype(BF16)
    full = _dot(h, w_ref[...])

    def proj(lo, hi):
        return full[:, lo:hi]

    cosa, sina = cosa_ref[...], sina_ref[...]
    aq = _group_rms(proj(C_AQ, C_AK), g32_ref) * gqa_ref[...]
    qa_ref[0] = _rope(aq, cosa, sina, DA_QK_DIM // 2).astype(BF16)
    ak = _group_rms(proj(C_AK, C_AV), g32_ref) * gka_ref[...]
    ka_ref[0] = _rope(ak, cosa, sina, DA_QK_DIM // 2).astype(BF16)
    va_t = (proj(C_AV, C_BQ) + vone_ref[...]).T.astype(BF16)
    for hd in range(DA_HEADS):
        va_ref[0, 0, hd * DA_VROWS:(hd + 1) * DA_VROWS, :] = va_t[hd * LANES:hd * LANES + DA_VROWS]

    cosb, sinb = cosb_ref[...], sinb_ref[...]
    bq = _group_rms(proj(C_BQ, C_BK), g64_ref) * gqb_ref[...]
    qb_ref[0] = _rope(bq, cosb, sinb, HEAD_DIM // 2).astype(BF16)
    bkv = proj(C_BK, C_CX)
    bk = _group_rms(bkv[:, :SW_KV], g64_ref) * gkb_ref[...]
    kb_ref[0] = _rope(bk, cosb[:, :SW_KV], sinb[:, :SW_KV], HEAD_DIM // 2).astype(BF16)
    vb_ref[0] = bkv[:, SW_KV:].astype(BF16)

    cxg = proj(C_CX, D_INX)
    cx_ref[0] = cxg[:, :LRU_WIDTH]
    cg_ref[0] = cxg[:, LRU_WIDTH:].astype(cg_ref.dtype)


def _input_projection(x, shift, scale, lw, rope):
    bsz, t_len, d = x.shape
    tm = min(TM, t_len)
    per_b = shift.shape[0] > 1
    mod_map = (lambda b, j: (b, 0, 0)) if per_b else (lambda b, j: (0, 0, 0))
    tok = lambda n: pl.BlockSpec((1, tm, n), lambda b, j: (b, j, 0))
    tab = lambda n: pl.BlockSpec((tm, n), lambda b, j: (j, 0))
    widths = (DA_Q, DA_Q, DA_VEXT, SW_Q, SW_KV, SW_KV, LRU_WIDTH, LRU_WIDTH)
    dtypes = (BF16,) * 6 + (F32, BF16)
    out_shape = [jax.ShapeDtypeStruct((bsz, t_len, n), dt) for n, dt in zip(widths, dtypes)]
    out_specs = [tok(n) for n in widths]
    out_shape[2] = jax.ShapeDtypeStruct((bsz, t_len // tm, DA_HEADS * DA_VROWS, tm), BF16)
    out_specs[2] = pl.BlockSpec((1, 1, DA_HEADS * DA_VROWS, tm), lambda b, j: (b, j, 0, 0))
    return pl.pallas_call(
        _inproj_kernel,
        out_shape=tuple(out_shape),
        grid=(bsz, t_len // tm),
        in_specs=[tok(d),
                  pl.BlockSpec((1, 1, d), mod_map), pl.BlockSpec((1, 1, d), mod_map),
                  _const_spec((1, d)), _const_spec((d, D_INX)), _const_spec((1, DA_VEXT)),
                  tab(DA_Q), tab(DA_Q), tab(SW_Q), tab(SW_Q),
                  _const_spec((1, DA_Q)), _const_spec((1, DA_Q)),
                  _const_spec((1, SW_Q)), _const_spec((1, SW_KV)),
                  _const_spec((MXU_DIM, MXU_DIM)), _const_spec((MXU_DIM, MXU_DIM))],
        out_specs=tuple(out_specs),
        compiler_params=_cparams(2),
        name="input_projection",
    )(x, shift, scale, lw["norm1_gain"], lw["w_in"], lw["v_one"], *rope,
      lw["gq_a"], lw["gk_a"], lw["gq_b"], lw["gk_b"], lw["g32"], lw["g64"])


def _da_kernel(*refs, n_seg, lambda_init):
    q_ref = refs[0]
    kv_refs = refs[1:1 + 2 * n_seg]
    hmask_ref, lamv_ref, sg_ref, o_ref, m_ref, acc_ref, s_ref = refs[1 + 2 * n_seg:]
    n_map = 2 * DA_HEADS

    m_ref[...] = jnp.full(m_ref.shape, NEG_INF, F32)
    acc_ref[...] = jnp.zeros(acc_ref.shape, F32)

    def scores(k_ref, tk, i, j):
        half = j // 4
        start = i * tk if isinstance(i, int) else pl.multiple_of(i * tk, tk)
        qm = q_ref[0, :, half * LANES:(half + 1) * LANES] * hmask_ref[j:j + 1, :]
        kblk = k_ref[0, pl.ds(start, tk), half * LANES:(half + 1) * LANES]
        return _nt_dot(kblk, qm)

    def tile(k_ref, v_ref, i, from_scratch, nxt):
        tk = v_ref.shape[3]
        pending = {}
        if not from_scratch:
            for j in range(DA_AHEAD):
                pending[j] = scores(k_ref, tk, i, j)
        for j in range(n_map):
            s_t = s_ref[j, :tk] if from_scratch and j < DA_AHEAD else pending.pop(j)
            ja = j + DA_AHEAD
            if ja < n_map:
                pending[ja] = scores(k_ref, tk, i, ja)
            elif nxt is not None:
                s_ref[ja - n_map, :nxt[1]] = scores(nxt[0], nxt[1], nxt[2], ja - n_map)
            m_old = m_ref[j, 0:1, :]
            s_b = s_t.astype(BF16)
            m_new = jnp.maximum(m_old, jnp.max(s_b, axis=0, keepdims=True).astype(F32))
            p_t = jnp.exp2(s_b - m_new.astype(BF16))
            pv = _dot(v_ref[0, i, (j // 2) * DA_VROWS:(j // 2 + 1) * DA_VROWS, :], p_t)
            acc_ref[j] = jnp.exp2(m_old - m_new) * acc_ref[j] + pv
            m_ref[j] = jnp.broadcast_to(m_new, m_ref.shape[1:])

    segs = [(kv_refs[2 * si], kv_refs[2 * si + 1]) for si in range(n_seg)]
    for si, (k_ref, v_ref) in enumerate(segs):
        n_t, tk = v_ref.shape[1], v_ref.shape[3]
        after = (segs[si + 1][0], segs[si + 1][1].shape[3], 0) if si + 1 < n_seg else None
        lo, hi = 0, n_t
        if si == 0:
            tile(k_ref, v_ref, 0, False, (k_ref, tk, 1) if n_t > 1 else after)
            lo = 1
        if after is not None and hi > lo:
            hi -= 1
        if hi > lo:
            unroll = max(u for u in range(1, DA_UNROLL + 1) if (hi - lo) % u == 0)

            def body(g, carry, k_ref=k_ref, v_ref=v_ref, n_t=n_t, tk=tk, lo=lo, unroll=unroll):
                for u in range(unroll):
                    i = lo + g * unroll + u
                    tile(k_ref, v_ref, i, True, (k_ref, tk, jnp.minimum(i + 1, n_t - 1)))
                return carry
            lax.fori_loop(0, (hi - lo) // unroll, body, 0)
        if after is not None and n_t > lo:
            tile(k_ref, v_ref, n_t - 1, True, after)

    lv = lamv_ref[...]
    lam = (jnp.exp(jnp.sum(lv[0:1] * lv[1:2], axis=1, keepdims=True))
           - jnp.exp(jnp.sum(lv[2:3] * lv[3:4], axis=1, keepdims=True)) + lambda_init)

    def head_out(h):
        a1, a2 = acc_ref[2 * h], acc_ref[2 * h + 1]
        o = (a1 / a1[DA_V_DIM:DA_V_DIM + 1] - lam * (a2 / a2[DA_V_DIM:DA_V_DIM + 1]))[:DA_V_DIM]
        ms = jnp.sum(o * o, axis=0, keepdims=True) * (1.0 / DA_V_DIM)
        return o * lax.rsqrt(ms + EPS)

    for c in range(DA_HEADS // 2):
        pair_t = jnp.concatenate([head_out(2 * c), head_out(2 * c + 1)], axis=0)
        o_ref[0, :, c * LANES:(c + 1) * LANES] = (pair_t.T * sg_ref[...]).astype(o_ref.dtype)


def _diff_attention(q, kvs, lw, lambda_init):
    bsz, t_q, _ = q.shape
    tq = min(DA_TQ, t_q)
    kv_specs, kv_args = [], []
    for k, v in kvs:
        kv_specs += [pl.BlockSpec((1,) + k.shape[1:], lambda b, j: (b, 0, 0)),
                     pl.BlockSpec((1,) + v.shape[1:], lambda b, j: (b, 0, 0, 0))]
        kv_args += [k, v]
    sg = lw["da_sg"] * (1.0 - lambda_init)
    return pl.pallas_call(
        functools.partial(_da_kernel, n_seg=len(kvs), lambda_init=lambda_init),
        out_shape=jax.ShapeDtypeStruct((bsz, t_q, DA_V), BF16),
        grid=(bsz, t_q // tq),
        in_specs=[pl.BlockSpec((1, tq, DA_Q), lambda b, j: (b, j, 0))] + kv_specs
                 + [_const_spec((2 * DA_HEADS, LANES)), _const_spec((4, DA_QK_DIM)),
                    _const_spec((1, LANES))],
        out_specs=pl.BlockSpec((1, tq, DA_V), lambda b, j: (b, j, 0)),
        scratch_shapes=[pltpu.VMEM((2 * DA_HEADS, SUBLANES, tq), F32),
                        pltpu.VMEM((2 * DA_HEADS, DA_VROWS, tq), F32),
                        pltpu.VMEM((DA_AHEAD, max(v.shape[3] for _, v in kvs), tq), F32)],
        compiler_params=_cparams(2),
        name="diff_attention",
    )(q, *kv_args, lw["da_hmask"], lw["da_lamv"], sg)


def _sw_kernel(*refs, has_lat, tq, t_len):
    if has_lat:
        sink_ref, q_ref, kc_ref, vc_ref, k_ref, v_ref, hm_ref, o_ref = refs
    else:
        sink_ref, q_ref, kc_ref, vc_ref, hm_ref, o_ref = refs
    def values_t(v):
        n_k = v.shape[0]
        v_t = v.astype(F32).T
        tail = (lax.broadcasted_iota(jnp.int32, (BF16_ROWS, n_k), 0) == 0).astype(F32)
        return [jnp.concatenate([v_t[kv * HEAD_DIM:(kv + 1) * HEAD_DIM], tail], axis=0).astype(BF16)
                for kv in range(SW_KV_HEADS)]

    kc = kc_ref[0]
    vc_t = values_t(vc_ref[0])
    if has_lat:
        band = tq + 2 * WINDOW
        q0 = pl.program_id(1) * tq
        start = pl.multiple_of(jnp.clip(q0 - WINDOW, 0, t_len - band), WINDOW)
        kb = k_ref[0, pl.ds(start, band), :]
        vb_t = values_t(v_ref[0, pl.ds(start, band), :])
        qpos = q0 + lax.broadcasted_iota(jnp.int32, (1, tq), 1)
        kpos = start + lax.broadcasted_iota(jnp.int32, (band, 1), 0)
        valid = jnp.abs(qpos - kpos) <= WINDOW

    combos = [(g, kv) for g in range(SW_GROUP) for kv in range(SW_KV_HEADS)]

    def scores(c):
        g, kv = combos[c]
        qm = q_ref[0, :, g * LANES:(g + 1) * LANES] * hm_ref[kv:kv + 1, :]
        s_lat = _nt_dot(kb, qm) if has_lat else None
        return _nt_dot(kc, qm), s_lat

    pending = {c: scores(c) for c in range(SW_AHEAD)}
    outs = {}
    for c, (g, kv) in enumerate(combos):
        s_ctx, s_lat = pending.pop(c)
        if c + SW_AHEAD < len(combos):
            pending[c + SW_AHEAD] = scores(c + SW_AHEAD)
        sink = sink_ref[kv * SW_GROUP + g]
        s_ctx = s_ctx.astype(BF16)
        m = jnp.maximum(jnp.max(s_ctx, axis=0, keepdims=True).astype(F32), sink)
        if has_lat:
            s_lat = jnp.where(valid, s_lat.astype(BF16), NEG_INF)
            m = jnp.maximum(m, jnp.max(s_lat, axis=0, keepdims=True).astype(F32))
        m_b = m.astype(BF16)
        m = m_b.astype(F32)
        acc = _dot(vc_t[kv], jnp.exp2(s_ctx - m_b))
        if has_lat:
            acc = acc + _dot(vb_t[kv], jnp.exp2(s_lat - m_b))
        l = acc[HEAD_DIM:HEAD_DIM + 1] + jnp.exp2(sink - m)
        outs[(g, kv)] = acc[:HEAD_DIM] / l
        if kv == SW_KV_HEADS - 1:
            pair_t = jnp.concatenate([outs.pop((g, k2)) for k2 in range(SW_KV_HEADS)], axis=0)
            o_ref[0, :, g * LANES:(g + 1) * LANES] = pair_t.T.astype(o_ref.dtype)


def _window_attention(q, kc, vc, k, v, lw):
    bsz, t_q, _ = q.shape
    has_lat = k is not None
    tq = min(SW_TQ, t_q)
    n_ctx = kc.shape[1]
    full = lambda t: pl.BlockSpec((1, t, SW_KV), lambda b, j: (b, 0, 0))
    specs = [pl.BlockSpec(memory_space=pltpu.SMEM),
             pl.BlockSpec((1, tq, SW_Q), lambda b, j: (b, j, 0)), full(n_ctx), full(n_ctx)]
    args = [lw["sw_sink"], q, kc, vc]
    if has_lat:
        specs += [full(t_q), full(t_q)]
        args += [k, v]
    specs.append(_const_spec((SW_KV_HEADS, LANES)))
    args.append(lw["sw_hmask"])
    return pl.pallas_call(
        functools.partial(_sw_kernel, has_lat=has_lat, tq=tq, t_len=t_q),
        out_shape=jax.ShapeDtypeStruct((bsz, t_q, SW_Q), BF16),
        grid=(bsz, t_q // tq),
        in_specs=specs,
        out_specs=pl.BlockSpec((1, tq, SW_Q), lambda b, j: (b, j, 0)),
        compiler_params=_cparams(2),
        name="window_attention",
    )(*args)


def _lru_kernel(xf_ref, xb_ref, xc_ref, cw_ref, cb_ref, wa_ref, ba_ref, wx_ref, bx_ref, lam_ref,
                hf_ref, hb_ref, hfc_ref, hbc_ref, halo_ref, carry_ref, *, tc):
    j = pl.program_id(1)
    is_ctx = j == 0
    sub = lax.broadcasted_iota(jnp.int32, (1, SUBLANES, 1), 1)

    for d, x_lat_ref, out_ref, outc_ref in ((0, xf_ref, hf_ref, hfc_ref), (1, xb_ref, hb_ref, hbc_ref)):
        x = jnp.where(is_ctx, xc_ref[0], x_lat_ref[0])
        halo = jnp.where(j <= 1, 0.0, halo_ref[d])
        cw = cw_ref[d]
        n_grp = tc // SUBLANES
        x3 = x.reshape(n_grp, SUBLANES, x.shape[1])
        xc3 = cb_ref[d] + x3 * cw[LRU_CONV - 1:LRU_CONV]
        for k in range(LRU_CONV - 1):
            s = LRU_CONV - 1 - k
            if d == 0:
                rolled = pltpu.roll(x3, s, 1)
                other = jnp.concatenate([pltpu.roll(halo, s, 0)[None], rolled[:-1]], axis=0)
                xs = jnp.where(sub < s, other, rolled)
            else:
                rolled = pltpu.roll(x3, SUBLANES - s, 1)
                other = jnp.concatenate([rolled[1:], pltpu.roll(halo, SUBLANES - s, 0)[None]], axis=0)
                xs = jnp.where(sub >= SUBLANES - s, other, rolled)
            xc3 = xc3 + xs * cw[k:k + 1]
        xc = xc3.reshape(tc, x.shape[1])
        halo_ref[d] = x[tc - SUBLANES:] if d == 0 else x[:SUBLANES]

        xcb = xc.astype(BF16)
        r = jax.nn.sigmoid(_dot(xcb, wa_ref[d]) + ba_ref[d])
        gi = jax.nn.sigmoid(_dot(xcb, wx_ref[d]) + bx_ref[d])
        nl = -lam_ref[d]
        softplus = jnp.maximum(nl, 0.0) + jnp.log1p(jnp.exp(-jnp.abs(nl)))
        log_a = -LRU_C * r * softplus
        a = jnp.exp(log_a)
        b = jnp.sqrt(-jnp.tanh(log_a) * (a * a + 1.0)) * (gi * xc)

        a3 = a.reshape(n_grp, SUBLANES, a.shape[1])
        b3 = b.reshape(n_grp, SUBLANES, b.shape[1])
        step = 1
        while step < SUBLANES:
            keep = (sub < step) if d == 0 else (sub >= SUBLANES - step)
            shift = step if d == 0 else SUBLANES - step
            a_s = jnp.where(keep, 1.0, pltpu.roll(a3, shift, 1))
            b_s = jnp.where(keep, 0.0, pltpu.roll(b3, shift, 1))
            b3 = a3 * b_s + b3
            a3 = a3 * a_s
            step *= 2

        edge_row = SUBLANES - 1 if d == 0 else 0
        h_in = jnp.where(is_ctx, 0.0, carry_ref[d, edge_row:edge_row + 1, :])
        groups = [None] * n_grp
        for gi in (range(n_grp) if d == 0 else range(n_grp - 1, -1, -1)):
            groups[gi] = a3[gi] * h_in + b3[gi]
            h_in = groups[gi][edge_row:edge_row + 1]
        h = jnp.concatenate(groups, axis=0)
        carry_ref[d] = groups[n_grp - 1] if d == 0 else groups[0]

        h_out = h.astype(out_ref.dtype)

        @pl.when(is_ctx)
        def _():
            outc_ref[0] = h_out

        @pl.when(jnp.logical_not(is_ctx))
        def _():
            out_ref[0] = h_out


def _rglru_scans(cx, cxc, lw):
    bsz, t_len, n = cx.shape
    tc = LRU_TC
    assert cxc.shape[1] == tc and t_len % tc == 0
    n_lat = t_len // tc
    fwd = lambda b, j: (b, jnp.maximum(j - 1, 0), 0)
    bwd = lambda b, j: (b, n_lat - jnp.maximum(j, 1), 0)
    ctx = lambda b, j: (b, 0, 0)
    blk = lambda m: pl.BlockSpec((1, tc, n), m)
    return pl.pallas_call(
        functools.partial(_lru_kernel, tc=tc),
        out_shape=(jax.ShapeDtypeStruct(cx.shape, BF16), jax.ShapeDtypeStruct(cx.shape, BF16),
                   jax.ShapeDtypeStruct(cxc.shape, BF16), jax.ShapeDtypeStruct(cxc.shape, BF16)),
        grid=(bsz, n_lat + 1),
        in_specs=[blk(fwd), blk(bwd), blk(ctx),
                  _const_spec((2, LRU_CONV, n)), _const_spec((2, 1, n)),
                  _const_spec((2, n, n)), _const_spec((2, 1, n)),
                  _const_spec((2, n, n)), _const_spec((2, 1, n)), _const_spec((2, 1, n))],
        out_specs=(blk(fwd), blk(bwd), blk(ctx), blk(ctx)),
        scratch_shapes=[pltpu.VMEM((2, SUBLANES, n), F32), pltpu.VMEM((2, SUBLANES, n), F32)],
        compiler_params=_cparams(2),
        name="rglru_scans",
    )(cx, cx, cxc, lw["lru_conv_w"], lw["lru_conv_b"], lw["lru_wa"], lw["lru_ba"],
      lw["lru_wx"], lw["lru_bx"], lw["lru_lambda"])


def _gelu_tanh(x):
    return 0.5 * x * (1.0 + jnp.tanh(math.sqrt(2.0 / math.pi) * (x + 0.044715 * (x * x * x))))


def _outproj_kernel(x_ref, ya_ref, yb_ref, hf_ref, hb_ref, cg_ref, g1_ref, w_ref, o_ref, cat_ref):
    cat_ref[:, 0:DA_V] = ya_ref[0]
    cat_ref[:, DA_V:DA_V + SW_Q] = yb_ref[0]
    yc = (hf_ref[0].astype(F32) + hb_ref[0].astype(F32)) * _gelu_tanh(cg_ref[0].astype(F32))
    cat_ref[:, DA_V + SW_Q:] = yc.astype(BF16)
    o_ref[0] = x_ref[0] + g1_ref[0] * _dot(cat_ref[...], w_ref[...])


def _output_projection(x, ya, yb, hf, hb, cg, gate, lw):
    bsz, t_len, d = x.shape
    tm = min(TM, t_len)
    per_b = gate.shape[0] > 1
    mod_map = (lambda b, j: (b, 0, 0)) if per_b else (lambda b, j: (0, 0, 0))
    tok = lambda n: pl.BlockSpec((1, tm, n), lambda b, j: (b, j, 0))
    return pl.pallas_call(
        _outproj_kernel,
        out_shape=jax.ShapeDtypeStruct(x.shape, F32),
        grid=(bsz, t_len // tm),
        in_specs=[tok(d), tok(DA_V), tok(SW_Q), tok(LRU_WIDTH), tok(LRU_WIDTH), tok(LRU_WIDTH),
                  pl.BlockSpec((1, 1, d), mod_map), _const_spec((d, d))],
        out_specs=tok(d),
        scratch_shapes=[pltpu.VMEM((tm, d), BF16)],
        compiler_params=_cparams(2),
        name="output_projection",
    )(x, ya, yb, hf, hb, cg, gate, lw["w_out"])


def _ffn_kernel(xm_ref, xp_ref, xn_ref, sh_ref, sc_ref, g2_ref, ng_ref, wup_ref, cw_ref, cb_ref,
                wdn_ref, o_ref, h_ref, *, tm):
    j = pl.program_id(1)
    last = pl.num_programs(1) - 1
    gain, shift, scale = ng_ref[...], sh_ref[0], sc_ref[0]
    xm = xm_ref[0]
    hp = jnp.where(j > 0, _norm_modulate(xp_ref[0], gain, shift, scale), 0.0)
    hn = jnp.where(j < last, _norm_modulate(xn_ref[0], gain, shift, scale), 0.0)
    h_ref[0:FFN_HALO] = hp.astype(BF16)
    h_ref[FFN_HALO:FFN_HALO + tm] = _norm_modulate(xm, gain, shift, scale).astype(BF16)
    h_ref[FFN_HALO + tm:] = hn.astype(BF16)

    rows = tm + 2 * FFN_HALO
    acc = jnp.zeros((tm, D_MODEL), F32)
    for lo, hi in FFN_CHUNKS:
        gp = _dot(h_ref[...], wup_ref[:, lo:hi])
        g_prev = pltpu.roll(gp, 1, 0)[FFN_HALO:FFN_HALO + tm]
        g_next = pltpu.roll(gp, rows - 1, 0)[FFN_HALO:FFN_HALO + tm]
        cw = cw_ref[:, lo:hi]
        gate = (cb_ref[:, lo:hi] + g_prev * cw[0:1] + gp[FFN_HALO:FFN_HALO + tm] * cw[1:2]
                + g_next * cw[2:3])
        val = _dot(h_ref[FFN_HALO:FFN_HALO + tm], wup_ref[:, D_FF + lo:D_FF + hi])
        act = (gate * jax.nn.sigmoid(gate) * val).astype(BF16)
        acc = acc + _dot(act, wdn_ref[lo:hi, :])
    o_ref[0] = xm + g2_ref[0] * acc


def _conv_ffn(x, shift, scale, gate, lw):
    bsz, t_len, d = x.shape
    tm = min(TM, t_len)
    per_b = gate.shape[0] > 1
    mod_map = (lambda b, j: (b, 0, 0)) if per_b else (lambda b, j: (0, 0, 0))
    hb = tm // FFN_HALO
    n_hb = t_len // FFN_HALO
    prev_map = lambda b, j: (b, jnp.maximum(j * hb - 1, 0), 0)
    next_map = lambda b, j: (b, jnp.minimum((j + 1) * hb, n_hb - 1), 0)
    mod_spec = pl.BlockSpec((1, 1, d), mod_map)
    return pl.pallas_call(
        functools.partial(_ffn_kernel, tm=tm),
        out_shape=jax.ShapeDtypeStruct(x.shape, F32),
        grid=(bsz, t_len // tm),
        in_specs=[pl.BlockSpec((1, tm, d), lambda b, j: (b, j, 0)),
                  pl.BlockSpec((1, FFN_HALO, d), prev_map),
                  pl.BlockSpec((1, FFN_HALO, d), next_map),
                  mod_spec, mod_spec, mod_spec,
                  _const_spec((1, d)), _const_spec((d, 2 * D_FF)),
                  _const_spec((FFN_CONV, D_FF)), _const_spec((1, D_FF)), _const_spec((D_FF, d))],
        out_specs=pl.BlockSpec((1, tm, d), lambda b, j: (b, j, 0)),
        scratch_shapes=[pltpu.VMEM((tm + 2 * FFN_HALO, d), BF16)],
        compiler_params=_cparams(2),
        name="conv_ffn",
    )(x, x, x, shift, scale, gate, lw["norm2_gain"], lw["w_up"], lw["ffn_conv_w"],
      lw["ffn_conv_b"], lw["w_down"])


def _sw_head_order():
    return [kv * SW_GROUP + g for g in range(SW_GROUP) for kv in range(SW_KV_HEADS)]


def _rope_tables(rows, head_dim, n_heads):
    row = jnp.repeat(jnp.arange(rows, dtype=F32), GRID_W)
    col = jnp.tile(jnp.arange(GRID_W, dtype=F32), rows)
    quarter = head_dim // 4
    inv_freq = ROPE_BASE ** (-jnp.arange(quarter, dtype=F32) / quarter)
    ang = jnp.concatenate([row[:, None] * inv_freq, col[:, None] * inv_freq], axis=-1)
    cos, sin = jnp.cos(ang), jnp.sin(ang)
    return (jnp.tile(jnp.concatenate([cos, cos], axis=-1), (1, n_heads)),
            jnp.tile(jnp.concatenate([-sin, sin], axis=-1), (1, n_heads)))


def _block_diag_mean(n, group):
    idx = np.arange(n) // group
    return jnp.asarray((idx[:, None] == idx[None, :]).astype(np.float32) / group, dtype=BF16)


def _block_diag(w):
    two, nb, bi, bj = w.shape
    eye = jnp.eye(nb, dtype=w.dtype)
    return jnp.einsum("dhij,hg->dhigj", w, eye).reshape(two, nb * bi, nb * bj)


def _prepare_layer(p, i):
    d = D_MODEL
    split = np.cumsum((DA_Q, DA_Q, DA_V, SW_Q, SW_KV, SW_KV, LRU_WIDTH))
    w_in = p["w_in"][i]
    w_aq, w_ak, w_av, w_bq, w_bk, w_bv, w_cx, w_cg = jnp.split(w_in, split.tolist(), axis=1)
    w_av = jnp.pad(w_av.reshape(d, DA_HEADS, DA_V_DIM), ((0, 0), (0, 0), (0, LANES - DA_V_DIM)))
    order = _sw_head_order()
    w_bq = w_bq.reshape(d, SW_HEADS, HEAD_DIM)[:, order].reshape(d, SW_Q)
    w_inx = jnp.concatenate([w_aq, w_ak, w_av.reshape(d, DA_VEXT), w_bq, w_bk, w_bv, w_cx, w_cg],
                            axis=1).astype(BF16)
    v_one = np.zeros((1, DA_VEXT), np.float32)
    v_one[0, DA_V_DIM::LANES] = 1.0

    w_out = p["w_out"][i]
    w_ob = w_out[DA_V:DA_V + SW_Q].reshape(SW_HEADS, HEAD_DIM, d)[jnp.asarray(order)]
    w_out = jnp.concatenate([w_out[:DA_V], w_ob.reshape(SW_Q, d), w_out[DA_V + SW_Q:]], axis=0)

    da_hmask = np.zeros((2 * DA_HEADS, LANES), np.float32)
    for j in range(2 * DA_HEADS):
        off = (j % 4) * DA_QK_DIM
        da_hmask[j, off:off + DA_QK_DIM] = 1.0
    sw_hmask = np.zeros((SW_KV_HEADS, LANES), np.float32)
    for kv in range(SW_KV_HEADS):
        sw_hmask[kv, kv * HEAD_DIM:(kv + 1) * HEAD_DIM] = 1.0

    return dict(
        norm1_gain=p["norm1_gain"][i].reshape(1, d), norm2_gain=p["norm2_gain"][i].reshape(1, d),
        w_in=w_inx, v_one=jnp.asarray(v_one),
        gq_a=(jnp.tile(p["da_q_gain"][i], 2 * DA_HEADS).reshape(1, DA_Q)
              * (DA_QK_DIM ** -0.5 * math.log2(math.e))),
        gk_a=jnp.tile(p["da_k_gain"][i], 2 * DA_HEADS).reshape(1, DA_Q),
        gq_b=(jnp.tile(p["sw_q_gain"][i], SW_HEADS).reshape(1, SW_Q)
              * (HEAD_DIM ** -0.5 * math.log2(math.e))),
        gk_b=jnp.tile(p["sw_k_gain"][i], SW_KV_HEADS).reshape(1, SW_KV),
        g32=_block_diag_mean(MXU_DIM, DA_QK_DIM), g64=_block_diag_mean(MXU_DIM, HEAD_DIM),
        da_hmask=jnp.asarray(da_hmask, dtype=BF16),
        da_lamv=jnp.stack([p["da_lam_q1"][i], p["da_lam_k1"][i], p["da_lam_q2"][i], p["da_lam_k2"][i]]),
        da_sg=jnp.tile(p["da_sub_gain"][i], LANES // DA_V_DIM).reshape(1, LANES),
        sw_sink=p["sw_sink"][i] * math.log2(math.e), sw_hmask=jnp.asarray(sw_hmask, dtype=BF16),
        lru_conv_w=p["lru_conv_w"][i], lru_conv_b=p["lru_conv_b"][i].reshape(2, 1, LRU_WIDTH),
        lru_wa=_block_diag(p["lru_wa"][i]).astype(BF16), lru_ba=p["lru_ba"][i].reshape(2, 1, LRU_WIDTH),
        lru_wx=_block_diag(p["lru_wx"][i]).astype(BF16), lru_bx=p["lru_bx"][i].reshape(2, 1, LRU_WIDTH),
        lru_lambda=p["lru_lambda"][i].reshape(2, 1, LRU_WIDTH),
        w_out=w_out.astype(BF16), w_up=p["w_up"][i].astype(BF16),
        ffn_conv_w=p["ffn_conv_w"][i], ffn_conv_b=p["ffn_conv_b"][i].reshape(1, D_FF),
        w_down=p["w_down"][i].astype(BF16),
    )


def kernel(x, c, ctx, c_ctx, w_mod, b_mod, norm1_gain, norm2_gain, w_in, da_q_gain, da_k_gain, da_lam_q1, da_lam_k1, da_lam_q2, da_lam_k2, da_sub_gain, sw_q_gain, sw_k_gain, sw_sink, lru_conv_w, lru_conv_b, lru_wa, lru_ba, lru_wx, lru_bx, lru_lambda, w_out, w_up, ffn_conv_w, ffn_conv_b, w_down):
    p = dict(w_in=w_in, norm1_gain=norm1_gain, norm2_gain=norm2_gain, da_q_gain=da_q_gain,
             da_k_gain=da_k_gain, da_lam_q1=da_lam_q1, da_lam_k1=da_lam_k1, da_lam_q2=da_lam_q2,
             da_lam_k2=da_lam_k2, da_sub_gain=da_sub_gain, sw_q_gain=sw_q_gain, sw_k_gain=sw_k_gain,
             sw_sink=sw_sink, lru_conv_w=lru_conv_w, lru_conv_b=lru_conv_b, lru_wa=lru_wa,
             lru_ba=lru_ba, lru_wx=lru_wx, lru_bx=lru_bx, lru_lambda=lru_lambda, w_out=w_out,
             w_up=w_up, ffn_conv_w=ffn_conv_w, ffn_conv_b=ffn_conv_b, w_down=w_down)
    bsz, n_tok, d = x.shape
    n_ctx = ctx.shape[1]
    depth = w_mod.shape[0]

    cc = jnp.zeros((2 * SUBLANES, d), F32).at[:bsz].set(c).at[bsz].set(c_ctx)
    mod_all = _modulation(cc, w_mod, b_mod)

    rope_lat = (_rope_tables(n_tok // GRID_W, DA_QK_DIM, 2 * DA_HEADS)
                + _rope_tables(n_tok // GRID_W, HEAD_DIM, SW_HEADS))
    rope_ctx = (jnp.ones((n_ctx, DA_Q), F32), jnp.zeros((n_ctx, DA_Q), F32),
                jnp.ones((n_ctx, SW_Q), F32), jnp.zeros((n_ctx, SW_Q), F32))

    xc = ctx
    for i in range(depth):
        lw = _prepare_layer(p, i)
        lambda_init = 0.8 - 0.6 * math.exp(-0.3 * i)
        ctx_out = i < depth - 1
        mod = mod_all[i, :bsz].reshape(bsz, 1, N_MOD, d)
        sh1, sc1, g1, sh2, sc2, g2 = [mod[:, :, m] for m in range(N_MOD)]
        mod_c = mod_all[i, bsz].reshape(1, 1, N_MOD, d)
        sh1c, sc1c, g1c, sh2c, sc2c, g2c = [mod_c[:, :, m] for m in range(N_MOD)]

        qa, ka, va, qb, kb, vb, cx, cg = _input_projection(x, sh1, sc1, lw, rope_lat)
        qac, kac, vac, qbc, kbc, vbc, cxc, cgc = _input_projection(xc, sh1c, sc1c, lw, rope_ctx)

        ya = _diff_attention(qa, [(kac, vac), (ka, va)], lw, lambda_init)
        yb = _window_attention(qb, kbc, vbc, kb, vb, lw)
        hf, hb, hfc, hbc = _rglru_scans(cx, cxc, lw)

        x = _output_projection(x, ya, yb, hf, hb, cg, g1, lw)
        x = _conv_ffn(x, sh2, sc2, g2, lw)
        if ctx_out:
            yac = _diff_attention(qac, [(kac, vac)], lw, lambda_init)
            ybc = _window_attention(qbc, kbc, vbc, None, None, lw)
            xc = _output_projection(xc, yac, ybc, hfc, hbc, cgc, g1c, lw)
            xc = _conv_ffn(xc, sh2c, sc2c, g2c, lw)
    return x
```

```python
import functools
import math

import numpy as np
import jax
import jax.numpy as jnp
from jax import lax
from jax.experimental import pallas as pl
from jax.experimental.pallas import tpu as pltpu

F32 = jnp.float32
BF16 = jnp.bfloat16

D_MODEL = 1024
GRID_W = 64
N_MOD = 6
EPS = 1e-6
NEG_INF = -1e30
ROPE_BASE = 10000.0
DA_HEADS = 4
DA_QK_DIM = 32
DA_V_DIM = 64
HEAD_DIM = 64
SW_HEADS = 6
SW_KV_HEADS = 2
SW_GROUP = SW_HEADS // SW_KV_HEADS
WINDOW = 128
LRU_WIDTH = 384
LRU_BLOCKS = 6
LRU_BLOCK_DIM = LRU_WIDTH // LRU_BLOCKS
LRU_CONV = 4
LRU_C = 8.0
D_FF = 2816
FFN_CONV = 3

DA_Q = DA_HEADS * 2 * DA_QK_DIM
DA_V = DA_HEADS * DA_V_DIM
SW_Q = SW_HEADS * HEAD_DIM
SW_KV = SW_KV_HEADS * HEAD_DIM

LANES = 128
SUBLANES = 8
BF16_ROWS = 16
MXU_DIM = 256
VMEM_LIMIT = 56 * 1024 * 1024

DA_VEXT = DA_HEADS * LANES
DA_VROWS = DA_V_DIM + BF16_ROWS
C_AQ = 0
C_AK = C_AQ + DA_Q
C_AV = C_AK + DA_Q
C_BQ = C_AV + DA_VEXT
C_BK = C_BQ + SW_Q
C_BV = C_BK + SW_KV
C_CX = C_BV + SW_KV
C_CG = C_CX + LRU_WIDTH
D_INX = C_CG + LRU_WIDTH

TM = 512
FFN_HALO = BF16_ROWS
FFN_CHUNKS = ((0, 1024), (1024, 2048), (2048, D_FF))
DA_TQ = 256
DA_AHEAD = 3
TM_OUT = 1024
SW_AHEAD = 2
SW_TQ = 256
SW_SUB = 2
LRU_TC = 256


def _cparams(n_axes):
    return pltpu.CompilerParams(dimension_semantics=("arbitrary",) * n_axes,
                                vmem_limit_bytes=VMEM_LIMIT)


def _const_spec(shape):
    nd = len(shape)
    return pl.BlockSpec(shape, lambda *_: (0,) * nd, pipeline_mode=pl.Buffered(1))


def _nt_dot(a, b):
    return lax.dot_general(a, b, (((1,), (1,)), ((), ())), preferred_element_type=F32)


def _dot(a, b):
    return jnp.dot(a, b, preferred_element_type=F32)


def _mod_kernel(c_ref, w_ref, b_ref, o_ref):
    c = c_ref[...]
    s = c * jax.nn.sigmoid(c)
    o_ref[0] = _dot(s.astype(BF16), w_ref[0].astype(BF16)) + b_ref[0]


def _modulation(cc, w_mod, b_mod):
    depth, d, n = w_mod.shape
    tn = 1536
    return pl.pallas_call(
        _mod_kernel,
        out_shape=jax.ShapeDtypeStruct((depth, cc.shape[0], n), F32),
        grid=(depth, n // tn),
        in_specs=[pl.BlockSpec(cc.shape, lambda i, j: (0, 0)),
                  pl.BlockSpec((1, d, tn), lambda i, j: (i, 0, j)),
                  pl.BlockSpec((1, 1, tn), lambda i, j: (i, 0, j))],
        out_specs=pl.BlockSpec((1, cc.shape[0], tn), lambda i, j: (i, 0, j)),
        compiler_params=_cparams(2),
        name="modulation",
    )(cc, w_mod, b_mod.reshape(depth, 1, n))


def _norm_modulate(x, gain, shift, scale):
    ms = jnp.mean(x * x, axis=-1, keepdims=True)
    h = x * lax.rsqrt(ms + EPS) * gain
    return h * (1.0 + scale) + shift


def _group_rms(a, g_ref):
    sq = (a * a).astype(BF16)
    n = a.shape[1]
    ms = [_dot(sq[:, lo:min(lo + MXU_DIM, n)], g_ref[:min(MXU_DIM, n - lo), :min(MXU_DIM, n - lo)])
          for lo in range(0, n, MXU_DIM)]
    ms = ms[0] if len(ms) == 1 else jnp.concatenate(ms, axis=1)
    return a * lax.rsqrt(ms + EPS)


def _rope(y, cos, sin_signed, half):
    n = y.shape[1]
    lane = lax.broadcasted_iota(jnp.int32, y.shape, 1)
    first = (lane % (2 * half)) < half
    partner = jnp.where(first, pltpu.roll(y, n - half, 1), pltpu.roll(y, half, 1))
    return y * cos + partner * sin_signed


def _inproj_kernel(x_ref, sh_ref, sc_ref, ng_ref, w_ref, vone_ref,
                   cosa_ref, sina_ref, cosb_ref, sinb_ref,
                   gqa_ref, gka_ref, gqb_ref, gkb_ref, g32_ref, g64_ref,
                   qa_ref, ka_ref, va_ref, qb_ref, kb_ref, vb_ref, cx_ref, cg_ref):
    h = _norm_modulate(x_ref[0], ng_ref[...], sh_ref[0], sc_ref[0]).astype(BF16)
    full = _dot(h, w_ref[...])

    def proj(lo, hi):
        return full[:, lo:hi]

    cosa, sina = cosa_ref[...], sina_ref[...]
    aq = _group_rms(proj(C_AQ, C_AK), g32_ref) * gqa_ref[...]
    qa_ref[0] = _rope(aq, cosa, sina, DA_QK_DIM // 2).astype(BF16)
    ak = _group_rms(proj(C_AK, C_AV), g32_ref) * gka_ref[...]
    ka_ref[0] = _rope(ak, cosa, sina, DA_QK_DIM // 2).astype(BF16)
    va_t = (proj(C_AV, C_BQ) + vone_ref[...]).T.astype(BF16)
    for hd in range(DA_HEADS):
        va_ref[0, 0, hd * DA_VROWS:(hd + 1) * DA_VROWS, :] = va_t[hd * LANES:hd * LANES + DA_VROWS]

    cosb, sinb = cosb_ref[...], sinb_ref[...]
    bq = _group_rms(proj(C_BQ, C_BK), g64_ref) * gqb_ref[...]
    qb_ref[0] = _rope(bq, cosb, sinb, HEAD_DIM // 2).astype(BF16)
    bkv = proj(C_BK, C_CX)
    bk = _group_rms(bkv[:, :SW_KV], g64_ref) * gkb_ref[...]
    kb_ref[0] = _rope(bk, cosb[:, :SW_KV], sinb[:, :SW_KV], HEAD_DIM // 2).astype(BF16)
    vb_ref[0] = bkv[:, SW_KV:].astype(BF16)

    cxg = proj(C_CX, D_INX)
    cx_ref[0] = cxg[:, :LRU_WIDTH]
    cg_ref[0] = cxg[:, LRU_WIDTH:].astype(cg_ref.dtype)


def _input_projection(x, shift, scale, lw, rope):
    bsz, t_len, d = x.shape
    tm = min(TM, t_len)
    per_b = shift.shape[0] > 1
    mod_map = (lambda b, j: (b, 0, 0)) if per_b else (lambda b, j: (0, 0, 0))
    tok = lambda n: pl.BlockSpec((1, tm, n), lambda b, j: (b, j, 0))
    tab = lambda n: pl.BlockSpec((tm, n), lambda b, j: (j, 0))
    widths = (DA_Q, DA_Q, DA_VEXT, SW_Q, SW_KV, SW_KV, LRU_WIDTH, LRU_WIDTH)
    dtypes = (BF16,) * 6 + (F32, BF16)
    out_shape = [jax.ShapeDtypeStruct((bsz, t_len, n), dt) for n, dt in zip(widths, dtypes)]
    out_specs = [tok(n) for n in widths]
    out_shape[2] = jax.ShapeDtypeStruct((bsz, t_len // tm, DA_HEADS * DA_VROWS, tm), BF16)
    out_specs[2] = pl.BlockSpec((1, 1, DA_HEADS * DA_VROWS, tm), lambda b, j: (b, j, 0, 0))
    return pl.pallas_call(
        _inproj_kernel,
        out_shape=tuple(out_shape),
        grid=(bsz, t_len // tm),
        in_specs=[tok(d),
                  pl.BlockSpec((1, 1, d), mod_map), pl.BlockSpec((1, 1, d), mod_map),
                  _const_spec((1, d)), _const_spec((d, D_INX)), _const_spec((1, DA_VEXT)),
                  tab(DA_Q), tab(DA_Q), tab(SW_Q), tab(SW_Q),
                  _const_spec((1, DA_Q)), _const_spec((1, DA_Q)),
                  _const_spec((1, SW_Q)), _const_spec((1, SW_KV)),
                  _const_spec((MXU_DIM, MXU_DIM)), _const_spec((MXU_DIM, MXU_DIM))],
        out_specs=tuple(out_specs),
        compiler_params=_cparams(2),
        name="input_projection",
    )(x, shift, scale, lw["norm1_gain"], lw["w_in"], lw["v_one"], *rope,
      lw["gq_a"], lw["gk_a"], lw["gq_b"], lw["gk_b"], lw["g32"], lw["g64"])


def _da_kernel(*refs, n_seg, lambda_init):
    q_ref = refs[0]
    kv_refs = refs[1:1 + 2 * n_seg]
    hmask_ref, lamv_ref, sg_ref, o_ref, m_ref, acc_ref = refs[1 + 2 * n_seg:]
    n_map = 2 * DA_HEADS

    m_ref[...] = jnp.full(m_ref.shape, NEG_INF, F32)
    acc_ref[...] = jnp.zeros(acc_ref.shape, F32)

    items = [(kv_refs[2 * si], kv_refs[2 * si + 1], i, j)
             for si in range(n_seg) for i in range(kv_refs[2 * si + 1].shape[1])
             for j in range(n_map)]

    def scores(k_ref, v_ref, i, j):
        half, tk = j // 4, v_ref.shape[3]
        qm = q_ref[0, :, half * LANES:(half + 1) * LANES] * hmask_ref[j:j + 1, :]
        kblk = k_ref[0, i * tk:(i + 1) * tk, half * LANES:(half + 1) * LANES]
        return _nt_dot(kblk, qm)

    pending = {n: scores(*items[n]) for n in range(min(DA_AHEAD, len(items)))}
    for n, (k_ref, v_ref, i, j) in enumerate(items):
        s_b = pending.pop(n).astype(BF16)
        if n + DA_AHEAD < len(items):
            pending[n + DA_AHEAD] = scores(*items[n + DA_AHEAD])
        m_old = m_ref[j, 0:1, :]
        m_new = jnp.maximum(m_old, jnp.max(s_b, axis=0, keepdims=True).astype(F32))
        p_t = jnp.exp2(s_b - m_new.astype(BF16))
        pv = _dot(v_ref[0, i, (j // 2) * DA_VROWS:(j // 2 + 1) * DA_VROWS, :], p_t)
        acc_ref[j] = jnp.exp2(m_old - m_new) * acc_ref[j] + pv
        m_ref[j] = jnp.broadcast_to(m_new, m_ref.shape[1:])

    lv = lamv_ref[...]
    lam = (jnp.exp(jnp.sum(lv[0:1] * lv[1:2], axis=1, keepdims=True))
           - jnp.exp(jnp.sum(lv[2:3] * lv[3:4], axis=1, keepdims=True)) + lambda_init)

    def head_out(h):
        a1, a2 = acc_ref[2 * h], acc_ref[2 * h + 1]
        o = (a1 / a1[DA_V_DIM:DA_V_DIM + 1] - lam * (a2 / a2[DA_V_DIM:DA_V_DIM + 1]))[:DA_V_DIM]
        ms = jnp.sum(o * o, axis=0, keepdims=True) * (1.0 / DA_V_DIM)
        return o * lax.rsqrt(ms + EPS)

    for c in range(DA_HEADS // 2):
        pair_t = jnp.concatenate([head_out(2 * c), head_out(2 * c + 1)], axis=0)
        o_ref[0, :, c * LANES:(c + 1) * LANES] = (pair_t.T * sg_ref[...]).astype(o_ref.dtype)


def _diff_attention(q, kvs, lw, lambda_init):
    bsz, t_q, _ = q.shape
    tq = min(DA_TQ, t_q)
    kv_specs, kv_args = [], []
    for k, v in kvs:
        kv_specs += [pl.BlockSpec((1,) + k.shape[1:], lambda b, j: (b, 0, 0)),
                     pl.BlockSpec((1,) + v.shape[1:], lambda b, j: (b, 0, 0, 0))]
        kv_args += [k, v]
    sg = lw["da_sg"] * (1.0 - lambda_init)
    return pl.pallas_call(
        functools.partial(_da_kernel, n_seg=len(kvs), lambda_init=lambda_init),
        out_shape=jax.ShapeDtypeStruct((bsz, t_q, DA_V), BF16),
        grid=(bsz, t_q // tq),
        in_specs=[pl.BlockSpec((1, tq, DA_Q), lambda b, j: (b, j, 0))] + kv_specs
                 + [_const_spec((2 * DA_HEADS, LANES)), _const_spec((4, DA_QK_DIM)),
                    _const_spec((1, LANES))],
        out_specs=pl.BlockSpec((1, tq, DA_V), lambda b, j: (b, j, 0)),
        scratch_shapes=[pltpu.VMEM((2 * DA_HEADS, SUBLANES, tq), F32),
                        pltpu.VMEM((2 * DA_HEADS, DA_VROWS, tq), F32)],
        compiler_params=_cparams(2),
        name="diff_attention",
    )(q, *kv_args, lw["da_hmask"], lw["da_lamv"], sg)


def _sw_kernel(*refs, has_lat, tq, t_len):
    if has_lat:
        sink_ref, q_ref, kc_ref, vc_ref, k_ref, v_ref, hm_ref, o_ref = refs
    else:
        sink_ref, q_ref, kc_ref, vc_ref, hm_ref, o_ref = refs
    def values_t(v):
        n_k = v.shape[0]
        v_t = v.astype(F32).T
        tail = (lax.broadcasted_iota(jnp.int32, (BF16_ROWS, n_k), 0) == 0).astype(F32)
        return [jnp.concatenate([v_t[kv * HEAD_DIM:(kv + 1) * HEAD_DIM], tail], axis=0).astype(BF16)
                for kv in range(SW_KV_HEADS)]

    kc = kc_ref[0]
    vc_t = values_t(vc_ref[0])
    n_sub = q_ref.shape[1] // tq
    bands = []
    for u in range(n_sub if has_lat else 0):
        band = tq + 2 * WINDOW
        q0 = (pl.program_id(1) * n_sub + u) * tq
        start = pl.multiple_of(jnp.clip(q0 - WINDOW, 0, t_len - band), WINDOW)
        qpos = q0 + lax.broadcasted_iota(jnp.int32, (1, tq), 1)
        kpos = start + lax.broadcasted_iota(jnp.int32, (band, 1), 0)
        bands.append((k_ref[0, pl.ds(start, band), :], values_t(v_ref[0, pl.ds(start, band), :]),
                      jnp.abs(qpos - kpos) <= WINDOW))

    combos = [(u, g, kv) for u in range(n_sub) for g in range(SW_GROUP)
              for kv in range(SW_KV_HEADS)]

    def scores(c):
        u, g, kv = combos[c]
        qm = q_ref[0, u * tq:(u + 1) * tq, g * LANES:(g + 1) * LANES] * hm_ref[kv:kv + 1, :]
        s_lat = _nt_dot(bands[u][0], qm) if has_lat else None
        return _nt_dot(kc, qm), s_lat

    pending = {c: scores(c) for c in range(SW_AHEAD)}
    outs = {}
    for c, (u, g, kv) in enumerate(combos):
        s_ctx, s_lat = pending.pop(c)
        if c + SW_AHEAD < len(combos):
            pending[c + SW_AHEAD] = scores(c + SW_AHEAD)
        sink = sink_ref[kv * SW_GROUP + g]
        s_ctx = s_ctx.astype(BF16)
        m = jnp.maximum(jnp.max(s_ctx, axis=0, keepdims=True).astype(F32), sink)
        if has_lat:
            s_lat = jnp.where(bands[u][2], s_lat.astype(BF16), NEG_INF)
            m = jnp.maximum(m, jnp.max(s_lat, axis=0, keepdims=True).astype(F32))
        m_b = m.astype(BF16)
        m = m_b.astype(F32)
        acc = _dot(vc_t[kv], jnp.exp2(s_ctx - m_b))
        if has_lat:
            acc = acc + _dot(bands[u][1][kv], jnp.exp2(s_lat - m_b))
        l = acc[HEAD_DIM:HEAD_DIM + 1] + jnp.exp2(sink - m)
        outs[kv] = acc[:HEAD_DIM] / l
        if kv == SW_KV_HEADS - 1:
            pair_t = jnp.concatenate([outs.pop(k2) for k2 in range(SW_KV_HEADS)], axis=0)
            o_ref[0, u * tq:(u + 1) * tq, g * LANES:(g + 1) * LANES] = pair_t.T.astype(o_ref.dtype)


def _window_attention(q, kc, vc, k, v, lw):
    bsz, t_q, _ = q.shape
    has_lat = k is not None
    tq = min(SW_TQ, t_q)
    tb = min(SW_SUB * tq, t_q)
    n_ctx = kc.shape[1]
    full = lambda t: pl.BlockSpec((1, t, SW_KV), lambda b, j: (b, 0, 0))
    specs = [pl.BlockSpec(memory_space=pltpu.SMEM),
             pl.BlockSpec((1, tb, SW_Q), lambda b, j: (b, j, 0)), full(n_ctx), full(n_ctx)]
    args = [lw["sw_sink"], q, kc, vc]
    if has_lat:
        specs += [full(t_q), full(t_q)]
        args += [k, v]
    specs.append(_const_spec((SW_KV_HEADS, LANES)))
    args.append(lw["sw_hmask"])
    return pl.pallas_call(
        functools.partial(_sw_kernel, has_lat=has_lat, tq=tq, t_len=t_q),
        out_shape=jax.ShapeDtypeStruct((bsz, t_q, SW_Q), BF16),
        grid=(bsz, t_q // tb),
        in_specs=specs,
        out_specs=pl.BlockSpec((1, tb, SW_Q), lambda b, j: (b, j, 0)),
        compiler_params=_cparams(2),
        name="window_attention",
    )(*args)


def _lru_kernel(xf_ref, xb_ref, xc_ref, cw_ref, cb_ref, wa_ref, ba_ref, wx_ref, bx_ref, lam_ref,
                hf_ref, hb_ref, hfc_ref, hbc_ref, halo_ref, carry_ref, *, tc):
    j = pl.program_id(1)
    is_ctx = j == 0
    sub = lax.broadcasted_iota(jnp.int32, (1, SUBLANES, 1), 1)

    for d, x_lat_ref, out_ref, outc_ref in ((0, xf_ref, hf_ref, hfc_ref), (1, xb_ref, hb_ref, hbc_ref)):
        x = jnp.where(is_ctx, xc_ref[0], x_lat_ref[0])
        halo = jnp.where(j <= 1, 0.0, halo_ref[d])
        cw = cw_ref[d]
        n_grp = tc // SUBLANES
        x3 = x.reshape(n_grp, SUBLANES, x.shape[1])
        xc3 = cb_ref[d] + x3 * cw[LRU_CONV - 1:LRU_CONV]
        for k in range(LRU_CONV - 1):
            s = LRU_CONV - 1 - k
            if d == 0:
                rolled = pltpu.roll(x3, s, 1)
                other = jnp.concatenate([pltpu.roll(halo, s, 0)[None], rolled[:-1]], axis=0)
                xs = jnp.where(sub < s, other, rolled)
            else:
                rolled = pltpu.roll(x3, SUBLANES - s, 1)
                other = jnp.concatenate([rolled[1:], pltpu.roll(halo, SUBLANES - s, 0)[None]], axis=0)
                xs = jnp.where(sub >= SUBLANES - s, other, rolled)
            xc3 = xc3 + xs * cw[k:k + 1]
        xc = xc3.reshape(tc, x.shape[1])
        halo_ref[d] = x[tc - SUBLANES:] if d == 0 else x[:SUBLANES]

        xcb = xc.astype(BF16)
        r = jax.nn.sigmoid(_dot(xcb, wa_ref[d]) + ba_ref[d])
        gi = jax.nn.sigmoid(_dot(xcb, wx_ref[d]) + bx_ref[d])
        nl = -lam_ref[d]
        softplus = jnp.maximum(nl, 0.0) + jnp.log1p(jnp.exp(-jnp.abs(nl)))
        log_a = -LRU_C * r * softplus
        a = jnp.exp(log_a)
        b = jnp.sqrt(-jnp.tanh(log_a) * (a * a + 1.0)) * (gi * xc)

        a3 = a.reshape(n_grp, SUBLANES, a.shape[1])
        b3 = b.reshape(n_grp, SUBLANES, b.shape[1])
        step = 1
        while step < SUBLANES:
            keep = (sub < step) if d == 0 else (sub >= SUBLANES - step)
            shift = step if d == 0 else SUBLANES - step
            a_s = jnp.where(keep, 1.0, pltpu.roll(a3, shift, 1))
            b_s = jnp.where(keep, 0.0, pltpu.roll(b3, shift, 1))
            b3 = a3 * b_s + b3
            a3 = a3 * a_s
            step *= 2

        edge_row = SUBLANES - 1 if d == 0 else 0
        h_in = jnp.where(is_ctx, 0.0, carry_ref[d, edge_row:edge_row + 1, :])
        groups = [None] * n_grp
        for gi in (range(n_grp) if d == 0 else range(n_grp - 1, -1, -1)):
            groups[gi] = a3[gi] * h_in + b3[gi]
            h_in = groups[gi][edge_row:edge_row + 1]
        h = jnp.concatenate(groups, axis=0)
        carry_ref[d] = groups[n_grp - 1] if d == 0 else groups[0]

        h_out = h.astype(out_ref.dtype)

        @pl.when(is_ctx)
        def _():
            outc_ref[0] = h_out

        @pl.when(jnp.logical_not(is_ctx))
        def _():
            out_ref[0] = h_out


def _rglru_scans(cx, cxc, lw):
    bsz, t_len, n = cx.shape
    tc = LRU_TC
    assert cxc.shape[1] == tc and t_len % tc == 0
    n_lat = t_len // tc
    fwd = lambda b, j: (b, jnp.maximum(j - 1, 0), 0)
    bwd = lambda b, j: (b, n_lat - jnp.maximum(j, 1), 0)
    ctx = lambda b, j: (b, 0, 0)
    blk = lambda m: pl.BlockSpec((1, tc, n), m)
    return pl.pallas_call(
        functools.partial(_lru_kernel, tc=tc),
        out_shape=(jax.ShapeDtypeStruct(cx.shape, BF16), jax.ShapeDtypeStruct(cx.shape, BF16),
                   jax.ShapeDtypeStruct(cxc.shape, BF16), jax.ShapeDtypeStruct(cxc.shape, BF16)),
        grid=(bsz, n_lat + 1),
        in_specs=[blk(fwd), blk(bwd), blk(ctx),
                  _const_spec((2, LRU_CONV, n)), _const_spec((2, 1, n)),
                  _const_spec((2, n, n)), _const_spec((2, 1, n)),
                  _const_spec((2, n, n)), _const_spec((2, 1, n)), _const_spec((2, 1, n))],
        out_specs=(blk(fwd), blk(bwd), blk(ctx), blk(ctx)),
        scratch_shapes=[pltpu.VMEM((2, SUBLANES, n), F32), pltpu.VMEM((2, SUBLANES, n), F32)],
        compiler_params=_cparams(2),
        name="rglru_scans",
    )(cx, cx, cxc, lw["lru_conv_w"], lw["lru_conv_b"], lw["lru_wa"], lw["lru_ba"],
      lw["lru_wx"], lw["lru_bx"], lw["lru_lambda"])


def _gelu_tanh(x):
    return 0.5 * x * (1.0 + jnp.tanh(math.sqrt(2.0 / math.pi) * (x + 0.044715 * (x * x * x))))


def _outproj_kernel(x_ref, ya_ref, yb_ref, hf_ref, hb_ref, cg_ref, g1_ref, w_ref, o_ref, cat_ref):
    cat_ref[:, 0:DA_V] = ya_ref[0]
    cat_ref[:, DA_V:DA_V + SW_Q] = yb_ref[0]
    yc = (hf_ref[0].astype(F32) + hb_ref[0].astype(F32)) * _gelu_tanh(cg_ref[0].astype(F32))
    cat_ref[:, DA_V + SW_Q:] = yc.astype(BF16)
    o_ref[0] = x_ref[0] + g1_ref[0] * _dot(cat_ref[...], w_ref[...])


def _output_projection(x, ya, yb, hf, hb, cg, gate, lw):
    bsz, t_len, d = x.shape
    tm = min(TM_OUT, t_len)
    per_b = gate.shape[0] > 1
    mod_map = (lambda b, j: (b, 0, 0)) if per_b else (lambda b, j: (0, 0, 0))
    tok = lambda n: pl.BlockSpec((1, tm, n), lambda b, j: (b, j, 0))
    return pl.pallas_call(
        _outproj_kernel,
        out_shape=jax.ShapeDtypeStruct(x.shape, F32),
        grid=(bsz, t_len // tm),
        in_specs=[tok(d), tok(DA_V), tok(SW_Q), tok(LRU_WIDTH), tok(LRU_WIDTH), tok(LRU_WIDTH),
                  pl.BlockSpec((1, 1, d), mod_map), _const_spec((d, d))],
        out_specs=tok(d),
        scratch_shapes=[pltpu.VMEM((tm, d), BF16)],
        compiler_params=_cparams(2),
        name="output_projection",
    )(x, ya, yb, hf, hb, cg, gate, lw["w_out"])


def _ffn_kernel(xm_ref, xp_ref, xn_ref, sh_ref, sc_ref, g2_ref, ng_ref, wup_ref, cw_ref, cb_ref,
                wdn_ref, o_ref, h_ref, *, tm):
    j = pl.program_id(1)
    last = pl.num_programs(1) - 1
    gain, shift, scale = ng_ref[...], sh_ref[0], sc_ref[0]
    xm = xm_ref[0]
    hp = jnp.where(j > 0, _norm_modulate(xp_ref[0], gain, shift, scale), 0.0)
    hn = jnp.where(j < last, _norm_modulate(xn_ref[0], gain, shift, scale), 0.0)
    h_ref[0:FFN_HALO] = hp.astype(BF16)
    h_ref[FFN_HALO:FFN_HALO + tm] = _norm_modulate(xm, gain, shift, scale).astype(BF16)
    h_ref[FFN_HALO + tm:] = hn.astype(BF16)

    rows = tm + 2 * FFN_HALO
    acc = jnp.zeros((tm, D_MODEL), F32)
    for lo, hi in FFN_CHUNKS:
        gp = _dot(h_ref[...], wup_ref[:, lo:hi])
        g_prev = pltpu.roll(gp, 1, 0)[FFN_HALO:FFN_HALO + tm]
        g_next = pltpu.roll(gp, rows - 1, 0)[FFN_HALO:FFN_HALO + tm]
        cw = cw_ref[:, lo:hi]
        gate = (cb_ref[:, lo:hi] + g_prev * cw[0:1] + gp[FFN_HALO:FFN_HALO + tm] * cw[1:2]
                + g_next * cw[2:3])
        val = _dot(h_ref[FFN_HALO:FFN_HALO + tm], wup_ref[:, D_FF + lo:D_FF + hi])
        act = (gate * jax.nn.sigmoid(gate) * val).astype(BF16)
        acc = acc + _dot(act, wdn_ref[lo:hi, :])
    o_ref[0] = xm + g2_ref[0] * acc


def _conv_ffn(x, shift, scale, gate, lw):
    bsz, t_len, d = x.shape
    tm = min(TM, t_len)
    per_b = gate.shape[0] > 1
    mod_map = (lambda b, j: (b, 0, 0)) if per_b else (lambda b, j: (0, 0, 0))
    hb = tm // FFN_HALO
    n_hb = t_len // FFN_HALO
    prev_map = lambda b, j: (b, jnp.maximum(j * hb - 1, 0), 0)
    next_map = lambda b, j: (b, jnp.minimum((j + 1) * hb, n_hb - 1), 0)
    mod_spec = pl.BlockSpec((1, 1, d), mod_map)
    return pl.pallas_call(
        functools.partial(_ffn_kernel, tm=tm),
        out_shape=jax.ShapeDtypeStruct(x.shape, F32),
        grid=(bsz, t_len // tm),
        in_specs=[pl.BlockSpec((1, tm, d), lambda b, j: (b, j, 0)),
                  pl.BlockSpec((1, FFN_HALO, d), prev_map),
                  pl.BlockSpec((1, FFN_HALO, d), next_map),
                  mod_spec, mod_spec, mod_spec,
                  _const_spec((1, d)), _const_spec((d, 2 * D_FF)),
                  _const_spec((FFN_CONV, D_FF)), _const_spec((1, D_FF)), _const_spec((D_FF, d))],
        out_specs=pl.BlockSpec((1, tm, d), lambda b, j: (b, j, 0)),
        scratch_shapes=[pltpu.VMEM((tm + 2 * FFN_HALO, d), BF16)],
        compiler_params=_cparams(2),
        name="conv_ffn",
    )(x, x, x, shift, scale, gate, lw["norm2_gain"], lw["w_up"], lw["ffn_conv_w"],
      lw["ffn_conv_b"], lw["w_down"])


def _sw_head_order():
    return [kv * SW_GROUP + g for g in range(SW_GROUP) for kv in range(SW_KV_HEADS)]


def _rope_tables(rows, head_dim, n_heads):
    row = jnp.repeat(jnp.arange(rows, dtype=F32), GRID_W)
    col = jnp.tile(jnp.arange(GRID_W, dtype=F32), rows)
    quarter = head_dim // 4
    inv_freq = ROPE_BASE ** (-jnp.arange(quarter, dtype=F32) / quarter)
    ang = jnp.concatenate([row[:, None] * inv_freq, col[:, None] * inv_freq], axis=-1)
    cos, sin = jnp.cos(ang), jnp.sin(ang)
    return (jnp.tile(jnp.concatenate([cos, cos], axis=-1), (1, n_heads)),
            jnp.tile(jnp.concatenate([-sin, sin], axis=-1), (1, n_heads)))


def _block_diag_mean(n, group):
    idx = np.arange(n) // group
    return jnp.asarray((idx[:, None] == idx[None, :]).astype(np.float32) / group, dtype=BF16)


def _block_diag(w):
    two, nb, bi, bj = w.shape
    eye = jnp.eye(nb, dtype=w.dtype)
    return jnp.einsum("dhij,hg->dhigj", w, eye).reshape(two, nb * bi, nb * bj)


def _prepare_layer(p, i):
    d = D_MODEL
    split = np.cumsum((DA_Q, DA_Q, DA_V, SW_Q, SW_KV, SW_KV, LRU_WIDTH))
    w_in = p["w_in"][i]
    w_aq, w_ak, w_av, w_bq, w_bk, w_bv, w_cx, w_cg = jnp.split(w_in, split.tolist(), axis=1)
    w_av = jnp.pad(w_av.reshape(d, DA_HEADS, DA_V_DIM), ((0, 0), (0, 0), (0, LANES - DA_V_DIM)))
    order = _sw_head_order()
    w_bq = w_bq.reshape(d, SW_HEADS, HEAD_DIM)[:, order].reshape(d, SW_Q)
    w_inx = jnp.concatenate([w_aq, w_ak, w_av.reshape(d, DA_VEXT), w_bq, w_bk, w_bv, w_cx, w_cg],
                            axis=1).astype(BF16)
    v_one = np.zeros((1, DA_VEXT), np.float32)
    v_one[0, DA_V_DIM::LANES] = 1.0

    w_out = p["w_out"][i]
    w_ob = w_out[DA_V:DA_V + SW_Q].reshape(SW_HEADS, HEAD_DIM, d)[jnp.asarray(order)]
    w_out = jnp.concatenate([w_out[:DA_V], w_ob.reshape(SW_Q, d), w_out[DA_V + SW_Q:]], axis=0)

    da_hmask = np.zeros((2 * DA_HEADS, LANES), np.float32)
    for j in range(2 * DA_HEADS):
        off = (j % 4) * DA_QK_DIM
        da_hmask[j, off:off + DA_QK_DIM] = 1.0
    sw_hmask = np.zeros((SW_KV_HEADS, LANES), np.float32)
    for kv in range(SW_KV_HEADS):
        sw_hmask[kv, kv * HEAD_DIM:(kv + 1) * HEAD_DIM] = 1.0

    return dict(
        norm1_gain=p["norm1_gain"][i].reshape(1, d), norm2_gain=p["norm2_gain"][i].reshape(1, d),
        w_in=w_inx, v_one=jnp.asarray(v_one),
        gq_a=(jnp.tile(p["da_q_gain"][i], 2 * DA_HEADS).reshape(1, DA_Q)
              * (DA_QK_DIM ** -0.5 * math.log2(math.e))),
        gk_a=jnp.tile(p["da_k_gain"][i], 2 * DA_HEADS).reshape(1, DA_Q),
        gq_b=(jnp.tile(p["sw_q_gain"][i], SW_HEADS).reshape(1, SW_Q)
              * (HEAD_DIM ** -0.5 * math.log2(math.e))),
        gk_b=jnp.tile(p["sw_k_gain"][i], SW_KV_HEADS).reshape(1, SW_KV),
        g32=_block_diag_mean(MXU_DIM, DA_QK_DIM), g64=_block_diag_mean(MXU_DIM, HEAD_DIM),
        da_hmask=jnp.asarray(da_hmask, dtype=BF16),
        da_lamv=jnp.stack([p["da_lam_q1"][i], p["da_lam_k1"][i], p["da_lam_q2"][i], p["da_lam_k2"][i]]),
        da_sg=jnp.tile(p["da_sub_gain"][i], LANES // DA_V_DIM).reshape(1, LANES),
        sw_sink=p["sw_sink"][i] * math.log2(math.e), sw_hmask=jnp.asarray(sw_hmask, dtype=BF16),
        lru_conv_w=p["lru_conv_w"][i], lru_conv_b=p["lru_conv_b"][i].reshape(2, 1, LRU_WIDTH),
        lru_wa=_block_diag(p["lru_wa"][i]).astype(BF16), lru_ba=p["lru_ba"][i].reshape(2, 1, LRU_WIDTH),
        lru_wx=_block_diag(p["lru_wx"][i]).astype(BF16), lru_bx=p["lru_bx"][i].reshape(2, 1, LRU_WIDTH),
        lru_lambda=p["lru_lambda"][i].reshape(2, 1, LRU_WIDTH),
        w_out=w_out.astype(BF16), w_up=p["w_up"][i].astype(BF16),
        ffn_conv_w=p["ffn_conv_w"][i], ffn_conv_b=p["ffn_conv_b"][i].reshape(1, D_FF),
        w_down=p["w_down"][i].astype(BF16),
    )


def kernel(x, c, ctx, c_ctx, w_mod, b_mod, norm1_gain, norm2_gain, w_in, da_q_gain, da_k_gain, da_lam_q1, da_lam_k1, da_lam_q2, da_lam_k2, da_sub_gain, sw_q_gain, sw_k_gain, sw_sink, lru_conv_w, lru_conv_b, lru_wa, lru_ba, lru_wx, lru_bx, lru_lambda, w_out, w_up, ffn_conv_w, ffn_conv_b, w_down):
    p = dict(w_in=w_in, norm1_gain=norm1_gain, norm2_gain=norm2_gain, da_q_gain=da_q_gain,
             da_k_gain=da_k_gain, da_lam_q1=da_lam_q1, da_lam_k1=da_lam_k1, da_lam_q2=da_lam_q2,
             da_lam_k2=da_lam_k2, da_sub_gain=da_sub_gain, sw_q_gain=sw_q_gain, sw_k_gain=sw_k_gain,
             sw_sink=sw_sink, lru_conv_w=lru_conv_w, lru_conv_b=lru_conv_b, lru_wa=lru_wa,
             lru_ba=lru_ba, lru_wx=lru_wx, lru_bx=lru_bx, lru_lambda=lru_lambda, w_out=w_out,
             w_up=w_up, ffn_conv_w=ffn_conv_w, ffn_conv_b=ffn_conv_b, w_down=w_down)
    bsz, n_tok, d = x.shape
    n_ctx = ctx.shape[1]
    depth = w_mod.shape[0]

    cc = jnp.zeros((2 * SUBLANES, d), F32).at[:bsz].set(c).at[bsz].set(c_ctx)
    mod_all = _modulation(cc, w_mod, b_mod)

    rope_lat = (_rope_tables(n_tok // GRID_W, DA_QK_DIM, 2 * DA_HEADS)
                + _rope_tables(n_tok // GRID_W, HEAD_DIM, SW_HEADS))
    rope_ctx = (jnp.ones((n_ctx, DA_Q), F32), jnp.zeros((n_ctx, DA_Q), F32),
                jnp.ones((n_ctx, SW_Q), F32), jnp.zeros((n_ctx, SW_Q), F32))

    xc = ctx
    for i in range(depth):
        lw = _prepare_layer(p, i)
        lambda_init = 0.8 - 0.6 * math.exp(-0.3 * i)
        ctx_out = i < depth - 1
        mod = mod_all[i, :bsz].reshape(bsz, 1, N_MOD, d)
        sh1, sc1, g1, sh2, sc2, g2 = [mod[:, :, m] for m in range(N_MOD)]
        mod_c = mod_all[i, bsz].reshape(1, 1, N_MOD, d)
        sh1c, sc1c, g1c, sh2c, sc2c, g2c = [mod_c[:, :, m] for m in range(N_MOD)]

        qa, ka, va, qb, kb, vb, cx, cg = _input_projection(x, sh1, sc1, lw, rope_lat)
        qac, kac, vac, qbc, kbc, vbc, cxc, cgc = _input_projection(xc, sh1c, sc1c, lw, rope_ctx)

        ya = _diff_attention(qa, [(kac, vac), (ka, va)], lw, lambda_init)
        yb = _window_attention(qb, kbc, vbc, kb, vb, lw)
        hf, hb, hfc, hbc = _rglru_scans(cx, cxc, lw)

        x = _output_projection(x, ya, yb, hf, hb, cg, g1, lw)
        x = _conv_ffn(x, sh2, sc2, g2, lw)
        if ctx_out:
            yac = _diff_attention(qac, [(kac, vac)], lw, lambda_init)
            ybc = _window_attention(qbc, kbc, vbc, None, None, lw)
            xc = _output_projection(xc, yac, ybc, hfc, hbc, cgc, g1c, lw)
            xc = _conv_ffn(xc, sh2c, sc2c, g2c, lw)
    return x
```

```python
import functools
import math

import numpy as np
import jax
import jax.numpy as jnp
from jax import lax
from jax.experimental import pallas as pl
from jax.experimental.pallas import tpu as pltpu

F32 = jnp.float32
BF16 = jnp.bfloat16

D_MODEL = 1024
GRID_W = 64
N_MOD = 6
EPS = 1e-6
NEG_INF = -1e30
ROPE_BASE = 10000.0
DA_HEADS = 4
DA_QK_DIM = 32
DA_V_DIM = 64
HEAD_DIM = 64
SW_HEADS = 6
SW_KV_HEADS = 2
SW_GROUP = SW_HEADS // SW_KV_HEADS
WINDOW = 128
LRU_WIDTH = 384
LRU_BLOCKS = 6
LRU_BLOCK_DIM = LRU_WIDTH // LRU_BLOCKS
LRU_CONV = 4
LRU_C = 8.0
D_FF = 2816
FFN_CONV = 3

DA_Q = DA_HEADS * 2 * DA_QK_DIM
DA_V = DA_HEADS * DA_V_DIM
SW_Q = SW_HEADS * HEAD_DIM
SW_KV = SW_KV_HEADS * HEAD_DIM

LANES = 128
SUBLANES = 8
BF16_ROWS = 16
MXU_DIM = 256
VMEM_LIMIT = 56 * 1024 * 1024

DA_VEXT = DA_HEADS * LANES
DA_VROWS = DA_V_DIM + BF16_ROWS
C_AQ = 0
C_AK = C_AQ + DA_Q
C_AV = C_AK + DA_Q
C_BQ = C_AV + DA_VEXT
C_BK = C_BQ + SW_Q
C_BV = C_BK + SW_KV
C_CX = C_BV + SW_KV
C_CG = C_CX + LRU_WIDTH
D_INX = C_CG + LRU_WIDTH

TM = 512
FFN_HALO = BF16_ROWS
FFN_CHUNKS = ((0, 1024), (1024, 2048), (2048, D_FF))
DA_TQ = 256
DA_AHEAD = 3
TM_OUT = 1024
SW_AHEAD = 2
SW_TQ = 256
SW_SUB = 4
LRU_TC = 256


def _cparams(n_axes):
    return pltpu.CompilerParams(dimension_semantics=("arbitrary",) * n_axes,
                                vmem_limit_bytes=VMEM_LIMIT)


def _const_spec(shape):
    nd = len(shape)
    return pl.BlockSpec(shape, lambda *_: (0,) * nd, pipeline_mode=pl.Buffered(1))


def _layer_spec(shape, layer):
    nd = len(shape)
    return pl.BlockSpec((None,) + tuple(shape[1:]), lambda *_: (layer,) + (0,) * (nd - 1),
                        pipeline_mode=pl.Buffered(1))


def _nt_dot(a, b):
    return lax.dot_general(a, b, (((1,), (1,)), ((), ())), preferred_element_type=F32)


def _dot(a, b):
    return jnp.dot(a, b, preferred_element_type=F32)


def _mod_kernel(c_ref, w_ref, b_ref, o_ref):
    c = c_ref[...]
    s = c * jax.nn.sigmoid(c)
    o_ref[0] = _dot(s.astype(BF16), w_ref[0].astype(BF16)) + b_ref[0]


def _modulation(cc, w_mod, b_mod):
    depth, d, n = w_mod.shape
    tn = 1536
    return pl.pallas_call(
        _mod_kernel,
        out_shape=jax.ShapeDtypeStruct((depth, cc.shape[0], n), F32),
        grid=(depth, n // tn),
        in_specs=[pl.BlockSpec(cc.shape, lambda i, j: (0, 0)),
                  pl.BlockSpec((1, d, tn), lambda i, j: (i, 0, j)),
                  pl.BlockSpec((1, 1, tn), lambda i, j: (i, 0, j))],
        out_specs=pl.BlockSpec((1, cc.shape[0], tn), lambda i, j: (i, 0, j)),
        compiler_params=_cparams(2),
        name="modulation",
    )(cc, w_mod, b_mod.reshape(depth, 1, n))


def _norm_modulate(x, gain, shift, scale):
    ms = jnp.mean(x * x, axis=-1, keepdims=True)
    h = x * lax.rsqrt(ms + EPS) * gain
    return h * (1.0 + scale) + shift


def _group_rms(a, g_ref):
    sq = (a * a).astype(BF16)
    n = a.shape[1]
    ms = [_dot(sq[:, lo:min(lo + MXU_DIM, n)], g_ref[:min(MXU_DIM, n - lo), :min(MXU_DIM, n - lo)])
          for lo in range(0, n, MXU_DIM)]
    ms = ms[0] if len(ms) == 1 else jnp.concatenate(ms, axis=1)
    return a * lax.rsqrt(ms + EPS)


def _rope(y, cos, sin_signed, half):
    n = y.shape[1]
    lane = lax.broadcasted_iota(jnp.int32, y.shape, 1)
    first = (lane % (2 * half)) < half
    partner = jnp.where(first, pltpu.roll(y, n - half, 1), pltpu.roll(y, half, 1))
    return y * cos + partner * sin_signed


def _inproj_kernel(x_ref, sh_ref, sc_ref, ng_ref, w_ref, vone_ref,
                   cosa_ref, sina_ref, cosb_ref, sinb_ref,
                   gqa_ref, gka_ref, gqb_ref, gkb_ref, g32_ref, g64_ref,
                   qa_ref, ka_ref, va_ref, qb_ref, kb_ref, vb_ref, cx_ref, cg_ref):
    h = _norm_modulate(x_ref[0], ng_ref[...], sh_ref[0], sc_ref[0]).astype(BF16)
    full = _dot(h, w_ref[...])

    def proj(lo, hi):
        return full[:, lo:hi]

    cosa, sina = cosa_ref[...], sina_ref[...]
    aq = _group_rms(proj(C_AQ, C_AK), g32_ref) * gqa_ref[...]
    qa_ref[0] = _rope(aq, cosa, sina, DA_QK_DIM // 2).astype(BF16)
    ak = _group_rms(proj(C_AK, C_AV), g32_ref) * gka_ref[...]
    ka_ref[0] = _rope(ak, cosa, sina, DA_QK_DIM // 2).astype(BF16)
    va_t = (proj(C_AV, C_BQ) + vone_ref[...]).T.astype(BF16)
    for hd in range(DA_HEADS):
        va_ref[0, 0, hd * DA_VROWS:(hd + 1) * DA_VROWS, :] = va_t[hd * LANES:hd * LANES + DA_VROWS]

    cosb, sinb = cosb_ref[...], sinb_ref[...]
    bq = _group_rms(proj(C_BQ, C_BK), g64_ref) * gqb_ref[...]
    qb_ref[0] = _rope(bq, cosb, sinb, HEAD_DIM // 2).astype(BF16)
    bkv = proj(C_BK, C_CX)
    bk = _group_rms(bkv[:, :SW_KV], g64_ref) * gkb_ref[...]
    kb_ref[0] = _rope(bk, cosb[:, :SW_KV], sinb[:, :SW_KV], HEAD_DIM // 2).astype(BF16)
    vb_ref[0] = bkv[:, SW_KV:].astype(BF16)

    cxg = proj(C_CX, D_INX)
    cx_ref[0] = cxg[:, :LRU_WIDTH]
    cg_ref[0] = cxg[:, LRU_WIDTH:].astype(cg_ref.dtype)


def _input_projection(x, shift, scale, lw, rope):
    bsz, t_len, d = x.shape
    tm = min(TM, t_len)
    per_b = shift.shape[0] > 1
    mod_map = (lambda b, j: (b, 0, 0)) if per_b else (lambda b, j: (0, 0, 0))
    tok = lambda n: pl.BlockSpec((1, tm, n), lambda b, j: (b, j, 0))
    tab = lambda n: pl.BlockSpec((tm, n), lambda b, j: (j, 0))
    widths = (DA_Q, DA_Q, DA_VEXT, SW_Q, SW_KV, SW_KV, LRU_WIDTH, LRU_WIDTH)
    dtypes = (BF16,) * 6 + (F32, BF16)
    out_shape = [jax.ShapeDtypeStruct((bsz, t_len, n), dt) for n, dt in zip(widths, dtypes)]
    out_specs = [tok(n) for n in widths]
    out_shape[2] = jax.ShapeDtypeStruct((bsz, t_len // tm, DA_HEADS * DA_VROWS, tm), BF16)
    out_specs[2] = pl.BlockSpec((1, 1, DA_HEADS * DA_VROWS, tm), lambda b, j: (b, j, 0, 0))
    return pl.pallas_call(
        _inproj_kernel,
        out_shape=tuple(out_shape),
        grid=(bsz, t_len // tm),
        in_specs=[tok(d),
                  pl.BlockSpec((1, 1, d), mod_map), pl.BlockSpec((1, 1, d), mod_map),
                  _const_spec((1, d)), _layer_spec(lw["w_in"].shape, lw["layer"]), _const_spec((1, DA_VEXT)),
                  tab(DA_Q), tab(DA_Q), tab(SW_Q), tab(SW_Q),
                  _const_spec((1, DA_Q)), _const_spec((1, DA_Q)),
                  _const_spec((1, SW_Q)), _const_spec((1, SW_KV)),
                  _const_spec((MXU_DIM, MXU_DIM)), _const_spec((MXU_DIM, MXU_DIM))],
        out_specs=tuple(out_specs),
        compiler_params=_cparams(2),
        name="input_projection",
    )(x, shift, scale, lw["norm1_gain"], lw["w_in"], lw["v_one"], *rope,
      lw["gq_a"], lw["gk_a"], lw["gq_b"], lw["gk_b"], lw["g32"], lw["g64"])


def _da_kernel(*refs, n_seg, lambda_init):
    q_ref = refs[0]
    kv_refs = refs[1:1 + 2 * n_seg]
    hmask_ref, lamv_ref, sg_ref, o_ref, m_ref, acc_ref = refs[1 + 2 * n_seg:]
    n_map = 2 * DA_HEADS

    m_ref[...] = jnp.full(m_ref.shape, NEG_INF, F32)
    acc_ref[...] = jnp.zeros(acc_ref.shape, F32)

    items = [(kv_refs[2 * si], kv_refs[2 * si + 1], i, j)
             for si in range(n_seg) for i in range(kv_refs[2 * si + 1].shape[1])
             for j in range(n_map)]

    def scores(k_ref, v_ref, i, j):
        half, tk = j // 4, v_ref.shape[3]
        qm = q_ref[0, :, half * LANES:(half + 1) * LANES] * hmask_ref[j:j + 1, :]
        kblk = k_ref[0, i * tk:(i + 1) * tk, half * LANES:(half + 1) * LANES]
        return _nt_dot(kblk, qm)

    pending = {n: scores(*items[n]) for n in range(min(DA_AHEAD, len(items)))}
    for n, (k_ref, v_ref, i, j) in enumerate(items):
        s_b = pending.pop(n).astype(BF16)
        if n + DA_AHEAD < len(items):
            pending[n + DA_AHEAD] = scores(*items[n + DA_AHEAD])
        m_old = m_ref[j, 0:1, :]
        m_new = jnp.maximum(m_old, jnp.max(s_b, axis=0, keepdims=True).astype(F32))
        p_t = jnp.exp2(s_b - m_new.astype(BF16))
        pv = _dot(v_ref[0, i, (j // 2) * DA_VROWS:(j // 2 + 1) * DA_VROWS, :], p_t)
        acc_ref[j] = jnp.exp2(m_old - m_new) * acc_ref[j] + pv
        m_ref[j] = jnp.broadcast_to(m_new, m_ref.shape[1:])

    lv = lamv_ref[...]
    lam = (jnp.exp(jnp.sum(lv[0:1] * lv[1:2], axis=1, keepdims=True))
           - jnp.exp(jnp.sum(lv[2:3] * lv[3:4], axis=1, keepdims=True)) + lambda_init)

    def head_out(h):
        a1, a2 = acc_ref[2 * h], acc_ref[2 * h + 1]
        o = (a1 / a1[DA_V_DIM:DA_V_DIM + 1] - lam * (a2 / a2[DA_V_DIM:DA_V_DIM + 1]))[:DA_V_DIM]
        ms = jnp.sum(o * o, axis=0, keepdims=True) * (1.0 / DA_V_DIM)
        return o * lax.rsqrt(ms + EPS)

    for c in range(DA_HEADS // 2):
        pair_t = jnp.concatenate([head_out(2 * c), head_out(2 * c + 1)], axis=0)
        o_ref[0, :, c * LANES:(c + 1) * LANES] = (pair_t.T * sg_ref[...]).astype(o_ref.dtype)


def _diff_attention(q, kvs, lw, lambda_init):
    bsz, t_q, _ = q.shape
    tq = min(DA_TQ, t_q)
    kv_specs, kv_args = [], []
    for k, v in kvs:
        kv_specs += [pl.BlockSpec((1,) + k.shape[1:], lambda b, j: (b, 0, 0)),
                     pl.BlockSpec((1,) + v.shape[1:], lambda b, j: (b, 0, 0, 0))]
        kv_args += [k, v]
    sg = lw["da_sg"] * (1.0 - lambda_init)
    return pl.pallas_call(
        functools.partial(_da_kernel, n_seg=len(kvs), lambda_init=lambda_init),
        out_shape=jax.ShapeDtypeStruct((bsz, t_q, DA_V), BF16),
        grid=(bsz, t_q // tq),
        in_specs=[pl.BlockSpec((1, tq, DA_Q), lambda b, j: (b, j, 0))] + kv_specs
                 + [_const_spec((2 * DA_HEADS, LANES)), _const_spec((4, DA_QK_DIM)),
                    _const_spec((1, LANES))],
        out_specs=pl.BlockSpec((1, tq, DA_V), lambda b, j: (b, j, 0)),
        scratch_shapes=[pltpu.VMEM((2 * DA_HEADS, SUBLANES, tq), F32),
                        pltpu.VMEM((2 * DA_HEADS, DA_VROWS, tq), F32)],
        compiler_params=_cparams(2),
        name="diff_attention",
    )(q, *kv_args, lw["da_hmask"], lw["da_lamv"], sg)


def _sw_kernel(*refs, has_lat, tq, t_len):
    if has_lat:
        sink_ref, q_ref, kc_ref, vc_ref, k_ref, v_ref, hm_ref, o_ref = refs
    else:
        sink_ref, q_ref, kc_ref, vc_ref, hm_ref, o_ref = refs
    def values_t(v):
        n_k = v.shape[0]
        v_t = v.astype(F32).T
        tail = (lax.broadcasted_iota(jnp.int32, (BF16_ROWS, n_k), 0) == 0).astype(F32)
        return [jnp.concatenate([v_t[kv * HEAD_DIM:(kv + 1) * HEAD_DIM], tail], axis=0).astype(BF16)
                for kv in range(SW_KV_HEADS)]

    kc = kc_ref[0]
    vc_t = values_t(vc_ref[0])
    n_sub = q_ref.shape[1] // tq
    bands = []
    for u in range(n_sub if has_lat else 0):
        band = tq + 2 * WINDOW
        q0 = (pl.program_id(1) * n_sub + u) * tq
        start = pl.multiple_of(jnp.clip(q0 - WINDOW, 0, t_len - band), WINDOW)
        qpos = q0 + lax.broadcasted_iota(jnp.int32, (1, tq), 1)
        kpos = start + lax.broadcasted_iota(jnp.int32, (band, 1), 0)
        bands.append((k_ref[0, pl.ds(start, band), :], values_t(v_ref[0, pl.ds(start, band), :]),
                      jnp.abs(qpos - kpos) <= WINDOW))

    combos = [(u, g, kv) for u in range(n_sub) for g in range(SW_GROUP)
              for kv in range(SW_KV_HEADS)]

    def scores(c):
        u, g, kv = combos[c]
        qm = q_ref[0, u * tq:(u + 1) * tq, g * LANES:(g + 1) * LANES] * hm_ref[kv:kv + 1, :]
        s_lat = _nt_dot(bands[u][0], qm) if has_lat else None
        return _nt_dot(kc, qm), s_lat

    pending = {c: scores(c) for c in range(SW_AHEAD)}
    outs = {}
    for c, (u, g, kv) in enumerate(combos):
        s_ctx, s_lat = pending.pop(c)
        if c + SW_AHEAD < len(combos):
            pending[c + SW_AHEAD] = scores(c + SW_AHEAD)
        sink = sink_ref[kv * SW_GROUP + g]
        s_ctx = s_ctx.astype(BF16)
        m = jnp.maximum(jnp.max(s_ctx, axis=0, keepdims=True).astype(F32), sink)
        if has_lat:
            s_lat = jnp.where(bands[u][2], s_lat.astype(BF16), NEG_INF)
            m = jnp.maximum(m, jnp.max(s_lat, axis=0, keepdims=True).astype(F32))
        m_b = m.astype(BF16)
        m = m_b.astype(F32)
        acc = _dot(vc_t[kv], jnp.exp2(s_ctx - m_b))
        if has_lat:
            acc = acc + _dot(bands[u][1][kv], jnp.exp2(s_lat - m_b))
        l = acc[HEAD_DIM:HEAD_DIM + 1] + jnp.exp2(sink - m)
        outs[kv] = acc[:HEAD_DIM] / l
        if kv == SW_KV_HEADS - 1:
            pair_t = jnp.concatenate([outs.pop(k2) for k2 in range(SW_KV_HEADS)], axis=0)
            o_ref[0, u * tq:(u + 1) * tq, g * LANES:(g + 1) * LANES] = pair_t.T.astype(o_ref.dtype)


def _window_attention(q, kc, vc, k, v, lw):
    bsz, t_q, _ = q.shape
    has_lat = k is not None
    tq = min(SW_TQ, t_q)
    tb = min(SW_SUB * tq, t_q)
    n_ctx = kc.shape[1]
    full = lambda t: pl.BlockSpec((1, t, SW_KV), lambda b, j: (b, 0, 0))
    specs = [pl.BlockSpec(memory_space=pltpu.SMEM),
             pl.BlockSpec((1, tb, SW_Q), lambda b, j: (b, j, 0)), full(n_ctx), full(n_ctx)]
    args = [lw["sw_sink"], q, kc, vc]
    if has_lat:
        specs += [full(t_q), full(t_q)]
        args += [k, v]
    specs.append(_const_spec((SW_KV_HEADS, LANES)))
    args.append(lw["sw_hmask"])
    return pl.pallas_call(
        functools.partial(_sw_kernel, has_lat=has_lat, tq=tq, t_len=t_q),
        out_shape=jax.ShapeDtypeStruct((bsz, t_q, SW_Q), BF16),
        grid=(bsz, t_q // tb),
        in_specs=specs,
        out_specs=pl.BlockSpec((1, tb, SW_Q), lambda b, j: (b, j, 0)),
        compiler_params=_cparams(2),
        name="window_attention",
    )(*args)


def _lru_kernel(xf_ref, xb_ref, xc_ref, cw_ref, cb_ref, wa_ref, ba_ref, wx_ref, bx_ref, lam_ref,
                hf_ref, hb_ref, hfc_ref, hbc_ref, halo_ref, carry_ref, *, tc):
    j = pl.program_id(1)
    is_ctx = j == 0
    sub = lax.broadcasted_iota(jnp.int32, (1, SUBLANES, 1), 1)

    for d, x_lat_ref, out_ref, outc_ref in ((0, xf_ref, hf_ref, hfc_ref), (1, xb_ref, hb_ref, hbc_ref)):
        x = jnp.where(is_ctx, xc_ref[0], x_lat_ref[0])
        halo = jnp.where(j <= 1, 0.0, halo_ref[d])
        cw = cw_ref[d]
        n_grp = tc // SUBLANES
        x3 = x.reshape(n_grp, SUBLANES, x.shape[1])
        xc3 = cb_ref[d] + x3 * cw[LRU_CONV - 1:LRU_CONV]
        for k in range(LRU_CONV - 1):
            s = LRU_CONV - 1 - k
            if d == 0:
                rolled = pltpu.roll(x3, s, 1)
                other = jnp.concatenate([pltpu.roll(halo, s, 0)[None], rolled[:-1]], axis=0)
                xs = jnp.where(sub < s, other, rolled)
            else:
                rolled = pltpu.roll(x3, SUBLANES - s, 1)
                other = jnp.concatenate([rolled[1:], pltpu.roll(halo, SUBLANES - s, 0)[None]], axis=0)
                xs = jnp.where(sub >= SUBLANES - s, other, rolled)
            xc3 = xc3 + xs * cw[k:k + 1]
        xc = xc3.reshape(tc, x.shape[1])
        halo_ref[d] = x[tc - SUBLANES:] if d == 0 else x[:SUBLANES]

        xcb = xc.astype(BF16)
        r = jax.nn.sigmoid(_dot(xcb, wa_ref[d]) + ba_ref[d])
        gi = jax.nn.sigmoid(_dot(xcb, wx_ref[d]) + bx_ref[d])
        nl = -lam_ref[d]
        softplus = jnp.maximum(nl, 0.0) + jnp.log1p(jnp.exp(-jnp.abs(nl)))
        log_a = -LRU_C * r * softplus
        a = jnp.exp(log_a)
        b = jnp.sqrt(-jnp.tanh(log_a) * (a * a + 1.0)) * (gi * xc)

        a3 = a.reshape(n_grp, SUBLANES, a.shape[1])
        b3 = b.reshape(n_grp, SUBLANES, b.shape[1])
        step = 1
        while step < SUBLANES:
            keep = (sub < step) if d == 0 else (sub >= SUBLANES - step)
            shift = step if d == 0 else SUBLANES - step
            a_s = jnp.where(keep, 1.0, pltpu.roll(a3, shift, 1))
            b_s = jnp.where(keep, 0.0, pltpu.roll(b3, shift, 1))
            b3 = a3 * b_s + b3
            a3 = a3 * a_s
            step *= 2

        edge_row = SUBLANES - 1 if d == 0 else 0
        h_in = jnp.where(is_ctx, 0.0, carry_ref[d, edge_row:edge_row + 1, :])
        groups = [None] * n_grp
        for gi in (range(n_grp) if d == 0 else range(n_grp - 1, -1, -1)):
            groups[gi] = a3[gi] * h_in + b3[gi]
            h_in = groups[gi][edge_row:edge_row + 1]
        h = jnp.concatenate(groups, axis=0)
        carry_ref[d] = groups[n_grp - 1] if d == 0 else groups[0]

        h_out = h.astype(out_ref.dtype)

        @pl.when(is_ctx)
        def _():
            outc_ref[0] = h_out

        @pl.when(jnp.logical_not(is_ctx))
        def _():
            out_ref[0] = h_out


def _rglru_scans(cx, cxc, lw):
    bsz, t_len, n = cx.shape
    tc = LRU_TC
    assert cxc.shape[1] == tc and t_len % tc == 0
    n_lat = t_len // tc
    fwd = lambda b, j: (b, jnp.maximum(j - 1, 0), 0)
    bwd = lambda b, j: (b, n_lat - jnp.maximum(j, 1), 0)
    ctx = lambda b, j: (b, 0, 0)
    blk = lambda m: pl.BlockSpec((1, tc, n), m)
    return pl.pallas_call(
        functools.partial(_lru_kernel, tc=tc),
        out_shape=(jax.ShapeDtypeStruct(cx.shape, BF16), jax.ShapeDtypeStruct(cx.shape, BF16),
                   jax.ShapeDtypeStruct(cxc.shape, BF16), jax.ShapeDtypeStruct(cxc.shape, BF16)),
        grid=(bsz, n_lat + 1),
        in_specs=[blk(fwd), blk(bwd), blk(ctx),
                  _const_spec((2, LRU_CONV, n)), _const_spec((2, 1, n)),
                  _const_spec((2, n, n)), _const_spec((2, 1, n)),
                  _const_spec((2, n, n)), _const_spec((2, 1, n)), _const_spec((2, 1, n))],
        out_specs=(blk(fwd), blk(bwd), blk(ctx), blk(ctx)),
        scratch_shapes=[pltpu.VMEM((2, SUBLANES, n), F32), pltpu.VMEM((2, SUBLANES, n), F32)],
        compiler_params=_cparams(2),
        name="rglru_scans",
    )(cx, cx, cxc, lw["lru_conv_w"], lw["lru_conv_b"], lw["lru_wa"], lw["lru_ba"],
      lw["lru_wx"], lw["lru_bx"], lw["lru_lambda"])


def _gelu_tanh(x):
    return 0.5 * x * (1.0 + jnp.tanh(math.sqrt(2.0 / math.pi) * (x + 0.044715 * (x * x * x))))


def _outproj_kernel(x_ref, ya_ref, yb_ref, hf_ref, hb_ref, cg_ref, g1_ref, w_ref, o_ref, cat_ref):
    cat_ref[:, 0:DA_V] = ya_ref[0]
    cat_ref[:, DA_V:DA_V + SW_Q] = yb_ref[0]
    yc = (hf_ref[0].astype(F32) + hb_ref[0].astype(F32)) * _gelu_tanh(cg_ref[0].astype(F32))
    cat_ref[:, DA_V + SW_Q:] = yc.astype(BF16)
    o_ref[0] = x_ref[0] + g1_ref[0] * _dot(cat_ref[...], w_ref[...])


def _output_projection(x, ya, yb, hf, hb, cg, gate, lw):
    bsz, t_len, d = x.shape
    tm = min(TM_OUT, t_len)
    per_b = gate.shape[0] > 1
    mod_map = (lambda b, j: (b, 0, 0)) if per_b else (lambda b, j: (0, 0, 0))
    tok = lambda n: pl.BlockSpec((1, tm, n), lambda b, j: (b, j, 0))
    return pl.pallas_call(
        _outproj_kernel,
        out_shape=jax.ShapeDtypeStruct(x.shape, F32),
        grid=(bsz, t_len // tm),
        in_specs=[tok(d), tok(DA_V), tok(SW_Q), tok(LRU_WIDTH), tok(LRU_WIDTH), tok(LRU_WIDTH),
                  pl.BlockSpec((1, 1, d), mod_map), _layer_spec(lw["w_out"].shape, lw["layer"])],
        out_specs=tok(d),
        scratch_shapes=[pltpu.VMEM((tm, d), BF16)],
        compiler_params=_cparams(2),
        name="output_projection",
    )(x, ya, yb, hf, hb, cg, gate, lw["w_out"])


def _ffn_kernel(xm_ref, xp_ref, xn_ref, sh_ref, sc_ref, g2_ref, ng_ref, wup_ref, cw_ref, cb_ref,
                wdn_ref, o_ref, h_ref, *, tm):
    j = pl.program_id(1)
    last = pl.num_programs(1) - 1
    gain, shift, scale = ng_ref[...], sh_ref[0], sc_ref[0]
    xm = xm_ref[0]
    hp = jnp.where(j > 0, _norm_modulate(xp_ref[0], gain, shift, scale), 0.0)
    hn = jnp.where(j < last, _norm_modulate(xn_ref[0], gain, shift, scale), 0.0)
    h_ref[0:FFN_HALO] = hp.astype(BF16)
    h_ref[FFN_HALO:FFN_HALO + tm] = _norm_modulate(xm, gain, shift, scale).astype(BF16)
    h_ref[FFN_HALO + tm:] = hn.astype(BF16)

    rows = tm + 2 * FFN_HALO
    acc = jnp.zeros((tm, D_MODEL), F32)
    for lo, hi in FFN_CHUNKS:
        gp = _dot(h_ref[...], wup_ref[:, lo:hi])
        g_prev = pltpu.roll(gp, 1, 0)[FFN_HALO:FFN_HALO + tm]
        g_next = pltpu.roll(gp, rows - 1, 0)[FFN_HALO:FFN_HALO + tm]
        cw = cw_ref[:, lo:hi]
        gate = (cb_ref[:, lo:hi] + g_prev * cw[0:1] + gp[FFN_HALO:FFN_HALO + tm] * cw[1:2]
                + g_next * cw[2:3])
        val = _dot(h_ref[FFN_HALO:FFN_HALO + tm], wup_ref[:, D_FF + lo:D_FF + hi])
        act = (gate * jax.nn.sigmoid(gate) * val).astype(BF16)
        acc = acc + _dot(act, wdn_ref[lo:hi, :])
    o_ref[0] = xm + g2_ref[0] * acc


def _conv_ffn(x, shift, scale, gate, lw):
    bsz, t_len, d = x.shape
    tm = min(TM, t_len)
    per_b = gate.shape[0] > 1
    mod_map = (lambda b, j: (b, 0, 0)) if per_b else (lambda b, j: (0, 0, 0))
    hb = tm // FFN_HALO
    n_hb = t_len // FFN_HALO
    prev_map = lambda b, j: (b, jnp.maximum(j * hb - 1, 0), 0)
    next_map = lambda b, j: (b, jnp.minimum((j + 1) * hb, n_hb - 1), 0)
    mod_spec = pl.BlockSpec((1, 1, d), mod_map)
    return pl.pallas_call(
        functools.partial(_ffn_kernel, tm=tm),
        out_shape=jax.ShapeDtypeStruct(x.shape, F32),
        grid=(bsz, t_len // tm),
        in_specs=[pl.BlockSpec((1, tm, d), lambda b, j: (b, j, 0)),
                  pl.BlockSpec((1, FFN_HALO, d), prev_map),
                  pl.BlockSpec((1, FFN_HALO, d), next_map),
                  mod_spec, mod_spec, mod_spec,
                  _const_spec((1, d)), _layer_spec(lw["w_up"].shape, lw["layer"]),
                  _const_spec((FFN_CONV, D_FF)), _const_spec((1, D_FF)),
                  _layer_spec(lw["w_down"].shape, lw["layer"])],
        out_specs=pl.BlockSpec((1, tm, d), lambda b, j: (b, j, 0)),
        scratch_shapes=[pltpu.VMEM((tm + 2 * FFN_HALO, d), BF16)],
        compiler_params=_cparams(2),
        name="conv_ffn",
    )(x, x, x, shift, scale, gate, lw["norm2_gain"], lw["w_up"], lw["ffn_conv_w"],
      lw["ffn_conv_b"], lw["w_down"])


def _sw_head_order():
    return [kv * SW_GROUP + g for g in range(SW_GROUP) for kv in range(SW_KV_HEADS)]


def _rope_tables(rows, head_dim, n_heads):
    row = np.repeat(np.arange(rows, dtype=np.float64), GRID_W)
    col = np.tile(np.arange(GRID_W, dtype=np.float64), rows)
    quarter = head_dim // 4
    inv_freq = ROPE_BASE ** (-np.arange(quarter, dtype=np.float64) / quarter)
    ang = np.concatenate([row[:, None] * inv_freq, col[:, None] * inv_freq], axis=-1)
    cos, sin = np.cos(ang), np.sin(ang)
    return (jnp.asarray(np.tile(np.concatenate([cos, cos], axis=-1), (1, n_heads)), dtype=F32),
            jnp.asarray(np.tile(np.concatenate([-sin, sin], axis=-1), (1, n_heads)), dtype=F32))


def _block_diag_mean(n, group):
    idx = np.arange(n) // group
    return jnp.asarray((idx[:, None] == idx[None, :]).astype(np.float32) / group, dtype=BF16)


def _block_diag(w):
    two, nb, bi, bj = w.shape
    eye = jnp.eye(nb, dtype=w.dtype)
    return jnp.einsum("dhij,hg->dhigj", w, eye).reshape(two, nb * bi, nb * bj)


def _prepare_weights(p):
    d = D_MODEL
    depth = p["w_in"].shape[0]
    split = np.cumsum((DA_Q, DA_Q, DA_V, SW_Q, SW_KV, SW_KV, LRU_WIDTH))
    w_aq, w_ak, w_av, w_bq, w_bk, w_bv, w_cx, w_cg = jnp.split(p["w_in"], split.tolist(), axis=2)
    w_av = jnp.pad(w_av.reshape(depth, d, DA_HEADS, DA_V_DIM),
                   ((0, 0), (0, 0), (0, 0), (0, LANES - DA_V_DIM))).reshape(depth, d, DA_VEXT)
    order = jnp.asarray(_sw_head_order())
    w_bq = w_bq.reshape(depth, d, SW_HEADS, HEAD_DIM)[:, :, order].reshape(depth, d, SW_Q)
    w_in = jnp.concatenate([w_aq, w_ak, w_av, w_bq, w_bk, w_bv, w_cx, w_cg], axis=2)

    w_out = p["w_out"]
    w_ob = w_out[:, DA_V:DA_V + SW_Q].reshape(depth, SW_HEADS, HEAD_DIM, d)[:, order]
    w_out = jnp.concatenate([w_out[:, :DA_V], w_ob.reshape(depth, SW_Q, d), w_out[:, DA_V + SW_Q:]],
                            axis=1)
    return dict(w_in=w_in.astype(BF16), w_out=w_out.astype(BF16),
                w_up=p["w_up"].astype(BF16), w_down=p["w_down"].astype(BF16))


def _prepare_layer(p, i):
    d = D_MODEL
    v_one = np.zeros((1, DA_VEXT), np.float32)
    v_one[0, DA_V_DIM::LANES] = 1.0

    da_hmask = np.zeros((2 * DA_HEADS, LANES), np.float32)
    for j in range(2 * DA_HEADS):
        off = (j % 4) * DA_QK_DIM
        da_hmask[j, off:off + DA_QK_DIM] = 1.0
    sw_hmask = np.zeros((SW_KV_HEADS, LANES), np.float32)
    for kv in range(SW_KV_HEADS):
        sw_hmask[kv, kv * HEAD_DIM:(kv + 1) * HEAD_DIM] = 1.0

    return dict(
        norm1_gain=p["norm1_gain"][i].reshape(1, d), norm2_gain=p["norm2_gain"][i].reshape(1, d),
        layer=i, v_one=jnp.asarray(v_one),
        gq_a=(jnp.tile(p["da_q_gain"][i], 2 * DA_HEADS).reshape(1, DA_Q)
              * (DA_QK_DIM ** -0.5 * math.log2(math.e))),
        gk_a=jnp.tile(p["da_k_gain"][i], 2 * DA_HEADS).reshape(1, DA_Q),
        gq_b=(jnp.tile(p["sw_q_gain"][i], SW_HEADS).reshape(1, SW_Q)
              * (HEAD_DIM ** -0.5 * math.log2(math.e))),
        gk_b=jnp.tile(p["sw_k_gain"][i], SW_KV_HEADS).reshape(1, SW_KV),
        g32=_block_diag_mean(MXU_DIM, DA_QK_DIM), g64=_block_diag_mean(MXU_DIM, HEAD_DIM),
        da_hmask=jnp.asarray(da_hmask, dtype=BF16),
        da_lamv=jnp.stack([p["da_lam_q1"][i], p["da_lam_k1"][i], p["da_lam_q2"][i], p["da_lam_k2"][i]]),
        da_sg=jnp.tile(p["da_sub_gain"][i], LANES // DA_V_DIM).reshape(1, LANES),
        sw_sink=p["sw_sink"][i] * math.log2(math.e), sw_hmask=jnp.asarray(sw_hmask, dtype=BF16),
        lru_conv_w=p["lru_conv_w"][i], lru_conv_b=p["lru_conv_b"][i].reshape(2, 1, LRU_WIDTH),
        lru_wa=_block_diag(p["lru_wa"][i]).astype(BF16), lru_ba=p["lru_ba"][i].reshape(2, 1, LRU_WIDTH),
        lru_wx=_block_diag(p["lru_wx"][i]).astype(BF16), lru_bx=p["lru_bx"][i].reshape(2, 1, LRU_WIDTH),
        lru_lambda=p["lru_lambda"][i].reshape(2, 1, LRU_WIDTH),
        ffn_conv_w=p["ffn_conv_w"][i], ffn_conv_b=p["ffn_conv_b"][i].reshape(1, D_FF),
    )


def kernel(x, c, ctx, c_ctx, w_mod, b_mod, norm1_gain, norm2_gain, w_in, da_q_gain, da_k_gain, da_lam_q1, da_lam_k1, da_lam_q2, da_lam_k2, da_sub_gain, sw_q_gain, sw_k_gain, sw_sink, lru_conv_w, lru_conv_b, lru_wa, lru_ba, lru_wx, lru_bx, lru_lambda, w_out, w_up, ffn_conv_w, ffn_conv_b, w_down):
    p = dict(w_in=w_in, norm1_gain=norm1_gain, norm2_gain=norm2_gain, da_q_gain=da_q_gain,
             da_k_gain=da_k_gain, da_lam_q1=da_lam_q1, da_lam_k1=da_lam_k1, da_lam_q2=da_lam_q2,
             da_lam_k2=da_lam_k2, da_sub_gain=da_sub_gain, sw_q_gain=sw_q_gain, sw_k_gain=sw_k_gain,
             sw_sink=sw_sink, lru_conv_w=lru_conv_w, lru_conv_b=lru_conv_b, lru_wa=lru_wa,
             lru_ba=lru_ba, lru_wx=lru_wx, lru_bx=lru_bx, lru_lambda=lru_lambda, w_out=w_out,
             w_up=w_up, ffn_conv_w=ffn_conv_w, ffn_conv_b=ffn_conv_b, w_down=w_down)
    bsz, n_tok, d = x.shape
    n_ctx = ctx.shape[1]
    depth = w_mod.shape[0]

    cc = jnp.zeros((2 * SUBLANES, d), F32).at[:bsz].set(c).at[bsz].set(c_ctx)
    mod_all = _modulation(cc, w_mod, b_mod)

    rope_lat = (_rope_tables(n_tok // GRID_W, DA_QK_DIM, 2 * DA_HEADS)
                + _rope_tables(n_tok // GRID_W, HEAD_DIM, SW_HEADS))
    rope_ctx = (jnp.ones((n_ctx, DA_Q), F32), jnp.zeros((n_ctx, DA_Q), F32),
                jnp.ones((n_ctx, SW_Q), F32), jnp.zeros((n_ctx, SW_Q), F32))

    weights = _prepare_weights(p)
    xc = ctx
    for i in range(depth):
        lw = {**_prepare_layer(p, i), **weights}
        lambda_init = 0.8 - 0.6 * math.exp(-0.3 * i)
        ctx_out = i < depth - 1
        mod = mod_all[i, :bsz].reshape(bsz, 1, N_MOD, d)
        sh1, sc1, g1, sh2, sc2, g2 = [mod[:, :, m] for m in range(N_MOD)]
        mod_c = mod_all[i, bsz].reshape(1, 1, N_MOD, d)
        sh1c, sc1c, g1c, sh2c, sc2c, g2c = [mod_c[:, :, m] for m in range(N_MOD)]

        qa, ka, va, qb, kb, vb, cx, cg = _input_projection(x, sh1, sc1, lw, rope_lat)
        qac, kac, vac, qbc, kbc, vbc, cxc, cgc = _input_projection(xc, sh1c, sc1c, lw, rope_ctx)

        ya = _diff_attention(qa, [(kac, vac), (ka, va)], lw, lambda_init)
        yb = _window_attention(qb, kbc, vbc, kb, vb, lw)
        hf, hb, hfc, hbc = _rglru_scans(cx, cxc, lw)

        x = _output_projection(x, ya, yb, hf, hb, cg, g1, lw)
        x = _conv_ffn(x, sh2, sc2, g2, lw)
        if ctx_out:
            yac = _diff_attention(qac, [(kac, vac)], lw, lambda_init)
            ybc = _window_attention(qbc, kbc, vbc, None, None, lw)
            xc = _output_projection(xc, yac, ybc, hfc, hbc, cgc, g1c, lw)
            xc = _conv_ffn(xc, sh2c, sc2c, g2c, lw)
    return x
```

```python
import functools
import math

import numpy as np
import jax
import jax.numpy as jnp
from jax import lax
from jax.experimental import pallas as pl
from jax.experimental.pallas import tpu as pltpu

F32 = jnp.float32
BF16 = jnp.bfloat16

D_MODEL = 1024
GRID_W = 64
N_MOD = 6
EPS = 1e-6
NEG_INF = -1e30
ROPE_BASE = 10000.0
DA_HEADS = 4
DA_QK_DIM = 32
DA_V_DIM = 64
HEAD_DIM = 64
SW_HEADS = 6
SW_KV_HEADS = 2
SW_GROUP = SW_HEADS // SW_KV_HEADS
WINDOW = 128
LRU_WIDTH = 384
LRU_BLOCKS = 6
LRU_BLOCK_DIM = LRU_WIDTH // LRU_BLOCKS
LRU_CONV = 4
LRU_C = 8.0
D_FF = 2816
FFN_CONV = 3

DA_Q = DA_HEADS * 2 * DA_QK_DIM
DA_V = DA_HEADS * DA_V_DIM
SW_Q = SW_HEADS * HEAD_DIM
SW_KV = SW_KV_HEADS * HEAD_DIM

LANES = 128
SUBLANES = 8
BF16_ROWS = 16
MXU_DIM = 256
VMEM_LIMIT = 56 * 1024 * 1024

DA_VEXT = DA_HEADS * LANES
DA_VROWS = DA_V_DIM + BF16_ROWS
C_AQ = 0
C_AK = C_AQ + DA_Q
C_AV = C_AK + DA_Q
C_BQ = C_AV + DA_VEXT
C_BK = C_BQ + SW_Q
C_BV = C_BK + SW_KV
C_CX = C_BV + SW_KV
C_CG = C_CX + LRU_WIDTH
D_INX = C_CG + LRU_WIDTH

TM = 512
FFN_HALO = BF16_ROWS
FFN_CHUNKS = ((0, 1024), (1024, 2048), (2048, D_FF))
DA_TQ = 256
DA_AHEAD = 3
TM_IN = 1024
IN_SUB = 256
TM_OUT = 1024
SW_AHEAD = 2
SW_TQ = 256
SW_SUB = 4
LRU_TC = 256


def _cparams(n_axes):
    return pltpu.CompilerParams(dimension_semantics=("arbitrary",) * n_axes,
                                vmem_limit_bytes=VMEM_LIMIT)


def _const_spec(shape):
    nd = len(shape)
    return pl.BlockSpec(shape, lambda *_: (0,) * nd, pipeline_mode=pl.Buffered(1))


def _layer_spec(shape, layer):
    nd = len(shape)
    return pl.BlockSpec((None,) + tuple(shape[1:]), lambda *_: (layer,) + (0,) * (nd - 1),
                        pipeline_mode=pl.Buffered(1))


def _nt_dot(a, b):
    return lax.dot_general(a, b, (((1,), (1,)), ((), ())), preferred_element_type=F32)


def _dot(a, b):
    return jnp.dot(a, b, preferred_element_type=F32)


def _mod_kernel(c_ref, w_ref, b_ref, o_ref):
    c = c_ref[...]
    s = c * jax.nn.sigmoid(c)
    o_ref[0] = _dot(s.astype(BF16), w_ref[0].astype(BF16)) + b_ref[0]


def _modulation(cc, w_mod, b_mod):
    depth, d, n = w_mod.shape
    tn = 1536
    return pl.pallas_call(
        _mod_kernel,
        out_shape=jax.ShapeDtypeStruct((depth, cc.shape[0], n), F32),
        grid=(depth, n // tn),
        in_specs=[pl.BlockSpec(cc.shape, lambda i, j: (0, 0)),
                  pl.BlockSpec((1, d, tn), lambda i, j: (i, 0, j)),
                  pl.BlockSpec((1, 1, tn), lambda i, j: (i, 0, j))],
        out_specs=pl.BlockSpec((1, cc.shape[0], tn), lambda i, j: (i, 0, j)),
        compiler_params=_cparams(2),
        name="modulation",
    )(cc, w_mod, b_mod.reshape(depth, 1, n))


def _norm_modulate(x, gain, shift, scale):
    ms = jnp.mean(x * x, axis=-1, keepdims=True)
    h = x * lax.rsqrt(ms + EPS) * gain
    return h * (1.0 + scale) + shift


def _group_rms(a, g_ref):
    sq = (a * a).astype(BF16)
    n = a.shape[1]
    ms = [_dot(sq[:, lo:min(lo + MXU_DIM, n)], g_ref[:min(MXU_DIM, n - lo), :min(MXU_DIM, n - lo)])
          for lo in range(0, n, MXU_DIM)]
    ms = ms[0] if len(ms) == 1 else jnp.concatenate(ms, axis=1)
    return a * lax.rsqrt(ms + EPS)


def _rope(y, cos, sin_signed, half):
    n = y.shape[1]
    lane = lax.broadcasted_iota(jnp.int32, y.shape, 1)
    first = (lane % (2 * half)) < half
    partner = jnp.where(first, pltpu.roll(y, n - half, 1), pltpu.roll(y, half, 1))
    return y * cos + partner * sin_signed


def _inproj_kernel(x_ref, sh_ref, sc_ref, ng_ref, w_ref, vone_ref,
                   cosa_ref, sina_ref, cosb_ref, sinb_ref,
                   gqa_ref, gka_ref, gqb_ref, gkb_ref, g32_ref, g64_ref,
                   qa_ref, ka_ref, va_ref, qb_ref, kb_ref, vb_ref, cx_ref, cg_ref):
    slab = va_ref.shape[3]
    sub = min(IN_SUB, slab)
    n_sub = x_ref.shape[1] // sub

    def project(u):
        rows = slice(u * sub, (u + 1) * sub)
        h = _norm_modulate(x_ref[0, rows], ng_ref[...], sh_ref[0], sc_ref[0]).astype(BF16)
        return _dot(h, w_ref[...])

    def finish(u, full):
        rows = slice(u * sub, (u + 1) * sub)

        def proj(lo, hi):
            return full[:, lo:hi]

        cosa, sina = cosa_ref[rows], sina_ref[rows]
        aq = _group_rms(proj(C_AQ, C_AK), g32_ref) * gqa_ref[...]
        qa_ref[0, rows] = _rope(aq, cosa, sina, DA_QK_DIM // 2).astype(BF16)
        ak = _group_rms(proj(C_AK, C_AV), g32_ref) * gka_ref[...]
        ka_ref[0, rows] = _rope(ak, cosa, sina, DA_QK_DIM // 2).astype(BF16)
        va_t = (proj(C_AV, C_BQ) + vone_ref[...]).T.astype(BF16)
        col = (u * sub) % slab
        for hd in range(DA_HEADS):
            va_ref[0, (u * sub) // slab, hd * DA_VROWS:(hd + 1) * DA_VROWS, col:col + sub] = (
                va_t[hd * LANES:hd * LANES + DA_VROWS])

        cosb, sinb = cosb_ref[rows], sinb_ref[rows]
        bq = _group_rms(proj(C_BQ, C_BK), g64_ref) * gqb_ref[...]
        qb_ref[0, rows] = _rope(bq, cosb, sinb, HEAD_DIM // 2).astype(BF16)
        bkv = proj(C_BK, C_CX)
        bk = _group_rms(bkv[:, :SW_KV], g64_ref) * gkb_ref[...]
        kb_ref[0, rows] = _rope(bk, cosb[:, :SW_KV], sinb[:, :SW_KV], HEAD_DIM // 2).astype(BF16)
        vb_ref[0, rows] = bkv[:, SW_KV:].astype(BF16)

        cxg = proj(C_CX, D_INX)
        cx_ref[0, rows] = cxg[:, :LRU_WIDTH]
        cg_ref[0, rows] = cxg[:, LRU_WIDTH:].astype(cg_ref.dtype)

    full = project(0)
    for u in range(n_sub):
        nxt = project(u + 1) if u + 1 < n_sub else None
        finish(u, full)
        full = nxt


def _input_projection(x, shift, scale, lw, rope):
    bsz, t_len, d = x.shape
    tm = min(TM_IN, t_len)
    sub = min(TM, tm)
    per_b = shift.shape[0] > 1
    mod_map = (lambda b, j: (b, 0, 0)) if per_b else (lambda b, j: (0, 0, 0))
    tok = lambda n: pl.BlockSpec((1, tm, n), lambda b, j: (b, j, 0))
    tab = lambda n: pl.BlockSpec((tm, n), lambda b, j: (j, 0))
    widths = (DA_Q, DA_Q, DA_VEXT, SW_Q, SW_KV, SW_KV, LRU_WIDTH, LRU_WIDTH)
    dtypes = (BF16,) * 6 + (F32, BF16)
    out_shape = [jax.ShapeDtypeStruct((bsz, t_len, n), dt) for n, dt in zip(widths, dtypes)]
    out_specs = [tok(n) for n in widths]
    out_shape[2] = jax.ShapeDtypeStruct((bsz, t_len // sub, DA_HEADS * DA_VROWS, sub), BF16)
    out_specs[2] = pl.BlockSpec((1, tm // sub, DA_HEADS * DA_VROWS, sub), lambda b, j: (b, j, 0, 0))
    return pl.pallas_call(
        _inproj_kernel,
        out_shape=tuple(out_shape),
        grid=(bsz, t_len // tm),
        in_specs=[tok(d),
                  pl.BlockSpec((1, 1, d), mod_map), pl.BlockSpec((1, 1, d), mod_map),
                  _const_spec((1, d)), _layer_spec(lw["w_in"].shape, lw["layer"]), _const_spec((1, DA_VEXT)),
                  tab(DA_Q), tab(DA_Q), tab(SW_Q), tab(SW_Q),
                  _const_spec((1, DA_Q)), _const_spec((1, DA_Q)),
                  _const_spec((1, SW_Q)), _const_spec((1, SW_KV)),
                  _const_spec((MXU_DIM, MXU_DIM)), _const_spec((MXU_DIM, MXU_DIM))],
        out_specs=tuple(out_specs),
        compiler_params=_cparams(2),
        name="input_projection",
    )(x, shift, scale, lw["norm1_gain"], lw["w_in"], lw["v_one"], *rope,
      lw["gq_a"], lw["gk_a"], lw["gq_b"], lw["gk_b"], lw["g32"], lw["g64"])


def _da_kernel(*refs, n_seg, lambda_init):
    q_ref = refs[0]
    kv_refs = refs[1:1 + 2 * n_seg]
    hmask_ref, lamv_ref, sg_ref, o_ref, m_ref, acc_ref = refs[1 + 2 * n_seg:]
    n_map = 2 * DA_HEADS

    m_ref[...] = jnp.full(m_ref.shape, NEG_INF, F32)
    acc_ref[...] = jnp.zeros(acc_ref.shape, F32)

    items = [(kv_refs[2 * si], kv_refs[2 * si + 1], i, j)
             for si in range(n_seg) for i in range(kv_refs[2 * si + 1].shape[1])
             for j in range(n_map)]

    def scores(k_ref, v_ref, i, j):
        half, tk = j // 4, v_ref.shape[3]
        qm = q_ref[0, :, half * LANES:(half + 1) * LANES] * hmask_ref[j:j + 1, :]
        kblk = k_ref[0, i * tk:(i + 1) * tk, half * LANES:(half + 1) * LANES]
        return _nt_dot(kblk, qm)

    pending = {n: scores(*items[n]) for n in range(min(DA_AHEAD, len(items)))}
    for n, (k_ref, v_ref, i, j) in enumerate(items):
        s_b = pending.pop(n).astype(BF16)
        if n + DA_AHEAD < len(items):
            pending[n + DA_AHEAD] = scores(*items[n + DA_AHEAD])
        m_old = m_ref[j, 0:1, :]
        m_new = jnp.maximum(m_old, jnp.max(s_b, axis=0, keepdims=True).astype(F32))
        p_t = jnp.exp2(s_b - m_new.astype(BF16))
        pv = _dot(v_ref[0, i, (j // 2) * DA_VROWS:(j // 2 + 1) * DA_VROWS, :], p_t)
        acc_ref[j] = jnp.exp2(m_old - m_new) * acc_ref[j] + pv
        m_ref[j] = jnp.broadcast_to(m_new, m_ref.shape[1:])

    lv = lamv_ref[...]
    lam = (jnp.exp(jnp.sum(lv[0:1] * lv[1:2], axis=1, keepdims=True))
           - jnp.exp(jnp.sum(lv[2:3] * lv[3:4], axis=1, keepdims=True)) + lambda_init)

    def head_out(h):
        a1, a2 = acc_ref[2 * h], acc_ref[2 * h + 1]
        o = (a1 / a1[DA_V_DIM:DA_V_DIM + 1] - lam * (a2 / a2[DA_V_DIM:DA_V_DIM + 1]))[:DA_V_DIM]
        ms = jnp.sum(o * o, axis=0, keepdims=True) * (1.0 / DA_V_DIM)
        return o * lax.rsqrt(ms + EPS)

    for c in range(DA_HEADS // 2):
        pair_t = jnp.concatenate([head_out(2 * c), head_out(2 * c + 1)], axis=0)
        o_ref[0, :, c * LANES:(c + 1) * LANES] = (pair_t.T * sg_ref[...]).astype(o_ref.dtype)


def _diff_attention(q, kvs, lw, lambda_init):
    bsz, t_q, _ = q.shape
    tq = min(DA_TQ, t_q)
    kv_specs, kv_args = [], []
    for k, v in kvs:
        kv_specs += [pl.BlockSpec((1,) + k.shape[1:], lambda b, j: (b, 0, 0)),
                     pl.BlockSpec((1,) + v.shape[1:], lambda b, j: (b, 0, 0, 0))]
        kv_args += [k, v]
    sg = lw["da_sg"] * (1.0 - lambda_init)
    return pl.pallas_call(
        functools.partial(_da_kernel, n_seg=len(kvs), lambda_init=lambda_init),
        out_shape=jax.ShapeDtypeStruct((bsz, t_q, DA_V), BF16),
        grid=(bsz, t_q // tq),
        in_specs=[pl.BlockSpec((1, tq, DA_Q), lambda b, j: (b, j, 0))] + kv_specs
                 + [_const_spec((2 * DA_HEADS, LANES)), _const_spec((4, DA_QK_DIM)),
                    _const_spec((1, LANES))],
        out_specs=pl.BlockSpec((1, tq, DA_V), lambda b, j: (b, j, 0)),
        scratch_shapes=[pltpu.VMEM((2 * DA_HEADS, SUBLANES, tq), F32),
                        pltpu.VMEM((2 * DA_HEADS, DA_VROWS, tq), F32)],
        compiler_params=_cparams(2),
        name="diff_attention",
    )(q, *kv_args, lw["da_hmask"], lw["da_lamv"], sg)


def _sw_kernel(*refs, has_lat, tq, t_len):
    if has_lat:
        sink_ref, q_ref, kc_ref, vc_ref, k_ref, v_ref, hm_ref, o_ref = refs
    else:
        sink_ref, q_ref, kc_ref, vc_ref, hm_ref, o_ref = refs
    def values_t(v):
        n_k = v.shape[0]
        v_t = v.astype(F32).T
        tail = (lax.broadcasted_iota(jnp.int32, (BF16_ROWS, n_k), 0) == 0).astype(F32)
        return [jnp.concatenate([v_t[kv * HEAD_DIM:(kv + 1) * HEAD_DIM], tail], axis=0).astype(BF16)
                for kv in range(SW_KV_HEADS)]

    kc = kc_ref[0]
    vc_t = values_t(vc_ref[0])
    n_sub = q_ref.shape[1] // tq
    bands = []
    for u in range(n_sub if has_lat else 0):
        band = tq + 2 * WINDOW
        q0 = (pl.program_id(1) * n_sub + u) * tq
        start = pl.multiple_of(jnp.clip(q0 - WINDOW, 0, t_len - band), WINDOW)
        qpos = q0 + lax.broadcasted_iota(jnp.int32, (1, tq), 1)
        kpos = start + lax.broadcasted_iota(jnp.int32, (band, 1), 0)
        bands.append((k_ref[0, pl.ds(start, band), :], values_t(v_ref[0, pl.ds(start, band), :]),
                      jnp.abs(qpos - kpos) <= WINDOW))

    combos = [(u, g, kv) for u in range(n_sub) for g in range(SW_GROUP)
              for kv in range(SW_KV_HEADS)]

    def scores(c):
        u, g, kv = combos[c]
        qm = q_ref[0, u * tq:(u + 1) * tq, g * LANES:(g + 1) * LANES] * hm_ref[kv:kv + 1, :]
        s_lat = _nt_dot(bands[u][0], qm) if has_lat else None
        return _nt_dot(kc, qm), s_lat

    pending = {c: scores(c) for c in range(SW_AHEAD)}
    outs = {}
    for c, (u, g, kv) in enumerate(combos):
        s_ctx, s_lat = pending.pop(c)
        if c + SW_AHEAD < len(combos):
            pending[c + SW_AHEAD] = scores(c + SW_AHEAD)
        sink = sink_ref[kv * SW_GROUP + g]
        s_ctx = s_ctx.astype(BF16)
        m = jnp.maximum(jnp.max(s_ctx, axis=0, keepdims=True).astype(F32), sink)
        if has_lat:
            s_lat = jnp.where(bands[u][2], s_lat.astype(BF16), NEG_INF)
            m = jnp.maximum(m, jnp.max(s_lat, axis=0, keepdims=True).astype(F32))
        m_b = m.astype(BF16)
        m = m_b.astype(F32)
        acc = _dot(vc_t[kv], jnp.exp2(s_ctx - m_b))
        if has_lat:
            acc = acc + _dot(bands[u][1][kv], jnp.exp2(s_lat - m_b))
        l = acc[HEAD_DIM:HEAD_DIM + 1] + jnp.exp2(sink - m)
        outs[kv] = acc[:HEAD_DIM] / l
        if kv == SW_KV_HEADS - 1:
            pair_t = jnp.concatenate([outs.pop(k2) for k2 in range(SW_KV_HEADS)], axis=0)
            o_ref[0, u * tq:(u + 1) * tq, g * LANES:(g + 1) * LANES] = pair_t.T.astype(o_ref.dtype)


def _window_attention(q, kc, vc, k, v, lw):
    bsz, t_q, _ = q.shape
    has_lat = k is not None
    tq = min(SW_TQ, t_q)
    tb = min(SW_SUB * tq, t_q)
    n_ctx = kc.shape[1]
    full = lambda t: pl.BlockSpec((1, t, SW_KV), lambda b, j: (b, 0, 0))
    specs = [pl.BlockSpec(memory_space=pltpu.SMEM),
             pl.BlockSpec((1, tb, SW_Q), lambda b, j: (b, j, 0)), full(n_ctx), full(n_ctx)]
    args = [lw["sw_sink"], q, kc, vc]
    if has_lat:
        specs += [full(t_q), full(t_q)]
        args += [k, v]
    specs.append(_const_spec((SW_KV_HEADS, LANES)))
    args.append(lw["sw_hmask"])
    return pl.pallas_call(
        functools.partial(_sw_kernel, has_lat=has_lat, tq=tq, t_len=t_q),
        out_shape=jax.ShapeDtypeStruct((bsz, t_q, SW_Q), BF16),
        grid=(bsz, t_q // tb),
        in_specs=specs,
        out_specs=pl.BlockSpec((1, tb, SW_Q), lambda b, j: (b, j, 0)),
        compiler_params=_cparams(2),
        name="window_attention",
    )(*args)


def _lru_kernel(xf_ref, xb_ref, xc_ref, cw_ref, cb_ref, wa_ref, ba_ref, wx_ref, bx_ref, lam_ref,
                hf_ref, hb_ref, hfc_ref, hbc_ref, halo_ref, carry_ref, *, tc):
    j = pl.program_id(1)
    is_ctx = j == 0
    sub = lax.broadcasted_iota(jnp.int32, (1, SUBLANES, 1), 1)

    for d, x_lat_ref, out_ref, outc_ref in ((0, xf_ref, hf_ref, hfc_ref), (1, xb_ref, hb_ref, hbc_ref)):
        x = jnp.where(is_ctx, xc_ref[0], x_lat_ref[0])
        halo = jnp.where(j <= 1, 0.0, halo_ref[d])
        cw = cw_ref[d]
        n_grp = tc // SUBLANES
        x3 = x.reshape(n_grp, SUBLANES, x.shape[1])
        xc3 = cb_ref[d] + x3 * cw[LRU_CONV - 1:LRU_CONV]
        for k in range(LRU_CONV - 1):
            s = LRU_CONV - 1 - k
            if d == 0:
                rolled = pltpu.roll(x3, s, 1)
                other = jnp.concatenate([pltpu.roll(halo, s, 0)[None], rolled[:-1]], axis=0)
                xs = jnp.where(sub < s, other, rolled)
            else:
                rolled = pltpu.roll(x3, SUBLANES - s, 1)
                other = jnp.concatenate([rolled[1:], pltpu.roll(halo, SUBLANES - s, 0)[None]], axis=0)
                xs = jnp.where(sub >= SUBLANES - s, other, rolled)
            xc3 = xc3 + xs * cw[k:k + 1]
        xc = xc3.reshape(tc, x.shape[1])
        halo_ref[d] = x[tc - SUBLANES:] if d == 0 else x[:SUBLANES]

        xcb = xc.astype(BF16)
        r = jax.nn.sigmoid(_dot(xcb, wa_ref[d]) + ba_ref[d])
        gi = jax.nn.sigmoid(_dot(xcb, wx_ref[d]) + bx_ref[d])
        nl = -lam_ref[d]
        softplus = jnp.maximum(nl, 0.0) + jnp.log1p(jnp.exp(-jnp.abs(nl)))
        log_a = -LRU_C * r * softplus
        a = jnp.exp(log_a)
        b = jnp.sqrt(-jnp.tanh(log_a) * (a * a + 1.0)) * (gi * xc)

        a3 = a.reshape(n_grp, SUBLANES, a.shape[1])
        b3 = b.reshape(n_grp, SUBLANES, b.shape[1])
        step = 1
        while step < SUBLANES:
            keep = (sub < step) if d == 0 else (sub >= SUBLANES - step)
            shift = step if d == 0 else SUBLANES - step
            a_s = jnp.where(keep, 1.0, pltpu.roll(a3, shift, 1))
            b_s = jnp.where(keep, 0.0, pltpu.roll(b3, shift, 1))
            b3 = a3 * b_s + b3
            a3 = a3 * a_s
            step *= 2

        edge_row = SUBLANES - 1 if d == 0 else 0
        h_in = jnp.where(is_ctx, 0.0, carry_ref[d, edge_row:edge_row + 1, :])
        groups = [None] * n_grp
        for gi in (range(n_grp) if d == 0 else range(n_grp - 1, -1, -1)):
            groups[gi] = a3[gi] * h_in + b3[gi]
            h_in = groups[gi][edge_row:edge_row + 1]
        h = jnp.concatenate(groups, axis=0)
        carry_ref[d] = groups[n_grp - 1] if d == 0 else groups[0]

        h_out = h.astype(out_ref.dtype)

        @pl.when(is_ctx)
        def _():
            outc_ref[0] = h_out

        @pl.when(jnp.logical_not(is_ctx))
        def _():
            out_ref[0] = h_out


def _rglru_scans(cx, cxc, lw):
    bsz, t_len, n = cx.shape
    tc = LRU_TC
    assert cxc.shape[1] == tc and t_len % tc == 0
    n_lat = t_len // tc
    fwd = lambda b, j: (b, jnp.maximum(j - 1, 0), 0)
    bwd = lambda b, j: (b, n_lat - jnp.maximum(j, 1), 0)
    ctx = lambda b, j: (b, 0, 0)
    blk = lambda m: pl.BlockSpec((1, tc, n), m)
    return pl.pallas_call(
        functools.partial(_lru_kernel, tc=tc),
        out_shape=(jax.ShapeDtypeStruct(cx.shape, BF16), jax.ShapeDtypeStruct(cx.shape, BF16),
                   jax.ShapeDtypeStruct(cxc.shape, BF16), jax.ShapeDtypeStruct(cxc.shape, BF16)),
        grid=(bsz, n_lat + 1),
        in_specs=[blk(fwd), blk(bwd), blk(ctx),
                  _const_spec((2, LRU_CONV, n)), _const_spec((2, 1, n)),
                  _const_spec((2, n, n)), _const_spec((2, 1, n)),
                  _const_spec((2, n, n)), _const_spec((2, 1, n)), _const_spec((2, 1, n))],
        out_specs=(blk(fwd), blk(bwd), blk(ctx), blk(ctx)),
        scratch_shapes=[pltpu.VMEM((2, SUBLANES, n), F32), pltpu.VMEM((2, SUBLANES, n), F32)],
        compiler_params=_cparams(2),
        name="rglru_scans",
    )(cx, cx, cxc, lw["lru_conv_w"], lw["lru_conv_b"], lw["lru_wa"], lw["lru_ba"],
      lw["lru_wx"], lw["lru_bx"], lw["lru_lambda"])


def _gelu_tanh(x):
    return 0.5 * x * (1.0 + jnp.tanh(math.sqrt(2.0 / math.pi) * (x + 0.044715 * (x * x * x))))


def _outproj_kernel(x_ref, ya_ref, yb_ref, hf_ref, hb_ref, cg_ref, g1_ref, w_ref, o_ref, cat_ref):
    cat_ref[:, 0:DA_V] = ya_ref[0]
    cat_ref[:, DA_V:DA_V + SW_Q] = yb_ref[0]
    yc = (hf_ref[0].astype(F32) + hb_ref[0].astype(F32)) * _gelu_tanh(cg_ref[0].astype(F32))
    cat_ref[:, DA_V + SW_Q:] = yc.astype(BF16)
    o_ref[0] = x_ref[0] + g1_ref[0] * _dot(cat_ref[...], w_ref[...])


def _output_projection(x, ya, yb, hf, hb, cg, gate, lw):
    bsz, t_len, d = x.shape
    tm = min(TM_OUT, t_len)
    per_b = gate.shape[0] > 1
    mod_map = (lambda b, j: (b, 0, 0)) if per_b else (lambda b, j: (0, 0, 0))
    tok = lambda n: pl.BlockSpec((1, tm, n), lambda b, j: (b, j, 0))
    return pl.pallas_call(
        _outproj_kernel,
        out_shape=jax.ShapeDtypeStruct(x.shape, F32),
        grid=(bsz, t_len // tm),
        in_specs=[tok(d), tok(DA_V), tok(SW_Q), tok(LRU_WIDTH), tok(LRU_WIDTH), tok(LRU_WIDTH),
                  pl.BlockSpec((1, 1, d), mod_map), _layer_spec(lw["w_out"].shape, lw["layer"])],
        out_specs=tok(d),
        scratch_shapes=[pltpu.VMEM((tm, d), BF16)],
        compiler_params=_cparams(2),
        name="output_projection",
    )(x, ya, yb, hf, hb, cg, gate, lw["w_out"])


def _ffn_kernel(xm_ref, xp_ref, xn_ref, sh_ref, sc_ref, g2_ref, ng_ref, wup_ref, cw_ref, cb_ref,
                wdn_ref, o_ref, h_ref, *, tm):
    j = pl.program_id(1)
    last = pl.num_programs(1) - 1
    gain, shift, scale = ng_ref[...], sh_ref[0], sc_ref[0]
    xm = xm_ref[0]
    hp = jnp.where(j > 0, _norm_modulate(xp_ref[0], gain, shift, scale), 0.0)
    hn = jnp.where(j < last, _norm_modulate(xn_ref[0], gain, shift, scale), 0.0)
    h_ref[0:FFN_HALO] = hp.astype(BF16)
    h_ref[FFN_HALO:FFN_HALO + tm] = _norm_modulate(xm, gain, shift, scale).astype(BF16)
    h_ref[FFN_HALO + tm:] = hn.astype(BF16)

    rows = tm + 2 * FFN_HALO
    acc = jnp.zeros((tm, D_MODEL), F32)
    for lo, hi in FFN_CHUNKS:
        gp = _dot(h_ref[...], wup_ref[:, lo:hi])
        g_prev = pltpu.roll(gp, 1, 0)[FFN_HALO:FFN_HALO + tm]
        g_next = pltpu.roll(gp, rows - 1, 0)[FFN_HALO:FFN_HALO + tm]
        cw = cw_ref[:, lo:hi]
        gate = (cb_ref[:, lo:hi] + g_prev * cw[0:1] + gp[FFN_HALO:FFN_HALO + tm] * cw[1:2]
                + g_next * cw[2:3])
        val = _dot(h_ref[FFN_HALO:FFN_HALO + tm], wup_ref[:, D_FF + lo:D_FF + hi])
        act = (gate * jax.nn.sigmoid(gate) * val).astype(BF16)
        acc = acc + _dot(act, wdn_ref[lo:hi, :])
    o_ref[0] = xm + g2_ref[0] * acc


def _conv_ffn(x, shift, scale, gate, lw):
    bsz, t_len, d = x.shape
    tm = min(TM, t_len)
    per_b = gate.shape[0] > 1
    mod_map = (lambda b, j: (b, 0, 0)) if per_b else (lambda b, j: (0, 0, 0))
    hb = tm // FFN_HALO
    n_hb = t_len // FFN_HALO
    prev_map = lambda b, j: (b, jnp.maximum(j * hb - 1, 0), 0)
    next_map = lambda b, j: (b, jnp.minimum((j + 1) * hb, n_hb - 1), 0)
    mod_spec = pl.BlockSpec((1, 1, d), mod_map)
    return pl.pallas_call(
        functools.partial(_ffn_kernel, tm=tm),
        out_shape=jax.ShapeDtypeStruct(x.shape, F32),
        grid=(bsz, t_len // tm),
        in_specs=[pl.BlockSpec((1, tm, d), lambda b, j: (b, j, 0)),
                  pl.BlockSpec((1, FFN_HALO, d), prev_map),
                  pl.BlockSpec((1, FFN_HALO, d), next_map),
                  mod_spec, mod_spec, mod_spec,
                  _const_spec((1, d)), _layer_spec(lw["w_up"].shape, lw["layer"]),
                  _const_spec((FFN_CONV, D_FF)), _const_spec((1, D_FF)),
                  _layer_spec(lw["w_down"].shape, lw["layer"])],
        out_specs=pl.BlockSpec((1, tm, d), lambda b, j: (b, j, 0)),
        scratch_shapes=[pltpu.VMEM((tm + 2 * FFN_HALO, d), BF16)],
        compiler_params=_cparams(2),
        name="conv_ffn",
    )(x, x, x, shift, scale, gate, lw["norm2_gain"], lw["w_up"], lw["ffn_conv_w"],
      lw["ffn_conv_b"], lw["w_down"])


def _sw_head_order():
    return [kv * SW_GROUP + g for g in range(SW_GROUP) for kv in range(SW_KV_HEADS)]


def _rope_tables(rows, head_dim, n_heads):
    row = np.repeat(np.arange(rows, dtype=np.float64), GRID_W)
    col = np.tile(np.arange(GRID_W, dtype=np.float64), rows)
    quarter = head_dim // 4
    inv_freq = ROPE_BASE ** (-np.arange(quarter, dtype=np.float64) / quarter)
    ang = np.concatenate([row[:, None] * inv_freq, col[:, None] * inv_freq], axis=-1)
    cos, sin = np.cos(ang), np.sin(ang)
    return (jnp.asarray(np.tile(np.concatenate([cos, cos], axis=-1), (1, n_heads)), dtype=F32),
            jnp.asarray(np.tile(np.concatenate([-sin, sin], axis=-1), (1, n_heads)), dtype=F32))


def _block_diag_mean(n, group):
    idx = np.arange(n) // group
    return jnp.asarray((idx[:, None] == idx[None, :]).astype(np.float32) / group, dtype=BF16)


def _block_diag(w):
    two, nb, bi, bj = w.shape
    eye = jnp.eye(nb, dtype=w.dtype)
    return jnp.einsum("dhij,hg->dhigj", w, eye).reshape(two, nb * bi, nb * bj)


def _prepare_weights(p):
    d = D_MODEL
    depth = p["w_in"].shape[0]
    split = np.cumsum((DA_Q, DA_Q, DA_V, SW_Q, SW_KV, SW_KV, LRU_WIDTH))
    w_aq, w_ak, w_av, w_bq, w_bk, w_bv, w_cx, w_cg = jnp.split(p["w_in"], split.tolist(), axis=2)
    w_av = jnp.pad(w_av.reshape(depth, d, DA_HEADS, DA_V_DIM),
                   ((0, 0), (0, 0), (0, 0), (0, LANES - DA_V_DIM))).reshape(depth, d, DA_VEXT)
    order = jnp.asarray(_sw_head_order())
    w_bq = w_bq.reshape(depth, d, SW_HEADS, HEAD_DIM)[:, :, order].reshape(depth, d, SW_Q)
    w_in = jnp.concatenate([w_aq, w_ak, w_av, w_bq, w_bk, w_bv, w_cx, w_cg], axis=2)

    w_out = p["w_out"]
    w_ob = w_out[:, DA_V:DA_V + SW_Q].reshape(depth, SW_HEADS, HEAD_DIM, d)[:, order]
    w_out = jnp.concatenate([w_out[:, :DA_V], w_ob.reshape(depth, SW_Q, d), w_out[:, DA_V + SW_Q:]],
                            axis=1)
    return dict(w_in=w_in.astype(BF16), w_out=w_out.astype(BF16),
                w_up=p["w_up"].astype(BF16), w_down=p["w_down"].astype(BF16))


def _prepare_layer(p, i):
    d = D_MODEL
    v_one = np.zeros((1, DA_VEXT), np.float32)
    v_one[0, DA_V_DIM::LANES] = 1.0

    da_hmask = np.zeros((2 * DA_HEADS, LANES), np.float32)
    for j in range(2 * DA_HEADS):
        off = (j % 4) * DA_QK_DIM
        da_hmask[j, off:off + DA_QK_DIM] = 1.0
    sw_hmask = np.zeros((SW_KV_HEADS, LANES), np.float32)
    for kv in range(SW_KV_HEADS):
        sw_hmask[kv, kv * HEAD_DIM:(kv + 1) * HEAD_DIM] = 1.0

    return dict(
        norm1_gain=p["norm1_gain"][i].reshape(1, d), norm2_gain=p["norm2_gain"][i].reshape(1, d),
        layer=i, v_one=jnp.asarray(v_one),
        gq_a=(jnp.tile(p["da_q_gain"][i], 2 * DA_HEADS).reshape(1, DA_Q)
              * (DA_QK_DIM ** -0.5 * math.log2(math.e))),
        gk_a=jnp.tile(p["da_k_gain"][i], 2 * DA_HEADS).reshape(1, DA_Q),
        gq_b=(jnp.tile(p["sw_q_gain"][i], SW_HEADS).reshape(1, SW_Q)
              * (HEAD_DIM ** -0.5 * math.log2(math.e))),
        gk_b=jnp.tile(p["sw_k_gain"][i], SW_KV_HEADS).reshape(1, SW_KV),
        g32=_block_diag_mean(MXU_DIM, DA_QK_DIM), g64=_block_diag_mean(MXU_DIM, HEAD_DIM),
        da_hmask=jnp.asarray(da_hmask, dtype=BF16),
        da_lamv=jnp.stack([p["da_lam_q1"][i], p["da_lam_k1"][i], p["da_lam_q2"][i], p["da_lam_k2"][i]]),
        da_sg=jnp.tile(p["da_sub_gain"][i], LANES // DA_V_DIM).reshape(1, LANES),
        sw_sink=p["sw_sink"][i] * math.log2(math.e), sw_hmask=jnp.asarray(sw_hmask, dtype=BF16),
        lru_conv_w=p["lru_conv_w"][i], lru_conv_b=p["lru_conv_b"][i].reshape(2, 1, LRU_WIDTH),
        lru_wa=_block_diag(p["lru_wa"][i]).astype(BF16), lru_ba=p["lru_ba"][i].reshape(2, 1, LRU_WIDTH),
        lru_wx=_block_diag(p["lru_wx"][i]).astype(BF16), lru_bx=p["lru_bx"][i].reshape(2, 1, LRU_WIDTH),
        lru_lambda=p["lru_lambda"][i].reshape(2, 1, LRU_WIDTH),
        ffn_conv_w=p["ffn_conv_w"][i], ffn_conv_b=p["ffn_conv_b"][i].reshape(1, D_FF),
    )


def kernel(x, c, ctx, c_ctx, w_mod, b_mod, norm1_gain, norm2_gain, w_in, da_q_gain, da_k_gain, da_lam_q1, da_lam_k1, da_lam_q2, da_lam_k2, da_sub_gain, sw_q_gain, sw_k_gain, sw_sink, lru_conv_w, lru_conv_b, lru_wa, lru_ba, lru_wx, lru_bx, lru_lambda, w_out, w_up, ffn_conv_w, ffn_conv_b, w_down):
    p = dict(w_in=w_in, norm1_gain=norm1_gain, norm2_gain=norm2_gain, da_q_gain=da_q_gain,
             da_k_gain=da_k_gain, da_lam_q1=da_lam_q1, da_lam_k1=da_lam_k1, da_lam_q2=da_lam_q2,
             da_lam_k2=da_lam_k2, da_sub_gain=da_sub_gain, sw_q_gain=sw_q_gain, sw_k_gain=sw_k_gain,
             sw_sink=sw_sink, lru_conv_w=lru_conv_w, lru_conv_b=lru_conv_b, lru_wa=lru_wa,
             lru_ba=lru_ba, lru_wx=lru_wx, lru_bx=lru_bx, lru_lambda=lru_lambda, w_out=w_out,
             w_up=w_up, ffn_conv_w=ffn_conv_w, ffn_conv_b=ffn_conv_b, w_down=w_down)
    bsz, n_tok, d = x.shape
    n_ctx = ctx.shape[1]
    depth = w_mod.shape[0]

    cc = jnp.zeros((2 * SUBLANES, d), F32).at[:bsz].set(c).at[bsz].set(c_ctx)
    mod_all = _modulation(cc, w_mod, b_mod)

    rope_lat = (_rope_tables(n_tok // GRID_W, DA_QK_DIM, 2 * DA_HEADS)
                + _rope_tables(n_tok // GRID_W, HEAD_DIM, SW_HEADS))
    rope_ctx = (jnp.ones((n_ctx, DA_Q), F32), jnp.zeros((n_ctx, DA_Q), F32),
                jnp.ones((n_ctx, SW_Q), F32), jnp.zeros((n_ctx, SW_Q), F32))

    weights = _prepare_weights(p)
    xc = ctx
    for i in range(depth):
        lw = {**_prepare_layer(p, i), **weights}
        lambda_init = 0.8 - 0.6 * math.exp(-0.3 * i)
        ctx_out = i < depth - 1
        mod = mod_all[i, :bsz].reshape(bsz, 1, N_MOD, d)
        sh1, sc1, g1, sh2, sc2, g2 = [mod[:, :, m] for m in range(N_MOD)]
        mod_c = mod_all[i, bsz].reshape(1, 1, N_MOD, d)
        sh1c, sc1c, g1c, sh2c, sc2c, g2c = [mod_c[:, :, m] for m in range(N_MOD)]

        qa, ka, va, qb, kb, vb, cx, cg = _input_projection(x, sh1, sc1, lw, rope_lat)
        qac, kac, vac, qbc, kbc, vbc, cxc, cgc = _input_projection(xc, sh1c, sc1c, lw, rope_ctx)

        ya = _diff_attention(qa, [(kac, vac), (ka, va)], lw, lambda_init)
        yb = _window_attention(qb, kbc, vbc, kb, vb, lw)
        hf, hb, hfc, hbc = _rglru_scans(cx, cxc, lw)

        x = _output_projection(x, ya, yb, hf, hb, cg, g1, lw)
        x = _conv_ffn(x, sh2, sc2, g2, lw)
        if ctx_out:
            yac = _diff_attention(qac, [(kac, vac)], lw, lambda_init)
            ybc = _window_attention(qbc, kbc, vbc, None, None, lw)
            xc = _output_projection(xc, yac, ybc, hfc, hbc, cgc, g1c, lw)
            xc = _conv_ffn(xc, sh2c, sc2c, g2c, lw)
    return x
```

```python
import functools
import math

import numpy as np
import jax
import jax.numpy as jnp
from jax import lax
from jax.experimental import pallas as pl
from jax.experimental.pallas import tpu as pltpu

F32 = jnp.float32
BF16 = jnp.bfloat16

D_MODEL = 1024
GRID_W = 64
N_MOD = 6
EPS = 1e-6
NEG_INF = -1e30
ROPE_BASE = 10000.0
DA_HEADS = 4
DA_QK_DIM = 32
DA_V_DIM = 64
HEAD_DIM = 64
SW_HEADS = 6
SW_KV_HEADS = 2
SW_GROUP = SW_HEADS // SW_KV_HEADS
WINDOW = 128
LRU_WIDTH = 384
LRU_BLOCKS = 6
LRU_BLOCK_DIM = LRU_WIDTH // LRU_BLOCKS
LRU_CONV = 4
LRU_C = 8.0
D_FF = 2816
FFN_CONV = 3

DA_Q = DA_HEADS * 2 * DA_QK_DIM
DA_V = DA_HEADS * DA_V_DIM
SW_Q = SW_HEADS * HEAD_DIM
SW_KV = SW_KV_HEADS * HEAD_DIM

LANES = 128
SUBLANES = 8
BF16_ROWS = 16
MXU_DIM = 256
VMEM_LIMIT = 56 * 1024 * 1024

DA_VEXT = DA_HEADS * LANES
DA_VROWS = DA_V_DIM + BF16_ROWS
C_AQ = 0
C_AK = C_AQ + DA_Q
C_AV = C_AK + DA_Q
C_BQ = C_AV + DA_VEXT
C_BK = C_BQ + SW_Q
C_BV = C_BK + SW_KV
C_CX = C_BV + SW_KV
C_CG = C_CX + LRU_WIDTH
D_INX = C_CG + LRU_WIDTH

TM = 512
FFN_HALO = BF16_ROWS
FFN_CHUNKS = ((0, 768), (768, 1536), (1536, 2304), (2304, D_FF))
DA_TQ = 256
DA_AHEAD = 3
TM_IN = 1024
IN_SUB = 256
TM_OUT = 1024
SW_AHEAD = 2
SW_TQ = 256
SW_SUB = 4
LRU_TC = 256


def _cparams(n_axes):
    return pltpu.CompilerParams(dimension_semantics=("arbitrary",) * n_axes,
                                vmem_limit_bytes=VMEM_LIMIT)


def _const_spec(shape):
    nd = len(shape)
    return pl.BlockSpec(shape, lambda *_: (0,) * nd, pipeline_mode=pl.Buffered(1))


def _layer_spec(shape, layer):
    nd = len(shape)
    return pl.BlockSpec((None,) + tuple(shape[1:]), lambda *_: (layer,) + (0,) * (nd - 1),
                        pipeline_mode=pl.Buffered(1))


def _nt_dot(a, b):
    return lax.dot_general(a, b, (((1,), (1,)), ((), ())), preferred_element_type=F32)


def _dot(a, b):
    return jnp.dot(a, b, preferred_element_type=F32)


def _mod_kernel(c_ref, w_ref, b_ref, o_ref):
    c = c_ref[...]
    s = c * jax.nn.sigmoid(c)
    o_ref[0] = _dot(s.astype(BF16), w_ref[0].astype(BF16)) + b_ref[0]


def _modulation(cc, w_mod, b_mod):
    depth, d, n = w_mod.shape
    tn = 1536
    return pl.pallas_call(
        _mod_kernel,
        out_shape=jax.ShapeDtypeStruct((depth, cc.shape[0], n), F32),
        grid=(depth, n // tn),
        in_specs=[pl.BlockSpec(cc.shape, lambda i, j: (0, 0)),
                  pl.BlockSpec((1, d, tn), lambda i, j: (i, 0, j)),
                  pl.BlockSpec((1, 1, tn), lambda i, j: (i, 0, j))],
        out_specs=pl.BlockSpec((1, cc.shape[0], tn), lambda i, j: (i, 0, j)),
        compiler_params=_cparams(2),
        name="modulation",
    )(cc, w_mod, b_mod.reshape(depth, 1, n))


def _norm_modulate(x, gain, shift, scale):
    ms = jnp.mean(x * x, axis=-1, keepdims=True)
    h = x * lax.rsqrt(ms + EPS) * gain
    return h * (1.0 + scale) + shift


def _group_rms(a, g_ref):
    sq = (a * a).astype(BF16)
    n = a.shape[1]
    ms = [_dot(sq[:, lo:min(lo + MXU_DIM, n)], g_ref[:min(MXU_DIM, n - lo), :min(MXU_DIM, n - lo)])
          for lo in range(0, n, MXU_DIM)]
    ms = ms[0] if len(ms) == 1 else jnp.concatenate(ms, axis=1)
    return a * lax.rsqrt(ms + EPS)


def _rope(y, cos, sin_signed, half):
    n = y.shape[1]
    lane = lax.broadcasted_iota(jnp.int32, y.shape, 1)
    first = (lane % (2 * half)) < half
    partner = jnp.where(first, pltpu.roll(y, n - half, 1), pltpu.roll(y, half, 1))
    return y * cos + partner * sin_signed


def _inproj_kernel(x_ref, sh_ref, sc_ref, ng_ref, w_ref, vone_ref,
                   cosa_ref, sina_ref, cosb_ref, sinb_ref,
                   gqa_ref, gka_ref, gqb_ref, gkb_ref, g32_ref, g64_ref,
                   qa_ref, ka_ref, va_ref, qb_ref, kb_ref, vb_ref, cx_ref, cg_ref):
    slab = va_ref.shape[3]
    sub = min(IN_SUB, slab)
    n_sub = x_ref.shape[1] // sub

    def project(u):
        rows = slice(u * sub, (u + 1) * sub)
        h = _norm_modulate(x_ref[0, rows], ng_ref[...], sh_ref[0], sc_ref[0]).astype(BF16)
        return _dot(h, w_ref[...])

    def finish(u, full):
        rows = slice(u * sub, (u + 1) * sub)

        def proj(lo, hi):
            return full[:, lo:hi]

        cosa, sina = cosa_ref[rows], sina_ref[rows]
        aq = _group_rms(proj(C_AQ, C_AK), g32_ref) * gqa_ref[...]
        qa_ref[0, rows] = _rope(aq, cosa, sina, DA_QK_DIM // 2).astype(BF16)
        ak = _group_rms(proj(C_AK, C_AV), g32_ref) * gka_ref[...]
        ka_ref[0, rows] = _rope(ak, cosa, sina, DA_QK_DIM // 2).astype(BF16)
        va_t = (proj(C_AV, C_BQ) + vone_ref[...]).T.astype(BF16)
        col = (u * sub) % slab
        for hd in range(DA_HEADS):
            va_ref[0, (u * sub) // slab, hd * DA_VROWS:(hd + 1) * DA_VROWS, col:col + sub] = (
                va_t[hd * LANES:hd * LANES + DA_VROWS])

        cosb, sinb = cosb_ref[rows], sinb_ref[rows]
        bq = _group_rms(proj(C_BQ, C_BK), g64_ref) * gqb_ref[...]
        qb_ref[0, rows] = _rope(bq, cosb, sinb, HEAD_DIM // 2).astype(BF16)
        bkv = proj(C_BK, C_CX)
        bk = _group_rms(bkv[:, :SW_KV], g64_ref) * gkb_ref[...]
        kb_ref[0, rows] = _rope(bk, cosb[:, :SW_KV], sinb[:, :SW_KV], HEAD_DIM // 2).astype(BF16)
        vb_ref[0, rows] = bkv[:, SW_KV:].astype(BF16)

        cxg = proj(C_CX, D_INX)
        cx_ref[0, rows] = cxg[:, :LRU_WIDTH]
        cg_ref[0, rows] = cxg[:, LRU_WIDTH:].astype(cg_ref.dtype)

    full = project(0)
    for u in range(n_sub):
        nxt = project(u + 1) if u + 1 < n_sub else None
        finish(u, full)
        full = nxt


def _input_projection(x, shift, scale, lw, rope):
    bsz, t_len, d = x.shape
    tm = min(TM_IN, t_len)
    sub = min(TM, tm)
    per_b = shift.shape[0] > 1
    mod_map = (lambda b, j: (b, 0, 0)) if per_b else (lambda b, j: (0, 0, 0))
    tok = lambda n: pl.BlockSpec((1, tm, n), lambda b, j: (b, j, 0))
    tab = lambda n: pl.BlockSpec((tm, n), lambda b, j: (j, 0))
    widths = (DA_Q, DA_Q, DA_VEXT, SW_Q, SW_KV, SW_KV, LRU_WIDTH, LRU_WIDTH)
    dtypes = (BF16,) * 6 + (F32, BF16)
    out_shape = [jax.ShapeDtypeStruct((bsz, t_len, n), dt) for n, dt in zip(widths, dtypes)]
    out_specs = [tok(n) for n in widths]
    out_shape[2] = jax.ShapeDtypeStruct((bsz, t_len // sub, DA_HEADS * DA_VROWS, sub), BF16)
    out_specs[2] = pl.BlockSpec((1, tm // sub, DA_HEADS * DA_VROWS, sub), lambda b, j: (b, j, 0, 0))
    return pl.pallas_call(
        _inproj_kernel,
        out_shape=tuple(out_shape),
        grid=(bsz, t_len // tm),
        in_specs=[tok(d),
                  pl.BlockSpec((1, 1, d), mod_map), pl.BlockSpec((1, 1, d), mod_map),
                  _const_spec((1, d)), _layer_spec(lw["w_in"].shape, lw["layer"]), _const_spec((1, DA_VEXT)),
                  tab(DA_Q), tab(DA_Q), tab(SW_Q), tab(SW_Q),
                  _const_spec((1, DA_Q)), _const_spec((1, DA_Q)),
                  _const_spec((1, SW_Q)), _const_spec((1, SW_KV)),
                  _const_spec((MXU_DIM, MXU_DIM)), _const_spec((MXU_DIM, MXU_DIM))],
        out_specs=tuple(out_specs),
        compiler_params=_cparams(2),
        name="input_projection",
    )(x, shift, scale, lw["norm1_gain"], lw["w_in"], lw["v_one"], *rope,
      lw["gq_a"], lw["gk_a"], lw["gq_b"], lw["gk_b"], lw["g32"], lw["g64"])


def _da_kernel(*refs, n_seg, lambda_init):
    q_ref = refs[0]
    kv_refs = refs[1:1 + 2 * n_seg]
    hmask_ref, lamv_ref, sg_ref, o_ref, m_ref, acc_ref = refs[1 + 2 * n_seg:]
    n_map = 2 * DA_HEADS

    m_ref[...] = jnp.full(m_ref.shape, NEG_INF, F32)
    acc_ref[...] = jnp.zeros(acc_ref.shape, F32)

    items = [(kv_refs[2 * si], kv_refs[2 * si + 1], i, j)
             for si in range(n_seg) for i in range(kv_refs[2 * si + 1].shape[1])
             for j in range(n_map)]

    def scores(k_ref, v_ref, i, j):
        half, tk = j // 4, v_ref.shape[3]
        qm = q_ref[0, :, half * LANES:(half + 1) * LANES] * hmask_ref[j:j + 1, :]
        kblk = k_ref[0, i * tk:(i + 1) * tk, half * LANES:(half + 1) * LANES]
        return _nt_dot(kblk, qm)

    pending = {n: scores(*items[n]) for n in range(min(DA_AHEAD, len(items)))}
    for n, (k_ref, v_ref, i, j) in enumerate(items):
        s_b = pending.pop(n).astype(BF16)
        if n + DA_AHEAD < len(items):
            pending[n + DA_AHEAD] = scores(*items[n + DA_AHEAD])
        m_old = m_ref[j, 0:1, :]
        m_new = jnp.maximum(m_old, jnp.max(s_b, axis=0, keepdims=True).astype(F32))
        p_t = jnp.exp2(s_b - m_new.astype(BF16))
        pv = _dot(v_ref[0, i, (j // 2) * DA_VROWS:(j // 2 + 1) * DA_VROWS, :], p_t)
        acc_ref[j] = jnp.exp2(m_old - m_new) * acc_ref[j] + pv
        m_ref[j] = jnp.broadcast_to(m_new, m_ref.shape[1:])

    lv = lamv_ref[...]
    lam = (jnp.exp(jnp.sum(lv[0:1] * lv[1:2], axis=1, keepdims=True))
           - jnp.exp(jnp.sum(lv[2:3] * lv[3:4], axis=1, keepdims=True)) + lambda_init)

    def head_out(h):
        a1, a2 = acc_ref[2 * h], acc_ref[2 * h + 1]
        o = (a1 / a1[DA_V_DIM:DA_V_DIM + 1] - lam * (a2 / a2[DA_V_DIM:DA_V_DIM + 1]))[:DA_V_DIM]
        ms = jnp.sum(o * o, axis=0, keepdims=True) * (1.0 / DA_V_DIM)
        return o * lax.rsqrt(ms + EPS)

    for c in range(DA_HEADS // 2):
        pair_t = jnp.concatenate([head_out(2 * c), head_out(2 * c + 1)], axis=0)
        o_ref[0, :, c * LANES:(c + 1) * LANES] = (pair_t.T * sg_ref[...]).astype(o_ref.dtype)


def _diff_attention(q, kvs, lw, lambda_init):
    bsz, t_q, _ = q.shape
    tq = min(DA_TQ, t_q)
    kv_specs, kv_args = [], []
    for k, v in kvs:
        kv_specs += [pl.BlockSpec((1,) + k.shape[1:], lambda b, j: (b, 0, 0)),
                     pl.BlockSpec((1,) + v.shape[1:], lambda b, j: (b, 0, 0, 0))]
        kv_args += [k, v]
    sg = lw["da_sg"] * (1.0 - lambda_init)
    return pl.pallas_call(
        functools.partial(_da_kernel, n_seg=len(kvs), lambda_init=lambda_init),
        out_shape=jax.ShapeDtypeStruct((bsz, t_q, DA_V), BF16),
        grid=(bsz, t_q // tq),
        in_specs=[pl.BlockSpec((1, tq, DA_Q), lambda b, j: (b, j, 0))] + kv_specs
                 + [_const_spec((2 * DA_HEADS, LANES)), _const_spec((4, DA_QK_DIM)),
                    _const_spec((1, LANES))],
        out_specs=pl.BlockSpec((1, tq, DA_V), lambda b, j: (b, j, 0)),
        scratch_shapes=[pltpu.VMEM((2 * DA_HEADS, SUBLANES, tq), F32),
                        pltpu.VMEM((2 * DA_HEADS, DA_VROWS, tq), F32)],
        compiler_params=_cparams(2),
        name="diff_attention",
    )(q, *kv_args, lw["da_hmask"], lw["da_lamv"], sg)


def _sw_kernel(*refs, has_lat, tq, t_len):
    if has_lat:
        sink_ref, q_ref, kc_ref, vc_ref, k_ref, v_ref, hm_ref, o_ref = refs
    else:
        sink_ref, q_ref, kc_ref, vc_ref, hm_ref, o_ref = refs
    def values_t(v):
        n_k = v.shape[0]
        v_t = v.astype(F32).T
        tail = (lax.broadcasted_iota(jnp.int32, (BF16_ROWS, n_k), 0) == 0).astype(F32)
        return [jnp.concatenate([v_t[kv * HEAD_DIM:(kv + 1) * HEAD_DIM], tail], axis=0).astype(BF16)
                for kv in range(SW_KV_HEADS)]

    kc = kc_ref[0]
    vc_t = values_t(vc_ref[0])
    n_sub = q_ref.shape[1] // tq
    bands = []
    for u in range(n_sub if has_lat else 0):
        band = tq + 2 * WINDOW
        q0 = (pl.program_id(1) * n_sub + u) * tq
        start = pl.multiple_of(jnp.clip(q0 - WINDOW, 0, t_len - band), WINDOW)
        qpos = q0 + lax.broadcasted_iota(jnp.int32, (1, tq), 1)
        kpos = start + lax.broadcasted_iota(jnp.int32, (band, 1), 0)
        bands.append((k_ref[0, pl.ds(start, band), :], values_t(v_ref[0, pl.ds(start, band), :]),
                      jnp.abs(qpos - kpos) <= WINDOW))

    combos = [(u, g, kv) for u in range(n_sub) for g in range(SW_GROUP)
              for kv in range(SW_KV_HEADS)]

    def scores(c):
        u, g, kv = combos[c]
        qm = q_ref[0, u * tq:(u + 1) * tq, g * LANES:(g + 1) * LANES] * hm_ref[kv:kv + 1, :]
        s_lat = _nt_dot(bands[u][0], qm) if has_lat else None
        return _nt_dot(kc, qm), s_lat

    pending = {c: scores(c) for c in range(SW_AHEAD)}
    outs = {}
    for c, (u, g, kv) in enumerate(combos):
        s_ctx, s_lat = pending.pop(c)
        if c + SW_AHEAD < len(combos):
            pending[c + SW_AHEAD] = scores(c + SW_AHEAD)
        sink = sink_ref[kv * SW_GROUP + g]
        s_ctx = s_ctx.astype(BF16)
        m = jnp.maximum(jnp.max(s_ctx, axis=0, keepdims=True).astype(F32), sink)
        if has_lat:
            s_lat = jnp.where(bands[u][2], s_lat.astype(BF16), NEG_INF)
            m = jnp.maximum(m, jnp.max(s_lat, axis=0, keepdims=True).astype(F32))
        m_b = m.astype(BF16)
        m = m_b.astype(F32)
        acc = _dot(vc_t[kv], jnp.exp2(s_ctx - m_b))
        if has_lat:
            acc = acc + _dot(bands[u][1][kv], jnp.exp2(s_lat - m_b))
        l = acc[HEAD_DIM:HEAD_DIM + 1] + jnp.exp2(sink - m)
        outs[kv] = acc[:HEAD_DIM] / l
        if kv == SW_KV_HEADS - 1:
            pair_t = jnp.concatenate([outs.pop(k2) for k2 in range(SW_KV_HEADS)], axis=0)
            o_ref[0, u * tq:(u + 1) * tq, g * LANES:(g + 1) * LANES] = pair_t.T.astype(o_ref.dtype)


def _window_attention(q, kc, vc, k, v, lw):
    bsz, t_q, _ = q.shape
    has_lat = k is not None
    tq = min(SW_TQ, t_q)
    tb = min(SW_SUB * tq, t_q)
    n_ctx = kc.shape[1]
    full = lambda t: pl.BlockSpec((1, t, SW_KV), lambda b, j: (b, 0, 0))
    specs = [pl.BlockSpec(memory_space=pltpu.SMEM),
             pl.BlockSpec((1, tb, SW_Q), lambda b, j: (b, j, 0)), full(n_ctx), full(n_ctx)]
    args = [lw["sw_sink"], q, kc, vc]
    if has_lat:
        specs += [full(t_q), full(t_q)]
        args += [k, v]
    specs.append(_const_spec((SW_KV_HEADS, LANES)))
    args.append(lw["sw_hmask"])
    return pl.pallas_call(
        functools.partial(_sw_kernel, has_lat=has_lat, tq=tq, t_len=t_q),
        out_shape=jax.ShapeDtypeStruct((bsz, t_q, SW_Q), BF16),
        grid=(bsz, t_q // tb),
        in_specs=specs,
        out_specs=pl.BlockSpec((1, tb, SW_Q), lambda b, j: (b, j, 0)),
        compiler_params=_cparams(2),
        name="window_attention",
    )(*args)


def _lru_kernel(xf_ref, xb_ref, xc_ref, cw_ref, cb_ref, wa_ref, ba_ref, wx_ref, bx_ref, lam_ref,
                hf_ref, hb_ref, hfc_ref, hbc_ref, halo_ref, carry_ref, *, tc):
    j = pl.program_id(1)
    is_ctx = j == 0
    sub = lax.broadcasted_iota(jnp.int32, (1, SUBLANES, 1), 1)

    for d, x_lat_ref, out_ref, outc_ref in ((0, xf_ref, hf_ref, hfc_ref), (1, xb_ref, hb_ref, hbc_ref)):
        x = jnp.where(is_ctx, xc_ref[0], x_lat_ref[0])
        halo = jnp.where(j <= 1, 0.0, halo_ref[d])
        cw = cw_ref[d]
        n_grp = tc // SUBLANES
        x3 = x.reshape(n_grp, SUBLANES, x.shape[1])
        xc3 = cb_ref[d] + x3 * cw[LRU_CONV - 1:LRU_CONV]
        for k in range(LRU_CONV - 1):
            s = LRU_CONV - 1 - k
            if d == 0:
                rolled = pltpu.roll(x3, s, 1)
                other = jnp.concatenate([pltpu.roll(halo, s, 0)[None], rolled[:-1]], axis=0)
                xs = jnp.where(sub < s, other, rolled)
            else:
                rolled = pltpu.roll(x3, SUBLANES - s, 1)
                other = jnp.concatenate([rolled[1:], pltpu.roll(halo, SUBLANES - s, 0)[None]], axis=0)
                xs = jnp.where(sub >= SUBLANES - s, other, rolled)
            xc3 = xc3 + xs * cw[k:k + 1]
        xc = xc3.reshape(tc, x.shape[1])
        halo_ref[d] = x[tc - SUBLANES:] if d == 0 else x[:SUBLANES]

        xcb = xc.astype(BF16)
        r = jax.nn.sigmoid(_dot(xcb, wa_ref[d]) + ba_ref[d])
        gi = jax.nn.sigmoid(_dot(xcb, wx_ref[d]) + bx_ref[d])
        nl = -lam_ref[d]
        softplus = jnp.maximum(nl, 0.0) + jnp.log1p(jnp.exp(-jnp.abs(nl)))
        log_a = -LRU_C * r * softplus
        a = jnp.exp(log_a)
        b = jnp.sqrt(-jnp.tanh(log_a) * (a * a + 1.0)) * (gi * xc)

        a3 = a.reshape(n_grp, SUBLANES, a.shape[1])
        b3 = b.reshape(n_grp, SUBLANES, b.shape[1])
        step = 1
        while step < SUBLANES:
            keep = (sub < step) if d == 0 else (sub >= SUBLANES - step)
            shift = step if d == 0 else SUBLANES - step
            a_s = jnp.where(keep, 1.0, pltpu.roll(a3, shift, 1))
            b_s = jnp.where(keep, 0.0, pltpu.roll(b3, shift, 1))
            b3 = a3 * b_s + b3
            a3 = a3 * a_s
            step *= 2

        edge_row = SUBLANES - 1 if d == 0 else 0
        h_in = jnp.where(is_ctx, 0.0, carry_ref[d, edge_row:edge_row + 1, :])
        groups = [None] * n_grp
        for gi in (range(n_grp) if d == 0 else range(n_grp - 1, -1, -1)):
            groups[gi] = a3[gi] * h_in + b3[gi]
            h_in = groups[gi][edge_row:edge_row + 1]
        h = jnp.concatenate(groups, axis=0)
        carry_ref[d] = groups[n_grp - 1] if d == 0 else groups[0]

        h_out = h.astype(out_ref.dtype)

        @pl.when(is_ctx)
        def _():
            outc_ref[0] = h_out

        @pl.when(jnp.logical_not(is_ctx))
        def _():
            out_ref[0] = h_out


def _rglru_scans(cx, cxc, lw):
    bsz, t_len, n = cx.shape
    tc = LRU_TC
    assert cxc.shape[1] == tc and t_len % tc == 0
    n_lat = t_len // tc
    fwd = lambda b, j: (b, jnp.maximum(j - 1, 0), 0)
    bwd = lambda b, j: (b, n_lat - jnp.maximum(j, 1), 0)
    ctx = lambda b, j: (b, 0, 0)
    blk = lambda m: pl.BlockSpec((1, tc, n), m)
    return pl.pallas_call(
        functools.partial(_lru_kernel, tc=tc),
        out_shape=(jax.ShapeDtypeStruct(cx.shape, BF16), jax.ShapeDtypeStruct(cx.shape, BF16),
                   jax.ShapeDtypeStruct(cxc.shape, BF16), jax.ShapeDtypeStruct(cxc.shape, BF16)),
        grid=(bsz, n_lat + 1),
        in_specs=[blk(fwd), blk(bwd), blk(ctx),
                  _const_spec((2, LRU_CONV, n)), _const_spec((2, 1, n)),
                  _const_spec((2, n, n)), _const_spec((2, 1, n)),
                  _const_spec((2, n, n)), _const_spec((2, 1, n)), _const_spec((2, 1, n))],
        out_specs=(blk(fwd), blk(bwd), blk(ctx), blk(ctx)),
        scratch_shapes=[pltpu.VMEM((2, SUBLANES, n), F32), pltpu.VMEM((2, SUBLANES, n), F32)],
        compiler_params=_cparams(2),
        name="rglru_scans",
    )(cx, cx, cxc, lw["lru_conv_w"], lw["lru_conv_b"], lw["lru_wa"], lw["lru_ba"],
      lw["lru_wx"], lw["lru_bx"], lw["lru_lambda"])


def _gelu_tanh(x):
    return 0.5 * x * (1.0 + jnp.tanh(math.sqrt(2.0 / math.pi) * (x + 0.044715 * (x * x * x))))


def _outproj_kernel(x_ref, ya_ref, yb_ref, hf_ref, hb_ref, cg_ref, g1_ref, w_ref, o_ref, cat_ref):
    cat_ref[:, 0:DA_V] = ya_ref[0]
    cat_ref[:, DA_V:DA_V + SW_Q] = yb_ref[0]
    yc = (hf_ref[0].astype(F32) + hb_ref[0].astype(F32)) * _gelu_tanh(cg_ref[0].astype(F32))
    cat_ref[:, DA_V + SW_Q:] = yc.astype(BF16)
    o_ref[0] = x_ref[0] + g1_ref[0] * _dot(cat_ref[...], w_ref[...])


def _output_projection(x, ya, yb, hf, hb, cg, gate, lw):
    bsz, t_len, d = x.shape
    tm = min(TM_OUT, t_len)
    per_b = gate.shape[0] > 1
    mod_map = (lambda b, j: (b, 0, 0)) if per_b else (lambda b, j: (0, 0, 0))
    tok = lambda n: pl.BlockSpec((1, tm, n), lambda b, j: (b, j, 0))
    return pl.pallas_call(
        _outproj_kernel,
        out_shape=jax.ShapeDtypeStruct(x.shape, F32),
        grid=(bsz, t_len // tm),
        in_specs=[tok(d), tok(DA_V), tok(SW_Q), tok(LRU_WIDTH), tok(LRU_WIDTH), tok(LRU_WIDTH),
                  pl.BlockSpec((1, 1, d), mod_map), _layer_spec(lw["w_out"].shape, lw["layer"])],
        out_specs=tok(d),
        scratch_shapes=[pltpu.VMEM((tm, d), BF16)],
        compiler_params=_cparams(2),
        name="output_projection",
    )(x, ya, yb, hf, hb, cg, gate, lw["w_out"])


def _ffn_kernel(xm_ref, xp_ref, xn_ref, sh_ref, sc_ref, g2_ref, ng_ref, wup_ref, cw_ref, cb_ref,
                wdn_ref, o_ref, h_ref, *, tm):
    j = pl.program_id(1)
    last = pl.num_programs(1) - 1
    gain, shift, scale = ng_ref[...], sh_ref[0], sc_ref[0]
    xm = xm_ref[0]
    hp = jnp.where(j > 0, _norm_modulate(xp_ref[0], gain, shift, scale), 0.0)
    hn = jnp.where(j < last, _norm_modulate(xn_ref[0], gain, shift, scale), 0.0)
    h_ref[0:FFN_HALO] = hp.astype(BF16)
    h_ref[FFN_HALO:FFN_HALO + tm] = _norm_modulate(xm, gain, shift, scale).astype(BF16)
    h_ref[FFN_HALO + tm:] = hn.astype(BF16)

    rows = tm + 2 * FFN_HALO

    def up(lo, hi):
        return (_dot(h_ref[...], wup_ref[:, lo:hi]),
                _dot(h_ref[FFN_HALO:FFN_HALO + tm], wup_ref[:, D_FF + lo:D_FF + hi]))

    acc = jnp.zeros((tm, D_MODEL), F32)
    nxt = up(*FFN_CHUNKS[0])
    for c, (lo, hi) in enumerate(FFN_CHUNKS):
        gp, val = nxt
        if c + 1 < len(FFN_CHUNKS):
            nxt = up(*FFN_CHUNKS[c + 1])
        g_prev = pltpu.roll(gp, 1, 0)[FFN_HALO:FFN_HALO + tm]
        g_next = pltpu.roll(gp, rows - 1, 0)[FFN_HALO:FFN_HALO + tm]
        cw = cw_ref[:, lo:hi]
        gate = (cb_ref[:, lo:hi] + g_prev * cw[0:1] + gp[FFN_HALO:FFN_HALO + tm] * cw[1:2]
                + g_next * cw[2:3])
        act = (gate * jax.nn.sigmoid(gate) * val).astype(BF16)
        acc = acc + _dot(act, wdn_ref[lo:hi, :])
    o_ref[0] = xm + g2_ref[0] * acc


def _conv_ffn(x, shift, scale, gate, lw):
    bsz, t_len, d = x.shape
    tm = min(TM, t_len)
    per_b = gate.shape[0] > 1
    mod_map = (lambda b, j: (b, 0, 0)) if per_b else (lambda b, j: (0, 0, 0))
    hb = tm // FFN_HALO
    n_hb = t_len // FFN_HALO
    prev_map = lambda b, j: (b, jnp.maximum(j * hb - 1, 0), 0)
    next_map = lambda b, j: (b, jnp.minimum((j + 1) * hb, n_hb - 1), 0)
    mod_spec = pl.BlockSpec((1, 1, d), mod_map)
    return pl.pallas_call(
        functools.partial(_ffn_kernel, tm=tm),
        out_shape=jax.ShapeDtypeStruct(x.shape, F32),
        grid=(bsz, t_len // tm),
        in_specs=[pl.BlockSpec((1, tm, d), lambda b, j: (b, j, 0)),
                  pl.BlockSpec((1, FFN_HALO, d), prev_map),
                  pl.BlockSpec((1, FFN_HALO, d), next_map),
                  mod_spec, mod_spec, mod_spec,
                  _const_spec((1, d)), _layer_spec(lw["w_up"].shape, lw["layer"]),
                  _const_spec((FFN_CONV, D_FF)), _const_spec((1, D_FF)),
                  _layer_spec(lw["w_down"].shape, lw["layer"])],
        out_specs=pl.BlockSpec((1, tm, d), lambda b, j: (b, j, 0)),
        scratch_shapes=[pltpu.VMEM((tm + 2 * FFN_HALO, d), BF16)],
        compiler_params=_cparams(2),
        name="conv_ffn",
    )(x, x, x, shift, scale, gate, lw["norm2_gain"], lw["w_up"], lw["ffn_conv_w"],
      lw["ffn_conv_b"], lw["w_down"])


def _sw_head_order():
    return [kv * SW_GROUP + g for g in range(SW_GROUP) for kv in range(SW_KV_HEADS)]


def _rope_tables(rows, head_dim, n_heads):
    row = np.repeat(np.arange(rows, dtype=np.float64), GRID_W)
    col = np.tile(np.arange(GRID_W, dtype=np.float64), rows)
    quarter = head_dim // 4
    inv_freq = ROPE_BASE ** (-np.arange(quarter, dtype=np.float64) / quarter)
    ang = np.concatenate([row[:, None] * inv_freq, col[:, None] * inv_freq], axis=-1)
    cos, sin = np.cos(ang), np.sin(ang)
    return (jnp.asarray(np.tile(np.concatenate([cos, cos], axis=-1), (1, n_heads)), dtype=F32),
            jnp.asarray(np.tile(np.concatenate([-sin, sin], axis=-1), (1, n_heads)), dtype=F32))


def _block_diag_mean(n, group):
    idx = np.arange(n) // group
    return jnp.asarray((idx[:, None] == idx[None, :]).astype(np.float32) / group, dtype=BF16)


def _block_diag(w):
    two, nb, bi, bj = w.shape
    eye = jnp.eye(nb, dtype=w.dtype)
    return jnp.einsum("dhij,hg->dhigj", w, eye).reshape(two, nb * bi, nb * bj)


def _prepare_weights(p):
    d = D_MODEL
    depth = p["w_in"].shape[0]
    split = np.cumsum((DA_Q, DA_Q, DA_V, SW_Q, SW_KV, SW_KV, LRU_WIDTH))
    w_aq, w_ak, w_av, w_bq, w_bk, w_bv, w_cx, w_cg = jnp.split(p["w_in"], split.tolist(), axis=2)
    w_av = jnp.pad(w_av.reshape(depth, d, DA_HEADS, DA_V_DIM),
                   ((0, 0), (0, 0), (0, 0), (0, LANES - DA_V_DIM))).reshape(depth, d, DA_VEXT)
    order = jnp.asarray(_sw_head_order())
    w_bq = w_bq.reshape(depth, d, SW_HEADS, HEAD_DIM)[:, :, order].reshape(depth, d, SW_Q)
    w_in = jnp.concatenate([w_aq, w_ak, w_av, w_bq, w_bk, w_bv, w_cx, w_cg], axis=2)

    w_out = p["w_out"]
    w_ob = w_out[:, DA_V:DA_V + SW_Q].reshape(depth, SW_HEADS, HEAD_DIM, d)[:, order]
    w_out = jnp.concatenate([w_out[:, :DA_V], w_ob.reshape(depth, SW_Q, d), w_out[:, DA_V + SW_Q:]],
                            axis=1)
    return dict(w_in=w_in.astype(BF16), w_out=w_out.astype(BF16),
                w_up=p["w_up"].astype(BF16), w_down=p["w_down"].astype(BF16))


def _prepare_layer(p, i):
    d = D_MODEL
    v_one = np.zeros((1, DA_VEXT), np.float32)
    v_one[0, DA_V_DIM::LANES] = 1.0

    da_hmask = np.zeros((2 * DA_HEADS, LANES), np.float32)
    for j in range(2 * DA_HEADS):
        off = (j % 4) * DA_QK_DIM
        da_hmask[j, off:off + DA_QK_DIM] = 1.0
    sw_hmask = np.zeros((SW_KV_HEADS, LANES), np.float32)
    for kv in range(SW_KV_HEADS):
        sw_hmask[kv, kv * HEAD_DIM:(kv + 1) * HEAD_DIM] = 1.0

    return dict(
        norm1_gain=p["norm1_gain"][i].reshape(1, d), norm2_gain=p["norm2_gain"][i].reshape(1, d),
        layer=i, v_one=jnp.asarray(v_one),
        gq_a=(jnp.tile(p["da_q_gain"][i], 2 * DA_HEADS).reshape(1, DA_Q)
              * (DA_QK_DIM ** -0.5 * math.log2(math.e))),
        gk_a=jnp.tile(p["da_k_gain"][i], 2 * DA_HEADS).reshape(1, DA_Q),
        gq_b=(jnp.tile(p["sw_q_gain"][i], SW_HEADS).reshape(1, SW_Q)
              * (HEAD_DIM ** -0.5 * math.log2(math.e))),
        gk_b=jnp.tile(p["sw_k_gain"][i], SW_KV_HEADS).reshape(1, SW_KV),
        g32=_block_diag_mean(MXU_DIM, DA_QK_DIM), g64=_block_diag_mean(MXU_DIM, HEAD_DIM),
        da_hmask=jnp.asarray(da_hmask, dtype=BF16),
        da_lamv=jnp.stack([p["da_lam_q1"][i], p["da_lam_k1"][i], p["da_lam_q2"][i], p["da_lam_k2"][i]]),
        da_sg=jnp.tile(p["da_sub_gain"][i], LANES // DA_V_DIM).reshape(1, LANES),
        sw_sink=p["sw_sink"][i] * math.log2(math.e), sw_hmask=jnp.asarray(sw_hmask, dtype=BF16),
        lru_conv_w=p["lru_conv_w"][i], lru_conv_b=p["lru_conv_b"][i].reshape(2, 1, LRU_WIDTH),
        lru_wa=_block_diag(p["lru_wa"][i]).astype(BF16), lru_ba=p["lru_ba"][i].reshape(2, 1, LRU_WIDTH),
        lru_wx=_block_diag(p["lru_wx"][i]).astype(BF16), lru_bx=p["lru_bx"][i].reshape(2, 1, LRU_WIDTH),
        lru_lambda=p["lru_lambda"][i].reshape(2, 1, LRU_WIDTH),
        ffn_conv_w=p["ffn_conv_w"][i], ffn_conv_b=p["ffn_conv_b"][i].reshape(1, D_FF),
    )


def kernel(x, c, ctx, c_ctx, w_mod, b_mod, norm1_gain, norm2_gain, w_in, da_q_gain, da_k_gain, da_lam_q1, da_lam_k1, da_lam_q2, da_lam_k2, da_sub_gain, sw_q_gain, sw_k_gain, sw_sink, lru_conv_w, lru_conv_b, lru_wa, lru_ba, lru_wx, lru_bx, lru_lambda, w_out, w_up, ffn_conv_w, ffn_conv_b, w_down):
    p = dict(w_in=w_in, norm1_gain=norm1_gain, norm2_gain=norm2_gain, da_q_gain=da_q_gain,
             da_k_gain=da_k_gain, da_lam_q1=da_lam_q1, da_lam_k1=da_lam_k1, da_lam_q2=da_lam_q2,
             da_lam_k2=da_lam_k2, da_sub_gain=da_sub_gain, sw_q_gain=sw_q_gain, sw_k_gain=sw_k_gain,
             sw_sink=sw_sink, lru_conv_w=lru_conv_w, lru_conv_b=lru_conv_b, lru_wa=lru_wa,
             lru_ba=lru_ba, lru_wx=lru_wx, lru_bx=lru_bx, lru_lambda=lru_lambda, w_out=w_out,
             w_up=w_up, ffn_conv_w=ffn_conv_w, ffn_conv_b=ffn_conv_b, w_down=w_down)
    bsz, n_tok, d = x.shape
    n_ctx = ctx.shape[1]
    depth = w_mod.shape[0]

    cc = jnp.zeros((2 * SUBLANES, d), F32).at[:bsz].set(c).at[bsz].set(c_ctx)
    mod_all = _modulation(cc, w_mod, b_mod)

    rope_lat = (_rope_tables(n_tok // GRID_W, DA_QK_DIM, 2 * DA_HEADS)
                + _rope_tables(n_tok // GRID_W, HEAD_DIM, SW_HEADS))
    rope_ctx = (jnp.ones((n_ctx, DA_Q), F32), jnp.zeros((n_ctx, DA_Q), F32),
                jnp.ones((n_ctx, SW_Q), F32), jnp.zeros((n_ctx, SW_Q), F32))

    weights = _prepare_weights(p)
    xc = ctx
    for i in range(depth):
        lw = {**_prepare_layer(p, i), **weights}
        lambda_init = 0.8 - 0.6 * math.exp(-0.3 * i)
        ctx_out = i < depth - 1
        mod = mod_all[i, :bsz].reshape(bsz, 1, N_MOD, d)
        sh1, sc1, g1, sh2, sc2, g2 = [mod[:, :, m] for m in range(N_MOD)]
        mod_c = mod_all[i, bsz].reshape(1, 1, N_MOD, d)
        sh1c, sc1c, g1c, sh2c, sc2c, g2c = [mod_c[:, :, m] for m in range(N_MOD)]

        qa, ka, va, qb, kb, vb, cx, cg = _input_projection(x, sh1, sc1, lw, rope_lat)
        qac, kac, vac, qbc, kbc, vbc, cxc, cgc = _input_projection(xc, sh1c, sc1c, lw, rope_ctx)

        ya = _diff_attention(qa, [(kac, vac), (ka, va)], lw, lambda_init)
        yb = _window_attention(qb, kbc, vbc, kb, vb, lw)
        hf, hb, hfc, hbc = _rglru_scans(cx, cxc, lw)

        x = _output_projection(x, ya, yb, hf, hb, cg, g1, lw)
        x = _conv_ffn(x, sh2, sc2, g2, lw)
        if ctx_out:
            yac = _diff_attention(qac, [(kac, vac)], lw, lambda_init)
            ybc = _window_attention(qbc, kbc, vbc, None, None, lw)
            xc = _output_projection(xc, yac, ybc, hfc, hbc, cgc, g1c, lw)
            xc = _conv_ffn(xc, sh2c, sc2c, g2c, lw)
    return x
```

```python
import functools
import math

import numpy as np
import jax
import jax.numpy as jnp
from jax import lax
from jax.experimental import pallas as pl
from jax.experimental.pallas import tpu as pltpu

F32 = jnp.float32
BF16 = jnp.bfloat16

D_MODEL = 1024
GRID_W = 64
N_MOD = 6
EPS = 1e-6
NEG_INF = -1e30
ROPE_BASE = 10000.0
DA_HEADS = 4
DA_QK_DIM = 32
DA_V_DIM = 64
HEAD_DIM = 64
SW_HEADS = 6
SW_KV_HEADS = 2
SW_GROUP = SW_HEADS // SW_KV_HEADS
WINDOW = 128
LRU_WIDTH = 384
LRU_BLOCKS = 6
LRU_BLOCK_DIM = LRU_WIDTH // LRU_BLOCKS
LRU_CONV = 4
LRU_C = 8.0
D_FF = 2816
FFN_CONV = 3

DA_Q = DA_HEADS * 2 * DA_QK_DIM
DA_V = DA_HEADS * DA_V_DIM
SW_Q = SW_HEADS * HEAD_DIM
SW_KV = SW_KV_HEADS * HEAD_DIM

LANES = 128
SUBLANES = 8
BF16_ROWS = 16
MXU_DIM = 256
VMEM_LIMIT = 56 * 1024 * 1024

DA_VEXT = DA_HEADS * LANES
DA_VROWS = DA_V_DIM + BF16_ROWS
C_AQ = 0
C_AK = C_AQ + DA_Q
C_AV = C_AK + DA_Q
C_BQ = C_AV + DA_VEXT
C_BK = C_BQ + SW_Q
C_BV = C_BK + SW_KV
C_CX = C_BV + SW_KV
C_CG = C_CX + LRU_WIDTH
D_INX = C_CG + LRU_WIDTH

TM = 512
FFN_HALO = BF16_ROWS
FFN_CHUNKS = ((0, 768), (768, 1536), (1536, 2304), (2304, D_FF))
DA_TQ = 256
DA_AHEAD = 3
DA_SUB = 2
TM_IN = 1024
IN_SUB = 256
TM_OUT = 1024
SW_AHEAD = 2
SW_TQ = 256
SW_SUB = 4
LRU_TC = 256


def _cparams(n_axes):
    return pltpu.CompilerParams(dimension_semantics=("arbitrary",) * n_axes,
                                vmem_limit_bytes=VMEM_LIMIT)


def _const_spec(shape):
    nd = len(shape)
    return pl.BlockSpec(shape, lambda *_: (0,) * nd, pipeline_mode=pl.Buffered(1))


def _layer_spec(shape, layer):
    nd = len(shape)
    return pl.BlockSpec((None,) + tuple(shape[1:]), lambda *_: (layer,) + (0,) * (nd - 1),
                        pipeline_mode=pl.Buffered(1))


def _nt_dot(a, b):
    return lax.dot_general(a, b, (((1,), (1,)), ((), ())), preferred_element_type=F32)


def _dot(a, b):
    return jnp.dot(a, b, preferred_element_type=F32)


def _mod_kernel(c_ref, w_ref, b_ref, o_ref):
    c = c_ref[...]
    s = c * jax.nn.sigmoid(c)
    o_ref[0] = _dot(s.astype(BF16), w_ref[0].astype(BF16)) + b_ref[0]


def _modulation(cc, w_mod, b_mod):
    depth, d, n = w_mod.shape
    tn = 1536
    return pl.pallas_call(
        _mod_kernel,
        out_shape=jax.ShapeDtypeStruct((depth, cc.shape[0], n), F32),
        grid=(depth, n // tn),
        in_specs=[pl.BlockSpec(cc.shape, lambda i, j: (0, 0)),
                  pl.BlockSpec((1, d, tn), lambda i, j: (i, 0, j)),
                  pl.BlockSpec((1, 1, tn), lambda i, j: (i, 0, j))],
        out_specs=pl.BlockSpec((1, cc.shape[0], tn), lambda i, j: (i, 0, j)),
        compiler_params=_cparams(2),
        name="modulation",
    )(cc, w_mod, b_mod.reshape(depth, 1, n))


def _norm_modulate(x, gain, shift, scale):
    ms = jnp.mean(x * x, axis=-1, keepdims=True)
    h = x * lax.rsqrt(ms + EPS) * gain
    return h * (1.0 + scale) + shift


def _group_rms(a, g_ref):
    sq = (a * a).astype(BF16)
    n = a.shape[1]
    ms = [_dot(sq[:, lo:min(lo + MXU_DIM, n)], g_ref[:min(MXU_DIM, n - lo), :min(MXU_DIM, n - lo)])
          for lo in range(0, n, MXU_DIM)]
    ms = ms[0] if len(ms) == 1 else jnp.concatenate(ms, axis=1)
    return a * lax.rsqrt(ms + EPS)


def _rope(y, cos, sin_signed, half):
    n = y.shape[1]
    lane = lax.broadcasted_iota(jnp.int32, y.shape, 1)
    first = (lane % (2 * half)) < half
    partner = jnp.where(first, pltpu.roll(y, n - half, 1), pltpu.roll(y, half, 1))
    return y * cos + partner * sin_signed


def _inproj_kernel(x_ref, sh_ref, sc_ref, ng_ref, w_ref, vone_ref,
                   cosa_ref, sina_ref, cosb_ref, sinb_ref,
                   gqa_ref, gka_ref, gqb_ref, gkb_ref, g32_ref, g64_ref,
                   qa_ref, ka_ref, va_ref, qb_ref, kb_ref, vb_ref, cx_ref, cg_ref):
    slab = va_ref.shape[3]
    sub = min(IN_SUB, slab)
    n_sub = x_ref.shape[1] // sub

    def project(u):
        rows = slice(u * sub, (u + 1) * sub)
        h = _norm_modulate(x_ref[0, rows], ng_ref[...], sh_ref[0], sc_ref[0]).astype(BF16)
        return _dot(h, w_ref[...])

    def finish(u, full):
        rows = slice(u * sub, (u + 1) * sub)

        def proj(lo, hi):
            return full[:, lo:hi]

        cosa, sina = cosa_ref[rows], sina_ref[rows]
        aq = _group_rms(proj(C_AQ, C_AK), g32_ref) * gqa_ref[...]
        qa_ref[0, rows] = _rope(aq, cosa, sina, DA_QK_DIM // 2).astype(BF16)
        ak = _group_rms(proj(C_AK, C_AV), g32_ref) * gka_ref[...]
        ka_ref[0, rows] = _rope(ak, cosa, sina, DA_QK_DIM // 2).astype(BF16)
        va_t = (proj(C_AV, C_BQ) + vone_ref[...]).T.astype(BF16)
        col = (u * sub) % slab
        for hd in range(DA_HEADS):
            va_ref[0, (u * sub) // slab, hd * DA_VROWS:(hd + 1) * DA_VROWS, col:col + sub] = (
                va_t[hd * LANES:hd * LANES + DA_VROWS])

        cosb, sinb = cosb_ref[rows], sinb_ref[rows]
        bq = _group_rms(proj(C_BQ, C_BK), g64_ref) * gqb_ref[...]
        qb_ref[0, rows] = _rope(bq, cosb, sinb, HEAD_DIM // 2).astype(BF16)
        bkv = proj(C_BK, C_CX)
        bk = _group_rms(bkv[:, :SW_KV], g64_ref) * gkb_ref[...]
        kb_ref[0, rows] = _rope(bk, cosb[:, :SW_KV], sinb[:, :SW_KV], HEAD_DIM // 2).astype(BF16)
        vb_ref[0, rows] = bkv[:, SW_KV:].astype(BF16)

        cxg = proj(C_CX, D_INX)
        cx_ref[0, rows] = cxg[:, :LRU_WIDTH]
        cg_ref[0, rows] = cxg[:, LRU_WIDTH:].astype(cg_ref.dtype)

    full = project(0)
    for u in range(n_sub):
        nxt = project(u + 1) if u + 1 < n_sub else None
        finish(u, full)
        full = nxt


def _input_projection(x, shift, scale, lw, rope):
    bsz, t_len, d = x.shape
    tm = min(TM_IN, t_len)
    sub = min(TM, tm)
    per_b = shift.shape[0] > 1
    mod_map = (lambda b, j: (b, 0, 0)) if per_b else (lambda b, j: (0, 0, 0))
    tok = lambda n: pl.BlockSpec((1, tm, n), lambda b, j: (b, j, 0))
    tab = lambda n: pl.BlockSpec((tm, n), lambda b, j: (j, 0))
    widths = (DA_Q, DA_Q, DA_VEXT, SW_Q, SW_KV, SW_KV, LRU_WIDTH, LRU_WIDTH)
    dtypes = (BF16,) * 6 + (F32, BF16)
    out_shape = [jax.ShapeDtypeStruct((bsz, t_len, n), dt) for n, dt in zip(widths, dtypes)]
    out_specs = [tok(n) for n in widths]
    out_shape[2] = jax.ShapeDtypeStruct((bsz, t_len // sub, DA_HEADS * DA_VROWS, sub), BF16)
    out_specs[2] = pl.BlockSpec((1, tm // sub, DA_HEADS * DA_VROWS, sub), lambda b, j: (b, j, 0, 0))
    return pl.pallas_call(
        _inproj_kernel,
        out_shape=tuple(out_shape),
        grid=(bsz, t_len // tm),
        in_specs=[tok(d),
                  pl.BlockSpec((1, 1, d), mod_map), pl.BlockSpec((1, 1, d), mod_map),
                  _const_spec((1, d)), _layer_spec(lw["w_in"].shape, lw["layer"]), _const_spec((1, DA_VEXT)),
                  tab(DA_Q), tab(DA_Q), tab(SW_Q), tab(SW_Q),
                  _const_spec((1, DA_Q)), _const_spec((1, DA_Q)),
                  _const_spec((1, SW_Q)), _const_spec((1, SW_KV)),
                  _const_spec((MXU_DIM, MXU_DIM)), _const_spec((MXU_DIM, MXU_DIM))],
        out_specs=tuple(out_specs),
        compiler_params=_cparams(2),
        name="input_projection",
    )(x, shift, scale, lw["norm1_gain"], lw["w_in"], lw["v_one"], *rope,
      lw["gq_a"], lw["gk_a"], lw["gq_b"], lw["gk_b"], lw["g32"], lw["g64"])


def _da_kernel(*refs, n_seg, lambda_init):
    q_ref = refs[0]
    kv_refs = refs[1:1 + 2 * n_seg]
    hmask_ref, lamv_ref, sg_ref, o_ref, m_ref, acc_ref = refs[1 + 2 * n_seg:]
    n_map = 2 * DA_HEADS
    tq = m_ref.shape[2]
    n_sub = q_ref.shape[1] // tq

    m_ref[...] = jnp.full(m_ref.shape, NEG_INF, F32)
    acc_ref[...] = jnp.zeros(acc_ref.shape, F32)

    items = [(kv_refs[2 * si], kv_refs[2 * si + 1], i, u, j)
             for si in range(n_seg) for i in range(kv_refs[2 * si + 1].shape[1])
             for u in range(n_sub) for j in range(n_map)]

    def scores(k_ref, v_ref, i, u, j):
        half, tk = j // 4, v_ref.shape[3]
        qm = (q_ref[0, u * tq:(u + 1) * tq, half * LANES:(half + 1) * LANES]
              * hmask_ref[j:j + 1, :])
        kblk = k_ref[0, i * tk:(i + 1) * tk, half * LANES:(half + 1) * LANES]
        return _nt_dot(kblk, qm)

    pending = {n: scores(*items[n]) for n in range(min(DA_AHEAD, len(items)))}
    for n, (k_ref, v_ref, i, u, j) in enumerate(items):
        s_b = pending.pop(n).astype(BF16)
        if n + DA_AHEAD < len(items):
            pending[n + DA_AHEAD] = scores(*items[n + DA_AHEAD])
        st = u * n_map + j
        m_old = m_ref[st, 0:1, :]
        m_new = jnp.maximum(m_old, jnp.max(s_b, axis=0, keepdims=True).astype(F32))
        p_t = jnp.exp2(s_b - m_new.astype(BF16))
        pv = _dot(v_ref[0, i, (j // 2) * DA_VROWS:(j // 2 + 1) * DA_VROWS, :], p_t)
        acc_ref[st] = jnp.exp2(m_old - m_new) * acc_ref[st] + pv
        m_ref[st] = jnp.broadcast_to(m_new, m_ref.shape[1:])

    lv = lamv_ref[...]
    lam = (jnp.exp(jnp.sum(lv[0:1] * lv[1:2], axis=1, keepdims=True))
           - jnp.exp(jnp.sum(lv[2:3] * lv[3:4], axis=1, keepdims=True)) + lambda_init)

    def head_out(u, h):
        a1, a2 = acc_ref[u * n_map + 2 * h], acc_ref[u * n_map + 2 * h + 1]
        o = (a1 / a1[DA_V_DIM:DA_V_DIM + 1] - lam * (a2 / a2[DA_V_DIM:DA_V_DIM + 1]))[:DA_V_DIM]
        ms = jnp.sum(o * o, axis=0, keepdims=True) * (1.0 / DA_V_DIM)
        return o * lax.rsqrt(ms + EPS)

    for u in range(n_sub):
        for c in range(DA_HEADS // 2):
            pair_t = jnp.concatenate([head_out(u, 2 * c), head_out(u, 2 * c + 1)], axis=0)
            o_ref[0, u * tq:(u + 1) * tq, c * LANES:(c + 1) * LANES] = (
                pair_t.T * sg_ref[...]).astype(o_ref.dtype)


def _diff_attention(q, kvs, lw, lambda_init):
    bsz, t_q, _ = q.shape
    tq = min(DA_TQ, t_q)
    n_sub = min(DA_SUB, t_q // tq)
    tb = n_sub * tq
    kv_specs, kv_args = [], []
    for k, v in kvs:
        kv_specs += [pl.BlockSpec((1,) + k.shape[1:], lambda b, j: (b, 0, 0)),
                     pl.BlockSpec((1,) + v.shape[1:], lambda b, j: (b, 0, 0, 0))]
        kv_args += [k, v]
    sg = lw["da_sg"] * (1.0 - lambda_init)
    return pl.pallas_call(
        functools.partial(_da_kernel, n_seg=len(kvs), lambda_init=lambda_init),
        out_shape=jax.ShapeDtypeStruct((bsz, t_q, DA_V), BF16),
        grid=(bsz, t_q // tb),
        in_specs=[pl.BlockSpec((1, tb, DA_Q), lambda b, j: (b, j, 0))] + kv_specs
                 + [_const_spec((2 * DA_HEADS, LANES)), _const_spec((4, DA_QK_DIM)),
                    _const_spec((1, LANES))],
        out_specs=pl.BlockSpec((1, tb, DA_V), lambda b, j: (b, j, 0)),
        scratch_shapes=[pltpu.VMEM((n_sub * 2 * DA_HEADS, SUBLANES, tq), F32),
                        pltpu.VMEM((n_sub * 2 * DA_HEADS, DA_VROWS, tq), F32)],
        compiler_params=_cparams(2),
        name="diff_attention",
    )(q, *kv_args, lw["da_hmask"], lw["da_lamv"], sg)


def _sw_kernel(*refs, has_lat, tq, t_len):
    if has_lat:
        sink_ref, q_ref, kc_ref, vc_ref, k_ref, v_ref, hm_ref, o_ref = refs
    else:
        sink_ref, q_ref, kc_ref, vc_ref, hm_ref, o_ref = refs
    def values_t(v):
        n_k = v.shape[0]
        v_t = v.astype(F32).T
        tail = (lax.broadcasted_iota(jnp.int32, (BF16_ROWS, n_k), 0) == 0).astype(F32)
        return [jnp.concatenate([v_t[kv * HEAD_DIM:(kv + 1) * HEAD_DIM], tail], axis=0).astype(BF16)
                for kv in range(SW_KV_HEADS)]

    kc = kc_ref[0]
    vc_t = values_t(vc_ref[0])
    n_sub = q_ref.shape[1] // tq
    bands = []
    for u in range(n_sub if has_lat else 0):
        band = tq + 2 * WINDOW
        q0 = (pl.program_id(1) * n_sub + u) * tq
        start = pl.multiple_of(jnp.clip(q0 - WINDOW, 0, t_len - band), WINDOW)
        qpos = q0 + lax.broadcasted_iota(jnp.int32, (1, tq), 1)
        kpos = start + lax.broadcasted_iota(jnp.int32, (band, 1), 0)
        bands.append((k_ref[0, pl.ds(start, band), :], values_t(v_ref[0, pl.ds(start, band), :]),
                      jnp.abs(qpos - kpos) <= WINDOW))

    combos = [(u, g, kv) for u in range(n_sub) for g in range(SW_GROUP)
              for kv in range(SW_KV_HEADS)]

    def scores(c):
        u, g, kv = combos[c]
        qm = q_ref[0, u * tq:(u + 1) * tq, g * LANES:(g + 1) * LANES] * hm_ref[kv:kv + 1, :]
        s_lat = _nt_dot(bands[u][0], qm) if has_lat else None
        return _nt_dot(kc, qm), s_lat

    pending = {c: scores(c) for c in range(SW_AHEAD)}
    outs = {}
    for c, (u, g, kv) in enumerate(combos):
        s_ctx, s_lat = pending.pop(c)
        if c + SW_AHEAD < len(combos):
            pending[c + SW_AHEAD] = scores(c + SW_AHEAD)
        sink = sink_ref[kv * SW_GROUP + g]
        s_ctx = s_ctx.astype(BF16)
        m = jnp.maximum(jnp.max(s_ctx, axis=0, keepdims=True).astype(F32), sink)
        if has_lat:
            s_lat = jnp.where(bands[u][2], s_lat.astype(BF16), NEG_INF)
            m = jnp.maximum(m, jnp.max(s_lat, axis=0, keepdims=True).astype(F32))
        m_b = m.astype(BF16)
        m = m_b.astype(F32)
        acc = _dot(vc_t[kv], jnp.exp2(s_ctx - m_b))
        if has_lat:
            acc = acc + _dot(bands[u][1][kv], jnp.exp2(s_lat - m_b))
        l = acc[HEAD_DIM:HEAD_DIM + 1] + jnp.exp2(sink - m)
        outs[kv] = acc[:HEAD_DIM] / l
        if kv == SW_KV_HEADS - 1:
            pair_t = jnp.concatenate([outs.pop(k2) for k2 in range(SW_KV_HEADS)], axis=0)
            o_ref[0, u * tq:(u + 1) * tq, g * LANES:(g + 1) * LANES] = pair_t.T.astype(o_ref.dtype)


def _window_attention(q, kc, vc, k, v, lw):
    bsz, t_q, _ = q.shape
    has_lat = k is not None
    tq = min(SW_TQ, t_q)
    tb = min(SW_SUB * tq, t_q)
    n_ctx = kc.shape[1]
    full = lambda t: pl.BlockSpec((1, t, SW_KV), lambda b, j: (b, 0, 0))
    specs = [pl.BlockSpec(memory_space=pltpu.SMEM),
             pl.BlockSpec((1, tb, SW_Q), lambda b, j: (b, j, 0)), full(n_ctx), full(n_ctx)]
    args = [lw["sw_sink"], q, kc, vc]
    if has_lat:
        specs += [full(t_q), full(t_q)]
        args += [k, v]
    specs.append(_const_spec((SW_KV_HEADS, LANES)))
    args.append(lw["sw_hmask"])
    return pl.pallas_call(
        functools.partial(_sw_kernel, has_lat=has_lat, tq=tq, t_len=t_q),
        out_shape=jax.ShapeDtypeStruct((bsz, t_q, SW_Q), BF16),
        grid=(bsz, t_q // tb),
        in_specs=specs,
        out_specs=pl.BlockSpec((1, tb, SW_Q), lambda b, j: (b, j, 0)),
        compiler_params=_cparams(2),
        name="window_attention",
    )(*args)


def _lru_kernel(xf_ref, xb_ref, xc_ref, cw_ref, cb_ref, wa_ref, ba_ref, wx_ref, bx_ref, lam_ref,
                hf_ref, hb_ref, hfc_ref, hbc_ref, halo_ref, carry_ref, *, tc):
    j = pl.program_id(1)
    is_ctx = j == 0
    sub = lax.broadcasted_iota(jnp.int32, (1, SUBLANES, 1), 1)

    for d, x_lat_ref, out_ref, outc_ref in ((0, xf_ref, hf_ref, hfc_ref), (1, xb_ref, hb_ref, hbc_ref)):
        x = jnp.where(is_ctx, xc_ref[0], x_lat_ref[0])
        halo = jnp.where(j <= 1, 0.0, halo_ref[d])
        cw = cw_ref[d]
        n_grp = tc // SUBLANES
        x3 = x.reshape(n_grp, SUBLANES, x.shape[1])
        xc3 = cb_ref[d] + x3 * cw[LRU_CONV - 1:LRU_CONV]
        for k in range(LRU_CONV - 1):
            s = LRU_CONV - 1 - k
            if d == 0:
                rolled = pltpu.roll(x3, s, 1)
                other = jnp.concatenate([pltpu.roll(halo, s, 0)[None], rolled[:-1]], axis=0)
                xs = jnp.where(sub < s, other, rolled)
            else:
                rolled = pltpu.roll(x3, SUBLANES - s, 1)
                other = jnp.concatenate([rolled[1:], pltpu.roll(halo, SUBLANES - s, 0)[None]], axis=0)
                xs = jnp.where(sub >= SUBLANES - s, other, rolled)
            xc3 = xc3 + xs * cw[k:k + 1]
        xc = xc3.reshape(tc, x.shape[1])
        halo_ref[d] = x[tc - SUBLANES:] if d == 0 else x[:SUBLANES]

        xcb = xc.astype(BF16)
        r = jax.nn.sigmoid(_dot(xcb, wa_ref[d]) + ba_ref[d])
        gi = jax.nn.sigmoid(_dot(xcb, wx_ref[d]) + bx_ref[d])
        nl = -lam_ref[d]
        softplus = jnp.maximum(nl, 0.0) + jnp.log1p(jnp.exp(-jnp.abs(nl)))
        log_a = -LRU_C * r * softplus
        a = jnp.exp(log_a)
        b = jnp.sqrt(-jnp.tanh(log_a) * (a * a + 1.0)) * (gi * xc)

        a3 = a.reshape(n_grp, SUBLANES, a.shape[1])
        b3 = b.reshape(n_grp, SUBLANES, b.shape[1])
        step = 1
        while step < SUBLANES:
            keep = (sub < step) if d == 0 else (sub >= SUBLANES - step)
            shift = step if d == 0 else SUBLANES - step
            a_s = jnp.where(keep, 1.0, pltpu.roll(a3, shift, 1))
            b_s = jnp.where(keep, 0.0, pltpu.roll(b3, shift, 1))
            b3 = a3 * b_s + b3
            a3 = a3 * a_s
            step *= 2

        edge_row = SUBLANES - 1 if d == 0 else 0
        h_in = jnp.where(is_ctx, 0.0, carry_ref[d, edge_row:edge_row + 1, :])
        groups = [None] * n_grp
        for gi in (range(n_grp) if d == 0 else range(n_grp - 1, -1, -1)):
            groups[gi] = a3[gi] * h_in + b3[gi]
            h_in = groups[gi][edge_row:edge_row + 1]
        h = jnp.concatenate(groups, axis=0)
        carry_ref[d] = groups[n_grp - 1] if d == 0 else groups[0]

        h_out = h.astype(out_ref.dtype)

        @pl.when(is_ctx)
        def _():
            outc_ref[0] = h_out

        @pl.when(jnp.logical_not(is_ctx))
        def _():
            out_ref[0] = h_out


def _rglru_scans(cx, cxc, lw):
    bsz, t_len, n = cx.shape
    tc = LRU_TC
    assert cxc.shape[1] == tc and t_len % tc == 0
    n_lat = t_len // tc
    fwd = lambda b, j: (b, jnp.maximum(j - 1, 0), 0)
    bwd = lambda b, j: (b, n_lat - jnp.maximum(j, 1), 0)
    ctx = lambda b, j: (b, 0, 0)
    blk = lambda m: pl.BlockSpec((1, tc, n), m)
    return pl.pallas_call(
        functools.partial(_lru_kernel, tc=tc),
        out_shape=(jax.ShapeDtypeStruct(cx.shape, BF16), jax.ShapeDtypeStruct(cx.shape, BF16),
                   jax.ShapeDtypeStruct(cxc.shape, BF16), jax.ShapeDtypeStruct(cxc.shape, BF16)),
        grid=(bsz, n_lat + 1),
        in_specs=[blk(fwd), blk(bwd), blk(ctx),
                  _const_spec((2, LRU_CONV, n)), _const_spec((2, 1, n)),
                  _const_spec((2, n, n)), _const_spec((2, 1, n)),
                  _const_spec((2, n, n)), _const_spec((2, 1, n)), _const_spec((2, 1, n))],
        out_specs=(blk(fwd), blk(bwd), blk(ctx), blk(ctx)),
        scratch_shapes=[pltpu.VMEM((2, SUBLANES, n), F32), pltpu.VMEM((2, SUBLANES, n), F32)],
        compiler_params=_cparams(2),
        name="rglru_scans",
    )(cx, cx, cxc, lw["lru_conv_w"], lw["lru_conv_b"], lw["lru_wa"], lw["lru_ba"],
      lw["lru_wx"], lw["lru_bx"], lw["lru_lambda"])


def _gelu_tanh(x):
    return 0.5 * x * (1.0 + jnp.tanh(math.sqrt(2.0 / math.pi) * (x + 0.044715 * (x * x * x))))


def _outproj_kernel(x_ref, ya_ref, yb_ref, hf_ref, hb_ref, cg_ref, g1_ref, w_ref, o_ref, cat_ref):
    cat_ref[:, 0:DA_V] = ya_ref[0]
    cat_ref[:, DA_V:DA_V + SW_Q] = yb_ref[0]
    yc = (hf_ref[0].astype(F32) + hb_ref[0].astype(F32)) * _gelu_tanh(cg_ref[0].astype(F32))
    cat_ref[:, DA_V + SW_Q:] = yc.astype(BF16)
    o_ref[0] = x_ref[0] + g1_ref[0] * _dot(cat_ref[...], w_ref[...])


def _output_projection(x, ya, yb, hf, hb, cg, gate, lw):
    bsz, t_len, d = x.shape
    tm = min(TM_OUT, t_len)
    per_b = gate.shape[0] > 1
    mod_map = (lambda b, j: (b, 0, 0)) if per_b else (lambda b, j: (0, 0, 0))
    tok = lambda n: pl.BlockSpec((1, tm, n), lambda b, j: (b, j, 0))
    return pl.pallas_call(
        _outproj_kernel,
        out_shape=jax.ShapeDtypeStruct(x.shape, F32),
        grid=(bsz, t_len // tm),
        in_specs=[tok(d), tok(DA_V), tok(SW_Q), tok(LRU_WIDTH), tok(LRU_WIDTH), tok(LRU_WIDTH),
                  pl.BlockSpec((1, 1, d), mod_map), _layer_spec(lw["w_out"].shape, lw["layer"])],
        out_specs=tok(d),
        scratch_shapes=[pltpu.VMEM((tm, d), BF16)],
        compiler_params=_cparams(2),
        name="output_projection",
    )(x, ya, yb, hf, hb, cg, gate, lw["w_out"])


def _ffn_kernel(xm_ref, xp_ref, xn_ref, sh_ref, sc_ref, g2_ref, ng_ref, wup_ref, cw_ref, cb_ref,
                wdn_ref, o_ref, h_ref, *, tm):
    j = pl.program_id(1)
    last = pl.num_programs(1) - 1
    gain, shift, scale = ng_ref[...], sh_ref[0], sc_ref[0]
    xm = xm_ref[0]
    hp = jnp.where(j > 0, _norm_modulate(xp_ref[0], gain, shift, scale), 0.0)
    hn = jnp.where(j < last, _norm_modulate(xn_ref[0], gain, shift, scale), 0.0)
    h_ref[0:FFN_HALO] = hp.astype(BF16)
    h_ref[FFN_HALO:FFN_HALO + tm] = _norm_modulate(xm, gain, shift, scale).astype(BF16)
    h_ref[FFN_HALO + tm:] = hn.astype(BF16)

    rows = tm + 2 * FFN_HALO

    def up(lo, hi):
        return (_dot(h_ref[...], wup_ref[:, lo:hi]),
                _dot(h_ref[FFN_HALO:FFN_HALO + tm], wup_ref[:, D_FF + lo:D_FF + hi]))

    acc = jnp.zeros((tm, D_MODEL), F32)
    nxt = up(*FFN_CHUNKS[0])
    for c, (lo, hi) in enumerate(FFN_CHUNKS):
        gp, val = nxt
        if c + 1 < len(FFN_CHUNKS):
            nxt = up(*FFN_CHUNKS[c + 1])
        g_prev = pltpu.roll(gp, 1, 0)[FFN_HALO:FFN_HALO + tm]
        g_next = pltpu.roll(gp, rows - 1, 0)[FFN_HALO:FFN_HALO + tm]
        cw = cw_ref[:, lo:hi]
        gate = (cb_ref[:, lo:hi] + g_prev * cw[0:1] + gp[FFN_HALO:FFN_HALO + tm] * cw[1:2]
                + g_next * cw[2:3])
        act = (gate * jax.nn.sigmoid(gate) * val).astype(BF16)
        acc = acc + _dot(act, wdn_ref[lo:hi, :])
    o_ref[0] = xm + g2_ref[0] * acc


def _conv_ffn(x, shift, scale, gate, lw):
    bsz, t_len, d = x.shape
    tm = min(TM, t_len)
    per_b = gate.shape[0] > 1
    mod_map = (lambda b, j: (b, 0, 0)) if per_b else (lambda b, j: (0, 0, 0))
    hb = tm // FFN_HALO
    n_hb = t_len // FFN_HALO
    prev_map = lambda b, j: (b, jnp.maximum(j * hb - 1, 0), 0)
    next_map = lambda b, j: (b, jnp.minimum((j + 1) * hb, n_hb - 1), 0)
    mod_spec = pl.BlockSpec((1, 1, d), mod_map)
    return pl.pallas_call(
        functools.partial(_ffn_kernel, tm=tm),
        out_shape=jax.ShapeDtypeStruct(x.shape, F32),
        grid=(bsz, t_len // tm),
        in_specs=[pl.BlockSpec((1, tm, d), lambda b, j: (b, j, 0)),
                  pl.BlockSpec((1, FFN_HALO, d), prev_map),
                  pl.BlockSpec((1, FFN_HALO, d), next_map),
                  mod_spec, mod_spec, mod_spec,
                  _const_spec((1, d)), _layer_spec(lw["w_up"].shape, lw["layer"]),
                  _const_spec((FFN_CONV, D_FF)), _const_spec((1, D_FF)),
                  _layer_spec(lw["w_down"].shape, lw["layer"])],
        out_specs=pl.BlockSpec((1, tm, d), lambda b, j: (b, j, 0)),
        scratch_shapes=[pltpu.VMEM((tm + 2 * FFN_HALO, d), BF16)],
        compiler_params=_cparams(2),
        name="conv_ffn",
    )(x, x, x, shift, scale, gate, lw["norm2_gain"], lw["w_up"], lw["ffn_conv_w"],
      lw["ffn_conv_b"], lw["w_down"])


def _sw_head_order():
    return [kv * SW_GROUP + g for g in range(SW_GROUP) for kv in range(SW_KV_HEADS)]


def _rope_tables(rows, head_dim, n_heads):
    row = np.repeat(np.arange(rows, dtype=np.float64), GRID_W)
    col = np.tile(np.arange(GRID_W, dtype=np.float64), rows)
    quarter = head_dim // 4
    inv_freq = ROPE_BASE ** (-np.arange(quarter, dtype=np.float64) / quarter)
    ang = np.concatenate([row[:, None] * inv_freq, col[:, None] * inv_freq], axis=-1)
    cos, sin = np.cos(ang), np.sin(ang)
    return (jnp.asarray(np.tile(np.concatenate([cos, cos], axis=-1), (1, n_heads)), dtype=F32),
            jnp.asarray(np.tile(np.concatenate([-sin, sin], axis=-1), (1, n_heads)), dtype=F32))


def _block_diag_mean(n, group):
    idx = np.arange(n) // group
    return jnp.asarray((idx[:, None] == idx[None, :]).astype(np.float32) / group, dtype=BF16)


def _block_diag(w):
    two, nb, bi, bj = w.shape
    eye = jnp.eye(nb, dtype=w.dtype)
    return jnp.einsum("dhij,hg->dhigj", w, eye).reshape(two, nb * bi, nb * bj)


def _prepare_weights(p):
    d = D_MODEL
    depth = p["w_in"].shape[0]
    split = np.cumsum((DA_Q, DA_Q, DA_V, SW_Q, SW_KV, SW_KV, LRU_WIDTH))
    w_aq, w_ak, w_av, w_bq, w_bk, w_bv, w_cx, w_cg = jnp.split(p["w_in"], split.tolist(), axis=2)
    w_av = jnp.pad(w_av.reshape(depth, d, DA_HEADS, DA_V_DIM),
                   ((0, 0), (0, 0), (0, 0), (0, LANES - DA_V_DIM))).reshape(depth, d, DA_VEXT)
    order = jnp.asarray(_sw_head_order())
    w_bq = w_bq.reshape(depth, d, SW_HEADS, HEAD_DIM)[:, :, order].reshape(depth, d, SW_Q)
    w_in = jnp.concatenate([w_aq, w_ak, w_av, w_bq, w_bk, w_bv, w_cx, w_cg], axis=2)

    w_out = p["w_out"]
    w_ob = w_out[:, DA_V:DA_V + SW_Q].reshape(depth, SW_HEADS, HEAD_DIM, d)[:, order]
    w_out = jnp.concatenate([w_out[:, :DA_V], w_ob.reshape(depth, SW_Q, d), w_out[:, DA_V + SW_Q:]],
                            axis=1)
    return dict(w_in=w_in.astype(BF16), w_out=w_out.astype(BF16),
                w_up=p["w_up"].astype(BF16), w_down=p["w_down"].astype(BF16))


def _prepare_layer(p, i):
    d = D_MODEL
    v_one = np.zeros((1, DA_VEXT), np.float32)
    v_one[0, DA_V_DIM::LANES] = 1.0

    da_hmask = np.zeros((2 * DA_HEADS, LANES), np.float32)
    for j in range(2 * DA_HEADS):
        off = (j % 4) * DA_QK_DIM
        da_hmask[j, off:off + DA_QK_DIM] = 1.0
    sw_hmask = np.zeros((SW_KV_HEADS, LANES), np.float32)
    for kv in range(SW_KV_HEADS):
        sw_hmask[kv, kv * HEAD_DIM:(kv + 1) * HEAD_DIM] = 1.0

    return dict(
        norm1_gain=p["norm1_gain"][i].reshape(1, d), norm2_gain=p["norm2_gain"][i].reshape(1, d),
        layer=i, v_one=jnp.asarray(v_one),
        gq_a=(jnp.tile(p["da_q_gain"][i], 2 * DA_HEADS).reshape(1, DA_Q)
              * (DA_QK_DIM ** -0.5 * math.log2(math.e))),
        gk_a=jnp.tile(p["da_k_gain"][i], 2 * DA_HEADS).reshape(1, DA_Q),
        gq_b=(jnp.tile(p["sw_q_gain"][i], SW_HEADS).reshape(1, SW_Q)
              * (HEAD_DIM ** -0.5 * math.log2(math.e))),
        gk_b=jnp.tile(p["sw_k_gain"][i], SW_KV_HEADS).reshape(1, SW_KV),
        g32=_block_diag_mean(MXU_DIM, DA_QK_DIM), g64=_block_diag_mean(MXU_DIM, HEAD_DIM),
        da_hmask=jnp.asarray(da_hmask, dtype=BF16),
        da_lamv=jnp.stack([p["da_lam_q1"][i], p["da_lam_k1"][i], p["da_lam_q2"][i], p["da_lam_k2"][i]]),
        da_sg=jnp.tile(p["da_sub_gain"][i], LANES // DA_V_DIM).reshape(1, LANES),
        sw_sink=p["sw_sink"][i] * math.log2(math.e), sw_hmask=jnp.asarray(sw_hmask, dtype=BF16),
        lru_conv_w=p["lru_conv_w"][i], lru_conv_b=p["lru_conv_b"][i].reshape(2, 1, LRU_WIDTH),
        lru_wa=_block_diag(p["lru_wa"][i]).astype(BF16), lru_ba=p["lru_ba"][i].reshape(2, 1, LRU_WIDTH),
        lru_wx=_block_diag(p["lru_wx"][i]).astype(BF16), lru_bx=p["lru_bx"][i].reshape(2, 1, LRU_WIDTH),
        lru_lambda=p["lru_lambda"][i].reshape(2, 1, LRU_WIDTH),
        ffn_conv_w=p["ffn_conv_w"][i], ffn_conv_b=p["ffn_conv_b"][i].reshape(1, D_FF),
    )


def kernel(x, c, ctx, c_ctx, w_mod, b_mod, norm1_gain, norm2_gain, w_in, da_q_gain, da_k_gain, da_lam_q1, da_lam_k1, da_lam_q2, da_lam_k2, da_sub_gain, sw_q_gain, sw_k_gain, sw_sink, lru_conv_w, lru_conv_b, lru_wa, lru_ba, lru_wx, lru_bx, lru_lambda, w_out, w_up, ffn_conv_w, ffn_conv_b, w_down):
    p = dict(w_in=w_in, norm1_gain=norm1_gain, norm2_gain=norm2_gain, da_q_gain=da_q_gain,
             da_k_gain=da_k_gain, da_lam_q1=da_lam_q1, da_lam_k1=da_lam_k1, da_lam_q2=da_lam_q2,
             da_lam_k2=da_lam_k2, da_sub_gain=da_sub_gain, sw_q_gain=sw_q_gain, sw_k_gain=sw_k_gain,
             sw_sink=sw_sink, lru_conv_w=lru_conv_w, lru_conv_b=lru_conv_b, lru_wa=lru_wa,
             lru_ba=lru_ba, lru_wx=lru_wx, lru_bx=lru_bx, lru_lambda=lru_lambda, w_out=w_out,
             w_up=w_up, ffn_conv_w=ffn_conv_w, ffn_conv_b=ffn_conv_b, w_down=w_down)
    bsz, n_tok, d = x.shape
    n_ctx = ctx.shape[1]
    depth = w_mod.shape[0]

    cc = jnp.zeros((2 * SUBLANES, d), F32).at[:bsz].set(c).at[bsz].set(c_ctx)
    mod_all = _modulation(cc, w_mod, b_mod)

    rope_lat = (_rope_tables(n_tok // GRID_W, DA_QK_DIM, 2 * DA_HEADS)
                + _rope_tables(n_tok // GRID_W, HEAD_DIM, SW_HEADS))
    rope_ctx = (jnp.ones((n_ctx, DA_Q), F32), jnp.zeros((n_ctx, DA_Q), F32),
                jnp.ones((n_ctx, SW_Q), F32), jnp.zeros((n_ctx, SW_Q), F32))

    weights = _prepare_weights(p)
    xc = ctx
    for i in range(depth):
        lw = {**_prepare_layer(p, i), **weights}
        lambda_init = 0.8 - 0.6 * math.exp(-0.3 * i)
        ctx_out = i < depth - 1
        mod = mod_all[i, :bsz].reshape(bsz, 1, N_MOD, d)
        sh1, sc1, g1, sh2, sc2, g2 = [mod[:, :, m] for m in range(N_MOD)]
        mod_c = mod_all[i, bsz].reshape(1, 1, N_MOD, d)
        sh1c, sc1c, g1c, sh2c, sc2c, g2c = [mod_c[:, :, m] for m in range(N_MOD)]

        qa, ka, va, qb, kb, vb, cx, cg = _input_projection(x, sh1, sc1, lw, rope_lat)
        qac, kac, vac, qbc, kbc, vbc, cxc, cgc = _input_projection(xc, sh1c, sc1c, lw, rope_ctx)

        ya = _diff_attention(qa, [(kac, vac), (ka, va)], lw, lambda_init)
        yb = _window_attention(qb, kbc, vbc, kb, vb, lw)
        hf, hb, hfc, hbc = _rglru_scans(cx, cxc, lw)

        x = _output_projection(x, ya, yb, hf, hb, cg, g1, lw)
        x = _conv_ffn(x, sh2, sc2, g2, lw)
        if ctx_out:
            yac = _diff_attention(qac, [(kac, vac)], lw, lambda_init)
            ybc = _window_attention(qbc, kbc, vbc, None, None, lw)
            xc = _output_projection(xc, yac, ybc, hfc, hbc, cgc, g1c, lw)
            xc = _conv_ffn(xc, sh2c, sc2c, g2c, lw)
    return x
```

```python
import functools
import math

import numpy as np
import jax
import jax.numpy as jnp
from jax import lax
from jax.experimental import pallas as pl
from jax.experimental.pallas import tpu as pltpu

F32 = jnp.float32
BF16 = jnp.bfloat16

D_MODEL = 1024
GRID_W = 64
N_MOD = 6
EPS = 1e-6
NEG_INF = -1e30
ROPE_BASE = 10000.0
DA_HEADS = 4
DA_QK_DIM = 32
DA_V_DIM = 64
HEAD_DIM = 64
SW_HEADS = 6
SW_KV_HEADS = 2
SW_GROUP = SW_HEADS // SW_KV_HEADS
WINDOW = 128
LRU_WIDTH = 384
LRU_BLOCKS = 6
LRU_BLOCK_DIM = LRU_WIDTH // LRU_BLOCKS
LRU_CONV = 4
LRU_C = 8.0
D_FF = 2816
FFN_CONV = 3

DA_Q = DA_HEADS * 2 * DA_QK_DIM
DA_V = DA_HEADS * DA_V_DIM
SW_Q = SW_HEADS * HEAD_DIM
SW_KV = SW_KV_HEADS * HEAD_DIM

LANES = 128
SUBLANES = 8
BF16_ROWS = 16
MXU_DIM = 256
VMEM_LIMIT = 56 * 1024 * 1024

DA_VEXT = DA_HEADS * LANES
DA_VROWS = DA_V_DIM + BF16_ROWS
C_AQ = 0
C_AK = C_AQ + DA_Q
C_AV = C_AK + DA_Q
C_BQ = C_AV + DA_VEXT
C_BK = C_BQ + SW_Q
C_BV = C_BK + SW_KV
C_CX = C_BV + SW_KV
C_CG = C_CX + LRU_WIDTH
D_INX = C_CG + LRU_WIDTH

TM = 512
FFN_HALO = BF16_ROWS
FFN_CHUNKS = ((0, 768), (768, 1536), (1536, 2304), (2304, D_FF))
DA_TQ = 256
DA_AHEAD = 3
DA_SUB = 4
TM_IN = 1024
IN_SUB = 256
TM_OUT = 1024
SW_AHEAD = 2
SW_TQ = 256
SW_SUB = 4
LRU_TC = 256


def _cparams(n_axes):
    return pltpu.CompilerParams(dimension_semantics=("arbitrary",) * n_axes,
                                vmem_limit_bytes=VMEM_LIMIT)


def _const_spec(shape):
    nd = len(shape)
    return pl.BlockSpec(shape, lambda *_: (0,) * nd, pipeline_mode=pl.Buffered(1))


def _layer_spec(shape, layer):
    nd = len(shape)
    return pl.BlockSpec((None,) + tuple(shape[1:]), lambda *_: (layer,) + (0,) * (nd - 1),
                        pipeline_mode=pl.Buffered(1))


def _nt_dot(a, b):
    return lax.dot_general(a, b, (((1,), (1,)), ((), ())), preferred_element_type=F32)


def _dot(a, b):
    return jnp.dot(a, b, preferred_element_type=F32)


def _mod_kernel(c_ref, w_ref, b_ref, o_ref):
    c = c_ref[...]
    s = c * jax.nn.sigmoid(c)
    o_ref[0] = _dot(s.astype(BF16), w_ref[0].astype(BF16)) + b_ref[0]


def _modulation(cc, w_mod, b_mod):
    depth, d, n = w_mod.shape
    tn = 1536
    return pl.pallas_call(
        _mod_kernel,
        out_shape=jax.ShapeDtypeStruct((depth, cc.shape[0], n), F32),
        grid=(depth, n // tn),
        in_specs=[pl.BlockSpec(cc.shape, lambda i, j: (0, 0)),
                  pl.BlockSpec((1, d, tn), lambda i, j: (i, 0, j)),
                  pl.BlockSpec((1, 1, tn), lambda i, j: (i, 0, j))],
        out_specs=pl.BlockSpec((1, cc.shape[0], tn), lambda i, j: (i, 0, j)),
        compiler_params=_cparams(2),
        name="modulation",
    )(cc, w_mod, b_mod.reshape(depth, 1, n))


def _norm_modulate(x, gain, shift, scale):
    ms = jnp.mean(x * x, axis=-1, keepdims=True)
    h = x * lax.rsqrt(ms + EPS) * gain
    return h * (1.0 + scale) + shift


def _group_rms(a, g_ref):
    sq = (a * a).astype(BF16)
    n = a.shape[1]
    ms = [_dot(sq[:, lo:min(lo + MXU_DIM, n)], g_ref[:min(MXU_DIM, n - lo), :min(MXU_DIM, n - lo)])
          for lo in range(0, n, MXU_DIM)]
    ms = ms[0] if len(ms) == 1 else jnp.concatenate(ms, axis=1)
    return a * lax.rsqrt(ms + EPS)


def _rope(y, cos, sin_signed, half):
    n = y.shape[1]
    lane = lax.broadcasted_iota(jnp.int32, y.shape, 1)
    first = (lane % (2 * half)) < half
    partner = jnp.where(first, pltpu.roll(y, n - half, 1), pltpu.roll(y, half, 1))
    return y * cos + partner * sin_signed


def _inproj_kernel(x_ref, sh_ref, sc_ref, ng_ref, w_ref, vone_ref,
                   cosa_ref, sina_ref, cosb_ref, sinb_ref,
                   gqa_ref, gka_ref, gqb_ref, gkb_ref, g32_ref, g64_ref,
                   qa_ref, ka_ref, va_ref, qb_ref, kb_ref, vb_ref, cx_ref, cg_ref):
    slab = va_ref.shape[3]
    sub = min(IN_SUB, slab)
    n_sub = x_ref.shape[1] // sub

    def project(u):
        rows = slice(u * sub, (u + 1) * sub)
        h = _norm_modulate(x_ref[0, rows], ng_ref[...], sh_ref[0], sc_ref[0]).astype(BF16)
        return _dot(h, w_ref[...])

    def finish(u, full):
        rows = slice(u * sub, (u + 1) * sub)

        def proj(lo, hi):
            return full[:, lo:hi]

        cosa, sina = cosa_ref[rows], sina_ref[rows]
        aq = _group_rms(proj(C_AQ, C_AK), g32_ref) * gqa_ref[...]
        qa_ref[0, rows] = _rope(aq, cosa, sina, DA_QK_DIM // 2).astype(BF16)
        ak = _group_rms(proj(C_AK, C_AV), g32_ref) * gka_ref[...]
        ka_ref[0, rows] = _rope(ak, cosa, sina, DA_QK_DIM // 2).astype(BF16)
        va_t = (proj(C_AV, C_BQ) + vone_ref[...]).T.astype(BF16)
        col = (u * sub) % slab
        for hd in range(DA_HEADS):
            va_ref[0, (u * sub) // slab, hd * DA_VROWS:(hd + 1) * DA_VROWS, col:col + sub] = (
                va_t[hd * LANES:hd * LANES + DA_VROWS])

        cosb, sinb = cosb_ref[rows], sinb_ref[rows]
        bq = _group_rms(proj(C_BQ, C_BK), g64_ref) * gqb_ref[...]
        qb_ref[0, rows] = _rope(bq, cosb, sinb, HEAD_DIM // 2).astype(BF16)
        bkv = proj(C_BK, C_CX)
        bk = _group_rms(bkv[:, :SW_KV], g64_ref) * gkb_ref[...]
        kb_ref[0, rows] = _rope(bk, cosb[:, :SW_KV], sinb[:, :SW_KV], HEAD_DIM // 2).astype(BF16)
        vb_ref[0, rows] = bkv[:, SW_KV:].astype(BF16)

        cxg = proj(C_CX, D_INX)
        cx_ref[0, rows] = cxg[:, :LRU_WIDTH]
        cg_ref[0, rows] = cxg[:, LRU_WIDTH:].astype(cg_ref.dtype)

    full = project(0)
    for u in range(n_sub):
        nxt = project(u + 1) if u + 1 < n_sub else None
        finish(u, full)
        full = nxt


def _input_projection(x, shift, scale, lw, rope):
    bsz, t_len, d = x.shape
    tm = min(TM_IN, t_len)
    sub = min(TM, tm)
    per_b = shift.shape[0] > 1
    mod_map = (lambda b, j: (b, 0, 0)) if per_b else (lambda b, j: (0, 0, 0))
    tok = lambda n: pl.BlockSpec((1, tm, n), lambda b, j: (b, j, 0))
    tab = lambda n: pl.BlockSpec((tm, n), lambda b, j: (j, 0))
    widths = (DA_Q, DA_Q, DA_VEXT, SW_Q, SW_KV, SW_KV, LRU_WIDTH, LRU_WIDTH)
    dtypes = (BF16,) * 6 + (F32, BF16)
    out_shape = [jax.ShapeDtypeStruct((bsz, t_len, n), dt) for n, dt in zip(widths, dtypes)]
    out_specs = [tok(n) for n in widths]
    out_shape[2] = jax.ShapeDtypeStruct((bsz, t_len // sub, DA_HEADS * DA_VROWS, sub), BF16)
    out_specs[2] = pl.BlockSpec((1, tm // sub, DA_HEADS * DA_VROWS, sub), lambda b, j: (b, j, 0, 0))
    return pl.pallas_call(
        _inproj_kernel,
        out_shape=tuple(out_shape),
        grid=(bsz, t_len // tm),
        in_specs=[tok(d),
                  pl.BlockSpec((1, 1, d), mod_map), pl.BlockSpec((1, 1, d), mod_map),
                  _const_spec((1, d)), _layer_spec(lw["w_in"].shape, lw["layer"]), _const_spec((1, DA_VEXT)),
                  tab(DA_Q), tab(DA_Q), tab(SW_Q), tab(SW_Q),
                  _const_spec((1, DA_Q)), _const_spec((1, DA_Q)),
                  _const_spec((1, SW_Q)), _const_spec((1, SW_KV)),
                  _const_spec((MXU_DIM, MXU_DIM)), _const_spec((MXU_DIM, MXU_DIM))],
        out_specs=tuple(out_specs),
        compiler_params=_cparams(2),
        name="input_projection",
    )(x, shift, scale, lw["norm1_gain"], lw["w_in"], lw["v_one"], *rope,
      lw["gq_a"], lw["gk_a"], lw["gq_b"], lw["gk_b"], lw["g32"], lw["g64"])


def _da_kernel(*refs, n_seg, lambda_init):
    q_ref = refs[0]
    kv_refs = refs[1:1 + 2 * n_seg]
    hmask_ref, lamv_ref, sg_ref, o_ref, m_ref, acc_ref = refs[1 + 2 * n_seg:]
    n_map = 2 * DA_HEADS
    tq = m_ref.shape[2]
    n_sub = q_ref.shape[1] // tq

    m_ref[...] = jnp.full(m_ref.shape, NEG_INF, F32)
    acc_ref[...] = jnp.zeros(acc_ref.shape, F32)

    items = [(kv_refs[2 * si], kv_refs[2 * si + 1], i, u, j)
             for si in range(n_seg) for i in range(kv_refs[2 * si + 1].shape[1])
             for u in range(n_sub) for j in range(n_map)]

    def scores(k_ref, v_ref, i, u, j):
        half, tk = j // 4, v_ref.shape[3]
        qm = (q_ref[0, u * tq:(u + 1) * tq, half * LANES:(half + 1) * LANES]
              * hmask_ref[j:j + 1, :])
        kblk = k_ref[0, i * tk:(i + 1) * tk, half * LANES:(half + 1) * LANES]
        return _nt_dot(kblk, qm)

    pending = {n: scores(*items[n]) for n in range(min(DA_AHEAD, len(items)))}
    for n, (k_ref, v_ref, i, u, j) in enumerate(items):
        s_b = pending.pop(n).astype(BF16)
        if n + DA_AHEAD < len(items):
            pending[n + DA_AHEAD] = scores(*items[n + DA_AHEAD])
        st = u * n_map + j
        m_old = m_ref[st, 0:1, :]
        m_new = jnp.maximum(m_old, jnp.max(s_b, axis=0, keepdims=True).astype(F32))
        p_t = jnp.exp2(s_b - m_new.astype(BF16))
        pv = _dot(v_ref[0, i, (j // 2) * DA_VROWS:(j // 2 + 1) * DA_VROWS, :], p_t)
        acc_ref[st] = jnp.exp2(m_old - m_new) * acc_ref[st] + pv
        m_ref[st] = jnp.broadcast_to(m_new, m_ref.shape[1:])

    lv = lamv_ref[...]
    lam = (jnp.exp(jnp.sum(lv[0:1] * lv[1:2], axis=1, keepdims=True))
           - jnp.exp(jnp.sum(lv[2:3] * lv[3:4], axis=1, keepdims=True)) + lambda_init)

    def head_out(u, h):
        a1, a2 = acc_ref[u * n_map + 2 * h], acc_ref[u * n_map + 2 * h + 1]
        o = (a1 / a1[DA_V_DIM:DA_V_DIM + 1] - lam * (a2 / a2[DA_V_DIM:DA_V_DIM + 1]))[:DA_V_DIM]
        ms = jnp.sum(o * o, axis=0, keepdims=True) * (1.0 / DA_V_DIM)
        return o * lax.rsqrt(ms + EPS)

    for u in range(n_sub):
        for c in range(DA_HEADS // 2):
            pair_t = jnp.concatenate([head_out(u, 2 * c), head_out(u, 2 * c + 1)], axis=0)
            o_ref[0, u * tq:(u + 1) * tq, c * LANES:(c + 1) * LANES] = (
                pair_t.T * sg_ref[...]).astype(o_ref.dtype)


def _diff_attention(q, kvs, lw, lambda_init):
    bsz, t_q, _ = q.shape
    tq = min(DA_TQ, t_q)
    n_sub = min(DA_SUB, t_q // tq)
    tb = n_sub * tq
    kv_specs, kv_args = [], []
    for k, v in kvs:
        kv_specs += [pl.BlockSpec((1,) + k.shape[1:], lambda b, j: (b, 0, 0)),
                     pl.BlockSpec((1,) + v.shape[1:], lambda b, j: (b, 0, 0, 0))]
        kv_args += [k, v]
    sg = lw["da_sg"] * (1.0 - lambda_init)
    return pl.pallas_call(
        functools.partial(_da_kernel, n_seg=len(kvs), lambda_init=lambda_init),
        out_shape=jax.ShapeDtypeStruct((bsz, t_q, DA_V), BF16),
        grid=(bsz, t_q // tb),
        in_specs=[pl.BlockSpec((1, tb, DA_Q), lambda b, j: (b, j, 0))] + kv_specs
                 + [_const_spec((2 * DA_HEADS, LANES)), _const_spec((4, DA_QK_DIM)),
                    _const_spec((1, LANES))],
        out_specs=pl.BlockSpec((1, tb, DA_V), lambda b, j: (b, j, 0)),
        scratch_shapes=[pltpu.VMEM((n_sub * 2 * DA_HEADS, SUBLANES, tq), F32),
                        pltpu.VMEM((n_sub * 2 * DA_HEADS, DA_VROWS, tq), F32)],
        compiler_params=_cparams(2),
        name="diff_attention",
    )(q, *kv_args, lw["da_hmask"], lw["da_lamv"], sg)


def _sw_kernel(*refs, has_lat, tq, t_len):
    if has_lat:
        sink_ref, q_ref, kc_ref, vc_ref, k_ref, v_ref, hm_ref, o_ref = refs
    else:
        sink_ref, q_ref, kc_ref, vc_ref, hm_ref, o_ref = refs
    def values_t(v):
        n_k = v.shape[0]
        v_t = v.astype(F32).T
        tail = (lax.broadcasted_iota(jnp.int32, (BF16_ROWS, n_k), 0) == 0).astype(F32)
        return [jnp.concatenate([v_t[kv * HEAD_DIM:(kv + 1) * HEAD_DIM], tail], axis=0).astype(BF16)
                for kv in range(SW_KV_HEADS)]

    kc = kc_ref[0]
    vc_t = values_t(vc_ref[0])
    n_sub = q_ref.shape[1] // tq
    bands = []
    for u in range(n_sub if has_lat else 0):
        band = tq + 2 * WINDOW
        q0 = (pl.program_id(1) * n_sub + u) * tq
        start = pl.multiple_of(jnp.clip(q0 - WINDOW, 0, t_len - band), WINDOW)
        qpos = q0 + lax.broadcasted_iota(jnp.int32, (1, tq), 1)
        kpos = start + lax.broadcasted_iota(jnp.int32, (band, 1), 0)
        bands.append((k_ref[0, pl.ds(start, band), :], values_t(v_ref[0, pl.ds(start, band), :]),
                      jnp.abs(qpos - kpos) <= WINDOW))

    combos = [(u, g, kv) for u in range(n_sub) for g in range(SW_GROUP)
              for kv in range(SW_KV_HEADS)]

    def scores(c):
        u, g, kv = combos[c]
        qm = q_ref[0, u * tq:(u + 1) * tq, g * LANES:(g + 1) * LANES] * hm_ref[kv:kv + 1, :]
        s_lat = _nt_dot(bands[u][0], qm) if has_lat else None
        return _nt_dot(kc, qm), s_lat

    pending = {c: scores(c) for c in range(SW_AHEAD)}
    outs = {}
    for c, (u, g, kv) in enumerate(combos):
        s_ctx, s_lat = pending.pop(c)
        if c + SW_AHEAD < len(combos):
            pending[c + SW_AHEAD] = scores(c + SW_AHEAD)
        sink = sink_ref[kv * SW_GROUP + g]
        s_ctx = s_ctx.astype(BF16)
        m = jnp.maximum(jnp.max(s_ctx, axis=0, keepdims=True).astype(F32), sink)
        if has_lat:
            s_lat = jnp.where(bands[u][2], s_lat.astype(BF16), NEG_INF)
            m = jnp.maximum(m, jnp.max(s_lat, axis=0, keepdims=True).astype(F32))
        m_b = m.astype(BF16)
        m = m_b.astype(F32)
        acc = _dot(vc_t[kv], jnp.exp2(s_ctx - m_b))
        if has_lat:
            acc = acc + _dot(bands[u][1][kv], jnp.exp2(s_lat - m_b))
        l = acc[HEAD_DIM:HEAD_DIM + 1] + jnp.exp2(sink - m)
        outs[kv] = acc[:HEAD_DIM] / l
        if kv == SW_KV_HEADS - 1:
            pair_t = jnp.concatenate([outs.pop(k2) for k2 in range(SW_KV_HEADS)], axis=0)
            o_ref[0, u * tq:(u + 1) * tq, g * LANES:(g + 1) * LANES] = pair_t.T.astype(o_ref.dtype)


def _window_attention(q, kc, vc, k, v, lw):
    bsz, t_q, _ = q.shape
    has_lat = k is not None
    tq = min(SW_TQ, t_q)
    tb = min(SW_SUB * tq, t_q)
    n_ctx = kc.shape[1]
    full = lambda t: pl.BlockSpec((1, t, SW_KV), lambda b, j: (b, 0, 0))
    specs = [pl.BlockSpec(memory_space=pltpu.SMEM),
             pl.BlockSpec((1, tb, SW_Q), lambda b, j: (b, j, 0)), full(n_ctx), full(n_ctx)]
    args = [lw["sw_sink"], q, kc, vc]
    if has_lat:
        specs += [full(t_q), full(t_q)]
        args += [k, v]
    specs.append(_const_spec((SW_KV_HEADS, LANES)))
    args.append(lw["sw_hmask"])
    return pl.pallas_call(
        functools.partial(_sw_kernel, has_lat=has_lat, tq=tq, t_len=t_q),
        out_shape=jax.ShapeDtypeStruct((bsz, t_q, SW_Q), BF16),
        grid=(bsz, t_q // tb),
        in_specs=specs,
        out_specs=pl.BlockSpec((1, tb, SW_Q), lambda b, j: (b, j, 0)),
        compiler_params=_cparams(2),
        name="window_attention",
    )(*args)


def _lru_kernel(xf_ref, xb_ref, xc_ref, cw_ref, cb_ref, wa_ref, ba_ref, wx_ref, bx_ref, lam_ref,
                hf_ref, hb_ref, hfc_ref, hbc_ref, halo_ref, carry_ref, *, tc):
    j = pl.program_id(1)
    is_ctx = j == 0
    sub = lax.broadcasted_iota(jnp.int32, (1, SUBLANES, 1), 1)

    for d, x_lat_ref, out_ref, outc_ref in ((0, xf_ref, hf_ref, hfc_ref), (1, xb_ref, hb_ref, hbc_ref)):
        x = jnp.where(is_ctx, xc_ref[0], x_lat_ref[0])
        halo = jnp.where(j <= 1, 0.0, halo_ref[d])
        cw = cw_ref[d]
        n_grp = tc // SUBLANES
        x3 = x.reshape(n_grp, SUBLANES, x.shape[1])
        xc3 = cb_ref[d] + x3 * cw[LRU_CONV - 1:LRU_CONV]
        for k in range(LRU_CONV - 1):
            s = LRU_CONV - 1 - k
            if d == 0:
                rolled = pltpu.roll(x3, s, 1)
                other = jnp.concatenate([pltpu.roll(halo, s, 0)[None], rolled[:-1]], axis=0)
                xs = jnp.where(sub < s, other, rolled)
            else:
                rolled = pltpu.roll(x3, SUBLANES - s, 1)
                other = jnp.concatenate([rolled[1:], pltpu.roll(halo, SUBLANES - s, 0)[None]], axis=0)
                xs = jnp.where(sub >= SUBLANES - s, other, rolled)
            xc3 = xc3 + xs * cw[k:k + 1]
        xc = xc3.reshape(tc, x.shape[1])
        halo_ref[d] = x[tc - SUBLANES:] if d == 0 else x[:SUBLANES]

        xcb = xc.astype(BF16)
        r = jax.nn.sigmoid(_dot(xcb, wa_ref[d]) + ba_ref[d])
        gi = jax.nn.sigmoid(_dot(xcb, wx_ref[d]) + bx_ref[d])
        nl = -lam_ref[d]
        softplus = jnp.maximum(nl, 0.0) + jnp.log1p(jnp.exp(-jnp.abs(nl)))
        log_a = -LRU_C * r * softplus
        a = jnp.exp(log_a)
        b = jnp.sqrt(-jnp.tanh(log_a) * (a * a + 1.0)) * (gi * xc)

        a3 = a.reshape(n_grp, SUBLANES, a.shape[1])
        b3 = b.reshape(n_grp, SUBLANES, b.shape[1])
        step = 1
        while step < SUBLANES:
            keep = (sub < step) if d == 0 else (sub >= SUBLANES - step)
            shift = step if d == 0 else SUBLANES - step
            a_s = jnp.where(keep, 1.0, pltpu.roll(a3, shift, 1))
            b_s = jnp.where(keep, 0.0, pltpu.roll(b3, shift, 1))
            b3 = a3 * b_s + b3
            a3 = a3 * a_s
            step *= 2

        edge_row = SUBLANES - 1 if d == 0 else 0
        h_in = jnp.where(is_ctx, 0.0, carry_ref[d, edge_row:edge_row + 1, :])
        groups = [None] * n_grp
        for gi in (range(n_grp) if d == 0 else range(n_grp - 1, -1, -1)):
            groups[gi] = a3[gi] * h_in + b3[gi]
            h_in = groups[gi][edge_row:edge_row + 1]
        h = jnp.concatenate(groups, axis=0)
        carry_ref[d] = groups[n_grp - 1] if d == 0 else groups[0]

        h_out = h.astype(out_ref.dtype)

        @pl.when(is_ctx)
        def _():
            outc_ref[0] = h_out

        @pl.when(jnp.logical_not(is_ctx))
        def _():
            out_ref[0] = h_out


def _rglru_scans(cx, cxc, lw):
    bsz, t_len, n = cx.shape
    tc = LRU_TC
    assert cxc.shape[1] == tc and t_len % tc == 0
    n_lat = t_len // tc
    fwd = lambda b, j: (b, jnp.maximum(j - 1, 0), 0)
    bwd = lambda b, j: (b, n_lat - jnp.maximum(j, 1), 0)
    ctx = lambda b, j: (b, 0, 0)
    blk = lambda m: pl.BlockSpec((1, tc, n), m)
    return pl.pallas_call(
        functools.partial(_lru_kernel, tc=tc),
        out_shape=(jax.ShapeDtypeStruct(cx.shape, BF16), jax.ShapeDtypeStruct(cx.shape, BF16),
                   jax.ShapeDtypeStruct(cxc.shape, BF16), jax.ShapeDtypeStruct(cxc.shape, BF16)),
        grid=(bsz, n_lat + 1),
        in_specs=[blk(fwd), blk(bwd), blk(ctx),
                  _const_spec((2, LRU_CONV, n)), _const_spec((2, 1, n)),
                  _const_spec((2, n, n)), _const_spec((2, 1, n)),
                  _const_spec((2, n, n)), _const_spec((2, 1, n)), _const_spec((2, 1, n))],
        out_specs=(blk(fwd), blk(bwd), blk(ctx), blk(ctx)),
        scratch_shapes=[pltpu.VMEM((2, SUBLANES, n), F32), pltpu.VMEM((2, SUBLANES, n), F32)],
        compiler_params=_cparams(2),
        name="rglru_scans",
    )(cx, cx, cxc, lw["lru_conv_w"], lw["lru_conv_b"], lw["lru_wa"], lw["lru_ba"],
      lw["lru_wx"], lw["lru_bx"], lw["lru_lambda"])


def _gelu_tanh(x):
    return 0.5 * x * (1.0 + jnp.tanh(math.sqrt(2.0 / math.pi) * (x + 0.044715 * (x * x * x))))


def _outproj_kernel(x_ref, ya_ref, yb_ref, hf_ref, hb_ref, cg_ref, g1_ref, w_ref, o_ref, cat_ref):
    cat_ref[:, 0:DA_V] = ya_ref[0]
    cat_ref[:, DA_V:DA_V + SW_Q] = yb_ref[0]
    yc = (hf_ref[0].astype(F32) + hb_ref[0].astype(F32)) * _gelu_tanh(cg_ref[0].astype(F32))
    cat_ref[:, DA_V + SW_Q:] = yc.astype(BF16)
    o_ref[0] = x_ref[0] + g1_ref[0] * _dot(cat_ref[...], w_ref[...])


def _output_projection(x, ya, yb, hf, hb, cg, gate, lw):
    bsz, t_len, d = x.shape
    tm = min(TM_OUT, t_len)
    per_b = gate.shape[0] > 1
    mod_map = (lambda b, j: (b, 0, 0)) if per_b else (lambda b, j: (0, 0, 0))
    tok = lambda n: pl.BlockSpec((1, tm, n), lambda b, j: (b, j, 0))
    return pl.pallas_call(
        _outproj_kernel,
        out_shape=jax.ShapeDtypeStruct(x.shape, F32),
        grid=(bsz, t_len // tm),
        in_specs=[tok(d), tok(DA_V), tok(SW_Q), tok(LRU_WIDTH), tok(LRU_WIDTH), tok(LRU_WIDTH),
                  pl.BlockSpec((1, 1, d), mod_map), _layer_spec(lw["w_out"].shape, lw["layer"])],
        out_specs=tok(d),
        scratch_shapes=[pltpu.VMEM((tm, d), BF16)],
        compiler_params=_cparams(2),
        name="output_projection",
    )(x, ya, yb, hf, hb, cg, gate, lw["w_out"])


def _ffn_kernel(xm_ref, xp_ref, xn_ref, sh_ref, sc_ref, g2_ref, ng_ref, wup_ref, cw_ref, cb_ref,
                wdn_ref, o_ref, h_ref, *, tm):
    j = pl.program_id(1)
    last = pl.num_programs(1) - 1
    gain, shift, scale = ng_ref[...], sh_ref[0], sc_ref[0]
    xm = xm_ref[0]
    hp = jnp.where(j > 0, _norm_modulate(xp_ref[0], gain, shift, scale), 0.0)
    hn = jnp.where(j < last, _norm_modulate(xn_ref[0], gain, shift, scale), 0.0)
    h_ref[0:FFN_HALO] = hp.astype(BF16)
    h_ref[FFN_HALO:FFN_HALO + tm] = _norm_modulate(xm, gain, shift, scale).astype(BF16)
    h_ref[FFN_HALO + tm:] = hn.astype(BF16)

    rows = tm + 2 * FFN_HALO

    def up(lo, hi):
        return (_dot(h_ref[...], wup_ref[:, lo:hi]),
                _dot(h_ref[FFN_HALO:FFN_HALO + tm], wup_ref[:, D_FF + lo:D_FF + hi]))

    acc = jnp.zeros((tm, D_MODEL), F32)
    nxt = up(*FFN_CHUNKS[0])
    for c, (lo, hi) in enumerate(FFN_CHUNKS):
        gp, val = nxt
        if c + 1 < len(FFN_CHUNKS):
            nxt = up(*FFN_CHUNKS[c + 1])
        g_prev = pltpu.roll(gp, 1, 0)[FFN_HALO:FFN_HALO + tm]
        g_next = pltpu.roll(gp, rows - 1, 0)[FFN_HALO:FFN_HALO + tm]
        cw = cw_ref[:, lo:hi]
        gate = (cb_ref[:, lo:hi] + g_prev * cw[0:1] + gp[FFN_HALO:FFN_HALO + tm] * cw[1:2]
                + g_next * cw[2:3])
        act = (gate * jax.nn.sigmoid(gate) * val).astype(BF16)
        acc = acc + _dot(act, wdn_ref[lo:hi, :])
    o_ref[0] = xm + g2_ref[0] * acc


def _conv_ffn(x, shift, scale, gate, lw):
    bsz, t_len, d = x.shape
    tm = min(TM, t_len)
    per_b = gate.shape[0] > 1
    mod_map = (lambda b, j: (b, 0, 0)) if per_b else (lambda b, j: (0, 0, 0))
    hb = tm // FFN_HALO
    n_hb = t_len // FFN_HALO
    prev_map = lambda b, j: (b, jnp.maximum(j * hb - 1, 0), 0)
    next_map = lambda b, j: (b, jnp.minimum((j + 1) * hb, n_hb - 1), 0)
    mod_spec = pl.BlockSpec((1, 1, d), mod_map)
    return pl.pallas_call(
        functools.partial(_ffn_kernel, tm=tm),
        out_shape=jax.ShapeDtypeStruct(x.shape, F32),
        grid=(bsz, t_len // tm),
        in_specs=[pl.BlockSpec((1, tm, d), lambda b, j: (b, j, 0)),
                  pl.BlockSpec((1, FFN_HALO, d), prev_map),
                  pl.BlockSpec((1, FFN_HALO, d), next_map),
                  mod_spec, mod_spec, mod_spec,
                  _const_spec((1, d)), _layer_spec(lw["w_up"].shape, lw["layer"]),
                  _const_spec((FFN_CONV, D_FF)), _const_spec((1, D_FF)),
                  _layer_spec(lw["w_down"].shape, lw["layer"])],
        out_specs=pl.BlockSpec((1, tm, d), lambda b, j: (b, j, 0)),
        scratch_shapes=[pltpu.VMEM((tm + 2 * FFN_HALO, d), BF16)],
        compiler_params=_cparams(2),
        name="conv_ffn",
    )(x, x, x, shift, scale, gate, lw["norm2_gain"], lw["w_up"], lw["ffn_conv_w"],
      lw["ffn_conv_b"], lw["w_down"])


def _sw_head_order():
    return [kv * SW_GROUP + g for g in range(SW_GROUP) for kv in range(SW_KV_HEADS)]


def _rope_tables(rows, head_dim, n_heads):
    row = np.repeat(np.arange(rows, dtype=np.float64), GRID_W)
    col = np.tile(np.arange(GRID_W, dtype=np.float64), rows)
    quarter = head_dim // 4
    inv_freq = ROPE_BASE ** (-np.arange(quarter, dtype=np.float64) / quarter)
    ang = np.concatenate([row[:, None] * inv_freq, col[:, None] * inv_freq], axis=-1)
    cos, sin = np.cos(ang), np.sin(ang)
    return (jnp.asarray(np.tile(np.concatenate([cos, cos], axis=-1), (1, n_heads)), dtype=F32),
            jnp.asarray(np.tile(np.concatenate([-sin, sin], axis=-1), (1, n_heads)), dtype=F32))


def _block_diag_mean(n, group):
    idx = np.arange(n) // group
    return jnp.asarray((idx[:, None] == idx[None, :]).astype(np.float32) / group, dtype=BF16)


def _block_diag(w):
    two, nb, bi, bj = w.shape
    eye = jnp.eye(nb, dtype=w.dtype)
    return jnp.einsum("dhij,hg->dhigj", w, eye).reshape(two, nb * bi, nb * bj)


def _prepare_weights(p):
    d = D_MODEL
    depth = p["w_in"].shape[0]
    split = np.cumsum((DA_Q, DA_Q, DA_V, SW_Q, SW_KV, SW_KV, LRU_WIDTH))
    w_aq, w_ak, w_av, w_bq, w_bk, w_bv, w_cx, w_cg = jnp.split(p["w_in"], split.tolist(), axis=2)
    w_av = jnp.pad(w_av.reshape(depth, d, DA_HEADS, DA_V_DIM),
                   ((0, 0), (0, 0), (0, 0), (0, LANES - DA_V_DIM))).reshape(depth, d, DA_VEXT)
    order = jnp.asarray(_sw_head_order())
    w_bq = w_bq.reshape(depth, d, SW_HEADS, HEAD_DIM)[:, :, order].reshape(depth, d, SW_Q)
    w_in = jnp.concatenate([w_aq, w_ak, w_av, w_bq, w_bk, w_bv, w_cx, w_cg], axis=2)

    w_out = p["w_out"]
    w_ob = w_out[:, DA_V:DA_V + SW_Q].reshape(depth, SW_HEADS, HEAD_DIM, d)[:, order]
    w_out = jnp.concatenate([w_out[:, :DA_V], w_ob.reshape(depth, SW_Q, d), w_out[:, DA_V + SW_Q:]],
                            axis=1)
    return dict(w_in=w_in.astype(BF16), w_out=w_out.astype(BF16),
                w_up=p["w_up"].astype(BF16), w_down=p["w_down"].astype(BF16))


def _prepare_layer(p, i):
    d = D_MODEL
    v_one = np.zeros((1, DA_VEXT), np.float32)
    v_one[0, DA_V_DIM::LANES] = 1.0

    da_hmask = np.zeros((2 * DA_HEADS, LANES), np.float32)
    for j in range(2 * DA_HEADS):
        off = (j % 4) * DA_QK_DIM
        da_hmask[j, off:off + DA_QK_DIM] = 1.0
    sw_hmask = np.zeros((SW_KV_HEADS, LANES), np.float32)
    for kv in range(SW_KV_HEADS):
        sw_hmask[kv, kv * HEAD_DIM:(kv + 1) * HEAD_DIM] = 1.0

    return dict(
        norm1_gain=p["norm1_gain"][i].reshape(1, d), norm2_gain=p["norm2_gain"][i].reshape(1, d),
        layer=i, v_one=jnp.asarray(v_one),
        gq_a=(jnp.tile(p["da_q_gain"][i], 2 * DA_HEADS).reshape(1, DA_Q)
              * (DA_QK_DIM ** -0.5 * math.log2(math.e))),
        gk_a=jnp.tile(p["da_k_gain"][i], 2 * DA_HEADS).reshape(1, DA_Q),
        gq_b=(jnp.tile(p["sw_q_gain"][i], SW_HEADS).reshape(1, SW_Q)
              * (HEAD_DIM ** -0.5 * math.log2(math.e))),
        gk_b=jnp.tile(p["sw_k_gain"][i], SW_KV_HEADS).reshape(1, SW_KV),
        g32=_block_diag_mean(MXU_DIM, DA_QK_DIM), g64=_block_diag_mean(MXU_DIM, HEAD_DIM),
        da_hmask=jnp.asarray(da_hmask, dtype=BF16),
        da_lamv=jnp.stack([p["da_lam_q1"][i], p["da_lam_k1"][i], p["da_lam_q2"][i], p["da_lam_k2"][i]]),
        da_sg=jnp.tile(p["da_sub_gain"][i], LANES // DA_V_DIM).reshape(1, LANES),
        sw_sink=p["sw_sink"][i] * math.log2(math.e), sw_hmask=jnp.asarray(sw_hmask, dtype=BF16),
        lru_conv_w=p["lru_conv_w"][i], lru_conv_b=p["lru_conv_b"][i].reshape(2, 1, LRU_WIDTH),
        lru_wa=_block_diag(p["lru_wa"][i]).astype(BF16), lru_ba=p["lru_ba"][i].reshape(2, 1, LRU_WIDTH),
        lru_wx=_block_diag(p["lru_wx"][i]).astype(BF16), lru_bx=p["lru_bx"][i].reshape(2, 1, LRU_WIDTH),
        lru_lambda=p["lru_lambda"][i].reshape(2, 1, LRU_WIDTH),
        ffn_conv_w=p["ffn_conv_w"][i], ffn_conv_b=p["ffn_conv_b"][i].reshape(1, D_FF),
    )


def kernel(x, c, ctx, c_ctx, w_mod, b_mod, norm1_gain, norm2_gain, w_in, da_q_gain, da_k_gain, da_lam_q1, da_lam_k1, da_lam_q2, da_lam_k2, da_sub_gain, sw_q_gain, sw_k_gain, sw_sink, lru_conv_w, lru_conv_b, lru_wa, lru_ba, lru_wx, lru_bx, lru_lambda, w_out, w_up, ffn_conv_w, ffn_conv_b, w_down):
    p = dict(w_in=w_in, norm1_gain=norm1_gain, norm2_gain=norm2_gain, da_q_gain=da_q_gain,
             da_k_gain=da_k_gain, da_lam_q1=da_lam_q1, da_lam_k1=da_lam_k1, da_lam_q2=da_lam_q2,
             da_lam_k2=da_lam_k2, da_sub_gain=da_sub_gain, sw_q_gain=sw_q_gain, sw_k_gain=sw_k_gain,
             sw_sink=sw_sink, lru_conv_w=lru_conv_w, lru_conv_b=lru_conv_b, lru_wa=lru_wa,
             lru_ba=lru_ba, lru_wx=lru_wx, lru_bx=lru_bx, lru_lambda=lru_lambda, w_out=w_out,
             w_up=w_up, ffn_conv_w=ffn_conv_w, ffn_conv_b=ffn_conv_b, w_down=w_down)
    bsz, n_tok, d = x.shape
    n_ctx = ctx.shape[1]
    depth = w_mod.shape[0]

    cc = jnp.zeros((2 * SUBLANES, d), F32).at[:bsz].set(c).at[bsz].set(c_ctx)
    mod_all = _modulation(cc, w_mod, b_mod)

    rope_lat = (_rope_tables(n_tok // GRID_W, DA_QK_DIM, 2 * DA_HEADS)
                + _rope_tables(n_tok // GRID_W, HEAD_DIM, SW_HEADS))
    rope_ctx = (jnp.ones((n_ctx, DA_Q), F32), jnp.zeros((n_ctx, DA_Q), F32),
                jnp.ones((n_ctx, SW_Q), F32), jnp.zeros((n_ctx, SW_Q), F32))

    weights = _prepare_weights(p)
    xc = ctx
    for i in range(depth):
        lw = {**_prepare_layer(p, i), **weights}
        lambda_init = 0.8 - 0.6 * math.exp(-0.3 * i)
        ctx_out = i < depth - 1
        mod = mod_all[i, :bsz].reshape(bsz, 1, N_MOD, d)
        sh1, sc1, g1, sh2, sc2, g2 = [mod[:, :, m] for m in range(N_MOD)]
        mod_c = mod_all[i, bsz].reshape(1, 1, N_MOD, d)
        sh1c, sc1c, g1c, sh2c, sc2c, g2c = [mod_c[:, :, m] for m in range(N_MOD)]

        qa, ka, va, qb, kb, vb, cx, cg = _input_projection(x, sh1, sc1, lw, rope_lat)
        qac, kac, vac, qbc, kbc, vbc, cxc, cgc = _input_projection(xc, sh1c, sc1c, lw, rope_ctx)

        ya = _diff_attention(qa, [(kac, vac), (ka, va)], lw, lambda_init)
        yb = _window_attention(qb, kbc, vbc, kb, vb, lw)
        hf, hb, hfc, hbc = _rglru_scans(cx, cxc, lw)

        x = _output_projection(x, ya, yb, hf, hb, cg, g1, lw)
        x = _conv_ffn(x, sh2, sc2, g2, lw)
        if ctx_out:
            yac = _diff_attention(qac, [(kac, vac)], lw, lambda_init)
            ybc = _window_attention(qbc, kbc, vbc, None, None, lw)
            xc = _output_projection(xc, yac, ybc, hfc, hbc, cgc, g1c, lw)
            xc = _conv_ffn(xc, sh2c, sc2c, g2c, lw)
    return x
```

```python
import functools
import math

import numpy as np
import jax
import jax.numpy as jnp
from jax import lax
from jax.experimental import pallas as pl
from jax.experimental.pallas import tpu as pltpu

F32 = jnp.float32
BF16 = jnp.bfloat16

D_MODEL = 1024
GRID_W = 64
N_MOD = 6
EPS = 1e-6
NEG_INF = -1e30
ROPE_BASE = 10000.0
DA_HEADS = 4
DA_QK_DIM = 32
DA_V_DIM = 64
HEAD_DIM = 64
SW_HEADS = 6
SW_KV_HEADS = 2
SW_GROUP = SW_HEADS // SW_KV_HEADS
WINDOW = 128
LRU_WIDTH = 384
LRU_BLOCKS = 6
LRU_BLOCK_DIM = LRU_WIDTH // LRU_BLOCKS
LRU_CONV = 4
LRU_C = 8.0
D_FF = 2816
FFN_CONV = 3

DA_Q = DA_HEADS * 2 * DA_QK_DIM
DA_V = DA_HEADS * DA_V_DIM
SW_Q = SW_HEADS * HEAD_DIM
SW_KV = SW_KV_HEADS * HEAD_DIM

LANES = 128
SUBLANES = 8
BF16_ROWS = 16
MXU_DIM = 256
VMEM_LIMIT = 56 * 1024 * 1024

DA_VEXT = DA_HEADS * LANES
DA_VROWS = DA_V_DIM + BF16_ROWS
C_AQ = 0
C_AK = C_AQ + DA_Q
C_AV = C_AK + DA_Q
C_BQ = C_AV + DA_VEXT
C_BK = C_BQ + SW_Q
C_BV = C_BK + SW_KV
C_CX = C_BV + SW_KV
C_CG = C_CX + LRU_WIDTH
D_INX = C_CG + LRU_WIDTH

TM = 512
FFN_HALO = BF16_ROWS
FFN_CHUNKS = ((0, 768), (768, 1536), (1536, 2304), (2304, D_FF))
DA_TQ = 256
DA_AHEAD = 3
DA_SUB = 2
TM_IN = 1024
IN_SUB = 256
TM_OUT = 1024
SW_AHEAD = 2
SW_TQ = 256
SW_SUB = 4
LRU_TC = 256


def _cparams(n_axes):
    return pltpu.CompilerParams(dimension_semantics=("arbitrary",) * n_axes,
                                vmem_limit_bytes=VMEM_LIMIT)


def _const_spec(shape):
    nd = len(shape)
    return pl.BlockSpec(shape, lambda *_: (0,) * nd, pipeline_mode=pl.Buffered(1))


def _layer_spec(shape, layer):
    nd = len(shape)
    return pl.BlockSpec((None,) + tuple(shape[1:]), lambda *_: (layer,) + (0,) * (nd - 1),
                        pipeline_mode=pl.Buffered(1))


def _nt_dot(a, b):
    return lax.dot_general(a, b, (((1,), (1,)), ((), ())), preferred_element_type=F32)


def _dot(a, b):
    return jnp.dot(a, b, preferred_element_type=F32)


def _mod_kernel(c_ref, w_ref, b_ref, o_ref):
    c = c_ref[...]
    s = c * jax.nn.sigmoid(c)
    o_ref[0] = _dot(s.astype(BF16), w_ref[0].astype(BF16)) + b_ref[0]


def _modulation(cc, w_mod, b_mod):
    depth, d, n = w_mod.shape
    tn = 1536
    return pl.pallas_call(
        _mod_kernel,
        out_shape=jax.ShapeDtypeStruct((depth, cc.shape[0], n), F32),
        grid=(depth, n // tn),
        in_specs=[pl.BlockSpec(cc.shape, lambda i, j: (0, 0)),
                  pl.BlockSpec((1, d, tn), lambda i, j: (i, 0, j)),
                  pl.BlockSpec((1, 1, tn), lambda i, j: (i, 0, j))],
        out_specs=pl.BlockSpec((1, cc.shape[0], tn), lambda i, j: (i, 0, j)),
        compiler_params=_cparams(2),
        name="modulation",
    )(cc, w_mod, b_mod.reshape(depth, 1, n))


def _norm_modulate(x, gain, shift, scale):
    ms = jnp.mean(x * x, axis=-1, keepdims=True)
    h = x * lax.rsqrt(ms + EPS) * gain
    return h * (1.0 + scale) + shift


def _group_rms(a, g_ref):
    sq = (a * a).astype(BF16)
    n = a.shape[1]
    ms = [_dot(sq[:, lo:min(lo + MXU_DIM, n)], g_ref[:min(MXU_DIM, n - lo), :min(MXU_DIM, n - lo)])
          for lo in range(0, n, MXU_DIM)]
    ms = ms[0] if len(ms) == 1 else jnp.concatenate(ms, axis=1)
    return a * lax.rsqrt(ms + EPS)


def _rope(y, cos, sin_signed, half):
    n = y.shape[1]
    lane = lax.broadcasted_iota(jnp.int32, y.shape, 1)
    first = (lane % (2 * half)) < half
    partner = jnp.where(first, pltpu.roll(y, n - half, 1), pltpu.roll(y, half, 1))
    return y * cos + partner * sin_signed


def _inproj_kernel(x_ref, sh_ref, sc_ref, ng_ref, w_ref, vone_ref,
                   cosa_ref, sina_ref, cosb_ref, sinb_ref,
                   gqa_ref, gka_ref, gqb_ref, gkb_ref, g32_ref, g64_ref,
                   qa_ref, ka_ref, va_ref, qb_ref, kb_ref, vb_ref, cx_ref, cg_ref):
    slab = va_ref.shape[3]
    sub = min(IN_SUB, slab)
    n_sub = x_ref.shape[1] // sub

    def project(u):
        rows = slice(u * sub, (u + 1) * sub)
        h = _norm_modulate(x_ref[0, rows], ng_ref[...], sh_ref[0], sc_ref[0]).astype(BF16)
        return _dot(h, w_ref[...])

    def finish(u, full):
        rows = slice(u * sub, (u + 1) * sub)

        def proj(lo, hi):
            return full[:, lo:hi]

        cosa, sina = cosa_ref[rows], sina_ref[rows]
        aq = _group_rms(proj(C_AQ, C_AK), g32_ref) * gqa_ref[...]
        qa_ref[0, rows] = _rope(aq, cosa, sina, DA_QK_DIM // 2).astype(BF16)
        ak = _group_rms(proj(C_AK, C_AV), g32_ref) * gka_ref[...]
        ka_ref[0, rows] = _rope(ak, cosa, sina, DA_QK_DIM // 2).astype(BF16)
        va_t = (proj(C_AV, C_BQ) + vone_ref[...]).T.astype(BF16)
        col = (u * sub) % slab
        for hd in range(DA_HEADS):
            va_ref[0, (u * sub) // slab, hd * DA_VROWS:(hd + 1) * DA_VROWS, col:col + sub] = (
                va_t[hd * LANES:hd * LANES + DA_VROWS])

        cosb, sinb = cosb_ref[rows], sinb_ref[rows]
        bq = _group_rms(proj(C_BQ, C_BK), g64_ref) * gqb_ref[...]
        qb_ref[0, rows] = _rope(bq, cosb, sinb, HEAD_DIM // 2).astype(BF16)
        bkv = proj(C_BK, C_CX)
        bk = _group_rms(bkv[:, :SW_KV], g64_ref) * gkb_ref[...]
        kb_ref[0, rows] = _rope(bk, cosb[:, :SW_KV], sinb[:, :SW_KV], HEAD_DIM // 2).astype(BF16)
        vb_ref[0, rows] = bkv[:, SW_KV:].astype(BF16)

        cxg = proj(C_CX, D_INX)
        cx_ref[0, rows] = cxg[:, :LRU_WIDTH]
        cg_ref[0, rows] = cxg[:, LRU_WIDTH:].astype(cg_ref.dtype)

    full = project(0)
    for u in range(n_sub):
        nxt = project(u + 1) if u + 1 < n_sub else None
        finish(u, full)
        full = nxt


def _input_projection(x, shift, scale, lw, rope):
    bsz, t_len, d = x.shape
    tm = min(TM_IN, t_len)
    sub = min(TM, tm)
    per_b = shift.shape[0] > 1
    mod_map = (lambda b, j: (b, 0, 0)) if per_b else (lambda b, j: (0, 0, 0))
    tok = lambda n: pl.BlockSpec((1, tm, n), lambda b, j: (b, j, 0))
    tab = lambda n: pl.BlockSpec((tm, n), lambda b, j: (j, 0))
    widths = (DA_Q, DA_Q, DA_VEXT, SW_Q, SW_KV, SW_KV, LRU_WIDTH, LRU_WIDTH)
    dtypes = (BF16,) * 6 + (F32, BF16)
    out_shape = [jax.ShapeDtypeStruct((bsz, t_len, n), dt) for n, dt in zip(widths, dtypes)]
    out_specs = [tok(n) for n in widths]
    out_shape[2] = jax.ShapeDtypeStruct((bsz, t_len // sub, DA_HEADS * DA_VROWS, sub), BF16)
    out_specs[2] = pl.BlockSpec((1, tm // sub, DA_HEADS * DA_VROWS, sub), lambda b, j: (b, j, 0, 0))
    return pl.pallas_call(
        _inproj_kernel,
        out_shape=tuple(out_shape),
        grid=(bsz, t_len // tm),
        in_specs=[tok(d),
                  pl.BlockSpec((1, 1, d), mod_map), pl.BlockSpec((1, 1, d), mod_map),
                  _const_spec((1, d)), _layer_spec(lw["w_in"].shape, lw["layer"]), _const_spec((1, DA_VEXT)),
                  tab(DA_Q), tab(DA_Q), tab(SW_Q), tab(SW_Q),
                  _const_spec((1, DA_Q)), _const_spec((1, DA_Q)),
                  _const_spec((1, SW_Q)), _const_spec((1, SW_KV)),
                  _const_spec((MXU_DIM, MXU_DIM)), _const_spec((MXU_DIM, MXU_DIM))],
        out_specs=tuple(out_specs),
        compiler_params=_cparams(2),
        name="input_projection",
    )(x, shift, scale, lw["norm1_gain"], lw["w_in"], lw["v_one"], *rope,
      lw["gq_a"], lw["gk_a"], lw["gq_b"], lw["gk_b"], lw["g32"], lw["g64"])


def _da_kernel(*refs, n_seg, lambda_init):
    q_ref = refs[0]
    kv_refs = refs[1:1 + 2 * n_seg]
    hmask_ref, lamv_ref, sg_ref, o_ref, m_ref, acc_ref = refs[1 + 2 * n_seg:]
    n_map = 2 * DA_HEADS
    tq = m_ref.shape[2]
    n_sub = q_ref.shape[1] // tq

    m_ref[...] = jnp.full(m_ref.shape, NEG_INF, F32)
    acc_ref[...] = jnp.zeros(acc_ref.shape, F32)

    items = [(kv_refs[2 * si], kv_refs[2 * si + 1], i, u, j)
             for si in range(n_seg) for i in range(kv_refs[2 * si + 1].shape[1])
             for u in range(n_sub) for j in range(n_map)]

    def scores(k_ref, v_ref, i, u, j):
        half, tk = j // 4, v_ref.shape[3]
        qm = (q_ref[0, u * tq:(u + 1) * tq, half * LANES:(half + 1) * LANES]
              * hmask_ref[j:j + 1, :])
        kblk = k_ref[0, i * tk:(i + 1) * tk, half * LANES:(half + 1) * LANES]
        return _nt_dot(kblk, qm)

    pending = {n: scores(*items[n]) for n in range(min(DA_AHEAD, len(items)))}
    for n, (k_ref, v_ref, i, u, j) in enumerate(items):
        s_b = pending.pop(n).astype(BF16)
        if n + DA_AHEAD < len(items):
            pending[n + DA_AHEAD] = scores(*items[n + DA_AHEAD])
        st = u * n_map + j
        m_old = m_ref[st, 0:1, :]
        m_new = jnp.maximum(m_old, jnp.max(s_b, axis=0, keepdims=True).astype(F32))
        p_t = jnp.exp2(s_b - m_new.astype(BF16))
        pv = _dot(v_ref[0, i, (j // 2) * DA_VROWS:(j // 2 + 1) * DA_VROWS, :], p_t)
        acc_ref[st] = jnp.exp2(m_old - m_new) * acc_ref[st] + pv
        m_ref[st] = jnp.broadcast_to(m_new, m_ref.shape[1:])

    lv = lamv_ref[...]
    lam = (jnp.exp(jnp.sum(lv[0:1] * lv[1:2], axis=1, keepdims=True))
           - jnp.exp(jnp.sum(lv[2:3] * lv[3:4], axis=1, keepdims=True)) + lambda_init)

    def head_out(u, h):
        a1, a2 = acc_ref[u * n_map + 2 * h], acc_ref[u * n_map + 2 * h + 1]
        o = (a1 / a1[DA_V_DIM:DA_V_DIM + 1] - lam * (a2 / a2[DA_V_DIM:DA_V_DIM + 1]))[:DA_V_DIM]
        ms = jnp.sum(o * o, axis=0, keepdims=True) * (1.0 / DA_V_DIM)
        return o * lax.rsqrt(ms + EPS)

    for u in range(n_sub):
        for c in range(DA_HEADS // 2):
            pair_t = jnp.concatenate([head_out(u, 2 * c), head_out(u, 2 * c + 1)], axis=0)
            o_ref[0, u * tq:(u + 1) * tq, c * LANES:(c + 1) * LANES] = (
                pair_t.T * sg_ref[...]).astype(o_ref.dtype)


def _diff_attention(q, kvs, lw, lambda_init):
    bsz, t_q, _ = q.shape
    tq = min(DA_TQ, t_q)
    n_sub = min(DA_SUB, t_q // tq)
    tb = n_sub * tq
    kv_specs, kv_args = [], []
    for k, v in kvs:
        kv_specs += [pl.BlockSpec((1,) + k.shape[1:], lambda b, j: (b, 0, 0)),
                     pl.BlockSpec((1,) + v.shape[1:], lambda b, j: (b, 0, 0, 0))]
        kv_args += [k, v]
    sg = lw["da_sg"] * (1.0 - lambda_init)
    return pl.pallas_call(
        functools.partial(_da_kernel, n_seg=len(kvs), lambda_init=lambda_init),
        out_shape=jax.ShapeDtypeStruct((bsz, t_q, DA_V), BF16),
        grid=(bsz, t_q // tb),
        in_specs=[pl.BlockSpec((1, tb, DA_Q), lambda b, j: (b, j, 0))] + kv_specs
                 + [_const_spec((2 * DA_HEADS, LANES)), _const_spec((4, DA_QK_DIM)),
                    _const_spec((1, LANES))],
        out_specs=pl.BlockSpec((1, tb, DA_V), lambda b, j: (b, j, 0)),
        scratch_shapes=[pltpu.VMEM((n_sub * 2 * DA_HEADS, SUBLANES, tq), F32),
                        pltpu.VMEM((n_sub * 2 * DA_HEADS, DA_VROWS, tq), F32)],
        compiler_params=_cparams(2),
        name="diff_attention",
    )(q, *kv_args, lw["da_hmask"], lw["da_lamv"], sg)


def _sw_kernel(*refs, has_lat, tq, t_len):
    if has_lat:
        sink_ref, q_ref, kc_ref, vc_ref, k_ref, v_ref, hm_ref, o_ref = refs
    else:
        sink_ref, q_ref, kc_ref, vc_ref, hm_ref, o_ref = refs
    def values_t(v):
        n_k = v.shape[0]
        v_t = v.astype(F32).T
        tail = (lax.broadcasted_iota(jnp.int32, (BF16_ROWS, n_k), 0) == 0).astype(F32)
        return [jnp.concatenate([v_t[kv * HEAD_DIM:(kv + 1) * HEAD_DIM], tail], axis=0).astype(BF16)
                for kv in range(SW_KV_HEADS)]

    kc = kc_ref[0]
    vc_t = values_t(vc_ref[0])
    n_sub = q_ref.shape[1] // tq
    bands = []
    for u in range(n_sub if has_lat else 0):
        band = tq + 2 * WINDOW
        q0 = (pl.program_id(1) * n_sub + u) * tq
        start = pl.multiple_of(jnp.clip(q0 - WINDOW, 0, t_len - band), WINDOW)
        qpos = q0 + lax.broadcasted_iota(jnp.int32, (1, tq), 1)
        kpos = start + lax.broadcasted_iota(jnp.int32, (band, 1), 0)
        bands.append((k_ref[0, pl.ds(start, band), :], values_t(v_ref[0, pl.ds(start, band), :]),
                      jnp.abs(qpos - kpos) <= WINDOW))

    combos = [(u, g, kv) for u in range(n_sub) for g in range(SW_GROUP)
              for kv in range(SW_KV_HEADS)]

    def scores(c):
        u, g, kv = combos[c]
        qm = q_ref[0, u * tq:(u + 1) * tq, g * LANES:(g + 1) * LANES] * hm_ref[kv:kv + 1, :]
        s_lat = _nt_dot(bands[u][0], qm) if has_lat else None
        return _nt_dot(kc, qm), s_lat

    pending = {c: scores(c) for c in range(SW_AHEAD)}
    outs = {}
    for c, (u, g, kv) in enumerate(combos):
        s_ctx, s_lat = pending.pop(c)
        if c + SW_AHEAD < len(combos):
            pending[c + SW_AHEAD] = scores(c + SW_AHEAD)
        sink = sink_ref[kv * SW_GROUP + g]
        s_ctx = s_ctx.astype(BF16)
        m = jnp.maximum(jnp.max(s_ctx, axis=0, keepdims=True).astype(F32), sink)
        if has_lat:
            s_lat = jnp.where(bands[u][2], s_lat.astype(BF16), NEG_INF)
            m = jnp.maximum(m, jnp.max(s_lat, axis=0, keepdims=True).astype(F32))
        m_b = m.astype(BF16)
        m = m_b.astype(F32)
        acc = _dot(vc_t[kv], jnp.exp2(s_ctx - m_b))
        if has_lat:
            acc = acc + _dot(bands[u][1][kv], jnp.exp2(s_lat - m_b))
        l = acc[HEAD_DIM:HEAD_DIM + 1] + jnp.exp2(sink - m)
        outs[kv] = acc[:HEAD_DIM] / l
        if kv == SW_KV_HEADS - 1:
            pair_t = jnp.concatenate([outs.pop(k2) for k2 in range(SW_KV_HEADS)], axis=0)
            o_ref[0, u * tq:(u + 1) * tq, g * LANES:(g + 1) * LANES] = pair_t.T.astype(o_ref.dtype)


def _window_attention(q, kc, vc, k, v, lw):
    bsz, t_q, _ = q.shape
    has_lat = k is not None
    tq = min(SW_TQ, t_q)
    tb = min(SW_SUB * tq, t_q)
    n_ctx = kc.shape[1]
    full = lambda t: pl.BlockSpec((1, t, SW_KV), lambda b, j: (b, 0, 0))
    specs = [pl.BlockSpec(memory_space=pltpu.SMEM),
             pl.BlockSpec((1, tb, SW_Q), lambda b, j: (b, j, 0)), full(n_ctx), full(n_ctx)]
    args = [lw["sw_sink"], q, kc, vc]
    if has_lat:
        specs += [full(t_q), full(t_q)]
        args += [k, v]
    specs.append(_const_spec((SW_KV_HEADS, LANES)))
    args.append(lw["sw_hmask"])
    return pl.pallas_call(
        functools.partial(_sw_kernel, has_lat=has_lat, tq=tq, t_len=t_q),
        out_shape=jax.ShapeDtypeStruct((bsz, t_q, SW_Q), BF16),
        grid=(bsz, t_q // tb),
        in_specs=specs,
        out_specs=pl.BlockSpec((1, tb, SW_Q), lambda b, j: (b, j, 0)),
        compiler_params=_cparams(2),
        name="window_attention",
    )(*args)


def _lru_kernel(xf_ref, xb_ref, xc_ref, cw_ref, cb_ref, wa_ref, ba_ref, wx_ref, bx_ref, lam_ref,
                hf_ref, hb_ref, hfc_ref, hbc_ref, halo_ref, carry_ref, *, tc):
    j = pl.program_id(1)
    is_ctx = j == 0
    sub = lax.broadcasted_iota(jnp.int32, (1, SUBLANES, 1), 1)

    dirs = ((0, xf_ref, hf_ref, hfc_ref), (1, xb_ref, hb_ref, hbc_ref))
    n_grp = tc // SUBLANES

    def gates(d, x_lat_ref):
        x = jnp.where(is_ctx, xc_ref[0], x_lat_ref[0])
        halo = jnp.where(j <= 1, 0.0, halo_ref[d])
        cw = cw_ref[d]
        x3 = x.reshape(n_grp, SUBLANES, x.shape[1])
        xc3 = cb_ref[d] + x3 * cw[LRU_CONV - 1:LRU_CONV]
        for k in range(LRU_CONV - 1):
            s = LRU_CONV - 1 - k
            if d == 0:
                rolled = pltpu.roll(x3, s, 1)
                other = jnp.concatenate([pltpu.roll(halo, s, 0)[None], rolled[:-1]], axis=0)
                xs = jnp.where(sub < s, other, rolled)
            else:
                rolled = pltpu.roll(x3, SUBLANES - s, 1)
                other = jnp.concatenate([rolled[1:], pltpu.roll(halo, SUBLANES - s, 0)[None]], axis=0)
                xs = jnp.where(sub >= SUBLANES - s, other, rolled)
            xc3 = xc3 + xs * cw[k:k + 1]
        xc = xc3.reshape(tc, x.shape[1])
        halo_ref[d] = x[tc - SUBLANES:] if d == 0 else x[:SUBLANES]

        xcb = xc.astype(BF16)
        r = jax.nn.sigmoid(_dot(xcb, wa_ref[d]) + ba_ref[d])
        gi = jax.nn.sigmoid(_dot(xcb, wx_ref[d]) + bx_ref[d])
        nl = -lam_ref[d]
        softplus = jnp.maximum(nl, 0.0) + jnp.log1p(jnp.exp(-jnp.abs(nl)))
        log_a = -LRU_C * r * softplus
        a = jnp.exp(log_a)
        b = jnp.sqrt(-jnp.tanh(log_a) * (a * a + 1.0)) * (gi * xc)

        a3 = a.reshape(n_grp, SUBLANES, a.shape[1])
        b3 = b.reshape(n_grp, SUBLANES, b.shape[1])
        step = 1
        while step < SUBLANES:
            keep = (sub < step) if d == 0 else (sub >= SUBLANES - step)
            shift = step if d == 0 else SUBLANES - step
            a_s = jnp.where(keep, 1.0, pltpu.roll(a3, shift, 1))
            b_s = jnp.where(keep, 0.0, pltpu.roll(b3, shift, 1))
            b3 = a3 * b_s + b3
            a3 = a3 * a_s
            step *= 2
        return a3, b3

    ab = [gates(d, x_lat_ref) for d, x_lat_ref, _, _ in dirs]

    edge = (SUBLANES - 1, 0)
    h_in = [jnp.where(is_ctx, 0.0, carry_ref[d, edge[d]:edge[d] + 1, :]) for d in range(2)]
    groups = [[None] * n_grp, [None] * n_grp]
    for t in range(n_grp):
        for d in range(2):
            gi = t if d == 0 else n_grp - 1 - t
            groups[d][gi] = ab[d][0][gi] * h_in[d] + ab[d][1][gi]
            h_in[d] = groups[d][gi][edge[d]:edge[d] + 1]

    for d, _, out_ref, outc_ref in dirs:
        carry_ref[d] = groups[d][n_grp - 1] if d == 0 else groups[d][0]
        h_out = jnp.concatenate(groups[d], axis=0).astype(out_ref.dtype)

        @pl.when(is_ctx)
        def _(outc_ref=outc_ref, h_out=h_out):
            outc_ref[0] = h_out

        @pl.when(jnp.logical_not(is_ctx))
        def _(out_ref=out_ref, h_out=h_out):
            out_ref[0] = h_out


def _rglru_scans(cx, cxc, lw):
    bsz, t_len, n = cx.shape
    tc = LRU_TC
    assert cxc.shape[1] == tc and t_len % tc == 0
    n_lat = t_len // tc
    fwd = lambda b, j: (b, jnp.maximum(j - 1, 0), 0)
    bwd = lambda b, j: (b, n_lat - jnp.maximum(j, 1), 0)
    ctx = lambda b, j: (b, 0, 0)
    blk = lambda m: pl.BlockSpec((1, tc, n), m)
    return pl.pallas_call(
        functools.partial(_lru_kernel, tc=tc),
        out_shape=(jax.ShapeDtypeStruct(cx.shape, BF16), jax.ShapeDtypeStruct(cx.shape, BF16),
                   jax.ShapeDtypeStruct(cxc.shape, BF16), jax.ShapeDtypeStruct(cxc.shape, BF16)),
        grid=(bsz, n_lat + 1),
        in_specs=[blk(fwd), blk(bwd), blk(ctx),
                  _const_spec((2, LRU_CONV, n)), _const_spec((2, 1, n)),
                  _const_spec((2, n, n)), _const_spec((2, 1, n)),
                  _const_spec((2, n, n)), _const_spec((2, 1, n)), _const_spec((2, 1, n))],
        out_specs=(blk(fwd), blk(bwd), blk(ctx), blk(ctx)),
        scratch_shapes=[pltpu.VMEM((2, SUBLANES, n), F32), pltpu.VMEM((2, SUBLANES, n), F32)],
        compiler_params=_cparams(2),
        name="rglru_scans",
    )(cx, cx, cxc, lw["lru_conv_w"], lw["lru_conv_b"], lw["lru_wa"], lw["lru_ba"],
      lw["lru_wx"], lw["lru_bx"], lw["lru_lambda"])


def _gelu_tanh(x):
    return 0.5 * x * (1.0 + jnp.tanh(math.sqrt(2.0 / math.pi) * (x + 0.044715 * (x * x * x))))


def _outproj_kernel(x_ref, ya_ref, yb_ref, hf_ref, hb_ref, cg_ref, g1_ref, w_ref, o_ref, cat_ref):
    cat_ref[:, 0:DA_V] = ya_ref[0]
    cat_ref[:, DA_V:DA_V + SW_Q] = yb_ref[0]
    yc = (hf_ref[0].astype(F32) + hb_ref[0].astype(F32)) * _gelu_tanh(cg_ref[0].astype(F32))
    cat_ref[:, DA_V + SW_Q:] = yc.astype(BF16)
    o_ref[0] = x_ref[0] + g1_ref[0] * _dot(cat_ref[...], w_ref[...])


def _output_projection(x, ya, yb, hf, hb, cg, gate, lw):
    bsz, t_len, d = x.shape
    tm = min(TM_OUT, t_len)
    per_b = gate.shape[0] > 1
    mod_map = (lambda b, j: (b, 0, 0)) if per_b else (lambda b, j: (0, 0, 0))
    tok = lambda n: pl.BlockSpec((1, tm, n), lambda b, j: (b, j, 0))
    return pl.pallas_call(
        _outproj_kernel,
        out_shape=jax.ShapeDtypeStruct(x.shape, F32),
        grid=(bsz, t_len // tm),
        in_specs=[tok(d), tok(DA_V), tok(SW_Q), tok(LRU_WIDTH), tok(LRU_WIDTH), tok(LRU_WIDTH),
                  pl.BlockSpec((1, 1, d), mod_map), _layer_spec(lw["w_out"].shape, lw["layer"])],
        out_specs=tok(d),
        scratch_shapes=[pltpu.VMEM((tm, d), BF16)],
        compiler_params=_cparams(2),
        name="output_projection",
    )(x, ya, yb, hf, hb, cg, gate, lw["w_out"])


def _ffn_kernel(xm_ref, xp_ref, xn_ref, sh_ref, sc_ref, g2_ref, ng_ref, wup_ref, cw_ref, cb_ref,
                wdn_ref, o_ref, h_ref, *, tm):
    j = pl.program_id(1)
    last = pl.num_programs(1) - 1
    gain, shift, scale = ng_ref[...], sh_ref[0], sc_ref[0]
    xm = xm_ref[0]
    hp = jnp.where(j > 0, _norm_modulate(xp_ref[0], gain, shift, scale), 0.0)
    hn = jnp.where(j < last, _norm_modulate(xn_ref[0], gain, shift, scale), 0.0)
    h_ref[0:FFN_HALO] = hp.astype(BF16)
    h_ref[FFN_HALO:FFN_HALO + tm] = _norm_modulate(xm, gain, shift, scale).astype(BF16)
    h_ref[FFN_HALO + tm:] = hn.astype(BF16)

    rows = tm + 2 * FFN_HALO

    def up(lo, hi):
        return (_dot(h_ref[...], wup_ref[:, lo:hi]),
                _dot(h_ref[FFN_HALO:FFN_HALO + tm], wup_ref[:, D_FF + lo:D_FF + hi]))

    acc = jnp.zeros((tm, D_MODEL), F32)
    nxt = up(*FFN_CHUNKS[0])
    for c, (lo, hi) in enumerate(FFN_CHUNKS):
        gp, val = nxt
        if c + 1 < len(FFN_CHUNKS):
            nxt = up(*FFN_CHUNKS[c + 1])
        g_prev = pltpu.roll(gp, 1, 0)[FFN_HALO:FFN_HALO + tm]
        g_next = pltpu.roll(gp, rows - 1, 0)[FFN_HALO:FFN_HALO + tm]
        cw = cw_ref[:, lo:hi]
        gate = (cb_ref[:, lo:hi] + g_prev * cw[0:1] + gp[FFN_HALO:FFN_HALO + tm] * cw[1:2]
                + g_next * cw[2:3])
        act = (gate * jax.nn.sigmoid(gate) * val).astype(BF16)
        acc = acc + _dot(act, wdn_ref[lo:hi, :])
    o_ref[0] = xm + g2_ref[0] * acc


def _conv_ffn(x, shift, scale, gate, lw):
    bsz, t_len, d = x.shape
    tm = min(TM, t_len)
    per_b = gate.shape[0] > 1
    mod_map = (lambda b, j: (b, 0, 0)) if per_b else (lambda b, j: (0, 0, 0))
    hb = tm // FFN_HALO
    n_hb = t_len // FFN_HALO
    prev_map = lambda b, j: (b, jnp.maximum(j * hb - 1, 0), 0)
    next_map = lambda b, j: (b, jnp.minimum((j + 1) * hb, n_hb - 1), 0)
    mod_spec = pl.BlockSpec((1, 1, d), mod_map)
    return pl.pallas_call(
        functools.partial(_ffn_kernel, tm=tm),
        out_shape=jax.ShapeDtypeStruct(x.shape, F32),
        grid=(bsz, t_len // tm),
        in_specs=[pl.BlockSpec((1, tm, d), lambda b, j: (b, j, 0)),
                  pl.BlockSpec((1, FFN_HALO, d), prev_map),
                  pl.BlockSpec((1, FFN_HALO, d), next_map),
                  mod_spec, mod_spec, mod_spec,
                  _const_spec((1, d)), _layer_spec(lw["w_up"].shape, lw["layer"]),
                  _const_spec((FFN_CONV, D_FF)), _const_spec((1, D_FF)),
                  _layer_spec(lw["w_down"].shape, lw["layer"])],
        out_specs=pl.BlockSpec((1, tm, d), lambda b, j: (b, j, 0)),
        scratch_shapes=[pltpu.VMEM((tm + 2 * FFN_HALO, d), BF16)],
        compiler_params=_cparams(2),
        name="conv_ffn",
    )(x, x, x, shift, scale, gate, lw["norm2_gain"], lw["w_up"], lw["ffn_conv_w"],
      lw["ffn_conv_b"], lw["w_down"])


def _sw_head_order():
    return [kv * SW_GROUP + g for g in range(SW_GROUP) for kv in range(SW_KV_HEADS)]


def _rope_tables(rows, head_dim, n_heads):
    row = np.repeat(np.arange(rows, dtype=np.float64), GRID_W)
    col = np.tile(np.arange(GRID_W, dtype=np.float64), rows)
    quarter = head_dim // 4
    inv_freq = ROPE_BASE ** (-np.arange(quarter, dtype=np.float64) / quarter)
    ang = np.concatenate([row[:, None] * inv_freq, col[:, None] * inv_freq], axis=-1)
    cos, sin = np.cos(ang), np.sin(ang)
    return (jnp.asarray(np.tile(np.concatenate([cos, cos], axis=-1), (1, n_heads)), dtype=F32),
            jnp.asarray(np.tile(np.concatenate([-sin, sin], axis=-1), (1, n_heads)), dtype=F32))


def _block_diag_mean(n, group):
    idx = np.arange(n) // group
    return jnp.asarray((idx[:, None] == idx[None, :]).astype(np.float32) / group, dtype=BF16)


def _block_diag(w):
    two, nb, bi, bj = w.shape
    eye = jnp.eye(nb, dtype=w.dtype)
    return jnp.einsum("dhij,hg->dhigj", w, eye).reshape(two, nb * bi, nb * bj)


def _prepare_weights(p):
    d = D_MODEL
    depth = p["w_in"].shape[0]
    split = np.cumsum((DA_Q, DA_Q, DA_V, SW_Q, SW_KV, SW_KV, LRU_WIDTH))
    w_aq, w_ak, w_av, w_bq, w_bk, w_bv, w_cx, w_cg = jnp.split(p["w_in"], split.tolist(), axis=2)
    w_av = jnp.pad(w_av.reshape(depth, d, DA_HEADS, DA_V_DIM),
                   ((0, 0), (0, 0), (0, 0), (0, LANES - DA_V_DIM))).reshape(depth, d, DA_VEXT)
    order = jnp.asarray(_sw_head_order())
    w_bq = w_bq.reshape(depth, d, SW_HEADS, HEAD_DIM)[:, :, order].reshape(depth, d, SW_Q)
    w_in = jnp.concatenate([w_aq, w_ak, w_av, w_bq, w_bk, w_bv, w_cx, w_cg], axis=2)

    w_out = p["w_out"]
    w_ob = w_out[:, DA_V:DA_V + SW_Q].reshape(depth, SW_HEADS, HEAD_DIM, d)[:, order]
    w_out = jnp.concatenate([w_out[:, :DA_V], w_ob.reshape(depth, SW_Q, d), w_out[:, DA_V + SW_Q:]],
                            axis=1)
    return dict(w_in=w_in.astype(BF16), w_out=w_out.astype(BF16),
                w_up=p["w_up"].astype(BF16), w_down=p["w_down"].astype(BF16))


def _prepare_layer(p, i):
    d = D_MODEL
    v_one = np.zeros((1, DA_VEXT), np.float32)
    v_one[0, DA_V_DIM::LANES] = 1.0

    da_hmask = np.zeros((2 * DA_HEADS, LANES), np.float32)
    for j in range(2 * DA_HEADS):
        off = (j % 4) * DA_QK_DIM
        da_hmask[j, off:off + DA_QK_DIM] = 1.0
    sw_hmask = np.zeros((SW_KV_HEADS, LANES), np.float32)
    for kv in range(SW_KV_HEADS):
        sw_hmask[kv, kv * HEAD_DIM:(kv + 1) * HEAD_DIM] = 1.0

    return dict(
        norm1_gain=p["norm1_gain"][i].reshape(1, d), norm2_gain=p["norm2_gain"][i].reshape(1, d),
        layer=i, v_one=jnp.asarray(v_one),
        gq_a=(jnp.tile(p["da_q_gain"][i], 2 * DA_HEADS).reshape(1, DA_Q)
              * (DA_QK_DIM ** -0.5 * math.log2(math.e))),
        gk_a=jnp.tile(p["da_k_gain"][i], 2 * DA_HEADS).reshape(1, DA_Q),
        gq_b=(jnp.tile(p["sw_q_gain"][i], SW_HEADS).reshape(1, SW_Q)
              * (HEAD_DIM ** -0.5 * math.log2(math.e))),
        gk_b=jnp.tile(p["sw_k_gain"][i], SW_KV_HEADS).reshape(1, SW_KV),
        g32=_block_diag_mean(MXU_DIM, DA_QK_DIM), g64=_block_diag_mean(MXU_DIM, HEAD_DIM),
        da_hmask=jnp.asarray(da_hmask, dtype=BF16),
        da_lamv=jnp.stack([p["da_lam_q1"][i], p["da_lam_k1"][i], p["da_lam_q2"][i], p["da_lam_k2"][i]]),
        da_sg=jnp.tile(p["da_sub_gain"][i], LANES // DA_V_DIM).reshape(1, LANES),
        sw_sink=p["sw_sink"][i] * math.log2(math.e), sw_hmask=jnp.asarray(sw_hmask, dtype=BF16),
        lru_conv_w=p["lru_conv_w"][i], lru_conv_b=p["lru_conv_b"][i].reshape(2, 1, LRU_WIDTH),
        lru_wa=_block_diag(p["lru_wa"][i]).astype(BF16), lru_ba=p["lru_ba"][i].reshape(2, 1, LRU_WIDTH),
        lru_wx=_block_diag(p["lru_wx"][i]).astype(BF16), lru_bx=p["lru_bx"][i].reshape(2, 1, LRU_WIDTH),
        lru_lambda=p["lru_lambda"][i].reshape(2, 1, LRU_WIDTH),
        ffn_conv_w=p["ffn_conv_w"][i], ffn_conv_b=p["ffn_conv_b"][i].reshape(1, D_FF),
    )


def kernel(x, c, ctx, c_ctx, w_mod, b_mod, norm1_gain, norm2_gain, w_in, da_q_gain, da_k_gain, da_lam_q1, da_lam_k1, da_lam_q2, da_lam_k2, da_sub_gain, sw_q_gain, sw_k_gain, sw_sink, lru_conv_w, lru_conv_b, lru_wa, lru_ba, lru_wx, lru_bx, lru_lambda, w_out, w_up, ffn_conv_w, ffn_conv_b, w_down):
    p = dict(w_in=w_in, norm1_gain=norm1_gain, norm2_gain=norm2_gain, da_q_gain=da_q_gain,
             da_k_gain=da_k_gain, da_lam_q1=da_lam_q1, da_lam_k1=da_lam_k1, da_lam_q2=da_lam_q2,
             da_lam_k2=da_lam_k2, da_sub_gain=da_sub_gain, sw_q_gain=sw_q_gain, sw_k_gain=sw_k_gain,
             sw_sink=sw_sink, lru_conv_w=lru_conv_w, lru_conv_b=lru_conv_b, lru_wa=lru_wa,
             lru_ba=lru_ba, lru_wx=lru_wx, lru_bx=lru_bx, lru_lambda=lru_lambda, w_out=w_out,
             w_up=w_up, ffn_conv_w=ffn_conv_w, ffn_conv_b=ffn_conv_b, w_down=w_down)
    bsz, n_tok, d = x.shape
    n_ctx = ctx.shape[1]
    depth = w_mod.shape[0]

    cc = jnp.zeros((2 * SUBLANES, d), F32).at[:bsz].set(c).at[bsz].set(c_ctx)
    mod_all = _modulation(cc, w_mod, b_mod)

    rope_lat = (_rope_tables(n_tok // GRID_W, DA_QK_DIM, 2 * DA_HEADS)
                + _rope_tables(n_tok // GRID_W, HEAD_DIM, SW_HEADS))
    rope_ctx = (jnp.ones((n_ctx, DA_Q), F32), jnp.zeros((n_ctx, DA_Q), F32),
                jnp.ones((n_ctx, SW_Q), F32), jnp.zeros((n_ctx, SW_Q), F32))

    weights = _prepare_weights(p)
    xc = ctx
    for i in range(depth):
        lw = {**_prepare_layer(p, i), **weights}
        lambda_init = 0.8 - 0.6 * math.exp(-0.3 * i)
        ctx_out = i < depth - 1
        mod = mod_all[i, :bsz].reshape(bsz, 1, N_MOD, d)
        sh1, sc1, g1, sh2, sc2, g2 = [mod[:, :, m] for m in range(N_MOD)]
        mod_c = mod_all[i, bsz].reshape(1, 1, N_MOD, d)
        sh1c, sc1c, g1c, sh2c, sc2c, g2c = [mod_c[:, :, m] for m in range(N_MOD)]

        qa, ka, va, qb, kb, vb, cx, cg = _input_projection(x, sh1, sc1, lw, rope_lat)
        qac, kac, vac, qbc, kbc, vbc, cxc, cgc = _input_projection(xc, sh1c, sc1c, lw, rope_ctx)

        ya = _diff_attention(qa, [(kac, vac), (ka, va)], lw, lambda_init)
        yb = _window_attention(qb, kbc, vbc, kb, vb, lw)
        hf, hb, hfc, hbc = _rglru_scans(cx, cxc, lw)

        x = _output_projection(x, ya, yb, hf, hb, cg, g1, lw)
        x = _conv_ffn(x, sh2, sc2, g2, lw)
        if ctx_out:
            yac = _diff_attention(qac, [(kac, vac)], lw, lambda_init)
            ybc = _window_attention(qbc, kbc, vbc, None, None, lw)
            xc = _output_projection(xc, yac, ybc, hfc, hbc, cgc, g1c, lw)
            xc = _conv_ffn(xc, sh2c, sc2c, g2c, lw)
    return x
```

```python
import functools
import math

import numpy as np
import jax
import jax.numpy as jnp
from jax import lax
from jax.experimental import pallas as pl
from jax.experimental.pallas import tpu as pltpu

F32 = jnp.float32
BF16 = jnp.bfloat16

D_MODEL = 1024
GRID_W = 64
N_MOD = 6
EPS = 1e-6
NEG_INF = -1e30
ROPE_BASE = 10000.0
DA_HEADS = 4
DA_QK_DIM = 32
DA_V_DIM = 64
HEAD_DIM = 64
SW_HEADS = 6
SW_KV_HEADS = 2
SW_GROUP = SW_HEADS // SW_KV_HEADS
WINDOW = 128
LRU_WIDTH = 384
LRU_BLOCKS = 6
LRU_BLOCK_DIM = LRU_WIDTH // LRU_BLOCKS
LRU_CONV = 4
LRU_C = 8.0
D_FF = 2816
FFN_CONV = 3

DA_Q = DA_HEADS * 2 * DA_QK_DIM
DA_V = DA_HEADS * DA_V_DIM
SW_Q = SW_HEADS * HEAD_DIM
SW_KV = SW_KV_HEADS * HEAD_DIM

LANES = 128
SUBLANES = 8
BF16_ROWS = 16
MXU_DIM = 256
VMEM_LIMIT = 56 * 1024 * 1024

DA_VEXT = DA_HEADS * LANES
DA_VROWS = DA_V_DIM + BF16_ROWS
C_AQ = 0
C_AK = C_AQ + DA_Q
C_AV = C_AK + DA_Q
C_BQ = C_AV + DA_VEXT
C_BK = C_BQ + SW_Q
C_BV = C_BK + SW_KV
C_CX = C_BV + SW_KV
C_CG = C_CX + LRU_WIDTH
D_INX = C_CG + LRU_WIDTH

TM = 512
FFN_HALO = BF16_ROWS
FFN_CHUNKS = ((0, 768), (768, 1536), (1536, 2304), (2304, D_FF))
DA_TQ = 256
DA_AHEAD = 3
DA_SUB = 2
TM_IN = 1024
IN_SUB = 256
TM_OUT = 1024
SW_AHEAD = 2
SW_TQ = 256
SW_SUB = 4
LRU_TC = 256


def _cparams(n_axes, fuse_inputs=None):
    return pltpu.CompilerParams(dimension_semantics=("arbitrary",) * n_axes,
                                vmem_limit_bytes=VMEM_LIMIT, allow_input_fusion=fuse_inputs)


def _const_spec(shape):
    nd = len(shape)
    return pl.BlockSpec(shape, lambda *_: (0,) * nd, pipeline_mode=pl.Buffered(1))


def _layer_spec(shape, layer):
    nd = len(shape)
    return pl.BlockSpec((None,) + tuple(shape[1:]), lambda *_: (layer,) + (0,) * (nd - 1),
                        pipeline_mode=pl.Buffered(1))


def _nt_dot(a, b):
    return lax.dot_general(a, b, (((1,), (1,)), ((), ())), preferred_element_type=F32)


def _dot(a, b):
    return jnp.dot(a, b, preferred_element_type=F32)


def _mod_kernel(c_ref, w_ref, b_ref, o_ref):
    c = c_ref[...]
    s = c * jax.nn.sigmoid(c)
    o_ref[0] = _dot(s.astype(BF16), w_ref[0].astype(BF16)) + b_ref[0]


def _modulation(cc, w_mod, b_mod):
    depth, d, n = w_mod.shape
    tn = 1536
    return pl.pallas_call(
        _mod_kernel,
        out_shape=jax.ShapeDtypeStruct((depth, cc.shape[0], n), F32),
        grid=(depth, n // tn),
        in_specs=[pl.BlockSpec(cc.shape, lambda i, j: (0, 0)),
                  pl.BlockSpec((1, d, tn), lambda i, j: (i, 0, j)),
                  pl.BlockSpec((1, 1, tn), lambda i, j: (i, 0, j))],
        out_specs=pl.BlockSpec((1, cc.shape[0], tn), lambda i, j: (i, 0, j)),
        compiler_params=_cparams(2),
        name="modulation",
    )(cc, w_mod, b_mod.reshape(depth, 1, n))


def _norm_modulate(x, gain, shift, scale):
    ms = jnp.mean(x * x, axis=-1, keepdims=True)
    h = x * lax.rsqrt(ms + EPS) * gain
    return h * (1.0 + scale) + shift


def _group_rms(a, g_ref):
    sq = (a * a).astype(BF16)
    n = a.shape[1]
    ms = [_dot(sq[:, lo:min(lo + MXU_DIM, n)], g_ref[:min(MXU_DIM, n - lo), :min(MXU_DIM, n - lo)])
          for lo in range(0, n, MXU_DIM)]
    ms = ms[0] if len(ms) == 1 else jnp.concatenate(ms, axis=1)
    return a * lax.rsqrt(ms + EPS)


def _rope(y, cos, sin_signed, half):
    n = y.shape[1]
    lane = lax.broadcasted_iota(jnp.int32, y.shape, 1)
    first = (lane % (2 * half)) < half
    partner = jnp.where(first, pltpu.roll(y, n - half, 1), pltpu.roll(y, half, 1))
    return y * cos + partner * sin_signed


def _inproj_kernel(x_ref, sh_ref, sc_ref, ng_ref, w_ref, vone_ref,
                   cosa_ref, sina_ref, cosb_ref, sinb_ref,
                   gqa_ref, gka_ref, gqb_ref, gkb_ref, g32_ref, g64_ref,
                   qa_ref, ka_ref, va_ref, qb_ref, kb_ref, vb_ref, cx_ref, cg_ref):
    slab = va_ref.shape[3]
    sub = min(IN_SUB, slab)
    n_sub = x_ref.shape[1] // sub

    def project(u):
        rows = slice(u * sub, (u + 1) * sub)
        h = _norm_modulate(x_ref[0, rows], ng_ref[...], sh_ref[0], sc_ref[0]).astype(BF16)
        return _dot(h, w_ref[...])

    def finish(u, full):
        rows = slice(u * sub, (u + 1) * sub)

        def proj(lo, hi):
            return full[:, lo:hi]

        cosa, sina = cosa_ref[rows], sina_ref[rows]
        aq = _group_rms(proj(C_AQ, C_AK), g32_ref) * gqa_ref[...]
        qa_ref[0, rows] = _rope(aq, cosa, sina, DA_QK_DIM // 2).astype(BF16)
        ak = _group_rms(proj(C_AK, C_AV), g32_ref) * gka_ref[...]
        ka_ref[0, rows] = _rope(ak, cosa, sina, DA_QK_DIM // 2).astype(BF16)
        va_t = (proj(C_AV, C_BQ) + vone_ref[...]).T.astype(BF16)
        col = (u * sub) % slab
        for hd in range(DA_HEADS):
            va_ref[0, (u * sub) // slab, hd * DA_VROWS:(hd + 1) * DA_VROWS, col:col + sub] = (
                va_t[hd * LANES:hd * LANES + DA_VROWS])

        cosb, sinb = cosb_ref[rows], sinb_ref[rows]
        bq = _group_rms(proj(C_BQ, C_BK), g64_ref) * gqb_ref[...]
        qb_ref[0, rows] = _rope(bq, cosb, sinb, HEAD_DIM // 2).astype(BF16)
        bkv = proj(C_BK, C_CX)
        bk = _group_rms(bkv[:, :SW_KV], g64_ref) * gkb_ref[...]
        kb_ref[0, rows] = _rope(bk, cosb[:, :SW_KV], sinb[:, :SW_KV], HEAD_DIM // 2).astype(BF16)
        vb_ref[0, rows] = bkv[:, SW_KV:].astype(BF16)

        cxg = proj(C_CX, D_INX)
        cx_ref[0, rows] = cxg[:, :LRU_WIDTH]
        cg_ref[0, rows] = cxg[:, LRU_WIDTH:].astype(cg_ref.dtype)

    full = project(0)
    for u in range(n_sub):
        nxt = project(u + 1) if u + 1 < n_sub else None
        finish(u, full)
        full = nxt


def _input_projection(x, shift, scale, lw, rope):
    bsz, t_len, d = x.shape
    tm = min(TM_IN, t_len)
    sub = min(TM, tm)
    per_b = shift.shape[0] > 1
    mod_map = (lambda b, j: (b, 0, 0)) if per_b else (lambda b, j: (0, 0, 0))
    tok = lambda n: pl.BlockSpec((1, tm, n), lambda b, j: (b, j, 0))
    tab = lambda n: pl.BlockSpec((tm, n), lambda b, j: (j, 0))
    widths = (DA_Q, DA_Q, DA_VEXT, SW_Q, SW_KV, SW_KV, LRU_WIDTH, LRU_WIDTH)
    dtypes = (BF16,) * 6 + (F32, BF16)
    out_shape = [jax.ShapeDtypeStruct((bsz, t_len, n), dt) for n, dt in zip(widths, dtypes)]
    out_specs = [tok(n) for n in widths]
    out_shape[2] = jax.ShapeDtypeStruct((bsz, t_len // sub, DA_HEADS * DA_VROWS, sub), BF16)
    out_specs[2] = pl.BlockSpec((1, tm // sub, DA_HEADS * DA_VROWS, sub), lambda b, j: (b, j, 0, 0))
    return pl.pallas_call(
        _inproj_kernel,
        out_shape=tuple(out_shape),
        grid=(bsz, t_len // tm),
        in_specs=[tok(d),
                  pl.BlockSpec((1, 1, d), mod_map), pl.BlockSpec((1, 1, d), mod_map),
                  _const_spec((1, d)), _layer_spec(lw["w_in"].shape, lw["layer"]), _const_spec((1, DA_VEXT)),
                  tab(DA_Q), tab(DA_Q), tab(SW_Q), tab(SW_Q),
                  _const_spec((1, DA_Q)), _const_spec((1, DA_Q)),
                  _const_spec((1, SW_Q)), _const_spec((1, SW_KV)),
                  _const_spec((MXU_DIM, MXU_DIM)), _const_spec((MXU_DIM, MXU_DIM))],
        out_specs=tuple(out_specs),
        compiler_params=_cparams(2),
        name="input_projection",
    )(x, shift, scale, lw["norm1_gain"], lw["w_in"], lw["v_one"], *rope,
      lw["gq_a"], lw["gk_a"], lw["gq_b"], lw["gk_b"], lw["g32"], lw["g64"])


def _da_kernel(*refs, n_seg, lambda_init):
    q_ref = refs[0]
    kv_refs = refs[1:1 + 2 * n_seg]
    hmask_ref, lamv_ref, sg_ref, o_ref, m_ref, acc_ref = refs[1 + 2 * n_seg:]
    n_map = 2 * DA_HEADS
    tq = m_ref.shape[2]
    n_sub = q_ref.shape[1] // tq

    m_ref[...] = jnp.full(m_ref.shape, NEG_INF, F32)
    acc_ref[...] = jnp.zeros(acc_ref.shape, F32)

    items = [(kv_refs[2 * si], kv_refs[2 * si + 1], i, u, j)
             for si in range(n_seg) for i in range(kv_refs[2 * si + 1].shape[1])
             for u in range(n_sub) for j in range(n_map)]

    def scores(k_ref, v_ref, i, u, j):
        half, tk = j // 4, v_ref.shape[3]
        qm = (q_ref[0, u * tq:(u + 1) * tq, half * LANES:(half + 1) * LANES]
              * hmask_ref[j:j + 1, :])
        kblk = k_ref[0, i * tk:(i + 1) * tk, half * LANES:(half + 1) * LANES]
        return _nt_dot(kblk, qm)

    pending = {n: scores(*items[n]) for n in range(min(DA_AHEAD, len(items)))}
    for n, (k_ref, v_ref, i, u, j) in enumerate(items):
        s_b = pending.pop(n).astype(BF16)
        if n + DA_AHEAD < len(items):
            pending[n + DA_AHEAD] = scores(*items[n + DA_AHEAD])
        st = u * n_map + j
        m_old = m_ref[st, 0:1, :]
        m_new = jnp.maximum(m_old, jnp.max(s_b, axis=0, keepdims=True).astype(F32))
        p_t = jnp.exp2(s_b - m_new.astype(BF16))
        pv = _dot(v_ref[0, i, (j // 2) * DA_VROWS:(j // 2 + 1) * DA_VROWS, :], p_t)
        acc_ref[st] = jnp.exp2(m_old - m_new) * acc_ref[st] + pv
        m_ref[st] = jnp.broadcast_to(m_new, m_ref.shape[1:])

    lv = lamv_ref[...]
    lam = (jnp.exp(jnp.sum(lv[0:1] * lv[1:2], axis=1, keepdims=True))
           - jnp.exp(jnp.sum(lv[2:3] * lv[3:4], axis=1, keepdims=True)) + lambda_init)

    def head_out(u, h):
        a1, a2 = acc_ref[u * n_map + 2 * h], acc_ref[u * n_map + 2 * h + 1]
        o = (a1 / a1[DA_V_DIM:DA_V_DIM + 1] - lam * (a2 / a2[DA_V_DIM:DA_V_DIM + 1]))[:DA_V_DIM]
        ms = jnp.sum(o * o, axis=0, keepdims=True) * (1.0 / DA_V_DIM)
        return o * lax.rsqrt(ms + EPS)

    for u in range(n_sub):
        for c in range(DA_HEADS // 2):
            pair_t = jnp.concatenate([head_out(u, 2 * c), head_out(u, 2 * c + 1)], axis=0)
            o_ref[0, u * tq:(u + 1) * tq, c * LANES:(c + 1) * LANES] = (
                pair_t.T * sg_ref[...]).astype(o_ref.dtype)


def _diff_attention(q, kvs, lw, lambda_init):
    bsz, t_q, _ = q.shape
    tq = min(DA_TQ, t_q)
    n_sub = min(DA_SUB, t_q // tq)
    tb = n_sub * tq
    kv_specs, kv_args = [], []
    for k, v in kvs:
        kv_specs += [pl.BlockSpec((1,) + k.shape[1:], lambda b, j: (b, 0, 0)),
                     pl.BlockSpec((1,) + v.shape[1:], lambda b, j: (b, 0, 0, 0))]
        kv_args += [k, v]
    sg = lw["da_sg"] * (1.0 - lambda_init)
    return pl.pallas_call(
        functools.partial(_da_kernel, n_seg=len(kvs), lambda_init=lambda_init),
        out_shape=jax.ShapeDtypeStruct((bsz, t_q, DA_V), BF16),
        grid=(bsz, t_q // tb),
        in_specs=[pl.BlockSpec((1, tb, DA_Q), lambda b, j: (b, j, 0))] + kv_specs
                 + [_const_spec((2 * DA_HEADS, LANES)), _const_spec((4, DA_QK_DIM)),
                    _const_spec((1, LANES))],
        out_specs=pl.BlockSpec((1, tb, DA_V), lambda b, j: (b, j, 0)),
        scratch_shapes=[pltpu.VMEM((n_sub * 2 * DA_HEADS, SUBLANES, tq), F32),
                        pltpu.VMEM((n_sub * 2 * DA_HEADS, DA_VROWS, tq), F32)],
        compiler_params=_cparams(2),
        name="diff_attention",
    )(q, *kv_args, lw["da_hmask"], lw["da_lamv"], sg)


def _sw_kernel(*refs, has_lat, tq, t_len):
    if has_lat:
        sink_ref, q_ref, kc_ref, vc_ref, k_ref, v_ref, hm_ref, o_ref = refs
    else:
        sink_ref, q_ref, kc_ref, vc_ref, hm_ref, o_ref = refs
    def values_t(v):
        n_k = v.shape[0]
        v_t = v.astype(F32).T
        tail = (lax.broadcasted_iota(jnp.int32, (BF16_ROWS, n_k), 0) == 0).astype(F32)
        return [jnp.concatenate([v_t[kv * HEAD_DIM:(kv + 1) * HEAD_DIM], tail], axis=0).astype(BF16)
                for kv in range(SW_KV_HEADS)]

    kc = kc_ref[0]
    vc_t = values_t(vc_ref[0])
    n_sub = q_ref.shape[1] // tq
    bands = []
    for u in range(n_sub if has_lat else 0):
        band = tq + 2 * WINDOW
        q0 = (pl.program_id(1) * n_sub + u) * tq
        start = pl.multiple_of(jnp.clip(q0 - WINDOW, 0, t_len - band), WINDOW)
        qpos = q0 + lax.broadcasted_iota(jnp.int32, (1, tq), 1)
        kpos = start + lax.broadcasted_iota(jnp.int32, (band, 1), 0)
        bands.append((k_ref[0, pl.ds(start, band), :], values_t(v_ref[0, pl.ds(start, band), :]),
                      jnp.abs(qpos - kpos) <= WINDOW))

    combos = [(u, g, kv) for u in range(n_sub) for g in range(SW_GROUP)
              for kv in range(SW_KV_HEADS)]

    def scores(c):
        u, g, kv = combos[c]
        qm = q_ref[0, u * tq:(u + 1) * tq, g * LANES:(g + 1) * LANES] * hm_ref[kv:kv + 1, :]
        s_lat = _nt_dot(bands[u][0], qm) if has_lat else None
        return _nt_dot(kc, qm), s_lat

    pending = {c: scores(c) for c in range(SW_AHEAD)}
    outs = {}
    for c, (u, g, kv) in enumerate(combos):
        s_ctx, s_lat = pending.pop(c)
        if c + SW_AHEAD < len(combos):
            pending[c + SW_AHEAD] = scores(c + SW_AHEAD)
        sink = sink_ref[kv * SW_GROUP + g]
        s_ctx = s_ctx.astype(BF16)
        m = jnp.maximum(jnp.max(s_ctx, axis=0, keepdims=True).astype(F32), sink)
        if has_lat:
            s_lat = jnp.where(bands[u][2], s_lat.astype(BF16), NEG_INF)
            m = jnp.maximum(m, jnp.max(s_lat, axis=0, keepdims=True).astype(F32))
        m_b = m.astype(BF16)
        m = m_b.astype(F32)
        acc = _dot(vc_t[kv], jnp.exp2(s_ctx - m_b))
        if has_lat:
            acc = acc + _dot(bands[u][1][kv], jnp.exp2(s_lat - m_b))
        l = acc[HEAD_DIM:HEAD_DIM + 1] + jnp.exp2(sink - m)
        outs[kv] = acc[:HEAD_DIM] / l
        if kv == SW_KV_HEADS - 1:
            pair_t = jnp.concatenate([outs.pop(k2) for k2 in range(SW_KV_HEADS)], axis=0)
            o_ref[0, u * tq:(u + 1) * tq, g * LANES:(g + 1) * LANES] = pair_t.T.astype(o_ref.dtype)


def _window_attention(q, kc, vc, k, v, lw):
    bsz, t_q, _ = q.shape
    has_lat = k is not None
    tq = min(SW_TQ, t_q)
    tb = min(SW_SUB * tq, t_q)
    n_ctx = kc.shape[1]
    full = lambda t: pl.BlockSpec((1, t, SW_KV), lambda b, j: (b, 0, 0))
    specs = [pl.BlockSpec(memory_space=pltpu.SMEM),
             pl.BlockSpec((1, tb, SW_Q), lambda b, j: (b, j, 0)), full(n_ctx), full(n_ctx)]
    args = [lw["sw_sink"], q, kc, vc]
    if has_lat:
        specs += [full(t_q), full(t_q)]
        args += [k, v]
    specs.append(_const_spec((SW_KV_HEADS, LANES)))
    args.append(lw["sw_hmask"])
    return pl.pallas_call(
        functools.partial(_sw_kernel, has_lat=has_lat, tq=tq, t_len=t_q),
        out_shape=jax.ShapeDtypeStruct((bsz, t_q, SW_Q), BF16),
        grid=(bsz, t_q // tb),
        in_specs=specs,
        out_specs=pl.BlockSpec((1, tb, SW_Q), lambda b, j: (b, j, 0)),
        compiler_params=_cparams(2),
        name="window_attention",
    )(*args)


def _lru_kernel(xf_ref, xb_ref, xc_ref, cw_ref, cb_ref, wa_ref, ba_ref, wx_ref, bx_ref, lam_ref,
                hf_ref, hb_ref, hfc_ref, hbc_ref, halo_ref, carry_ref, *, tc):
    j = pl.program_id(1)
    is_ctx = j == 0
    sub = lax.broadcasted_iota(jnp.int32, (1, SUBLANES, 1), 1)

    dirs = ((0, xf_ref, hf_ref, hfc_ref), (1, xb_ref, hb_ref, hbc_ref))
    n_grp = tc // SUBLANES

    def gates(d, x_lat_ref):
        x = jnp.where(is_ctx, xc_ref[0], x_lat_ref[0])
        halo = jnp.where(j <= 1, 0.0, halo_ref[d])
        cw = cw_ref[d]
        x3 = x.reshape(n_grp, SUBLANES, x.shape[1])
        xc3 = cb_ref[d] + x3 * cw[LRU_CONV - 1:LRU_CONV]
        for k in range(LRU_CONV - 1):
            s = LRU_CONV - 1 - k
            if d == 0:
                rolled = pltpu.roll(x3, s, 1)
                other = jnp.concatenate([pltpu.roll(halo, s, 0)[None], rolled[:-1]], axis=0)
                xs = jnp.where(sub < s, other, rolled)
            else:
                rolled = pltpu.roll(x3, SUBLANES - s, 1)
                other = jnp.concatenate([rolled[1:], pltpu.roll(halo, SUBLANES - s, 0)[None]], axis=0)
                xs = jnp.where(sub >= SUBLANES - s, other, rolled)
            xc3 = xc3 + xs * cw[k:k + 1]
        xc = xc3.reshape(tc, x.shape[1])
        halo_ref[d] = x[tc - SUBLANES:] if d == 0 else x[:SUBLANES]

        xcb = xc.astype(BF16)
        r = jax.nn.sigmoid(_dot(xcb, wa_ref[d]) + ba_ref[d])
        gi = jax.nn.sigmoid(_dot(xcb, wx_ref[d]) + bx_ref[d])
        nl = -lam_ref[d]
        softplus = jnp.maximum(nl, 0.0) + jnp.log1p(jnp.exp(-jnp.abs(nl)))
        log_a = -LRU_C * r * softplus
        a = jnp.exp(log_a)
        b = jnp.sqrt(-jnp.tanh(log_a) * (a * a + 1.0)) * (gi * xc)

        a3 = a.reshape(n_grp, SUBLANES, a.shape[1])
        b3 = b.reshape(n_grp, SUBLANES, b.shape[1])
        step = 1
        while step < SUBLANES:
            keep = (sub < step) if d == 0 else (sub >= SUBLANES - step)
            shift = step if d == 0 else SUBLANES - step
            a_s = jnp.where(keep, 1.0, pltpu.roll(a3, shift, 1))
            b_s = jnp.where(keep, 0.0, pltpu.roll(b3, shift, 1))
            b3 = a3 * b_s + b3
            a3 = a3 * a_s
            step *= 2
        return a3, b3

    ab = [gates(d, x_lat_ref) for d, x_lat_ref, _, _ in dirs]

    edge = (SUBLANES - 1, 0)
    h_in = [jnp.where(is_ctx, 0.0, carry_ref[d, edge[d]:edge[d] + 1, :]) for d in range(2)]
    groups = [[None] * n_grp, [None] * n_grp]
    for t in range(n_grp):
        for d in range(2):
            gi = t if d == 0 else n_grp - 1 - t
            groups[d][gi] = ab[d][0][gi] * h_in[d] + ab[d][1][gi]
            h_in[d] = groups[d][gi][edge[d]:edge[d] + 1]

    for d, _, out_ref, outc_ref in dirs:
        carry_ref[d] = groups[d][n_grp - 1] if d == 0 else groups[d][0]
        h_out = jnp.concatenate(groups[d], axis=0).astype(out_ref.dtype)

        @pl.when(is_ctx)
        def _(outc_ref=outc_ref, h_out=h_out):
            outc_ref[0] = h_out

        @pl.when(jnp.logical_not(is_ctx))
        def _(out_ref=out_ref, h_out=h_out):
            out_ref[0] = h_out


def _rglru_scans(cx, cxc, lw):
    bsz, t_len, n = cx.shape
    tc = LRU_TC
    assert cxc.shape[1] == tc and t_len % tc == 0
    n_lat = t_len // tc
    fwd = lambda b, j: (b, jnp.maximum(j - 1, 0), 0)
    bwd = lambda b, j: (b, n_lat - jnp.maximum(j, 1), 0)
    ctx = lambda b, j: (b, 0, 0)
    blk = lambda m: pl.BlockSpec((1, tc, n), m)
    return pl.pallas_call(
        functools.partial(_lru_kernel, tc=tc),
        out_shape=(jax.ShapeDtypeStruct(cx.shape, BF16), jax.ShapeDtypeStruct(cx.shape, BF16),
                   jax.ShapeDtypeStruct(cxc.shape, BF16), jax.ShapeDtypeStruct(cxc.shape, BF16)),
        grid=(bsz, n_lat + 1),
        in_specs=[blk(fwd), blk(bwd), blk(ctx),
                  _const_spec((2, LRU_CONV, n)), _const_spec((2, 1, n)),
                  _const_spec((2, n, n)), _const_spec((2, 1, n)),
                  _const_spec((2, n, n)), _const_spec((2, 1, n)), _const_spec((2, 1, n))],
        out_specs=(blk(fwd), blk(bwd), blk(ctx), blk(ctx)),
        scratch_shapes=[pltpu.VMEM((2, SUBLANES, n), F32), pltpu.VMEM((2, SUBLANES, n), F32)],
        compiler_params=_cparams(2),
        name="rglru_scans",
    )(cx, cx, cxc, lw["lru_conv_w"], lw["lru_conv_b"], lw["lru_wa"], lw["lru_ba"],
      lw["lru_wx"], lw["lru_bx"], lw["lru_lambda"])


def _gelu_tanh(x):
    return 0.5 * x * (1.0 + jnp.tanh(math.sqrt(2.0 / math.pi) * (x + 0.044715 * (x * x * x))))


def _outproj_kernel(x_ref, ya_ref, yb_ref, hf_ref, hb_ref, cg_ref, g1_ref, w_ref, o_ref, cat_ref):
    cat_ref[:, 0:DA_V] = ya_ref[0]
    cat_ref[:, DA_V:DA_V + SW_Q] = yb_ref[0]
    yc = (hf_ref[0].astype(F32) + hb_ref[0].astype(F32)) * _gelu_tanh(cg_ref[0].astype(F32))
    cat_ref[:, DA_V + SW_Q:] = yc.astype(BF16)
    o_ref[0] = x_ref[0] + g1_ref[0] * _dot(cat_ref[...], w_ref[...])


def _output_projection(x, ya, yb, hf, hb, cg, gate, lw):
    bsz, t_len, d = x.shape
    tm = min(TM_OUT, t_len)
    per_b = gate.shape[0] > 1
    mod_map = (lambda b, j: (b, 0, 0)) if per_b else (lambda b, j: (0, 0, 0))
    tok = lambda n: pl.BlockSpec((1, tm, n), lambda b, j: (b, j, 0))
    return pl.pallas_call(
        _outproj_kernel,
        out_shape=jax.ShapeDtypeStruct(x.shape, F32),
        grid=(bsz, t_len // tm),
        in_specs=[tok(d), tok(DA_V), tok(SW_Q), tok(LRU_WIDTH), tok(LRU_WIDTH), tok(LRU_WIDTH),
                  pl.BlockSpec((1, 1, d), mod_map), _layer_spec(lw["w_out"].shape, lw["layer"])],
        out_specs=tok(d),
        scratch_shapes=[pltpu.VMEM((tm, d), BF16)],
        compiler_params=_cparams(2),
        name="output_projection",
    )(x, ya, yb, hf, hb, cg, gate, lw["w_out"])


def _ffn_kernel(xm_ref, xp_ref, xn_ref, sh_ref, sc_ref, g2_ref, ng_ref, wup_ref, cw_ref, cb_ref,
                wdn_ref, o_ref, h_ref, *, tm):
    j = pl.program_id(1)
    last = pl.num_programs(1) - 1
    gain, shift, scale = ng_ref[...], sh_ref[0], sc_ref[0]
    xm = xm_ref[0]
    hp = jnp.where(j > 0, _norm_modulate(xp_ref[0], gain, shift, scale), 0.0)
    hn = jnp.where(j < last, _norm_modulate(xn_ref[0], gain, shift, scale), 0.0)
    h_ref[0:FFN_HALO] = hp.astype(BF16)
    h_ref[FFN_HALO:FFN_HALO + tm] = _norm_modulate(xm, gain, shift, scale).astype(BF16)
    h_ref[FFN_HALO + tm:] = hn.astype(BF16)

    rows = tm + 2 * FFN_HALO

    def up(lo, hi):
        return (_dot(h_ref[...], wup_ref[:, lo:hi]),
                _dot(h_ref[FFN_HALO:FFN_HALO + tm], wup_ref[:, D_FF + lo:D_FF + hi]))

    acc = jnp.zeros((tm, D_MODEL), F32)
    nxt = up(*FFN_CHUNKS[0])
    for c, (lo, hi) in enumerate(FFN_CHUNKS):
        gp, val = nxt
        if c + 1 < len(FFN_CHUNKS):
            nxt = up(*FFN_CHUNKS[c + 1])
        g_prev = pltpu.roll(gp, 1, 0)[FFN_HALO:FFN_HALO + tm]
        g_next = pltpu.roll(gp, rows - 1, 0)[FFN_HALO:FFN_HALO + tm]
        cw = cw_ref[:, lo:hi]
        gate = (cb_ref[:, lo:hi] + g_prev * cw[0:1] + gp[FFN_HALO:FFN_HALO + tm] * cw[1:2]
                + g_next * cw[2:3])
        act = (gate * jax.nn.sigmoid(gate) * val).astype(BF16)
        acc = acc + _dot(act, wdn_ref[lo:hi, :])
    o_ref[0] = xm + g2_ref[0] * acc


def _conv_ffn(x, shift, scale, gate, lw):
    bsz, t_len, d = x.shape
    tm = min(TM, t_len)
    per_b = gate.shape[0] > 1
    mod_map = (lambda b, j: (b, 0, 0)) if per_b else (lambda b, j: (0, 0, 0))
    hb = tm // FFN_HALO
    n_hb = t_len // FFN_HALO
    prev_map = lambda b, j: (b, jnp.maximum(j * hb - 1, 0), 0)
    next_map = lambda b, j: (b, jnp.minimum((j + 1) * hb, n_hb - 1), 0)
    mod_spec = pl.BlockSpec((1, 1, d), mod_map)
    return pl.pallas_call(
        functools.partial(_ffn_kernel, tm=tm),
        out_shape=jax.ShapeDtypeStruct(x.shape, F32),
        grid=(bsz, t_len // tm),
        in_specs=[pl.BlockSpec((1, tm, d), lambda b, j: (b, j, 0)),
                  pl.BlockSpec((1, FFN_HALO, d), prev_map),
                  pl.BlockSpec((1, FFN_HALO, d), next_map),
                  mod_spec, mod_spec, mod_spec,
                  _const_spec((1, d)), _layer_spec(lw["w_up"].shape, lw["layer"]),
                  _const_spec((FFN_CONV, D_FF)), _const_spec((1, D_FF)),
                  _layer_spec(lw["w_down"].shape, lw["layer"])],
        out_specs=pl.BlockSpec((1, tm, d), lambda b, j: (b, j, 0)),
        scratch_shapes=[pltpu.VMEM((tm + 2 * FFN_HALO, d), BF16)],
        compiler_params=_cparams(2, fuse_inputs=[False] * 7 + [True, False, False, True]),
        name="conv_ffn",
    )(x, x, x, shift, scale, gate, lw["norm2_gain"], lw["w_up"], lw["ffn_conv_w"],
      lw["ffn_conv_b"], lw["w_down"])


def _sw_head_order():
    return [kv * SW_GROUP + g for g in range(SW_GROUP) for kv in range(SW_KV_HEADS)]


def _rope_tables(rows, head_dim, n_heads):
    row = np.repeat(np.arange(rows, dtype=np.float64), GRID_W)
    col = np.tile(np.arange(GRID_W, dtype=np.float64), rows)
    quarter = head_dim // 4
    inv_freq = ROPE_BASE ** (-np.arange(quarter, dtype=np.float64) / quarter)
    ang = np.concatenate([row[:, None] * inv_freq, col[:, None] * inv_freq], axis=-1)
    cos, sin = np.cos(ang), np.sin(ang)
    return (jnp.asarray(np.tile(np.concatenate([cos, cos], axis=-1), (1, n_heads)), dtype=F32),
            jnp.asarray(np.tile(np.concatenate([-sin, sin], axis=-1), (1, n_heads)), dtype=F32))


def _block_diag_mean(n, group):
    idx = np.arange(n) // group
    return jnp.asarray((idx[:, None] == idx[None, :]).astype(np.float32) / group, dtype=BF16)


def _block_diag(w):
    two, nb, bi, bj = w.shape
    eye = jnp.eye(nb, dtype=w.dtype)
    return jnp.einsum("dhij,hg->dhigj", w, eye).reshape(two, nb * bi, nb * bj)


def _prepare_weights(p):
    d = D_MODEL
    depth = p["w_in"].shape[0]
    split = np.cumsum((DA_Q, DA_Q, DA_V, SW_Q, SW_KV, SW_KV, LRU_WIDTH))
    w_aq, w_ak, w_av, w_bq, w_bk, w_bv, w_cx, w_cg = jnp.split(p["w_in"], split.tolist(), axis=2)
    w_av = jnp.pad(w_av.reshape(depth, d, DA_HEADS, DA_V_DIM),
                   ((0, 0), (0, 0), (0, 0), (0, LANES - DA_V_DIM))).reshape(depth, d, DA_VEXT)
    order = jnp.asarray(_sw_head_order())
    w_bq = w_bq.reshape(depth, d, SW_HEADS, HEAD_DIM)[:, :, order].reshape(depth, d, SW_Q)
    w_in = jnp.concatenate([w_aq, w_ak, w_av, w_bq, w_bk, w_bv, w_cx, w_cg], axis=2)

    w_out = p["w_out"]
    w_ob = w_out[:, DA_V:DA_V + SW_Q].reshape(depth, SW_HEADS, HEAD_DIM, d)[:, order]
    w_out = jnp.concatenate([w_out[:, :DA_V], w_ob.reshape(depth, SW_Q, d), w_out[:, DA_V + SW_Q:]],
                            axis=1)
    return dict(w_in=w_in.astype(BF16), w_out=w_out.astype(BF16),
                w_up=p["w_up"].astype(BF16), w_down=p["w_down"].astype(BF16))


def _prepare_layer(p, i):
    d = D_MODEL
    v_one = np.zeros((1, DA_VEXT), np.float32)
    v_one[0, DA_V_DIM::LANES] = 1.0

    da_hmask = np.zeros((2 * DA_HEADS, LANES), np.float32)
    for j in range(2 * DA_HEADS):
        off = (j % 4) * DA_QK_DIM
        da_hmask[j, off:off + DA_QK_DIM] = 1.0
    sw_hmask = np.zeros((SW_KV_HEADS, LANES), np.float32)
    for kv in range(SW_KV_HEADS):
        sw_hmask[kv, kv * HEAD_DIM:(kv + 1) * HEAD_DIM] = 1.0

    return dict(
        norm1_gain=p["norm1_gain"][i].reshape(1, d), norm2_gain=p["norm2_gain"][i].reshape(1, d),
        layer=i, v_one=jnp.asarray(v_one),
        gq_a=(jnp.tile(p["da_q_gain"][i], 2 * DA_HEADS).reshape(1, DA_Q)
              * (DA_QK_DIM ** -0.5 * math.log2(math.e))),
        gk_a=jnp.tile(p["da_k_gain"][i], 2 * DA_HEADS).reshape(1, DA_Q),
        gq_b=(jnp.tile(p["sw_q_gain"][i], SW_HEADS).reshape(1, SW_Q)
              * (HEAD_DIM ** -0.5 * math.log2(math.e))),
        gk_b=jnp.tile(p["sw_k_gain"][i], SW_KV_HEADS).reshape(1, SW_KV),
        g32=_block_diag_mean(MXU_DIM, DA_QK_DIM), g64=_block_diag_mean(MXU_DIM, HEAD_DIM),
        da_hmask=jnp.asarray(da_hmask, dtype=BF16),
        da_lamv=jnp.stack([p["da_lam_q1"][i], p["da_lam_k1"][i], p["da_lam_q2"][i], p["da_lam_k2"][i]]),
        da_sg=jnp.tile(p["da_sub_gain"][i], LANES // DA_V_DIM).reshape(1, LANES),
        sw_sink=p["sw_sink"][i] * math.log2(math.e), sw_hmask=jnp.asarray(sw_hmask, dtype=BF16),
        lru_conv_w=p["lru_conv_w"][i], lru_conv_b=p["lru_conv_b"][i].reshape(2, 1, LRU_WIDTH),
        lru_wa=_block_diag(p["lru_wa"][i]).astype(BF16), lru_ba=p["lru_ba"][i].reshape(2, 1, LRU_WIDTH),
        lru_wx=_block_diag(p["lru_wx"][i]).astype(BF16), lru_bx=p["lru_bx"][i].reshape(2, 1, LRU_WIDTH),
        lru_lambda=p["lru_lambda"][i].reshape(2, 1, LRU_WIDTH),
        ffn_conv_w=p["ffn_conv_w"][i], ffn_conv_b=p["ffn_conv_b"][i].reshape(1, D_FF),
    )


def kernel(x, c, ctx, c_ctx, w_mod, b_mod, norm1_gain, norm2_gain, w_in, da_q_gain, da_k_gain, da_lam_q1, da_lam_k1, da_lam_q2, da_lam_k2, da_sub_gain, sw_q_gain, sw_k_gain, sw_sink, lru_conv_w, lru_conv_b, lru_wa, lru_ba, lru_wx, lru_bx, lru_lambda, w_out, w_up, ffn_conv_w, ffn_conv_b, w_down):
    p = dict(w_in=w_in, norm1_gain=norm1_gain, norm2_gain=norm2_gain, da_q_gain=da_q_gain,
             da_k_gain=da_k_gain, da_lam_q1=da_lam_q1, da_lam_k1=da_lam_k1, da_lam_q2=da_lam_q2,
             da_lam_k2=da_lam_k2, da_sub_gain=da_sub_gain, sw_q_gain=sw_q_gain, sw_k_gain=sw_k_gain,
             sw_sink=sw_sink, lru_conv_w=lru_conv_w, lru_conv_b=lru_conv_b, lru_wa=lru_wa,
             lru_ba=lru_ba, lru_wx=lru_wx, lru_bx=lru_bx, lru_lambda=lru_lambda, w_out=w_out,
             w_up=w_up, ffn_conv_w=ffn_conv_w, ffn_conv_b=ffn_conv_b, w_down=w_down)
    bsz, n_tok, d = x.shape
    n_ctx = ctx.shape[1]
    depth = w_mod.shape[0]

    cc = jnp.zeros((2 * SUBLANES, d), F32).at[:bsz].set(c).at[bsz].set(c_ctx)
    mod_all = _modulation(cc, w_mod, b_mod)

    rope_lat = (_rope_tables(n_tok // GRID_W, DA_QK_DIM, 2 * DA_HEADS)
                + _rope_tables(n_tok // GRID_W, HEAD_DIM, SW_HEADS))
    rope_ctx = (jnp.ones((n_ctx, DA_Q), F32), jnp.zeros((n_ctx, DA_Q), F32),
                jnp.ones((n_ctx, SW_Q), F32), jnp.zeros((n_ctx, SW_Q), F32))

    weights = _prepare_weights(p)
    xc = ctx
    for i in range(depth):
        lw = {**_prepare_layer(p, i), **weights}
        lambda_init = 0.8 - 0.6 * math.exp(-0.3 * i)
        ctx_out = i < depth - 1
        mod = mod_all[i, :bsz].reshape(bsz, 1, N_MOD, d)
        sh1, sc1, g1, sh2, sc2, g2 = [mod[:, :, m] for m in range(N_MOD)]
        mod_c = mod_all[i, bsz].reshape(1, 1, N_MOD, d)
        sh1c, sc1c, g1c, sh2c, sc2c, g2c = [mod_c[:, :, m] for m in range(N_MOD)]

        qa, ka, va, qb, kb, vb, cx, cg = _input_projection(x, sh1, sc1, lw, rope_lat)
        qac, kac, vac, qbc, kbc, vbc, cxc, cgc = _input_projection(xc, sh1c, sc1c, lw, rope_ctx)

        ya = _diff_attention(qa, [(kac, vac), (ka, va)], lw, lambda_init)
        yb = _window_attention(qb, kbc, vbc, kb, vb, lw)
        hf, hb, hfc, hbc = _rglru_scans(cx, cxc, lw)

        x = _output_projection(x, ya, yb, hf, hb, cg, g1, lw)
        x = _conv_ffn(x, sh2, sc2, g2, lw)
        if ctx_out:
            yac = _diff_attention(qac, [(kac, vac)], lw, lambda_init)
            ybc = _window_attention(qbc, kbc, vbc, None, None, lw)
            xc = _output_projection(xc, yac, ybc, hfc, hbc, cgc, g1c, lw)
            xc = _conv_ffn(xc, sh2c, sc2c, g2c, lw)
    return x
```
